```python
import math
import jax, jax.numpy as jnp
from jax import lax
import numpy as np

D_MODEL = 1024
BATCH = 4
SEQ = 4096
DEPTH = 4

GRID_W = 64
CTX_LEN = 256
N_EVEN = (DEPTH + 1) // 2
N_ODD = DEPTH // 2
N_MOD = 6
NORM_EPS = 1e-6
W_A = D_MODEL // 2
HY_ORDER = 2
HY_SHORT = 3
HY_BANDS = 16
HY_EMB = 2 * HY_BANDS + 1
HY_FFN = 64
HY_TARGET = 1e-2
HY_FAST_PCT = 0.3
HY_SLOW_PCT = 1.5
W_B = D_MODEL // 2
SGU_GROUPS = 4
SGU_DH = W_B // SGU_GROUPS
CHUNK = 128
D_RNN = ((4 * D_MODEL // 3 + 127) // 128) * 128
RG_HEADS = 16
RG_DH = D_RNN // RG_HEADS
RG_CONV = 4
RG_C = 8.0
D_FF = ((8 * D_MODEL // 3 + 255) // 256) * 256

kernel_name = 'hybrid_hyena_sgu_rglru_dit_block'


def rmsnorm(x, g):
    xf = x.astype(jnp.float32)
    xf = xf * lax.rsqrt(jnp.mean(xf * xf, axis=-1, keepdims=True) + NORM_EPS)
    return xf.astype(x.dtype) * g


def layernorm(x, g):
    xf = x.astype(jnp.float32)
    xc = xf - jnp.mean(xf, axis=-1, keepdims=True)
    xf = xc * lax.rsqrt(jnp.mean(xc * xc, axis=-1, keepdims=True) + NORM_EPS)
    return xf.astype(x.dtype) * g


def modulate(h, shift, scale):
    return h * (1 + scale) + shift


def depthwise_conv(x, w, b, left):
    k_w, L = w.shape[0], x.shape[1]
    xp = jnp.pad(x, ((0, 0), (left, k_w - 1 - left), (0, 0)))
    y = b
    for k in range(k_w):
        y = y + xp[:, k:k + L] * w[k]
    return y


def to_col_major(x):
    b, L, d = x.shape
    rows = L // GRID_W
    return x.reshape(b, rows, GRID_W, d).transpose(0, 2, 1, 3).reshape(b, L, d)


def to_row_major(x):
    b, L, d = x.shape
    rows = L // GRID_W
    return x.reshape(b, GRID_W, rows, d).transpose(0, 2, 1, 3).reshape(b, L, d)


def swiglu(h, w_in, w_out):
    z = h @ w_in
    return (jax.nn.silu(z[..., :D_FF]) * z[..., D_FF:]) @ w_out


def hyena_filter_spectrum(L, f1_w, f1_b, f2_w, f2_b, f3_w, f3_b, sin_freq):
    t = jnp.linspace(0.0, 1.0, L, dtype=jnp.float32)[:, None]
    w = 2.0 * math.pi * jnp.arange(L, dtype=jnp.float32)[:, None] / L
    f = jnp.linspace(1e-4, HY_BANDS - 1, HY_BANDS, dtype=jnp.float32)[None, :]
    emb = jnp.concatenate([t, jnp.cos(f * w), -jnp.sin(f * w)], axis=-1)
    hdn = jnp.sin(sin_freq * (emb @ f1_w + f1_b))
    hdn = jnp.sin(sin_freq * (hdn @ f2_w + f2_b))
    h = (hdn @ f3_w + f3_b).astype(jnp.float32).reshape(L, HY_ORDER, 2, W_A)
    deltas = jnp.abs(jnp.linspace(math.log(HY_TARGET) / HY_SLOW_PCT,
                                  math.log(HY_TARGET) / HY_FAST_PCT, W_A, dtype=jnp.float32))
    h = h * jnp.exp(-t[:, :, None, None] * deltas)
    k = jnp.concatenate([h[:, :, 0], jnp.zeros((1, HY_ORDER, W_A), jnp.float32), h[:0:-1, :, 1]], axis=0)
    k = k * lax.rsqrt(jnp.sum(k * k, axis=0, keepdims=True) + NORM_EPS)
    return jnp.fft.rfft(k, n=2 * L, axis=0)


def long_conv(z, kf):
    L = z.shape[1]
    zf = jnp.fft.rfft(z.astype(jnp.float32), n=2 * L, axis=1)
    return jnp.fft.irfft(zf * kf, n=2 * L, axis=1)[:, :L].astype(z.dtype)


def hyena_sgu_mixer(h, w_in, w_out, hy_conv_w, hy_conv_b, hy_f1_w, hy_f1_b, hy_f2_w, hy_f2_b,
                    hy_f3_w, hy_f3_b, hy_sin_freq, hy_skip, sgu_ln_g, sgu_w, sgu_b):
    b, L, _ = h.shape
    z = h @ w_in
    za = depthwise_conv(z[..., :3 * W_A], hy_conv_w, hy_conv_b, left=1)
    v, g1, g2 = za[..., :W_A], za[..., W_A:2 * W_A], za[..., 2 * W_A:]
    kf = hyena_filter_spectrum(L, hy_f1_w, hy_f1_b, hy_f2_w, hy_f2_b, hy_f3_w, hy_f3_b, hy_sin_freq)
    y = g1 * (long_conv(v, kf[:, 0]) + hy_skip[0] * v)
    y_a = g2 * (long_conv(y, kf[:, 1]) + hy_skip[1] * y)
    zb = jax.nn.gelu(z[..., 3 * W_A:])
    u, vb = zb[..., :W_B], zb[..., W_B:]
    vb = layernorm(vb, sgu_ln_g).reshape(b, L // CHUNK, CHUNK, SGU_GROUPS, SGU_DH)
    s = jnp.einsum('gpq,bnqgd->bnpgd', sgu_w, vb) + sgu_b.T[None, None, :, :, None]
    y_b = u * s.reshape(b, L, W_B)
    return jnp.concatenate([y_a, y_b], axis=-1) @ w_out


def rglru_coeffs(xb, w_a, b_a, w_x, b_x, lam):
    b, L, _ = xb.shape
    xh = xb.reshape(b, L, RG_HEADS, RG_DH)
    r = jax.nn.sigmoid(jnp.einsum('blhd,hde->blhe', xh, w_a).reshape(b, L, D_RNN) + b_a)
    gi = jax.nn.sigmoid(jnp.einsum('blhd,hde->blhe', xh, w_x).reshape(b, L, D_RNN) + b_x)
    log_a = (-RG_C * r.astype(jnp.float32)) * jax.nn.softplus(-lam.astype(jnp.float32))
    a = jnp.exp(log_a)
    bx = jnp.sqrt(-jnp.expm1(2.0 * log_a)) * (gi * xb).astype(jnp.float32)
    return a, bx


def linear_scan(a, bx, h0, reverse):
    if h0 is not None:
        edge = -1 if reverse else 0
        bx = bx.at[:, edge].add(a[:, edge] * h0)

    def combine(e1, e2):
        a1, b1 = e1
        a2, b2 = e2
        return a1 * a2, a2 * b1 + b2

    _, h = lax.associative_scan(combine, (a, bx), reverse=reverse, axis=1)
    return h


def bidir_rglru_mixer(h_lat, h_ctx, col_major, ctx_out, w_in, conv_w, conv_b, wa, ba, wx, bx, lam, w_out):
    xc = depthwise_conv(h_ctx @ w_in[:, D_RNN:], conv_w, conv_b, left=2)
    af, bf = rglru_coeffs(xc, wa[0], ba[0], wx[0], bx[0], lam[0])
    ab, bb = rglru_coeffs(xc, wa[1], ba[1], wx[1], bx[1], lam[1])
    hf_c = linear_scan(af, bf, None, False)
    hb_c = linear_scan(ab, bb, None, True)
    if col_major:
        h_lat = to_col_major(h_lat)
    z = h_lat @ w_in
    gate = jax.nn.gelu(z[..., :D_RNN])
    xl = depthwise_conv(z[..., D_RNN:], conv_w, conv_b, left=2)
    af, bf = rglru_coeffs(xl, wa[0], ba[0], wx[0], bx[0], lam[0])
    ab, bb = rglru_coeffs(xl, wa[1], ba[1], wx[1], bx[1], lam[1])
    hf = linear_scan(af, bf, hf_c[:, -1], False)
    hb = linear_scan(ab, bb, hb_c[:, 0], True)
    y_lat = (gate * (hf + hb).astype(gate.dtype)) @ w_out
    if col_major:
        y_lat = to_row_major(y_lat)
    if ctx_out:
        gate_c = jax.nn.gelu(h_ctx @ w_in[:, :D_RNN])
        y_ctx = (gate_c * (hf_c + hb_c).astype(gate_c.dtype)) @ w_out
        return y_lat, y_ctx
    return y_lat, None


def setup_inputs(seed: int = 0) -> dict:
    key = jax.random.key(seed)
    keys = jax.random.split(key, 40)
    counter = [0]

    def nxt():
        k = keys[counter[0]]
        counter[0] += 1
        return k

    def nrm(shape, scale):
        return jax.random.normal(nxt(), shape, jnp.float32) * scale

    def gain(shape):
        return 1.0 + nrm(shape, 0.05)

    D = D_MODEL
    u = jax.random.uniform(nxt(), (N_ODD, 2, D_RNN), jnp.float32, minval=0.9, maxval=0.999)
    sig = u ** (1.0 / RG_C)
    rg_lam = jnp.log(sig / (1.0 - sig))
    return {
        'x': nrm((BATCH, SEQ, D), 1.0),
        'c': nrm((BATCH, D), 1.0),
        'ctx': nrm((BATCH, CTX_LEN, D), 1.0),
        'c_ctx': nrm((D,), 1.0),
        'w_ada': nrm((DEPTH, D, N_MOD * D), 0.5 * D ** -0.5),
        'b_ada': nrm((DEPTH, N_MOD * D), 0.02),
        'norm_mix_g': gain((DEPTH, D)),
        'norm_ffn_g': gain((DEPTH, D)),
        'w_in_even': nrm((N_EVEN, D, 3 * W_A + 2 * W_B), D ** -0.5),
        'w_out_even': nrm((N_EVEN, W_A + W_B, D), (W_A + W_B) ** -0.5),
        'hy_conv_w': nrm((N_EVEN, HY_SHORT, 3 * W_A), HY_SHORT ** -0.5),
        'hy_conv_b': nrm((N_EVEN, 3 * W_A), 0.02),
        'hy_f1_w': nrm((N_EVEN, HY_EMB, HY_FFN), HY_EMB ** -0.5),
        'hy_f1_b': nrm((N_EVEN, HY_FFN), 0.1),
        'hy_f2_w': nrm((N_EVEN, HY_FFN, HY_FFN), HY_FFN ** -0.5),
        'hy_f2_b': nrm((N_EVEN, HY_FFN), 0.1),
        'hy_f3_w': nrm((N_EVEN, HY_FFN, HY_ORDER * 2 * W_A), HY_FFN ** -0.5),
        'hy_f3_b': nrm((N_EVEN, HY_ORDER * 2 * W_A), 0.02),
        'hy_sin_freq': gain((N_EVEN, HY_FFN)),
        'hy_skip': nrm((N_EVEN, HY_ORDER, W_A), 0.5),
        'sgu_ln_g': gain((N_EVEN, W_B)),
        'sgu_w': nrm((N_EVEN, SGU_GROUPS, CHUNK, CHUNK), CHUNK ** -0.5),
        'sgu_b': 1.0 + nrm((N_EVEN, SGU_GROUPS, CHUNK), 0.1),
        'w_in_odd': nrm((N_ODD, D, 2 * D_RNN), D ** -0.5),
        'rg_conv_w': nrm((N_ODD, RG_CONV, D_RNN), RG_CONV ** -0.5),
        'rg_conv_b': nrm((N_ODD, D_RNN), 0.02),
        'rg_wa': nrm((N_ODD, 2, RG_HEADS, RG_DH, RG_DH), RG_DH ** -0.5),
        'rg_ba': nrm((N_ODD, 2, D_RNN), 0.02),
        'rg_wx': nrm((N_ODD, 2, RG_HEADS, RG_DH, RG_DH), RG_DH ** -0.5),
        'rg_bx': nrm((N_ODD, 2, D_RNN), 0.02),
        'rg_lam': rg_lam,
        'w_out_odd': nrm((N_ODD, D_RNN, D), D_RNN ** -0.5),
        'w_ffn_in': nrm((DEPTH, D, 2 * D_FF), D ** -0.5),
        'w_ffn_out': nrm((DEPTH, D_FF, D), D_FF ** -0.5),
        'final_norm_g': gain((D,)),
    }


def reference(x, c, ctx, c_ctx, w_ada, b_ada, norm_mix_g, norm_ffn_g,
              w_in_even, w_out_even, hy_conv_w, hy_conv_b, hy_f1_w, hy_f1_b, hy_f2_w, hy_f2_b,
              hy_f3_w, hy_f3_b, hy_sin_freq, hy_skip, sgu_ln_g, sgu_w, sgu_b,
              w_in_odd, rg_conv_w, rg_conv_b, rg_wa, rg_ba, rg_wx, rg_bx, rg_lam, w_out_odd,
              w_ffn_in, w_ffn_out, final_norm_g):
    silu_c = jax.nn.silu(c)
    silu_cc = jax.nn.silu(c_ctx)
    for l in range(DEPTH):
        run_ctx = l < DEPTH - 1
        is_rec = l % 2 == 1
        i = l // 2
        mod_x = (silu_c @ w_ada[l] + b_ada[l]).reshape(-1, 1, N_MOD, D_MODEL)
        hx = modulate(rmsnorm(x, norm_mix_g[l]), mod_x[:, :, 0], mod_x[:, :, 1])
        if run_ctx or is_rec:
            mod_c = (silu_cc @ w_ada[l] + b_ada[l]).reshape(1, 1, N_MOD, D_MODEL)
            hc = modulate(rmsnorm(ctx, norm_mix_g[l]), mod_c[:, :, 0], mod_c[:, :, 1])
        if is_rec:
            mix_x, mix_c = bidir_rglru_mixer(hx, hc, i % 2 == 1, run_ctx, w_in_odd[i], rg_conv_w[i],
                                             rg_conv_b[i], rg_wa[i], rg_ba[i], rg_wx[i], rg_bx[i],
                                             rg_lam[i], w_out_odd[i])
        else:
            ep = (w_in_even[i], w_out_even[i], hy_conv_w[i], hy_conv_b[i], hy_f1_w[i], hy_f1_b[i],
                  hy_f2_w[i], hy_f2_b[i], hy_f3_w[i], hy_f3_b[i], hy_sin_freq[i], hy_skip[i],
                  sgu_ln_g[i], sgu_w[i], sgu_b[i])
            mix_x = hyena_sgu_mixer(hx, *ep)
            mix_c = hyena_sgu_mixer(hc, *ep) if run_ctx else None
        x = x + mod_x[:, :, 2] * mix_x
        hx = modulate(rmsnorm(x, norm_ffn_g[l]), mod_x[:, :, 3], mod_x[:, :, 4])
        x = x + mod_x[:, :, 5] * swiglu(hx, w_ffn_in[l], w_ffn_out[l])
        if run_ctx:
            ctx = ctx + mod_c[:, :, 2] * mix_c
            hc = modulate(rmsnorm(ctx, norm_ffn_g[l]), mod_c[:, :, 3], mod_c[:, :, 4])
            ctx = ctx + mod_c[:, :, 5] * swiglu(hc, w_ffn_in[l], w_ffn_out[l])
    return rmsnorm(x, final_norm_g)
```

```python
import functools
import math

import numpy as np
import jax
import jax.numpy as jnp
from jax import lax
from jax.experimental import pallas as pl
from jax.experimental.pallas import tpu as pltpu

F32 = jnp.float32
BF16 = jnp.bfloat16

D_MODEL = 1024
BATCH = 4
SEQ = 4096
DEPTH = 4
GRID_W = 64
CTX_LEN = 256
N_MOD = 6
NORM_EPS = 1e-6
W_A = D_MODEL // 2
HY_ORDER = 2
HY_SHORT = 3
HY_BANDS = 16
HY_TARGET = 1e-2
HY_FAST_PCT = 0.3
HY_SLOW_PCT = 1.5
W_B = D_MODEL // 2
SGU_GROUPS = 4
SGU_DH = W_B // SGU_GROUPS
CHUNK = 128
D_RNN = ((4 * D_MODEL // 3 + 127) // 128) * 128
RG_HEADS = 16
RG_DH = D_RNN // RG_HEADS
RG_CONV = 4
RG_C = 8.0
D_FF = ((8 * D_MODEL // 3 + 255) // 256) * 256

VMEM_BYTES_V7X = 64 * 1024 * 1024
VMEM_LIMIT = VMEM_BYTES_V7X - 8 * 1024 * 1024
LANES = 128

FFT_N = 2 * SEQ
FFT_N1 = 64
FFT_N2 = 128
FFT_HALF_N1 = FFT_N1 // 2
FFT_PITCH = FFT_N2 + 8
CTX_N = 2 * CTX_LEN


def _cparams(*sem):
    return pltpu.CompilerParams(dimension_semantics=sem, vmem_limit_bytes=VMEM_LIMIT)


def _single_buffered(block_shape, index_map):
    return pl.BlockSpec(block_shape, index_map, pipeline_mode=pl.Buffered(1))


@functools.lru_cache(maxsize=None)
def _dft_tables_f32():
    n = FFT_N
    k1 = np.arange(FFT_N1)
    n2 = np.arange(FFT_N2)

    def angle(n1):
        m = (FFT_N2 * n1[None, None, :] + n2[:, None, None]) * k1[None, :, None]
        return 2.0 * np.pi * (m % n) / n

    a = angle(np.arange(FFT_HALF_N1))
    c, s = np.cos(a), np.sin(a)
    g = np.empty((FFT_N2, FFT_N1, 2, 2, FFT_HALF_N1))
    g[:, :, 0, 0], g[:, :, 0, 1] = c, s
    g[:, :, 1, 0], g[:, :, 1, 1] = -s, c
    g = g.reshape(FFT_N2, 2 * FFT_N1, 2 * FFT_HALF_N1)

    a = angle(np.arange(FFT_N1))
    gf = np.stack([np.cos(a), -np.sin(a)], axis=2).reshape(FFT_N2, 2 * FFT_N1, FFT_N1)

    a = np.transpose(angle(np.arange(FFT_HALF_N1)), (0, 2, 1))
    c, s = np.cos(a) / n, np.sin(a) / n
    h = np.empty((FFT_N2, 2, FFT_HALF_N1, FFT_N1, 2))
    h[:, 0, :, :, 0], h[:, 0, :, :, 1] = c, -s
    h[:, 1, :, :, 0], h[:, 1, :, :, 1] = s, c
    h = h.reshape(FFT_N2, 2 * FFT_HALF_N1, 2 * FFT_N1)

    a = 2.0 * np.pi * np.outer(n2, n2) / FFT_N2
    c, s = np.cos(a), np.sin(a)
    f2 = np.block([[c, s], [-s, c]])
    f2i = np.block([[c, -s], [s, c]])

    kk = np.arange(CTX_N)
    a = 2.0 * np.pi * np.outer(kk, np.arange(CTX_LEN)) / CTX_N
    c, s = np.cos(a), np.sin(a)
    fc = np.block([[c, s], [-s, c]])
    a = 2.0 * np.pi * np.outer(kk, np.arange(CTX_N)) / CTX_N
    fcf = np.concatenate([np.cos(a), -np.sin(a)], axis=0)
    a = 2.0 * np.pi * np.outer(np.arange(CTX_LEN), kk) / CTX_N
    c, s = np.cos(a) / CTX_N, np.sin(a) / CTX_N
    fci = np.block([[c, -s], [s, c]])

    tables = dict(g=g, gf=gf, h=h, f2=f2, f2i=f2i, fc=fc, fcf=fcf, fci=fci)
    return {k: np.asarray(v, dtype=np.float32) for k, v in tables.items()}


def _dft_tables():
    return {k: jnp.asarray(v).astype(BF16) for k, v in _dft_tables_f32().items()}


SEQ_TILE = 1024
N1_PER_TILE = SEQ_TILE // FFT_N2


def _scatter_n2(s_ref, n2, val):
    for p in range(s_ref.shape[0]):
        s_ref[p, pl.ds(n2, 2 * FFT_N1, stride=FFT_PITCH), :] = val[:, p * LANES:(p + 1) * LANES]


def _gather_n2(s_ref, n2):
    return jnp.concatenate(
        [s_ref[p, pl.ds(n2, 2 * FFT_N1, stride=FFT_PITCH), :] for p in range(s_ref.shape[0])],
        axis=1)


def _spectrum_rows(s_ref, k1):
    base = pl.multiple_of(k1 * (2 * FFT_PITCH), 8)
    return base, jnp.concatenate(
        [jnp.concatenate([s_ref[p, pl.ds(base + o, FFT_N2), :] for p in range(s_ref.shape[0])],
                         axis=1) for o in (0, FFT_PITCH)], axis=0)


def _filter_spectrum_kernel(k_ref, skip_ref, gf_ref, f2_ref, o_ref, s_ref):
    def stage1(n2, carry):
        x = k_ref[pl.ds(pl.multiple_of(n2 * FFT_N1, FFT_N1), FFT_N1), :].astype(BF16)
        _scatter_n2(s_ref, n2, jnp.dot(gf_ref[n2], x, preferred_element_type=F32))
        return carry

    lax.fori_loop(0, FFT_N2, stage1, 0)
    skip = skip_ref[...]

    def stage2(k1, carry):
        _, r = _spectrum_rows(s_ref, k1)
        z = jnp.dot(f2_ref[...], r.astype(BF16), preferred_element_type=F32)
        row = pl.multiple_of(k1 * (2 * FFT_N2), 2 * FFT_N2)
        o_ref[pl.ds(row, FFT_N2), :] = z[:FFT_N2] + skip
        o_ref[pl.ds(row + FFT_N2, FFT_N2), :] = z[FFT_N2:]
        return carry

    lax.fori_loop(0, FFT_N1, stage2, 0)


def _filter_spectrum(k_taps, skip):
    t = _dft_tables()
    c = k_taps.shape[1]
    ct = LANES
    return pl.pallas_call(
        _filter_spectrum_kernel,
        grid=(c // ct,),
        in_specs=[
            pl.BlockSpec((FFT_N, ct), lambda j: (0, j)),
            pl.BlockSpec((1, ct), lambda j: (0, j)),
            pl.BlockSpec(t["gf"].shape, lambda j: (0, 0, 0)),
            pl.BlockSpec(t["f2"].shape, lambda j: (0, 0)),
        ],
        out_specs=pl.BlockSpec((2 * FFT_N, ct), lambda j: (0, j)),
        out_shape=jax.ShapeDtypeStruct((2 * FFT_N, c), F32),
        scratch_shapes=[pltpu.VMEM((ct // LANES, 2 * FFT_N1 * FFT_PITCH, LANES), F32)],
        compiler_params=_cparams("arbitrary"),
        name="hyena_filter_spectrum",
    )(k_taps, skip, t["gf"], t["f2"])


LC_OUT_CHUNKS = 8
LC_N2_PER_CHUNK = FFT_N2 // LC_OUT_CHUNKS


def _longconv_kernel(v_ref, gate_ref, kf_ref, g_ref, h_ref, f2_ref, f2i_ref, o_ref, s_ref):
    ct = v_ref.shape[2]
    t = pl.program_id(1)

    @pl.when(t == 0)
    def _():
        def stage1(n2, carry):
            r0 = pl.multiple_of(n2 * N1_PER_TILE, N1_PER_TILE)
            blk = [jnp.concatenate([v_ref[b, pl.ds(i * SEQ_TILE + r0, N1_PER_TILE), :]
                                    for i in range(SEQ // SEQ_TILE)], axis=0)
                   for b in range(BATCH)]
            x = jnp.concatenate([jnp.concatenate([blk[0], blk[2]], axis=1),
                                 jnp.concatenate([blk[1], blk[3]], axis=1)], axis=0)
            _scatter_n2(s_ref, n2, jnp.dot(g_ref[n2], x.astype(BF16), preferred_element_type=F32))
            return carry

        lax.fori_loop(0, FFT_N2, stage1, 0)

        def stage2(k1, carry):
            base, r = _spectrum_rows(s_ref, k1)
            z = jnp.dot(f2_ref[...], r.astype(BF16), preferred_element_type=F32)
            kf = kf_ref[pl.ds(pl.multiple_of(k1 * (2 * FFT_N2), 2 * FFT_N2), 2 * FFT_N2), :]
            kr = jnp.concatenate([kf[:FFT_N2]] * 2, axis=1)
            ki = jnp.concatenate([kf[FFT_N2:]] * 2, axis=1)
            zr, zi = z[:FFT_N2], z[FFT_N2:]
            p = jnp.concatenate([zr * kr - zi * ki, zr * ki + zi * kr], axis=0)
            q = jnp.dot(f2i_ref[...], p.astype(BF16), preferred_element_type=F32)
            for p in range(2):
                s_ref[p, pl.ds(base, FFT_N2), :] = q[:FFT_N2, p * ct:(p + 1) * ct]
                s_ref[p, pl.ds(base + FFT_PITCH, FFT_N2), :] = q[FFT_N2:, p * ct:(p + 1) * ct]
            return carry

        lax.fori_loop(0, FFT_N1, stage2, 0)

    def stage3(j, carry):
        n2 = t * LC_N2_PER_CHUNK + j
        rq = _gather_n2(s_ref, n2)
        y = jnp.dot(h_ref[n2], rq.astype(BF16), preferred_element_type=F32)
        r0 = pl.multiple_of(j * N1_PER_TILE, N1_PER_TILE)
        for b in range(BATCH):
            ri, pair = b % 2, b // 2
            for i in range(SEQ // SEQ_TILE):
                n1 = ri * FFT_HALF_N1 + i * N1_PER_TILE
                yb = y[n1:n1 + N1_PER_TILE, pair * ct:(pair + 1) * ct]
                o_ref[b, i, pl.ds(r0, N1_PER_TILE), :] = (
                    gate_ref[b, i, pl.ds(r0, N1_PER_TILE), :] * yb)
        return carry

    lax.fori_loop(0, LC_N2_PER_CHUNK, stage3, 0)


def _longconv(v, v_tile, gate, gate_tile, kf, kf_col):
    t = _dft_tables()
    ct = LANES
    n_seq_tiles = SEQ // SEQ_TILE
    rows = SEQ_TILE // LC_OUT_CHUNKS
    gate = gate.reshape(gate.shape[0], BATCH, n_seq_tiles, SEQ_TILE, ct)
    out = pl.pallas_call(
        _longconv_kernel,
        grid=(W_A // ct, LC_OUT_CHUNKS),
        in_specs=[
            _single_buffered((None, BATCH, SEQ, ct), lambda j, i: (v_tile + j, 0, 0, 0)),
            pl.BlockSpec((None, BATCH, n_seq_tiles, rows, ct),
                         lambda j, i: (gate_tile + j, 0, 0, i, 0)),
            _single_buffered((2 * FFT_N, ct), lambda j, i: (0, kf_col + j)),
            _single_buffered(t["g"].shape, lambda j, i: (0, 0, 0)),
            _single_buffered(t["h"].shape, lambda j, i: (0, 0, 0)),
            _single_buffered(t["f2"].shape, lambda j, i: (0, 0)),
            _single_buffered(t["f2i"].shape, lambda j, i: (0, 0)),
        ],
        out_specs=pl.BlockSpec((None, BATCH, n_seq_tiles, rows, ct),
                               lambda j, i: (j, 0, 0, i, 0)),
        out_shape=jax.ShapeDtypeStruct((W_A // ct, BATCH, n_seq_tiles, SEQ_TILE, ct), F32),
        scratch_shapes=[pltpu.VMEM((2, 2 * FFT_N1 * FFT_PITCH, ct), F32)],
        compiler_params=_cparams("arbitrary", "arbitrary"),
        name="hyena_longconv",
    )(v, gate, kf, t["g"], t["h"], t["f2"], t["f2i"])
    return out.reshape(W_A // ct, BATCH, SEQ, ct)


def _ctx_filter_spectrum_kernel(k_ref, skip_ref, fcf_ref, o_ref):
    z = jnp.dot(fcf_ref[...], k_ref[...].astype(BF16), preferred_element_type=F32)
    o_ref[:CTX_N, :] = z[:CTX_N] + skip_ref[...]
    o_ref[CTX_N:, :] = z[CTX_N:]


def _ctx_filter_spectrum(k_taps, skip):
    t = _dft_tables()
    c = k_taps.shape[1]
    ct = 2 * LANES
    return pl.pallas_call(
        _ctx_filter_spectrum_kernel,
        grid=(c // ct,),
        in_specs=[
            pl.BlockSpec((CTX_N, ct), lambda j: (0, j)),
            pl.BlockSpec((1, ct), lambda j: (0, j)),
            pl.BlockSpec(t["fcf"].shape, lambda j: (0, 0)),
        ],
        out_specs=pl.BlockSpec((2 * CTX_N, ct), lambda j: (0, j)),
        out_shape=jax.ShapeDtypeStruct((2 * CTX_N, c), F32),
        compiler_params=_cparams("arbitrary"),
        name="hyena_ctx_filter_spectrum",
    )(k_taps, skip, t["fcf"])


def _ctx_conv_kernel(v_ref, gate_ref, kf_ref, fc_ref, fci_ref, o_ref):
    ct = v_ref.shape[2]
    x = jnp.concatenate([jnp.concatenate([v_ref[0], v_ref[2]], axis=1),
                         jnp.concatenate([v_ref[1], v_ref[3]], axis=1)], axis=0)
    z = jnp.dot(fc_ref[...], x.astype(BF16), preferred_element_type=F32)
    kf = kf_ref[...]
    kr = jnp.concatenate([kf[:CTX_N]] * 2, axis=1)
    ki = jnp.concatenate([kf[CTX_N:]] * 2, axis=1)
    zr, zi = z[:CTX_N], z[CTX_N:]
    p = jnp.concatenate([zr * kr - zi * ki, zr * ki + zi * kr], axis=0)
    y = jnp.dot(fci_ref[...], p.astype(BF16), preferred_element_type=F32)
    for b in range(BATCH):
        ri, pair = b % 2, b // 2
        o_ref[b] = gate_ref[b] * y[ri * CTX_LEN:(ri + 1) * CTX_LEN, pair * ct:(pair + 1) * ct]


def _ctx_conv(v, v_tile, gate, gate_tile, kf, kf_col):
    t = _dft_tables()
    ct = LANES
    return pl.pallas_call(
        _ctx_conv_kernel,
        grid=(W_A // ct,),
        in_specs=[
            pl.BlockSpec((None, BATCH, CTX_LEN, ct), lambda j: (v_tile + j, 0, 0, 0)),
            pl.BlockSpec((None, BATCH, CTX_LEN, ct), lambda j: (gate_tile + j, 0, 0, 0)),
            pl.BlockSpec((2 * CTX_N, ct), lambda j: (0, kf_col + j)),
            pl.BlockSpec(t["fc"].shape, lambda j: (0, 0)),
            pl.BlockSpec(t["fci"].shape, lambda j: (0, 0)),
        ],
        out_specs=pl.BlockSpec((None, BATCH, CTX_LEN, ct), lambda j: (j, 0, 0, 0)),
        out_shape=jax.ShapeDtypeStruct((W_A // ct, BATCH, CTX_LEN, ct), F32),
        compiler_params=_cparams("arbitrary"),
        name="hyena_ctx_conv",
    )(v, gate, kf, t["fc"], t["fci"])


MOD_ROWS = 8
CTX_MOD_ROW = BATCH


def _ada_kernel(c_ref, w_ref, b_ref, o_ref):
    cv = c_ref[...]
    s = cv * jax.nn.sigmoid(cv)
    o_ref[0] = jnp.dot(s.astype(BF16), w_ref[0].astype(BF16),
                       preferred_element_type=F32) + b_ref[0]


def _ada_mods(c, c_ctx, w_ada, b_ada):
    cv = jnp.concatenate(
        [c, c_ctx[None], jnp.zeros((MOD_ROWS - BATCH - 1, D_MODEL), F32)], axis=0)
    n = N_MOD * D_MODEL
    tn = n // 4
    return pl.pallas_call(
        _ada_kernel,
        grid=(DEPTH, n // tn),
        in_specs=[
            pl.BlockSpec((MOD_ROWS, D_MODEL), lambda l, j: (0, 0)),
            pl.BlockSpec((1, D_MODEL, tn), lambda l, j: (l, 0, j)),
            pl.BlockSpec((1, 1, tn), lambda l, j: (l, 0, j)),
        ],
        out_specs=pl.BlockSpec((1, MOD_ROWS, tn), lambda l, j: (l, 0, j)),
        out_shape=jax.ShapeDtypeStruct((DEPTH, MOD_ROWS, n), F32),
        compiler_params=_cparams("arbitrary", "arbitrary"),
        name="ada_mods",
    )(cv, w_ada, b_ada.reshape(DEPTH, 1, n))


def _mod_spec(m, row_fn):
    return pl.BlockSpec((1, 1, D_MODEL), lambda i, *_: (row_fn(i), 0, m))


def _rms_mod(x, g, shift, scale):
    ms = jnp.mean(x * x, axis=-1, keepdims=True)
    return (x * lax.rsqrt(ms + NORM_EPS) * g) * (1.0 + scale) + shift


def _gelu(x):
    return jax.nn.gelu(x, approximate=True)


HALO = 8


def _shift_rows(z, zp, zn, d):
    tm = z.shape[0]
    row = lax.broadcasted_iota(jnp.int32, z.shape, 0)
    if d < 0:
        s = pltpu.roll(z, -d, 0)
        for r in range(-d):
            s = jnp.where(row == r, zp[HALO + d + r:HALO + d + r + 1], s)
    else:
        s = pltpu.roll(z, tm - d, 0)
        for r in range(d):
            s = jnp.where(row == tm - d + r, zn[r:r + 1], s)
    return s


def _dwconv(z, zp, zn, w, b, left, pos, seq_len):
    y = b + w[left:left + 1] * z
    for k in range(w.shape[0]):
        d = k - left
        if d == 0:
            continue
        valid = jnp.where((pos + d >= 0) & (pos + d < seq_len), 1.0, 0.0)
        y = y + (_shift_rows(z, zp, zn, d) * valid) * w[k:k + 1]
    return y


def _halo_specs(tm, n_rows):
    per = tm // HALO
    last = n_rows // HALO - 1
    return [
        pl.BlockSpec((tm, D_MODEL), lambda i: (i, 0)),
        pl.BlockSpec((HALO, D_MODEL), lambda i: (jnp.maximum(i * per - 1, 0), 0)),
        pl.BlockSpec((HALO, D_MODEL), lambda i: (jnp.minimum((i + 1) * per, last), 0)),
    ]


def _normed_tile(x_ref, xp_ref, xn_ref, sh_ref, sc_ref, g_ref):
    g, shift, scale = g_ref[...], sh_ref[0], sc_ref[0]
    h = _rms_mod(x_ref[...], g, shift, scale).astype(BF16)
    hh = _rms_mod(jnp.concatenate([xp_ref[...], xn_ref[...]], axis=0), g, shift, scale)
    return h, hh.astype(BF16)


def _row_pos(tm, seq_len):
    row = pl.program_id(0) * tm + lax.broadcasted_iota(jnp.int32, (tm, 1), 0)
    return row & (seq_len - 1)


EVEN_TM = SEQ_TILE
HY_COLS = 3 * W_A
HY_TILES = HY_COLS // LANES


def _in_even_kernel(x_ref, xp_ref, xn_ref, sh_ref, sc_ref, g_ref, w_ref, cw_ref, cb_ref,
                    lng_ref, sgw_ref, sgb_ref, za_ref, yb_ref, *, seq_len, n2_major):
    tm = x_ref.shape[0]
    h, hh = _normed_tile(x_ref, xp_ref, xn_ref, sh_ref, sc_ref, g_ref)
    pos = _row_pos(tm, seq_len)
    cw = 4 * LANES
    for cc in range(HY_COLS // cw):
        cols = slice(cc * cw, (cc + 1) * cw)
        z = jnp.dot(h, w_ref[:, cols], preferred_element_type=F32)
        zh = jnp.dot(hh, w_ref[:, cols], preferred_element_type=F32)
        y = _dwconv(z, zh[:HALO], zh[HALO:], cw_ref[:, cols], cb_ref[:, cols], 1, pos, seq_len)
        for c in range(cw // LANES):
            tile = cc * (cw // LANES) + c
            yc = y[:, c * LANES:(c + 1) * LANES]
            if n2_major:
                for j in range(N1_PER_TILE):
                    za_ref[tile, pl.ds(j, FFT_N2, stride=N1_PER_TILE), :] = (
                        yc[j * FFT_N2:(j + 1) * FFT_N2])
            else:
                za_ref[tile] = yc
    u = _gelu(jnp.dot(h, w_ref[:, HY_COLS:HY_COLS + W_B], preferred_element_type=F32))
    vb = _gelu(jnp.dot(h, w_ref[:, HY_COLS + W_B:], preferred_element_type=F32))
    vc = vb - jnp.mean(vb, axis=-1, keepdims=True)
    vn = vc * lax.rsqrt(jnp.mean(vc * vc, axis=-1, keepdims=True) + NORM_EPS) * lng_ref[...]
    vn = vn.astype(BF16)
    for ch in range(tm // CHUNK):
        rows = slice(ch * CHUNK, (ch + 1) * CHUNK)
        for q in range(SGU_GROUPS):
            cols = slice(q * SGU_DH, (q + 1) * SGU_DH)
            s = jnp.dot(sgw_ref[q], vn[rows, cols], preferred_element_type=F32) + sgb_ref[:, cols]
            yb_ref[rows, cols] = (u[rows, cols] * s).astype(BF16)


def _in_even(x, mods, mod_row, g, w, cw, cb, lng, sgw, sgb, seq_len, n2_major):
    n_rows = x.shape[0]
    tm = EVEN_TM
    za_shape = (HY_TILES, n_rows, LANES)
    za_spec = pl.BlockSpec((HY_TILES, tm, LANES), lambda i: (0, i, 0))
    const = lambda shape: _single_buffered(shape, lambda i: (0,) * len(shape))
    return pl.pallas_call(
        functools.partial(_in_even_kernel, seq_len=seq_len, n2_major=n2_major),
        grid=(n_rows // tm,),
        in_specs=_halo_specs(tm, n_rows) + [
            _mod_spec(0, mod_row), _mod_spec(1, mod_row),
            const((1, D_MODEL)), const(w.shape), const(cw.shape), const(cb.shape),
            const(lng.shape), const(sgw.shape), const(sgb.shape),
        ],
        out_specs=[za_spec, pl.BlockSpec((tm, W_B), lambda i: (i, 0))],
        out_shape=[jax.ShapeDtypeStruct(za_shape, F32),
                   jax.ShapeDtypeStruct((n_rows, W_B), BF16)],
        compiler_params=_cparams("arbitrary"),
        name="in_proj_even",
    )(x, x, x, mods, mods, g, w, cw, cb, lng, sgw, sgb)


def _out_even_kernel(x_ref, ya_ref, yb_ref, gt_ref, w_ref, o_ref, *, n2_major):
    acc = jnp.dot(yb_ref[...], w_ref[W_A:, :], preferred_element_type=F32)
    for c in range(W_A // LANES):
        if n2_major:
            yac = jnp.concatenate(
                [ya_ref[c, pl.ds(j, FFT_N2, stride=N1_PER_TILE), :]
                 for j in range(N1_PER_TILE)], axis=0)
        else:
            yac = ya_ref[c]
        acc = acc + jnp.dot(yac.astype(BF16), w_ref[c * LANES:(c + 1) * LANES, :],
                            preferred_element_type=F32)
    o_ref[...] = x_ref[...] + gt_ref[0] * acc


def _out_even(x, ya, yb, mods, mod_row, w, n2_major):
    n_rows = x.shape[0]
    tm = EVEN_TM
    nt = W_A // LANES
    ya = ya.reshape(nt, n_rows, LANES)
    ya_spec = pl.BlockSpec((nt, tm, LANES), lambda i: (0, i, 0))
    return pl.pallas_call(
        functools.partial(_out_even_kernel, n2_major=n2_major),
        grid=(n_rows // tm,),
        in_specs=[
            pl.BlockSpec((tm, D_MODEL), lambda i: (i, 0)),
            ya_spec,
            pl.BlockSpec((tm, W_B), lambda i: (i, 0)),
            _mod_spec(2, mod_row),
            _single_buffered(w.shape, lambda i: (0, 0)),
        ],
        out_specs=pl.BlockSpec((tm, D_MODEL), lambda i: (i, 0)),
        out_shape=jax.ShapeDtypeStruct((n_rows, D_MODEL), F32),
        compiler_params=_cparams("arbitrary"),
        name="out_proj_even",
    )(x, ya, yb, mods, w)


def _out_odd_kernel(x_ref, y_ref, gt_ref, w_ref, o_ref):
    acc = jnp.dot(y_ref[...], w_ref[...], preferred_element_type=F32)
    o_ref[...] = x_ref[...] + gt_ref[0] * acc


def _out_odd(x, y, mods, mod_row, w):
    n_rows = x.shape[0]
    tm = 512
    return pl.pallas_call(
        _out_odd_kernel,
        grid=(n_rows // tm,),
        in_specs=[
            pl.BlockSpec((tm, D_MODEL), lambda i: (i, 0)),
            pl.BlockSpec((tm, D_RNN), lambda i: (i, 0)),
            _mod_spec(2, mod_row),
            _single_buffered(w.shape, lambda i: (0, 0)),
        ],
        out_specs=pl.BlockSpec((tm, D_MODEL), lambda i: (i, 0)),
        out_shape=jax.ShapeDtypeStruct((n_rows, D_MODEL), F32),
        compiler_params=_cparams("arbitrary"),
        name="out_proj_odd",
    )(x, y, mods, w)


FFN_TM = 512
FFN_CHUNKS = 2
FFN_CW = D_FF // FFN_CHUNKS


def _ffn_kernel(x_ref, sh_ref, sc_ref, gt_ref, g_ref, w1_ref, w2_ref, wo_ref, fg_ref, o_ref,
                h_ref, acc_ref, *, final_norm):
    j = pl.program_id(1)

    @pl.when(j == 0)
    def _():
        h_ref[...] = _rms_mod(x_ref[...], g_ref[...], sh_ref[0], sc_ref[0]).astype(BF16)

    h = h_ref[...]
    z1 = jnp.dot(h, w1_ref[...], preferred_element_type=F32)
    z2 = jnp.dot(h, w2_ref[...], preferred_element_type=F32)
    a = (z1 * jax.nn.sigmoid(z1) * z2).astype(BF16)
    part = jnp.dot(a, wo_ref[...], preferred_element_type=F32)

    @pl.when(j == 0)
    def _():
        acc_ref[...] = part

    @pl.when(j > 0)
    def _():
        acc_ref[...] += part

    @pl.when(j == FFN_CHUNKS - 1)
    def _():
        y = x_ref[...] + gt_ref[0] * acc_ref[...]
        if final_norm:
            y = y * lax.rsqrt(jnp.mean(y * y, axis=-1, keepdims=True) + NORM_EPS) * fg_ref[...]
        o_ref[...] = y


def _ffn(x, mods, mod_row, g, w_in, w_out, final_g, final_norm):
    n_rows = x.shape[0]
    tm = FFN_TM
    return pl.pallas_call(
        functools.partial(_ffn_kernel, final_norm=final_norm),
        grid=(n_rows // tm, FFN_CHUNKS),
        in_specs=[
            pl.BlockSpec((tm, D_MODEL), lambda i, j: (i, 0)),
            _mod_spec(3, mod_row), _mod_spec(4, mod_row), _mod_spec(5, mod_row),
            pl.BlockSpec((1, D_MODEL), lambda i, j: (0, 0)),
            pl.BlockSpec((D_MODEL, FFN_CW), lambda i, j: (0, j)),
            pl.BlockSpec((D_MODEL, FFN_CW), lambda i, j: (0, FFN_CHUNKS + j)),
            pl.BlockSpec((FFN_CW, D_MODEL), lambda i, j: (j, 0)),
            pl.BlockSpec((1, D_MODEL), lambda i, j: (0, 0)),
        ],
        out_specs=pl.BlockSpec((tm, D_MODEL), lambda i, j: (i, 0)),
        out_shape=jax.ShapeDtypeStruct((n_rows, D_MODEL), F32),
        scratch_shapes=[pltpu.VMEM((tm, D_MODEL), BF16), pltpu.VMEM((tm, D_MODEL), F32)],
        compiler_params=_cparams("arbitrary", "arbitrary"),
        name="ffn",
    )(x, mods, mods, mods, g, w_in, w_in, w_out, final_g)


ODD_TM = 512
RNN_TILES = D_RNN // LANES


def _in_odd_kernel(x_ref, xp_ref, xn_ref, sh_ref, sc_ref, g_ref, w_ref, cw_ref, cb_ref,
                   gate_ref, xl_ref, *, seq_len):
    tm = x_ref.shape[0]
    h, hh = _normed_tile(x_ref, xp_ref, xn_ref, sh_ref, sc_ref, g_ref)
    pos = _row_pos(tm, seq_len)
    gate_ref[...] = _gelu(jnp.dot(h, w_ref[:, :D_RNN], preferred_element_type=F32)).astype(BF16)
    cw = 4 * LANES
    for c0 in range(0, D_RNN, cw):
        cols = slice(c0, min(c0 + cw, D_RNN))
        wcols = slice(D_RNN + cols.start, D_RNN + cols.stop)
        z = jnp.dot(h, w_ref[:, wcols], preferred_element_type=F32)
        zh = jnp.dot(hh, w_ref[:, wcols], preferred_element_type=F32)
        xl_ref[:, cols] = _dwconv(z, zh[:HALO], zh[HALO:], cw_ref[:, cols], cb_ref[:, cols],
                                  2, pos, seq_len)


def _in_odd(x, mods, mod_row, g, w, cw, cb, seq_len):
    n_rows = x.shape[0]
    tm = ODD_TM
    const = lambda shape: _single_buffered(shape, lambda i: (0,) * len(shape))
    return pl.pallas_call(
        functools.partial(_in_odd_kernel, seq_len=seq_len),
        grid=(n_rows // tm,),
        in_specs=_halo_specs(tm, n_rows) + [
            _mod_spec(0, mod_row), _mod_spec(1, mod_row),
            const((1, D_MODEL)), const(w.shape), const(cw.shape), const(cb.shape),
        ],
        out_specs=[pl.BlockSpec((tm, D_RNN), lambda i: (i, 0)),
                   pl.BlockSpec((tm, D_RNN), lambda i: (i, 0))],
        out_shape=[jax.ShapeDtypeStruct((n_rows, D_RNN), BF16),
                   jax.ShapeDtypeStruct((n_rows, D_RNN), F32)],
        compiler_params=_cparams("arbitrary"),
        name="in_proj_odd",
    )(x, x, x, mods, mods, g, w, cw, cb)


RG_T = 512
RG_NCH = SEQ // RG_T
RG_GROUPS = 8
RG_WIN = 3 * LANES


def _rg_window_start(j):
    return min(max(LANES * (j - 1), 0), D_RNN - RG_WIN)


def _rg_chunk(x_ref, w_ref, ba_ref, bx_ref, lam_ref, a_pl, b_pl, carry_ref, reverse, emit):
    t_rows = x_ref.shape[0]
    kg = t_rows // RG_GROUPS
    pitch = kg + 8
    x32 = x_ref[...]
    xb = x32.astype(BF16)
    for j in range(RNN_TILES):
        tile = slice(j * LANES, (j + 1) * LANES)
        ws = _rg_window_start(j)
        pre = jnp.dot(xb[:, ws:ws + RG_WIN], w_ref[j], preferred_element_type=F32)
        lam = lam_ref[:, tile]
        softplus_neg = jnp.maximum(-lam, 0.0) + jnp.log1p(jnp.exp(-jnp.abs(lam)))
        r = jax.nn.sigmoid(pre[:, :LANES] + ba_ref[:, tile])
        gi = jax.nn.sigmoid(pre[:, LANES:] + bx_ref[:, tile])
        log_a = (-RG_C * r) * softplus_neg
        th = jnp.tanh(log_a)
        bxv = jnp.sqrt(-2.0 * th / (1.0 - th)) * (gi * x32[:, tile])
        av = jnp.exp(log_a)
        for gq in range(RG_GROUPS):
            a_pl[j, gq * pitch:gq * pitch + kg, :] = av[gq * kg:(gq + 1) * kg]
            b_pl[j, gq * pitch:gq * pitch + kg, :] = bxv[gq * kg:(gq + 1) * kg]

    def rows_k(pl_ref, j, k):
        return pl_ref[j, pl.ds(k, RG_GROUPS, stride=pitch), :]

    order = list(range(kg))[::-1] if reverse else list(range(kg))
    groups = list(range(RG_GROUPS))[::-1] if reverse else list(range(RG_GROUPS))
    big_a = [None] * RNN_TILES
    big_b = [None] * RNN_TILES
    for n, k in enumerate(order):
        for j in range(RNN_TILES):
            ak, bk = rows_k(a_pl, j, k), rows_k(b_pl, j, k)
            if n == 0:
                big_a[j], big_b[j] = ak, bk
            else:
                big_b[j] = ak * big_b[j] + bk
                big_a[j] = ak * big_a[j]
    h = [None] * RNN_TILES
    for j in range(RNN_TILES):
        c = carry_ref[j, 0:1, :]
        rows = [None] * RG_GROUPS
        for gq in groups:
            rows[gq] = c
            c = big_a[j][gq:gq + 1] * c + big_b[j][gq:gq + 1]
        carry_ref[j, 0:1, :] = c
        h[j] = jnp.concatenate(rows, axis=0)
    for k in order:
        for j in range(RNN_TILES):
            h[j] = rows_k(a_pl, j, k) * h[j] + rows_k(b_pl, j, k)
            b_pl[j, pl.ds(k, RG_GROUPS, stride=pitch), :] = h[j]
    for j in range(RNN_TILES):
        emit(j, jnp.concatenate(
            [b_pl[j, gq * pitch:gq * pitch + kg, :] for gq in range(RG_GROUPS)], axis=0))


def _rg_kernel(*refs, reverse):
    if reverse:
        (xc_ref, xl_ref, w_ref, ba_ref, bx_ref, lam_ref, oc_ref, ol_ref,
         a_pl, b_pl, carry_ref) = refs
    else:
        (xc_ref, xl_ref, w_ref, ba_ref, bx_ref, lam_ref, gc_ref, gl_ref, hc_ref, hl_ref,
         oc_ref, ol_ref, a_pl, b_pl, carry_ref) = refs
    s = pl.program_id(1)

    def emitter(o_ref, g_ref, hb_ref):
        def emit(j, hcur):
            tile = slice(j * LANES, (j + 1) * LANES)
            if reverse:
                o_ref[:, tile] = hcur.astype(BF16)
            else:
                o_ref[:, tile] = (g_ref[:, tile].astype(F32)
                                  * (hcur + hb_ref[:, tile].astype(F32))).astype(BF16)
        return emit

    @pl.when(s == 0)
    def _():
        carry_ref[...] = jnp.zeros_like(carry_ref)
        _rg_chunk(xc_ref, w_ref, ba_ref, bx_ref, lam_ref, a_pl, b_pl, carry_ref, reverse,
                  emitter(oc_ref, None if reverse else gc_ref, None if reverse else hc_ref))

    @pl.when(s > 0)
    def _():
        _rg_chunk(xl_ref, w_ref, ba_ref, bx_ref, lam_ref, a_pl, b_pl, carry_ref, reverse,
                  emitter(ol_ref, None if reverse else gl_ref, None if reverse else hl_ref))


def _rg_scan(xc, xl, w, ba, bx, lam, reverse, gate_c=None, gate_l=None, hb_c=None, hb_l=None):
    if reverse:
        lat = lambda b, s: (b, RG_NCH - jnp.maximum(s, 1), 0)
    else:
        lat = lambda b, s: (b, jnp.maximum(s - 1, 0), 0)
    ctx_spec = pl.BlockSpec((None, CTX_LEN, D_RNN), lambda b, s: (b, 0, 0))
    lat_spec = pl.BlockSpec((None, RG_T, D_RNN), lat)
    const = lambda shape: _single_buffered(shape, lambda b, s: (0,) * len(shape))
    in_specs = [ctx_spec, lat_spec, const(w.shape), const(ba.shape), const(bx.shape),
                const(lam.shape)]
    args = [xc, xl, w, ba, bx, lam]
    if not reverse:
        in_specs += [ctx_spec, lat_spec, ctx_spec, lat_spec]
        args += [gate_c, gate_l, hb_c, hb_l]
    pitch = RG_T // RG_GROUPS + 8
    return pl.pallas_call(
        functools.partial(_rg_kernel, reverse=reverse),
        grid=(BATCH, 1 + RG_NCH),
        in_specs=in_specs,
        out_specs=[ctx_spec, lat_spec],
        out_shape=[jax.ShapeDtypeStruct((BATCH, CTX_LEN, D_RNN), BF16),
                   jax.ShapeDtypeStruct((BATCH, SEQ, D_RNN), BF16)],
        scratch_shapes=[pltpu.VMEM((RNN_TILES, RG_GROUPS * pitch, LANES), F32),
                        pltpu.VMEM((RNN_TILES, RG_GROUPS * pitch, LANES), F32),
                        pltpu.VMEM((RNN_TILES, 8, LANES), F32)],
        compiler_params=_cparams("arbitrary", "arbitrary"),
        name="rglru_bwd" if reverse else "rglru_fwd",
    )(*args)


def _rg_gate_weights(wa, wx):
    eye = jnp.eye(RG_HEADS, dtype=F32)

    def dense(w):
        return (w[:, :, None, :] * eye[:, None, :, None]).reshape(D_RNN, D_RNN)

    da, dx = dense(wa), dense(wx)
    wins = []
    for j in range(RNN_TILES):
        ws = _rg_window_start(j)
        tile = slice(j * LANES, (j + 1) * LANES)
        wins.append(jnp.concatenate([da[ws:ws + RG_WIN, tile], dx[ws:ws + RG_WIN, tile]], axis=1))
    return jnp.stack(wins).astype(BF16)


GRID_H = SEQ // GRID_W
GT_ROWS = 8


GT_PITCH = GRID_W + 8


def _grid_transpose_kernel(x_ref, o_ref, s_ref):
    nt = D_MODEL // LANES
    for t in range(nt):
        for j in range(GT_ROWS):
            s_ref[t, j * GT_PITCH:j * GT_PITCH + GRID_W, :] = (
                x_ref[j * GRID_W:(j + 1) * GRID_W, t * LANES:(t + 1) * LANES])
    for c in range(GRID_W):
        o_ref[c] = jnp.concatenate(
            [s_ref[t, pl.ds(c, GT_ROWS, stride=GT_PITCH), :] for t in range(nt)], axis=1)


def _grid_transpose(x):
    nblk = GRID_H // GT_ROWS
    out = pl.pallas_call(
        _grid_transpose_kernel,
        grid=(BATCH, nblk),
        in_specs=[pl.BlockSpec((GT_ROWS * GRID_W, D_MODEL), lambda b, i: (b * nblk + i, 0))],
        out_specs=pl.BlockSpec((None, GRID_W, GT_ROWS, D_MODEL), lambda b, i: (b, 0, i, 0)),
        out_shape=jax.ShapeDtypeStruct((BATCH, GRID_W, GRID_H, D_MODEL), F32),
        scratch_shapes=[pltpu.VMEM((D_MODEL // LANES, GT_ROWS * GT_PITCH, LANES), F32)],
        compiler_params=_cparams("arbitrary", "arbitrary"),
        name="grid_transpose",
    )(x)
    return out.reshape(BATCH * SEQ, D_MODEL)


def _hyena_filter_taps(seq, f1_w, f1_b, f2_w, f2_b, f3_w, f3_b, sin_freq):
    t = jnp.linspace(0.0, 1.0, seq, dtype=F32)[:, None]
    w = 2.0 * math.pi * jnp.arange(seq, dtype=F32)[:, None] / seq
    f = jnp.linspace(1e-4, HY_BANDS - 1, HY_BANDS, dtype=F32)[None, :]
    emb = jnp.concatenate([t, jnp.cos(f * w), -jnp.sin(f * w)], axis=-1)
    hdn = jnp.sin(sin_freq * (emb @ f1_w + f1_b))
    hdn = jnp.sin(sin_freq * (hdn @ f2_w + f2_b))
    h = (hdn @ f3_w + f3_b).astype(F32).reshape(seq, HY_ORDER, 2, W_A)
    deltas = jnp.abs(jnp.linspace(math.log(HY_TARGET) / HY_SLOW_PCT,
                                  math.log(HY_TARGET) / HY_FAST_PCT, W_A, dtype=F32))
    h = h * jnp.exp(-t[:, :, None, None] * deltas)
    k = jnp.concatenate([h[:, :, 0], jnp.zeros((1, HY_ORDER, W_A), F32), h[:0:-1, :, 1]], axis=0)
    k = k * lax.rsqrt(jnp.sum(k * k, axis=0, keepdims=True) + NORM_EPS)
    return k.reshape(2 * seq, HY_ORDER * W_A)


def kernel(x, c, ctx, c_ctx, w_ada, b_ada, norm_mix_g, norm_ffn_g, w_in_even, w_out_even, hy_conv_w, hy_conv_b, hy_f1_w, hy_f1_b, hy_f2_w, hy_f2_b, hy_f3_w, hy_f3_b, hy_sin_freq, hy_skip, sgu_ln_g, sgu_w, sgu_b, w_in_odd, rg_conv_w, rg_conv_b, rg_wa, rg_ba, rg_wx, rg_bx, rg_lam, w_out_odd, w_ffn_in, w_ffn_out, final_norm_g):
    mods_all = _ada_mods(c, c_ctx, w_ada, b_ada)
    xs = x.reshape(BATCH * SEQ, D_MODEL)
    cs = ctx.reshape(BATCH * CTX_LEN, D_MODEL)
    lat_row_even = lambda i: i // (SEQ // EVEN_TM)
    lat_row_odd = lambda i: i // (SEQ // ODD_TM)
    lat_row_ffn = lambda i: i // (SEQ // FFN_TM)
    ctx_row = lambda i: CTX_MOD_ROW
    n_tiles = W_A // LANES
    for l in range(DEPTH):
        run_ctx = l < DEPTH - 1
        is_rec = l % 2 == 1
        i = l // 2
        mods = mods_all[l].reshape(MOD_ROWS, 1, N_MOD * D_MODEL)
        g_mix = norm_mix_g[l].reshape(1, D_MODEL)
        if is_rec:
            col_major = i % 2 == 1
            if col_major:
                xs = _grid_transpose(xs)
            w_in = w_in_odd[i].astype(BF16)
            cw, cb = rg_conv_w[i], rg_conv_b[i].reshape(1, D_RNN)
            gate_l, xl = _in_odd(xs, mods, lat_row_odd, g_mix, w_in, cw, cb, SEQ)
            gate_c, xc = _in_odd(cs, mods, ctx_row, g_mix, w_in, cw, cb, CTX_LEN)
            xl = xl.reshape(BATCH, SEQ, D_RNN)
            xc = xc.reshape(BATCH, CTX_LEN, D_RNN)
            gate_l = gate_l.reshape(BATCH, SEQ, D_RNN)
            gate_c = gate_c.reshape(BATCH, CTX_LEN, D_RNN)
            row = lambda v: v.reshape(1, D_RNN)
            hb_c, hb_l = _rg_scan(xc, xl, _rg_gate_weights(rg_wa[i, 1], rg_wx[i, 1]),
                                  row(rg_ba[i, 1]), row(rg_bx[i, 1]), row(rg_lam[i, 1]), True)
            y_c, y_l = _rg_scan(xc, xl, _rg_gate_weights(rg_wa[i, 0], rg_wx[i, 0]),
                                row(rg_ba[i, 0]), row(rg_bx[i, 0]), row(rg_lam[i, 0]), False,
                                gate_c, gate_l, hb_c, hb_l)
            w_out = w_out_odd[i].astype(BF16)
            xs = _out_odd(xs, y_l.reshape(BATCH * SEQ, D_RNN), mods, lat_row_odd, w_out)
            if run_ctx:
                cs = _out_odd(cs, y_c.reshape(BATCH * CTX_LEN, D_RNN), mods, ctx_row, w_out)
        else:
            w_in = w_in_even[i].astype(BF16)
            w_out = w_out_even[i].astype(BF16)
            cw, cb = hy_conv_w[i], hy_conv_b[i].reshape(1, HY_COLS)
            lng = sgu_ln_g[i].reshape(1, W_B)
            sgw = sgu_w[i].astype(BF16)
            sgb = jnp.repeat(sgu_b[i].T, SGU_DH, axis=1)
            skip = hy_skip[i].reshape(1, HY_ORDER * W_A)
            fargs = (hy_f1_w[i], hy_f1_b[i], hy_f2_w[i], hy_f2_b[i], hy_f3_w[i], hy_f3_b[i],
                     hy_sin_freq[i])
            taps = _hyena_filter_taps(SEQ, *fargs)
            taps = taps.reshape(FFT_N1, FFT_N2, -1).transpose(1, 0, 2).reshape(FFT_N, -1)
            kf = _filter_spectrum(taps, skip)
            za, yb = _in_even(xs, mods, lat_row_even, g_mix, w_in, cw, cb, lng, sgw, sgb, SEQ, True)
            za = za.reshape(HY_TILES, BATCH, SEQ, LANES)
            y1 = _longconv(za, 0, za, n_tiles, kf, 0)
            ya = _longconv(y1, 0, za, 2 * n_tiles, kf, n_tiles)
            xs = _out_even(xs, ya, yb, mods, lat_row_even, w_out, True)
            if run_ctx:
                kfc = _ctx_filter_spectrum(_hyena_filter_taps(CTX_LEN, *fargs), skip)
                zc, ybc = _in_even(cs, mods, ctx_row, g_mix, w_in, cw, cb, lng, sgw, sgb,
                                   CTX_LEN, False)
                zc = zc.reshape(HY_TILES, BATCH, CTX_LEN, LANES)
                y1c = _ctx_conv(zc, 0, zc, n_tiles, kfc, 0)
                yac = _ctx_conv(y1c, 0, zc, 2 * n_tiles, kfc, n_tiles)
                cs = _out_even(cs, yac, ybc, mods, ctx_row, w_out, False)
        g_ffn = norm_ffn_g[l].reshape(1, D_MODEL)
        w1 = w_ffn_in[l].astype(BF16)
        w2 = w_ffn_out[l].astype(BF16)
        last = l == DEPTH - 1
        xs = _ffn(xs, mods, lat_row_ffn, g_ffn, w1, w2, final_norm_g.reshape(1, D_MODEL), last)
        if run_ctx:
            cs = _ffn(cs, mods, ctx_row, g_ffn, w1, w2, final_norm_g.reshape(1, D_MODEL), False)
        if is_rec and col_major:
            xs = _grid_transpose(xs)
    return xs.reshape(BATCH, SEQ, D_MODEL)
```

```python
import functools
import math

import numpy as np
import jax
import jax.numpy as jnp
from jax import lax
from jax.experimental import pallas as pl
from jax.experimental.pallas import tpu as pltpu

F32 = jnp.float32
BF16 = jnp.bfloat16

D_MODEL = 1024
BATCH = 4
SEQ = 4096
DEPTH = 4
GRID_W = 64
CTX_LEN = 256
N_MOD = 6
NORM_EPS = 1e-6
W_A = D_MODEL // 2
HY_ORDER = 2
HY_SHORT = 3
HY_BANDS = 16
HY_TARGET = 1e-2
HY_FAST_PCT = 0.3
HY_SLOW_PCT = 1.5
W_B = D_MODEL // 2
SGU_GROUPS = 4
SGU_DH = W_B // SGU_GROUPS
CHUNK = 128
D_RNN = ((4 * D_MODEL // 3 + 127) // 128) * 128
RG_HEADS = 16
RG_DH = D_RNN // RG_HEADS
RG_CONV = 4
RG_C = 8.0
D_FF = ((8 * D_MODEL // 3 + 255) // 256) * 256

VMEM_BYTES_V7X = 64 * 1024 * 1024
VMEM_LIMIT = VMEM_BYTES_V7X - 8 * 1024 * 1024
LANES = 128

FFT_N = 2 * SEQ
FFT_N1 = 64
FFT_N2 = 128
FFT_HALF_N1 = FFT_N1 // 2
FFT_PITCH = FFT_N2 + 8
CTX_N = 2 * CTX_LEN


def _cparams(*sem):
    return pltpu.CompilerParams(dimension_semantics=sem, vmem_limit_bytes=VMEM_LIMIT)


def _single_buffered(block_shape, index_map):
    return pl.BlockSpec(block_shape, index_map, pipeline_mode=pl.Buffered(1))


@functools.lru_cache(maxsize=None)
def _dft_tables_f32():
    n = FFT_N
    k1 = np.arange(FFT_N1)
    n2 = np.arange(FFT_N2)

    def angle(n1):
        m = (FFT_N2 * n1[None, None, :] + n2[:, None, None]) * k1[None, :, None]
        return 2.0 * np.pi * (m % n) / n

    a = angle(np.arange(FFT_HALF_N1))
    c, s = np.cos(a), np.sin(a)
    g = np.empty((FFT_N2, FFT_N1, 2, 2, FFT_HALF_N1))
    g[:, :, 0, 0], g[:, :, 0, 1] = c, s
    g[:, :, 1, 0], g[:, :, 1, 1] = -s, c
    g = g.reshape(FFT_N2, 2 * FFT_N1, 2 * FFT_HALF_N1)

    a = angle(np.arange(FFT_N1))
    gf = np.stack([np.cos(a), -np.sin(a)], axis=2).reshape(FFT_N2, 2 * FFT_N1, FFT_N1)

    a = np.transpose(angle(np.arange(FFT_HALF_N1)), (0, 2, 1))
    c, s = np.cos(a) / n, np.sin(a) / n
    h = np.empty((FFT_N2, 2, FFT_HALF_N1, FFT_N1, 2))
    h[:, 0, :, :, 0], h[:, 0, :, :, 1] = c, -s
    h[:, 1, :, :, 0], h[:, 1, :, :, 1] = s, c
    h = h.reshape(FFT_N2, 2 * FFT_HALF_N1, 2 * FFT_N1)

    a = 2.0 * np.pi * np.outer(n2, n2) / FFT_N2
    c, s = np.cos(a), np.sin(a)
    f2 = np.block([[c, s], [-s, c]])
    f2i = np.block([[c, -s], [s, c]])

    kk = np.arange(CTX_N)
    a = 2.0 * np.pi * np.outer(kk, np.arange(CTX_LEN)) / CTX_N
    c, s = np.cos(a), np.sin(a)
    fc = np.block([[c, s], [-s, c]])
    a = 2.0 * np.pi * np.outer(kk, np.arange(CTX_N)) / CTX_N
    fcf = np.concatenate([np.cos(a), -np.sin(a)], axis=0)
    a = 2.0 * np.pi * np.outer(np.arange(CTX_LEN), kk) / CTX_N
    c, s = np.cos(a) / CTX_N, np.sin(a) / CTX_N
    fci = np.block([[c, -s], [s, c]])

    tables = dict(g=g, gf=gf, h=h, f2=f2, f2i=f2i, fc=fc, fcf=fcf, fci=fci)
    return {k: np.asarray(v, dtype=np.float32) for k, v in tables.items()}


def _dft_tables():
    return {k: jnp.asarray(v).astype(BF16) for k, v in _dft_tables_f32().items()}


SEQ_TILE = 1024
N1_PER_TILE = SEQ_TILE // FFT_N2


def _scatter_n2(s_ref, n2, val):
    for p in range(s_ref.shape[0]):
        s_ref[p, pl.ds(n2, 2 * FFT_N1, stride=FFT_PITCH), :] = val[:, p * LANES:(p + 1) * LANES]


def _gather_n2(s_ref, n2):
    return jnp.concatenate(
        [s_ref[p, pl.ds(n2, 2 * FFT_N1, stride=FFT_PITCH), :] for p in range(s_ref.shape[0])],
        axis=1)


def _spectrum_rows(s_ref, k1):
    base = pl.multiple_of(k1 * (2 * FFT_PITCH), 8)
    return base, jnp.concatenate(
        [jnp.concatenate([s_ref[p, pl.ds(base + o, FFT_N2), :] for p in range(s_ref.shape[0])],
                         axis=1) for o in (0, FFT_PITCH)], axis=0)


HY_FFN = 64
FILT_ROWS = 1024
HIGHEST = lax.Precision.HIGHEST


def _tap_rows(row, seq, n2_major):
    if n2_major:
        n2, n1 = row >> int(math.log2(FFT_N1)), row & (FFT_N1 - 1)
        bwd = n1 >= FFT_HALF_N1
        m = FFT_N2 * (n1 - FFT_HALF_N1) + n2
        fwd_pos = FFT_N2 * n1 + n2
    else:
        bwd = row >= seq
        m = row - seq
        fwd_pos = row
    pos = jnp.where(bwd, seq - m, fwd_pos).astype(F32)
    bwd_f = jnp.where(bwd, 1.0, 0.0)
    keep_f = jnp.where(bwd & (m == 0), 0.0, 1.0)
    return pos, bwd_f, keep_f


def _filter_kernel(ec_ref, f1w_ref, f1b_ref, f2w_ref, f2b_ref, sf_ref, w3f_ref, w3b_ref, b3f_ref,
                   b3b_ref, dl_ref, skip_ref, *rest, seq, n2_major):
    if n2_major:
        gf_ref, f2_ref, o_ref, hid_ref, k_ref, s_ref = rest
    else:
        fcf_ref, o_ref, hid_ref, k_ref = rest
    n_rows = 2 * seq
    rb = min(FILT_ROWS, n_rows)

    def rows_of(i):
        r0 = pl.multiple_of(i * rb, rb)
        return r0, _tap_rows(r0 + lax.broadcasted_iota(jnp.int32, (rb, 1), 0), seq, n2_major)

    @pl.when(pl.program_id(0) == 0)
    def _():
        sf = sf_ref[...]

        def hidden(i, carry):
            r0, (pos, _, _) = rows_of(i)
            t = pos * (1.0 / (seq - 1))
            w = pos * (2.0 * math.pi / seq)
            emb = jnp.sin(w * ec_ref[0:1] + ec_ref[1:2]) + t * ec_ref[2:3]
            h = jnp.sin(sf * (jnp.dot(emb, f1w_ref[...], precision=HIGHEST,
                                      preferred_element_type=F32) + f1b_ref[...]))
            h = jnp.sin(sf * (jnp.dot(h.astype(BF16), f2w_ref[...].astype(BF16),
                                      preferred_element_type=F32) + f2b_ref[...]))
            hid_ref[pl.ds(r0, rb), :] = h
            return carry

        lax.fori_loop(0, n_rows // rb, hidden, 0)

    def taps(i, ssq):
        r0, (pos, bwd_f, keep_f) = rows_of(i)
        h = hid_ref[pl.ds(r0, rb), :].astype(BF16)
        kf = jnp.dot(h, w3f_ref[0].astype(BF16), preferred_element_type=F32) + b3f_ref[0]
        kb = jnp.dot(h, w3b_ref[0].astype(BF16), preferred_element_type=F32) + b3b_ref[0]
        window = jnp.exp(-(pos * (1.0 / (seq - 1))) * dl_ref[0])
        k = (kf + bwd_f * (kb - kf)) * window * keep_f
        k_ref[pl.ds(r0, rb), :] = k
        return ssq + jnp.sum(k * k, axis=0, keepdims=True)

    ssq = lax.fori_loop(0, n_rows // rb, taps, jnp.zeros((1, LANES), F32))
    scale = lax.rsqrt(ssq + NORM_EPS)
    skip = skip_ref[...]

    if n2_major:
        def stage1(n2, carry):
            x = k_ref[pl.ds(pl.multiple_of(n2 * FFT_N1, FFT_N1), FFT_N1), :] * scale
            _scatter_n2(s_ref, n2, jnp.dot(gf_ref[n2], x.astype(BF16),
                                           preferred_element_type=F32))
            return carry

        lax.fori_loop(0, FFT_N2, stage1, 0, unroll=4)

        def stage2(k1, carry):
            _, r = _spectrum_rows(s_ref, k1)
            z = jnp.dot(f2_ref[...], r.astype(BF16), preferred_element_type=F32)
            row = pl.multiple_of(k1 * (2 * FFT_N2), 2 * FFT_N2)
            o_ref[pl.ds(row, FFT_N2), :] = z[:FFT_N2] + skip
            o_ref[pl.ds(row + FFT_N2, FFT_N2), :] = z[FFT_N2:]
            return carry

        lax.fori_loop(0, FFT_N1, stage2, 0, unroll=4)
    else:
        z = jnp.dot(fcf_ref[...], (k_ref[...] * scale).astype(BF16), preferred_element_type=F32)
        o_ref[:n_rows, :] = z[:n_rows] + skip
        o_ref[n_rows:, :] = z[n_rows:]


def _filter_spectrum(seq, f1_w, f1_b, f2_w, f2_b, f3_w, f3_b, sin_freq, skip):
    n2_major = seq == SEQ
    t = _dft_tables()
    pad = lambda a, r, c: jnp.pad(a, ((0, r - a.shape[0]), (0, c - a.shape[1])))
    f = jnp.linspace(1e-4, HY_BANDS - 1, HY_BANDS, dtype=F32)
    zeros = lambda n: jnp.zeros((n,), F32)
    ones = lambda n: jnp.ones((n,), F32)
    rest = LANES - 1 - 2 * HY_BANDS
    ec = jnp.stack([
        jnp.concatenate([zeros(1), f, f, zeros(rest)]),
        jnp.concatenate([zeros(1), (math.pi / 2) * ones(HY_BANDS), math.pi * ones(HY_BANDS),
                         zeros(rest)]),
        jnp.concatenate([ones(1), zeros(LANES - 1)]),
    ] + [zeros(LANES)] * 5)
    row = lambda v: pad(v.reshape(1, -1), 1, LANES)
    n_tiles = HY_ORDER * W_A // LANES
    w3 = f3_w.reshape(HY_FFN, HY_ORDER, 2, W_A)
    b3 = f3_b.reshape(HY_ORDER, 2, W_A)
    tiles_w = lambda d: jnp.pad(
        w3[:, :, d, :].reshape(HY_FFN, n_tiles, LANES).transpose(1, 0, 2),
        ((0, 0), (0, LANES - HY_FFN), (0, 0)))
    tiles_b = lambda d: b3[:, d, :].reshape(n_tiles, 1, LANES)
    deltas = jnp.abs(jnp.linspace(math.log(HY_TARGET) / HY_SLOW_PCT,
                                  math.log(HY_TARGET) / HY_FAST_PCT, W_A, dtype=F32))
    dl = jnp.tile(deltas, HY_ORDER).reshape(n_tiles, 1, LANES)
    const = lambda shape: _single_buffered(shape, lambda j: (0,) * len(shape))
    tile = lambda shape: pl.BlockSpec((1,) + shape, lambda j: (j, 0, 0))
    args = [ec, pad(f1_w, LANES, LANES), row(f1_b), pad(f2_w, LANES, LANES), row(f2_b),
            row(sin_freq), tiles_w(0), tiles_w(1), tiles_b(0), tiles_b(1), dl, skip]
    in_specs = [const((8, LANES)), const((LANES, LANES)), const((1, LANES)),
                const((LANES, LANES)), const((1, LANES)), const((1, LANES)),
                tile((LANES, LANES)), tile((LANES, LANES)), tile((1, LANES)), tile((1, LANES)),
                tile((1, LANES)), pl.BlockSpec((1, LANES), lambda j: (0, j))]
    scratch = [pltpu.VMEM((2 * seq, LANES), F32), pltpu.VMEM((2 * seq, LANES), F32)]
    if n2_major:
        args += [t["gf"], t["f2"]]
        in_specs += [const(t["gf"].shape), const(t["f2"].shape)]
        scratch += [pltpu.VMEM((1, 2 * FFT_N1 * FFT_PITCH, LANES), F32)]
    else:
        args += [t["fcf"]]
        in_specs += [const(t["fcf"].shape)]
    return pl.pallas_call(
        functools.partial(_filter_kernel, seq=seq, n2_major=n2_major),
        grid=(n_tiles,),
        in_specs=in_specs,
        out_specs=pl.BlockSpec((4 * seq, LANES), lambda j: (0, j)),
        out_shape=jax.ShapeDtypeStruct((4 * seq, HY_ORDER * W_A), F32),
        scratch_shapes=scratch,
        compiler_params=_cparams("arbitrary"),
        name="hyena_filter" if n2_major else "hyena_ctx_filter",
    )(*args)


LC_OUT_CHUNKS = 8
LC_N2_PER_CHUNK = FFT_N2 // LC_OUT_CHUNKS


def _longconv_kernel(v_ref, gate_ref, kf_ref, g_ref, h_ref, f2_ref, f2i_ref, o_ref, s_ref):
    ct = v_ref.shape[2]
    t = pl.program_id(1)

    @pl.when(t == 0)
    def _():
        def stage1(n2, carry):
            r0 = pl.multiple_of(n2 * N1_PER_TILE, N1_PER_TILE)
            blk = [jnp.concatenate([v_ref[b, pl.ds(i * SEQ_TILE + r0, N1_PER_TILE), :]
                                    for i in range(SEQ // SEQ_TILE)], axis=0)
                   for b in range(BATCH)]
            x = jnp.concatenate([jnp.concatenate([blk[0], blk[2]], axis=1),
                                 jnp.concatenate([blk[1], blk[3]], axis=1)], axis=0)
            _scatter_n2(s_ref, n2, jnp.dot(g_ref[n2], x.astype(BF16), preferred_element_type=F32))
            return carry

        lax.fori_loop(0, FFT_N2, stage1, 0, unroll=4)

        def stage2(k1, carry):
            base, r = _spectrum_rows(s_ref, k1)
            z = jnp.dot(f2_ref[...], r.astype(BF16), preferred_element_type=F32)
            kf = kf_ref[pl.ds(pl.multiple_of(k1 * (2 * FFT_N2), 2 * FFT_N2), 2 * FFT_N2), :]
            kr = jnp.concatenate([kf[:FFT_N2]] * 2, axis=1)
            ki = jnp.concatenate([kf[FFT_N2:]] * 2, axis=1)
            zr, zi = z[:FFT_N2], z[FFT_N2:]
            p = jnp.concatenate([zr * kr - zi * ki, zr * ki + zi * kr], axis=0)
            q = jnp.dot(f2i_ref[...], p.astype(BF16), preferred_element_type=F32)
            for p in range(2):
                s_ref[p, pl.ds(base, FFT_N2), :] = q[:FFT_N2, p * ct:(p + 1) * ct]
                s_ref[p, pl.ds(base + FFT_PITCH, FFT_N2), :] = q[FFT_N2:, p * ct:(p + 1) * ct]
            return carry

        lax.fori_loop(0, FFT_N1, stage2, 0, unroll=2)

    def stage3(j, carry):
        n2 = t * LC_N2_PER_CHUNK + j
        rq = _gather_n2(s_ref, n2)
        y = jnp.dot(h_ref[n2], rq.astype(BF16), preferred_element_type=F32)
        r0 = pl.multiple_of(j * N1_PER_TILE, N1_PER_TILE)
        for b in range(BATCH):
            ri, pair = b % 2, b // 2
            for i in range(SEQ // SEQ_TILE):
                n1 = ri * FFT_HALF_N1 + i * N1_PER_TILE
                yb = y[n1:n1 + N1_PER_TILE, pair * ct:(pair + 1) * ct]
                o_ref[b, i, pl.ds(r0, N1_PER_TILE), :] = (
                    gate_ref[b, i, pl.ds(r0, N1_PER_TILE), :] * yb)
        return carry

    lax.fori_loop(0, LC_N2_PER_CHUNK, stage3, 0, unroll=4)


def _longconv(v, v_tile, gate, gate_tile, kf, kf_col):
    t = _dft_tables()
    ct = LANES
    n_seq_tiles = SEQ // SEQ_TILE
    rows = SEQ_TILE // LC_OUT_CHUNKS
    gate = gate.reshape(gate.shape[0], BATCH, n_seq_tiles, SEQ_TILE, ct)
    out = pl.pallas_call(
        _longconv_kernel,
        grid=(W_A // ct, LC_OUT_CHUNKS),
        in_specs=[
            _single_buffered((None, BATCH, SEQ, ct), lambda j, i: (v_tile + j, 0, 0, 0)),
            pl.BlockSpec((None, BATCH, n_seq_tiles, rows, ct),
                         lambda j, i: (gate_tile + j, 0, 0, i, 0)),
            _single_buffered((2 * FFT_N, ct), lambda j, i: (0, kf_col + j)),
            _single_buffered(t["g"].shape, lambda j, i: (0, 0, 0)),
            _single_buffered(t["h"].shape, lambda j, i: (0, 0, 0)),
            _single_buffered(t["f2"].shape, lambda j, i: (0, 0)),
            _single_buffered(t["f2i"].shape, lambda j, i: (0, 0)),
        ],
        out_specs=pl.BlockSpec((None, BATCH, n_seq_tiles, rows, ct),
                               lambda j, i: (j, 0, 0, i, 0)),
        out_shape=jax.ShapeDtypeStruct((W_A // ct, BATCH, n_seq_tiles, SEQ_TILE, ct), F32),
        scratch_shapes=[pltpu.VMEM((2, 2 * FFT_N1 * FFT_PITCH, ct), F32)],
        compiler_params=_cparams("arbitrary", "arbitrary"),
        name="hyena_longconv",
    )(v, gate, kf, t["g"], t["h"], t["f2"], t["f2i"])
    return out.reshape(W_A // ct, BATCH, SEQ, ct)


def _ctx_conv_kernel(v_ref, gate_ref, kf_ref, fc_ref, fci_ref, o_ref):
    ct = v_ref.shape[2]
    x = jnp.concatenate([jnp.concatenate([v_ref[0], v_ref[2]], axis=1),
                         jnp.concatenate([v_ref[1], v_ref[3]], axis=1)], axis=0)
    z = jnp.dot(fc_ref[...], x.astype(BF16), preferred_element_type=F32)
    kf = kf_ref[...]
    kr = jnp.concatenate([kf[:CTX_N]] * 2, axis=1)
    ki = jnp.concatenate([kf[CTX_N:]] * 2, axis=1)
    zr, zi = z[:CTX_N], z[CTX_N:]
    p = jnp.concatenate([zr * kr - zi * ki, zr * ki + zi * kr], axis=0)
    y = jnp.dot(fci_ref[...], p.astype(BF16), preferred_element_type=F32)
    for b in range(BATCH):
        ri, pair = b % 2, b // 2
        o_ref[b] = gate_ref[b] * y[ri * CTX_LEN:(ri + 1) * CTX_LEN, pair * ct:(pair + 1) * ct]


def _ctx_conv(v, v_tile, gate, gate_tile, kf, kf_col):
    t = _dft_tables()
    ct = LANES
    return pl.pallas_call(
        _ctx_conv_kernel,
        grid=(W_A // ct,),
        in_specs=[
            pl.BlockSpec((None, BATCH, CTX_LEN, ct), lambda j: (v_tile + j, 0, 0, 0)),
            pl.BlockSpec((None, BATCH, CTX_LEN, ct), lambda j: (gate_tile + j, 0, 0, 0)),
            pl.BlockSpec((2 * CTX_N, ct), lambda j: (0, kf_col + j)),
            pl.BlockSpec(t["fc"].shape, lambda j: (0, 0)),
            pl.BlockSpec(t["fci"].shape, lambda j: (0, 0)),
        ],
        out_specs=pl.BlockSpec((None, BATCH, CTX_LEN, ct), lambda j: (j, 0, 0, 0)),
        out_shape=jax.ShapeDtypeStruct((W_A // ct, BATCH, CTX_LEN, ct), F32),
        compiler_params=_cparams("arbitrary"),
        name="hyena_ctx_conv",
    )(v, gate, kf, t["fc"], t["fci"])


MOD_ROWS = 8
CTX_MOD_ROW = BATCH


def _ada_kernel(c_ref, w_ref, b_ref, o_ref):
    cv = c_ref[...]
    s = cv * jax.nn.sigmoid(cv)
    o_ref[0] = jnp.dot(s.astype(BF16), w_ref[0].astype(BF16),
                       preferred_element_type=F32) + b_ref[0]


def _ada_mods(c, c_ctx, w_ada, b_ada):
    cv = jnp.concatenate(
        [c, c_ctx[None], jnp.zeros((MOD_ROWS - BATCH - 1, D_MODEL), F32)], axis=0)
    n = N_MOD * D_MODEL
    tn = n // 4
    return pl.pallas_call(
        _ada_kernel,
        grid=(DEPTH, n // tn),
        in_specs=[
            pl.BlockSpec((MOD_ROWS, D_MODEL), lambda l, j: (0, 0)),
            pl.BlockSpec((1, D_MODEL, tn), lambda l, j: (l, 0, j)),
            pl.BlockSpec((1, 1, tn), lambda l, j: (l, 0, j)),
        ],
        out_specs=pl.BlockSpec((1, MOD_ROWS, tn), lambda l, j: (l, 0, j)),
        out_shape=jax.ShapeDtypeStruct((DEPTH, MOD_ROWS, n), F32),
        compiler_params=_cparams("arbitrary", "arbitrary"),
        name="ada_mods",
    )(cv, w_ada, b_ada.reshape(DEPTH, 1, n))


def _mod_spec(m, row_fn):
    return pl.BlockSpec((1, 1, D_MODEL), lambda i, *_: (row_fn(i), 0, m))


def _rms_mod(x, g, shift, scale):
    ms = jnp.mean(x * x, axis=-1, keepdims=True)
    return (x * lax.rsqrt(ms + NORM_EPS) * g) * (1.0 + scale) + shift


GELU_C = math.sqrt(2.0 / math.pi)


def _gelu(x):
    hx = 0.5 * x
    return hx + hx * jnp.tanh(x * (GELU_C + (GELU_C * 0.044715) * (x * x)))


HALO = 8


def _dwconv(z_all, tm, w, b, left, valid):
    n = z_all.shape[0]
    y = b + w[left:left + 1] * z_all[:tm]
    for k in range(w.shape[0]):
        d = k - left
        if d == 0:
            continue
        s = pltpu.roll(z_all, (-d) % n, 0)[:tm]
        if valid is not None:
            s = s * valid(d)
        y = y + s * w[k:k + 1]
    return y


def _halo_specs(tm, n_rows):
    per = tm // HALO
    last = n_rows // HALO - 1
    return [
        pl.BlockSpec((tm, D_MODEL), lambda i: (i, 0)),
        pl.BlockSpec((HALO, D_MODEL), lambda i: (jnp.maximum(i * per - 1, 0), 0)),
        pl.BlockSpec((HALO, D_MODEL), lambda i: (jnp.minimum((i + 1) * per, last), 0)),
    ]


def _normed_tile(x_ref, xp_ref, xn_ref, sh_ref, sc_ref, g_ref, seq_len):
    tm = x_ref.shape[0]
    g, shift, scale = g_ref[...], sh_ref[0], sc_ref[0]
    h = _rms_mod(x_ref[...], g, shift, scale)
    hn = _rms_mod(xn_ref[...], g, shift, scale)
    hp = _rms_mod(xp_ref[...], g, shift, scale)
    if seq_len >= tm:
        r0 = pl.program_id(0) * tm
        hp = hp * jnp.where((r0 & (seq_len - 1)) == 0, 0.0, 1.0)
        hn = hn * jnp.where(((r0 + tm) & (seq_len - 1)) == 0, 0.0, 1.0)
        valid = None
    else:
        hp, hn = jnp.zeros_like(hp), jnp.zeros_like(hn)
        pos = lax.broadcasted_iota(jnp.int32, (tm, 1), 0) & (seq_len - 1)
        valid = lambda d: jnp.where((pos + d >= 0) & (pos + d < seq_len), 1.0, 0.0)
    return jnp.concatenate([h, hn, hp], axis=0).astype(BF16), valid


EVEN_TM = SEQ_TILE
HY_COLS = 3 * W_A
HY_TILES = HY_COLS // LANES


def _in_even_kernel(x_ref, xp_ref, xn_ref, sh_ref, sc_ref, g_ref, w_ref, cw_ref, cb_ref,
                    lng_ref, sgw_ref, sgb_ref, za_ref, yb_ref, *, seq_len, n2_major):
    tm = x_ref.shape[0]
    h_all, valid = _normed_tile(x_ref, xp_ref, xn_ref, sh_ref, sc_ref, g_ref, seq_len)
    h = h_all[:tm]
    cw = 4 * LANES
    for cc in range(HY_COLS // cw):
        cols = slice(cc * cw, (cc + 1) * cw)
        z_all = jnp.dot(h_all, w_ref[:, cols], preferred_element_type=F32)
        y = _dwconv(z_all, tm, cw_ref[:, cols], cb_ref[:, cols], 1, valid)
        for c in range(cw // LANES):
            tile = cc * (cw // LANES) + c
            yc = y[:, c * LANES:(c + 1) * LANES]
            if n2_major:
                for j in range(N1_PER_TILE):
                    za_ref[tile, pl.ds(j, FFT_N2, stride=N1_PER_TILE), :] = (
                        yc[j * FFT_N2:(j + 1) * FFT_N2])
            else:
                za_ref[tile] = yc
    u = _gelu(jnp.dot(h, w_ref[:, HY_COLS:HY_COLS + W_B], preferred_element_type=F32))
    vb = _gelu(jnp.dot(h, w_ref[:, HY_COLS + W_B:], preferred_element_type=F32))
    vc = vb - jnp.mean(vb, axis=-1, keepdims=True)
    vn = vc * lax.rsqrt(jnp.mean(vc * vc, axis=-1, keepdims=True) + NORM_EPS) * lng_ref[...]
    vn = vn.astype(BF16)
    for ch in range(tm // CHUNK):
        rows = slice(ch * CHUNK, (ch + 1) * CHUNK)
        for q in range(SGU_GROUPS):
            cols = slice(q * SGU_DH, (q + 1) * SGU_DH)
            s = jnp.dot(sgw_ref[q], vn[rows, cols], preferred_element_type=F32) + sgb_ref[:, cols]
            yb_ref[rows, cols] = (u[rows, cols] * s).astype(BF16)


def _in_even(x, mods, mod_row, g, w, cw, cb, lng, sgw, sgb, seq_len, n2_major):
    n_rows = x.shape[0]
    tm = EVEN_TM
    za_shape = (HY_TILES, n_rows, LANES)
    za_spec = pl.BlockSpec((HY_TILES, tm, LANES), lambda i: (0, i, 0))
    const = lambda shape: _single_buffered(shape, lambda i: (0,) * len(shape))
    return pl.pallas_call(
        functools.partial(_in_even_kernel, seq_len=seq_len, n2_major=n2_major),
        grid=(n_rows // tm,),
        in_specs=_halo_specs(tm, n_rows) + [
            _mod_spec(0, mod_row), _mod_spec(1, mod_row),
            const((1, D_MODEL)), const(w.shape), const(cw.shape), const(cb.shape),
            const(lng.shape), const(sgw.shape), const(sgb.shape),
        ],
        out_specs=[za_spec, pl.BlockSpec((tm, W_B), lambda i: (i, 0))],
        out_shape=[jax.ShapeDtypeStruct(za_shape, F32),
                   jax.ShapeDtypeStruct((n_rows, W_B), BF16)],
        compiler_params=_cparams("arbitrary"),
        name="in_proj_even",
    )(x, x, x, mods, mods, g, w, cw, cb, lng, sgw, sgb)


def _out_even_kernel(x_ref, ya_ref, yb_ref, gt_ref, w_ref, o_ref, *, n2_major):
    acc = jnp.dot(yb_ref[...], w_ref[W_A:, :], preferred_element_type=F32)
    for c in range(W_A // LANES):
        if n2_major:
            yac = jnp.concatenate(
                [ya_ref[c, pl.ds(j, FFT_N2, stride=N1_PER_TILE), :]
                 for j in range(N1_PER_TILE)], axis=0)
        else:
            yac = ya_ref[c]
        acc = acc + jnp.dot(yac.astype(BF16), w_ref[c * LANES:(c + 1) * LANES, :],
                            preferred_element_type=F32)
    o_ref[...] = x_ref[...] + gt_ref[0] * acc


def _out_even(x, ya, yb, mods, mod_row, w, n2_major):
    n_rows = x.shape[0]
    tm = EVEN_TM
    nt = W_A // LANES
    ya = ya.reshape(nt, n_rows, LANES)
    ya_spec = pl.BlockSpec((nt, tm, LANES), lambda i: (0, i, 0))
    return pl.pallas_call(
        functools.partial(_out_even_kernel, n2_major=n2_major),
        grid=(n_rows // tm,),
        in_specs=[
            pl.BlockSpec((tm, D_MODEL), lambda i: (i, 0)),
            ya_spec,
            pl.BlockSpec((tm, W_B), lambda i: (i, 0)),
            _mod_spec(2, mod_row),
            _single_buffered(w.shape, lambda i: (0, 0)),
        ],
        out_specs=pl.BlockSpec((tm, D_MODEL), lambda i: (i, 0)),
        out_shape=jax.ShapeDtypeStruct((n_rows, D_MODEL), F32),
        compiler_params=_cparams("arbitrary"),
        name="out_proj_even",
    )(x, ya, yb, mods, w)


def _out_odd_kernel(x_ref, y_ref, gt_ref, w_ref, o_ref):
    acc = jnp.dot(y_ref[...], w_ref[...], preferred_element_type=F32)
    o_ref[...] = x_ref[...] + gt_ref[0] * acc


def _out_odd(x, y, mods, mod_row, w):
    n_rows = x.shape[0]
    tm = 512
    return pl.pallas_call(
        _out_odd_kernel,
        grid=(n_rows // tm,),
        in_specs=[
            pl.BlockSpec((tm, D_MODEL), lambda i: (i, 0)),
            pl.BlockSpec((tm, D_RNN), lambda i: (i, 0)),
            _mod_spec(2, mod_row),
            _single_buffered(w.shape, lambda i: (0, 0)),
        ],
        out_specs=pl.BlockSpec((tm, D_MODEL), lambda i: (i, 0)),
        out_shape=jax.ShapeDtypeStruct((n_rows, D_MODEL), F32),
        compiler_params=_cparams("arbitrary"),
        name="out_proj_odd",
    )(x, y, mods, w)


FFN_TM = 1024
FFN_CHUNKS = 2
FFN_CW = D_FF // FFN_CHUNKS
FFN_SUB = 256


def _ffn_kernel(x_ref, sh_ref, sc_ref, gt_ref, g_ref, w1_ref, w2_ref, wo_ref, fg_ref, o_ref,
                h_ref, acc_ref, *, final_norm):
    j = pl.program_id(1)

    @pl.when(j == 0)
    def _():
        h_ref[...] = _rms_mod(x_ref[...], g_ref[...], sh_ref[0], sc_ref[0]).astype(BF16)

    h = h_ref[...]
    acts = []
    for c0 in range(0, FFN_CW, FFN_SUB):
        cols = slice(c0, min(c0 + FFN_SUB, FFN_CW))
        hz = 0.5 * jnp.dot(h, w1_ref[:, cols], preferred_element_type=F32)
        z2 = jnp.dot(h, w2_ref[:, cols], preferred_element_type=F32)
        acts.append(((hz + hz * jnp.tanh(hz)) * z2).astype(BF16))
    part = jnp.dot(jnp.concatenate(acts, axis=1), wo_ref[...], preferred_element_type=F32)

    @pl.when(j == 0)
    def _():
        acc_ref[...] = part

    @pl.when(j > 0)
    def _():
        acc_ref[...] += part

    @pl.when(j == FFN_CHUNKS - 1)
    def _():
        y = x_ref[...] + gt_ref[0] * acc_ref[...]
        if final_norm:
            y = y * lax.rsqrt(jnp.mean(y * y, axis=-1, keepdims=True) + NORM_EPS) * fg_ref[...]
        o_ref[...] = y


def _ffn(x, mods, mod_row, g, w_in, w_out, final_g, final_norm):
    n_rows = x.shape[0]
    tm = FFN_TM
    return pl.pallas_call(
        functools.partial(_ffn_kernel, final_norm=final_norm),
        grid=(n_rows // tm, FFN_CHUNKS),
        in_specs=[
            pl.BlockSpec((tm, D_MODEL), lambda i, j: (i, 0)),
            _mod_spec(3, mod_row), _mod_spec(4, mod_row), _mod_spec(5, mod_row),
            pl.BlockSpec((1, D_MODEL), lambda i, j: (0, 0)),
            pl.BlockSpec((D_MODEL, FFN_CW), lambda i, j: (0, j)),
            pl.BlockSpec((D_MODEL, FFN_CW), lambda i, j: (0, FFN_CHUNKS + j)),
            pl.BlockSpec((FFN_CW, D_MODEL), lambda i, j: (j, 0)),
            pl.BlockSpec((1, D_MODEL), lambda i, j: (0, 0)),
        ],
        out_specs=pl.BlockSpec((tm, D_MODEL), lambda i, j: (i, 0)),
        out_shape=jax.ShapeDtypeStruct((n_rows, D_MODEL), F32),
        scratch_shapes=[pltpu.VMEM((tm, D_MODEL), BF16), pltpu.VMEM((tm, D_MODEL), F32)],
        compiler_params=_cparams("arbitrary", "arbitrary"),
        name="ffn",
    )(x, mods, mods, mods, g, w_in, w_in, w_out, final_g)


ODD_TM = 512
RNN_TILES = D_RNN // LANES


def _in_odd_kernel(x_ref, xp_ref, xn_ref, sh_ref, sc_ref, g_ref, w_ref, cw_ref, cb_ref,
                   gate_ref, xl_ref, *, seq_len):
    tm = x_ref.shape[0]
    h_all, valid = _normed_tile(x_ref, xp_ref, xn_ref, sh_ref, sc_ref, g_ref, seq_len)
    gate_ref[...] = _gelu(jnp.dot(h_all[:tm], w_ref[:, :D_RNN],
                                  preferred_element_type=F32)).astype(BF16)
    cw = 4 * LANES
    for c0 in range(0, D_RNN, cw):
        cols = slice(c0, min(c0 + cw, D_RNN))
        wcols = slice(D_RNN + cols.start, D_RNN + cols.stop)
        z_all = jnp.dot(h_all, w_ref[:, wcols], preferred_element_type=F32)
        xl_ref[:, cols] = _dwconv(z_all, tm, cw_ref[:, cols], cb_ref[:, cols], 2, valid)


def _in_odd(x, mods, mod_row, g, w, cw, cb, seq_len):
    n_rows = x.shape[0]
    tm = ODD_TM
    const = lambda shape: _single_buffered(shape, lambda i: (0,) * len(shape))
    return pl.pallas_call(
        functools.partial(_in_odd_kernel, seq_len=seq_len),
        grid=(n_rows // tm,),
        in_specs=_halo_specs(tm, n_rows) + [
            _mod_spec(0, mod_row), _mod_spec(1, mod_row),
            const((1, D_MODEL)), const(w.shape), const(cw.shape), const(cb.shape),
        ],
        out_specs=[pl.BlockSpec((tm, D_RNN), lambda i: (i, 0)),
                   pl.BlockSpec((tm, D_RNN), lambda i: (i, 0))],
        out_shape=[jax.ShapeDtypeStruct((n_rows, D_RNN), BF16),
                   jax.ShapeDtypeStruct((n_rows, D_RNN), F32)],
        compiler_params=_cparams("arbitrary"),
        name="in_proj_odd",
    )(x, x, x, mods, mods, g, w, cw, cb)


RG_T = 512
RG_NCH = SEQ // RG_T
RG_GROUPS = 8
RG_WIN = 3 * LANES


def _rg_window_start(j):
    return min(max(LANES * (j - 1), 0), D_RNN - RG_WIN)


def _rg_chunk(x_ref, w_ref, ba_ref, bx_ref, lam_ref, a_pl, b_pl, carry_ref, reverse, emit):
    t_rows = x_ref.shape[0]
    kg = t_rows // RG_GROUPS
    pitch = kg + 8
    x32 = x_ref[...]
    xb = x32.astype(BF16)
    for j in range(RNN_TILES):
        tile = slice(j * LANES, (j + 1) * LANES)
        ws = _rg_window_start(j)
        pre = jnp.dot(xb[:, ws:ws + RG_WIN], w_ref[j], preferred_element_type=F32)
        lam = lam_ref[:, tile]
        softplus_neg = jnp.maximum(-lam, 0.0) + jnp.log1p(jnp.exp(-jnp.abs(lam)))
        th_r = jnp.tanh(pre[:, :LANES] + ba_ref[:, tile])
        th_i = jnp.tanh(pre[:, LANES:] + bx_ref[:, tile])
        c3 = (-0.5 * RG_C * math.log2(math.e)) * softplus_neg
        av = jnp.exp2(c3 + c3 * th_r)
        hx = 0.5 * x32[:, tile]
        bxv = jnp.sqrt(1.0 - av * av) * (hx + hx * th_i)
        for gq in range(RG_GROUPS):
            a_pl[j, gq * pitch:gq * pitch + kg, :] = av[gq * kg:(gq + 1) * kg]
            b_pl[j, gq * pitch:gq * pitch + kg, :] = bxv[gq * kg:(gq + 1) * kg]

    def rows_k(pl_ref, j, k):
        return pl_ref[j, pl.ds(k, RG_GROUPS, stride=pitch), :]

    order = list(range(kg))[::-1] if reverse else list(range(kg))
    groups = list(range(RG_GROUPS))[::-1] if reverse else list(range(RG_GROUPS))
    big_a = [None] * RNN_TILES
    big_b = [None] * RNN_TILES
    for n, k in enumerate(order):
        for j in range(RNN_TILES):
            ak, bk = rows_k(a_pl, j, k), rows_k(b_pl, j, k)
            if n == 0:
                big_a[j], big_b[j] = ak, bk
            else:
                big_b[j] = ak * big_b[j] + bk
                big_a[j] = ak * big_a[j]
    h = [None] * RNN_TILES
    for j in range(RNN_TILES):
        c = carry_ref[j, 0:1, :]
        rows = [None] * RG_GROUPS
        for gq in groups:
            rows[gq] = c
            c = big_a[j][gq:gq + 1] * c + big_b[j][gq:gq + 1]
        carry_ref[j, 0:1, :] = c
        h[j] = jnp.concatenate(rows, axis=0)
    for k in order:
        for j in range(RNN_TILES):
            h[j] = rows_k(a_pl, j, k) * h[j] + rows_k(b_pl, j, k)
            b_pl[j, pl.ds(k, RG_GROUPS, stride=pitch), :] = h[j]
    for j in range(RNN_TILES):
        emit(j, jnp.concatenate(
            [b_pl[j, gq * pitch:gq * pitch + kg, :] for gq in range(RG_GROUPS)], axis=0))


def _rg_kernel(*refs, reverse):
    if reverse:
        (xc_ref, xl_ref, w_ref, ba_ref, bx_ref, lam_ref, oc_ref, ol_ref,
         a_pl, b_pl, carry_ref) = refs
    else:
        (xc_ref, xl_ref, w_ref, ba_ref, bx_ref, lam_ref, gc_ref, gl_ref, hc_ref, hl_ref,
         oc_ref, ol_ref, a_pl, b_pl, carry_ref) = refs
    s = pl.program_id(1)

    def emitter(o_ref, g_ref, hb_ref):
        def emit(j, hcur):
            tile = slice(j * LANES, (j + 1) * LANES)
            if reverse:
                o_ref[:, tile] = hcur.astype(BF16)
            else:
                o_ref[:, tile] = (g_ref[:, tile].astype(F32)
                                  * (hcur + hb_ref[:, tile].astype(F32))).astype(BF16)
        return emit

    @pl.when(s == 0)
    def _():
        carry_ref[...] = jnp.zeros_like(carry_ref)
        _rg_chunk(xc_ref, w_ref, ba_ref, bx_ref, lam_ref, a_pl, b_pl, carry_ref, reverse,
                  emitter(oc_ref, None if reverse else gc_ref, None if reverse else hc_ref))

    @pl.when(s > 0)
    def _():
        _rg_chunk(xl_ref, w_ref, ba_ref, bx_ref, lam_ref, a_pl, b_pl, carry_ref, reverse,
                  emitter(ol_ref, None if reverse else gl_ref, None if reverse else hl_ref))


def _rg_scan(xc, xl, w, ba, bx, lam, reverse, gate_c=None, gate_l=None, hb_c=None, hb_l=None):
    if reverse:
        lat = lambda b, s: (b, RG_NCH - jnp.maximum(s, 1), 0)
    else:
        lat = lambda b, s: (b, jnp.maximum(s - 1, 0), 0)
    ctx_spec = pl.BlockSpec((None, CTX_LEN, D_RNN), lambda b, s: (b, 0, 0))
    lat_spec = pl.BlockSpec((None, RG_T, D_RNN), lat)
    const = lambda shape: _single_buffered(shape, lambda b, s: (0,) * len(shape))
    in_specs = [ctx_spec, lat_spec, const(w.shape), const(ba.shape), const(bx.shape),
                const(lam.shape)]
    args = [xc, xl, w, ba, bx, lam]
    if not reverse:
        in_specs += [ctx_spec, lat_spec, ctx_spec, lat_spec]
        args += [gate_c, gate_l, hb_c, hb_l]
    pitch = RG_T // RG_GROUPS + 8
    return pl.pallas_call(
        functools.partial(_rg_kernel, reverse=reverse),
        grid=(BATCH, 1 + RG_NCH),
        in_specs=in_specs,
        out_specs=[ctx_spec, lat_spec],
        out_shape=[jax.ShapeDtypeStruct((BATCH, CTX_LEN, D_RNN), BF16),
                   jax.ShapeDtypeStruct((BATCH, SEQ, D_RNN), BF16)],
        scratch_shapes=[pltpu.VMEM((RNN_TILES, RG_GROUPS * pitch, LANES), F32),
                        pltpu.VMEM((RNN_TILES, RG_GROUPS * pitch, LANES), F32),
                        pltpu.VMEM((RNN_TILES, 8, LANES), F32)],
        compiler_params=_cparams("arbitrary", "arbitrary"),
        name="rglru_bwd" if reverse else "rglru_fwd",
    )(*args)


def _rg_gate_weights(wa, wx):
    eye = jnp.eye(RG_HEADS, dtype=F32)

    def dense(w):
        return (w[:, :, None, :] * eye[:, None, :, None]).reshape(D_RNN, D_RNN)

    da, dx = dense(wa), dense(wx)
    wins = []
    for j in range(RNN_TILES):
        ws = _rg_window_start(j)
        tile = slice(j * LANES, (j + 1) * LANES)
        wins.append(jnp.concatenate([da[ws:ws + RG_WIN, tile], dx[ws:ws + RG_WIN, tile]], axis=1))
    return (0.5 * jnp.stack(wins)).astype(BF16)


GRID_H = SEQ // GRID_W
GT_ROWS = 8


GT_PITCH = GRID_W + 8


def _grid_transpose_kernel(x_ref, o_ref, s_ref):
    nt = D_MODEL // LANES
    for t in range(nt):
        for j in range(GT_ROWS):
            s_ref[t, j * GT_PITCH:j * GT_PITCH + GRID_W, :] = (
                x_ref[j * GRID_W:(j + 1) * GRID_W, t * LANES:(t + 1) * LANES])
    for c in range(GRID_W):
        o_ref[c] = jnp.concatenate(
            [s_ref[t, pl.ds(c, GT_ROWS, stride=GT_PITCH), :] for t in range(nt)], axis=1)


def _grid_transpose(x):
    nblk = GRID_H // GT_ROWS
    out = pl.pallas_call(
        _grid_transpose_kernel,
        grid=(BATCH, nblk),
        in_specs=[pl.BlockSpec((GT_ROWS * GRID_W, D_MODEL), lambda b, i: (b * nblk + i, 0))],
        out_specs=pl.BlockSpec((None, GRID_W, GT_ROWS, D_MODEL), lambda b, i: (b, 0, i, 0)),
        out_shape=jax.ShapeDtypeStruct((BATCH, GRID_W, GRID_H, D_MODEL), F32),
        scratch_shapes=[pltpu.VMEM((D_MODEL // LANES, GT_ROWS * GT_PITCH, LANES), F32)],
        compiler_params=_cparams("arbitrary", "arbitrary"),
        name="grid_transpose",
    )(x)
    return out.reshape(BATCH * SEQ, D_MODEL)


def kernel(x, c, ctx, c_ctx, w_ada, b_ada, norm_mix_g, norm_ffn_g, w_in_even, w_out_even, hy_conv_w, hy_conv_b, hy_f1_w, hy_f1_b, hy_f2_w, hy_f2_b, hy_f3_w, hy_f3_b, hy_sin_freq, hy_skip, sgu_ln_g, sgu_w, sgu_b, w_in_odd, rg_conv_w, rg_conv_b, rg_wa, rg_ba, rg_wx, rg_bx, rg_lam, w_out_odd, w_ffn_in, w_ffn_out, final_norm_g):
    mods_all = _ada_mods(c, c_ctx, w_ada, b_ada)
    xs = x.reshape(BATCH * SEQ, D_MODEL)
    cs = ctx.reshape(BATCH * CTX_LEN, D_MODEL)
    lat_row_even = lambda i: i // (SEQ // EVEN_TM)
    lat_row_odd = lambda i: i // (SEQ // ODD_TM)
    lat_row_ffn = lambda i: i // (SEQ // FFN_TM)
    ctx_row = lambda i: CTX_MOD_ROW
    n_tiles = W_A // LANES
    for l in range(DEPTH):
        run_ctx = l < DEPTH - 1
        is_rec = l % 2 == 1
        i = l // 2
        mods = mods_all[l].reshape(MOD_ROWS, 1, N_MOD * D_MODEL)
        g_mix = norm_mix_g[l].reshape(1, D_MODEL)
        if is_rec:
            col_major = i % 2 == 1
            if col_major:
                xs = _grid_transpose(xs)
            w_in = w_in_odd[i].astype(BF16)
            cw, cb = rg_conv_w[i], rg_conv_b[i].reshape(1, D_RNN)
            gate_l, xl = _in_odd(xs, mods, lat_row_odd, g_mix, w_in, cw, cb, SEQ)
            gate_c, xc = _in_odd(cs, mods, ctx_row, g_mix, w_in, cw, cb, CTX_LEN)
            xl = xl.reshape(BATCH, SEQ, D_RNN)
            xc = xc.reshape(BATCH, CTX_LEN, D_RNN)
            gate_l = gate_l.reshape(BATCH, SEQ, D_RNN)
            gate_c = gate_c.reshape(BATCH, CTX_LEN, D_RNN)
            row = lambda v: v.reshape(1, D_RNN)
            half = lambda v: 0.5 * row(v)
            hb_c, hb_l = _rg_scan(xc, xl, _rg_gate_weights(rg_wa[i, 1], rg_wx[i, 1]),
                                  half(rg_ba[i, 1]), half(rg_bx[i, 1]), row(rg_lam[i, 1]), True)
            y_c, y_l = _rg_scan(xc, xl, _rg_gate_weights(rg_wa[i, 0], rg_wx[i, 0]),
                                half(rg_ba[i, 0]), half(rg_bx[i, 0]), row(rg_lam[i, 0]), False,
                                gate_c, gate_l, hb_c, hb_l)
            w_out = w_out_odd[i].astype(BF16)
            xs = _out_odd(xs, y_l.reshape(BATCH * SEQ, D_RNN), mods, lat_row_odd, w_out)
            if run_ctx:
                cs = _out_odd(cs, y_c.reshape(BATCH * CTX_LEN, D_RNN), mods, ctx_row, w_out)
        else:
            w_in = w_in_even[i].astype(BF16)
            w_out = w_out_even[i].astype(BF16)
            cw, cb = hy_conv_w[i], hy_conv_b[i].reshape(1, HY_COLS)
            lng = sgu_ln_g[i].reshape(1, W_B)
            sgw = sgu_w[i].astype(BF16)
            sgb = jnp.repeat(sgu_b[i].T, SGU_DH, axis=1)
            skip = hy_skip[i].reshape(1, HY_ORDER * W_A)
            fargs = (hy_f1_w[i], hy_f1_b[i], hy_f2_w[i], hy_f2_b[i], hy_f3_w[i], hy_f3_b[i],
                     hy_sin_freq[i])
            kf = _filter_spectrum(SEQ, *fargs, skip)
            za, yb = _in_even(xs, mods, lat_row_even, g_mix, w_in, cw, cb, lng, sgw, sgb, SEQ, True)
            za = za.reshape(HY_TILES, BATCH, SEQ, LANES)
            y1 = _longconv(za, 0, za, n_tiles, kf, 0)
            ya = _longconv(y1, 0, za, 2 * n_tiles, kf, n_tiles)
            xs = _out_even(xs, ya, yb, mods, lat_row_even, w_out, True)
            if run_ctx:
                kfc = _filter_spectrum(CTX_LEN, *fargs, skip)
                zc, ybc = _in_even(cs, mods, ctx_row, g_mix, w_in, cw, cb, lng, sgw, sgb,
                                   CTX_LEN, False)
                zc = zc.reshape(HY_TILES, BATCH, CTX_LEN, LANES)
                y1c = _ctx_conv(zc, 0, zc, n_tiles, kfc, 0)
                yac = _ctx_conv(y1c, 0, zc, 2 * n_tiles, kfc, n_tiles)
                cs = _out_even(cs, yac, ybc, mods, ctx_row, w_out, False)
        g_ffn = norm_ffn_g[l].reshape(1, D_MODEL)
        w1 = w_ffn_in[l].astype(BF16)
        w2 = w_ffn_out[l].astype(BF16)
        last = l == DEPTH - 1
        xs = _ffn(xs, mods, lat_row_ffn, g_ffn, w1, w2, final_norm_g.reshape(1, D_MODEL), last)
        if run_ctx:
            cs = _ffn(cs, mods, ctx_row, g_ffn, w1, w2, final_norm_g.reshape(1, D_MODEL), False)
        if is_rec and col_major:
            xs = _grid_transpose(xs)
    return xs.reshape(BATCH, SEQ, D_MODEL)
```

```python
import functools
import math

import numpy as np
import jax
import jax.numpy as jnp
from jax import lax
from jax.experimental import pallas as pl
from jax.experimental.pallas import tpu as pltpu

F32 = jnp.float32
BF16 = jnp.bfloat16

D_MODEL = 1024
BATCH = 4
SEQ = 4096
DEPTH = 4
GRID_W = 64
CTX_LEN = 256
N_MOD = 6
NORM_EPS = 1e-6
W_A = D_MODEL // 2
HY_ORDER = 2
HY_SHORT = 3
HY_BANDS = 16
HY_TARGET = 1e-2
HY_FAST_PCT = 0.3
HY_SLOW_PCT = 1.5
W_B = D_MODEL // 2
SGU_GROUPS = 4
SGU_DH = W_B // SGU_GROUPS
CHUNK = 128
D_RNN = ((4 * D_MODEL // 3 + 127) // 128) * 128
RG_HEADS = 16
RG_DH = D_RNN // RG_HEADS
RG_CONV = 4
RG_C = 8.0
D_FF = ((8 * D_MODEL // 3 + 255) // 256) * 256

VMEM_BYTES_V7X = 64 * 1024 * 1024
VMEM_LIMIT = VMEM_BYTES_V7X - 8 * 1024 * 1024
LANES = 128

FFT_N = 2 * SEQ
FFT_N1 = 64
FFT_N2 = 128
FFT_HALF_N1 = FFT_N1 // 2
FFT_PITCH = FFT_N2 + 8
CTX_N = 2 * CTX_LEN


def _cparams(*sem):
    return pltpu.CompilerParams(dimension_semantics=sem, vmem_limit_bytes=VMEM_LIMIT)


def _single_buffered(block_shape, index_map):
    return pl.BlockSpec(block_shape, index_map, pipeline_mode=pl.Buffered(1))


@functools.lru_cache(maxsize=None)
def _dft_tables_f32():
    n = FFT_N
    k1 = np.arange(FFT_N1)
    n2 = np.arange(FFT_N2)

    def angle(n1):
        m = (FFT_N2 * n1[None, None, :] + n2[:, None, None]) * k1[None, :, None]
        return 2.0 * np.pi * (m % n) / n

    a = angle(np.arange(FFT_HALF_N1))
    c, s = np.cos(a), np.sin(a)
    g = np.empty((FFT_N2, FFT_N1, 2, 2, FFT_HALF_N1))
    g[:, :, 0, 0], g[:, :, 0, 1] = c, s
    g[:, :, 1, 0], g[:, :, 1, 1] = -s, c
    g = g.reshape(FFT_N2, 2 * FFT_N1, 2 * FFT_HALF_N1)

    a = angle(np.arange(FFT_N1))
    gf = np.stack([np.cos(a), -np.sin(a)], axis=2).reshape(FFT_N2, 2 * FFT_N1, FFT_N1)

    a = np.transpose(angle(np.arange(FFT_HALF_N1)), (0, 2, 1))
    c, s = np.cos(a) / n, np.sin(a) / n
    h = np.empty((FFT_N2, 2, FFT_HALF_N1, FFT_N1, 2))
    h[:, 0, :, :, 0], h[:, 0, :, :, 1] = c, -s
    h[:, 1, :, :, 0], h[:, 1, :, :, 1] = s, c
    h = h.reshape(FFT_N2, 2 * FFT_HALF_N1, 2 * FFT_N1)

    a = 2.0 * np.pi * np.outer(n2, n2) / FFT_N2
    c, s = np.cos(a), np.sin(a)
    f2 = np.block([[c, s], [-s, c]])
    f2i = np.block([[c, -s], [s, c]])

    kk = np.arange(CTX_N)
    a = 2.0 * np.pi * np.outer(kk, np.arange(CTX_LEN)) / CTX_N
    c, s = np.cos(a), np.sin(a)
    fc = np.block([[c, s], [-s, c]])
    a = 2.0 * np.pi * np.outer(kk, np.arange(CTX_N)) / CTX_N
    fcf = np.concatenate([np.cos(a), -np.sin(a)], axis=0)
    a = 2.0 * np.pi * np.outer(np.arange(CTX_LEN), kk) / CTX_N
    c, s = np.cos(a) / CTX_N, np.sin(a) / CTX_N
    fci = np.block([[c, -s], [s, c]])

    tables = dict(g=g, gf=gf, h=h, f2=f2, f2i=f2i, fc=fc, fcf=fcf, fci=fci)
    return {k: np.asarray(v, dtype=np.float32) for k, v in tables.items()}


def _dft_tables():
    return {k: jnp.asarray(v).astype(BF16) for k, v in _dft_tables_f32().items()}


SEQ_TILE = 1024
N1_PER_TILE = SEQ_TILE // FFT_N2


def _scatter_n2(s_ref, n2, val):
    for p in range(s_ref.shape[0]):
        s_ref[p, pl.ds(n2, 2 * FFT_N1, stride=FFT_PITCH), :] = val[:, p * LANES:(p + 1) * LANES]


def _gather_n2(s_ref, n2):
    return jnp.concatenate(
        [s_ref[p, pl.ds(n2, 2 * FFT_N1, stride=FFT_PITCH), :] for p in range(s_ref.shape[0])],
        axis=1)


def _spectrum_rows(s_ref, k1):
    base = pl.multiple_of(k1 * (2 * FFT_PITCH), 8)
    return base, jnp.concatenate(
        [jnp.concatenate([s_ref[p, pl.ds(base + o, FFT_N2), :] for p in range(s_ref.shape[0])],
                         axis=1) for o in (0, FFT_PITCH)], axis=0)


HY_FFN = 64
FILT_ROWS = 1024
HIGHEST = lax.Precision.HIGHEST


def _tap_rows(row, seq, n2_major):
    if n2_major:
        n2, n1 = row >> int(math.log2(FFT_N1)), row & (FFT_N1 - 1)
        bwd = n1 >= FFT_HALF_N1
        m = FFT_N2 * (n1 - FFT_HALF_N1) + n2
        fwd_pos = FFT_N2 * n1 + n2
    else:
        bwd = row >= seq
        m = row - seq
        fwd_pos = row
    pos = jnp.where(bwd, seq - m, fwd_pos).astype(F32)
    bwd_f = jnp.where(bwd, 1.0, 0.0)
    keep_f = jnp.where(bwd & (m == 0), 0.0, 1.0)
    return pos, bwd_f, keep_f


def _filter_kernel(ec_ref, f1w_ref, f1b_ref, f2w_ref, f2b_ref, sf_ref, w3f_ref, w3b_ref, b3f_ref,
                   b3b_ref, dl_ref, skip_ref, *rest, seq, n2_major):
    if n2_major:
        gf_ref, f2_ref, o_ref, hid_ref, k_ref, s_ref = rest
    else:
        fcf_ref, o_ref, hid_ref, k_ref = rest
    n_rows = 2 * seq
    rb = min(FILT_ROWS, n_rows)

    def rows_of(i):
        r0 = pl.multiple_of(i * rb, rb)
        return r0, _tap_rows(r0 + lax.broadcasted_iota(jnp.int32, (rb, 1), 0), seq, n2_major)

    @pl.when(pl.program_id(0) == 0)
    def _():
        sf = sf_ref[...]

        def hidden(i, carry):
            r0, (pos, _, _) = rows_of(i)
            t = pos * (1.0 / (seq - 1))
            w = pos * (2.0 * math.pi / seq)
            emb = jnp.sin(w * ec_ref[0:1] + ec_ref[1:2]) + t * ec_ref[2:3]
            h = jnp.sin(sf * (jnp.dot(emb, f1w_ref[...], precision=HIGHEST,
                                      preferred_element_type=F32) + f1b_ref[...]))
            h = jnp.sin(sf * (jnp.dot(h.astype(BF16), f2w_ref[...].astype(BF16),
                                      preferred_element_type=F32) + f2b_ref[...]))
            hid_ref[pl.ds(r0, rb), :] = h
            return carry

        lax.fori_loop(0, n_rows // rb, hidden, 0)

    def taps(i, ssq):
        r0, (pos, bwd_f, keep_f) = rows_of(i)
        h = hid_ref[pl.ds(r0, rb), :].astype(BF16)
        kf = jnp.dot(h, w3f_ref[0].astype(BF16), preferred_element_type=F32) + b3f_ref[0]
        kb = jnp.dot(h, w3b_ref[0].astype(BF16), preferred_element_type=F32) + b3b_ref[0]
        window = jnp.exp(-(pos * (1.0 / (seq - 1))) * dl_ref[0])
        k = (kf + bwd_f * (kb - kf)) * window * keep_f
        k_ref[pl.ds(r0, rb), :] = k
        return ssq + jnp.sum(k * k, axis=0, keepdims=True)

    ssq = lax.fori_loop(0, n_rows // rb, taps, jnp.zeros((1, LANES), F32))
    scale = lax.rsqrt(ssq + NORM_EPS)
    skip = skip_ref[...]

    if n2_major:
        def stage1(n2, carry):
            x = k_ref[pl.ds(pl.multiple_of(n2 * FFT_N1, FFT_N1), FFT_N1), :] * scale
            _scatter_n2(s_ref, n2, jnp.dot(gf_ref[n2], x.astype(BF16),
                                           preferred_element_type=F32))
            return carry

        lax.fori_loop(0, FFT_N2, stage1, 0, unroll=8)

        def stage2(k1, carry):
            _, r = _spectrum_rows(s_ref, k1)
            z = jnp.dot(f2_ref[...], r.astype(BF16), preferred_element_type=F32)
            row = pl.multiple_of(k1 * (2 * FFT_N2), 2 * FFT_N2)
            o_ref[pl.ds(row, FFT_N2), :] = z[:FFT_N2] + skip
            o_ref[pl.ds(row + FFT_N2, FFT_N2), :] = z[FFT_N2:]
            return carry

        lax.fori_loop(0, FFT_N1, stage2, 0, unroll=8)
    else:
        z = jnp.dot(fcf_ref[...], (k_ref[...] * scale).astype(BF16), preferred_element_type=F32)
        o_ref[:n_rows, :] = z[:n_rows] + skip
        o_ref[n_rows:, :] = z[n_rows:]


def _filter_spectrum(seq, f1_w, f1_b, f2_w, f2_b, f3_w, f3_b, sin_freq, skip):
    n2_major = seq == SEQ
    t = _dft_tables()
    pad = lambda a, r, c: jnp.pad(a, ((0, r - a.shape[0]), (0, c - a.shape[1])))
    f = jnp.linspace(1e-4, HY_BANDS - 1, HY_BANDS, dtype=F32)
    zeros = lambda n: jnp.zeros((n,), F32)
    ones = lambda n: jnp.ones((n,), F32)
    rest = LANES - 1 - 2 * HY_BANDS
    ec = jnp.stack([
        jnp.concatenate([zeros(1), f, f, zeros(rest)]),
        jnp.concatenate([zeros(1), (math.pi / 2) * ones(HY_BANDS), math.pi * ones(HY_BANDS),
                         zeros(rest)]),
        jnp.concatenate([ones(1), zeros(LANES - 1)]),
    ] + [zeros(LANES)] * 5)
    row = lambda v: pad(v.reshape(1, -1), 1, LANES)
    n_tiles = HY_ORDER * W_A // LANES
    w3 = f3_w.reshape(HY_FFN, HY_ORDER, 2, W_A)
    b3 = f3_b.reshape(HY_ORDER, 2, W_A)
    tiles_w = lambda d: jnp.pad(
        w3[:, :, d, :].reshape(HY_FFN, n_tiles, LANES).transpose(1, 0, 2),
        ((0, 0), (0, LANES - HY_FFN), (0, 0)))
    tiles_b = lambda d: b3[:, d, :].reshape(n_tiles, 1, LANES)
    deltas = jnp.abs(jnp.linspace(math.log(HY_TARGET) / HY_SLOW_PCT,
                                  math.log(HY_TARGET) / HY_FAST_PCT, W_A, dtype=F32))
    dl = jnp.tile(deltas, HY_ORDER).reshape(n_tiles, 1, LANES)
    const = lambda shape: _single_buffered(shape, lambda j: (0,) * len(shape))
    tile = lambda shape: pl.BlockSpec((1,) + shape, lambda j: (j, 0, 0))
    args = [ec, pad(f1_w, LANES, LANES), row(f1_b), pad(f2_w, LANES, LANES), row(f2_b),
            row(sin_freq), tiles_w(0), tiles_w(1), tiles_b(0), tiles_b(1), dl, skip]
    in_specs = [const((8, LANES)), const((LANES, LANES)), const((1, LANES)),
                const((LANES, LANES)), const((1, LANES)), const((1, LANES)),
                tile((LANES, LANES)), tile((LANES, LANES)), tile((1, LANES)), tile((1, LANES)),
                tile((1, LANES)), pl.BlockSpec((1, LANES), lambda j: (0, j))]
    scratch = [pltpu.VMEM((2 * seq, LANES), F32), pltpu.VMEM((2 * seq, LANES), F32)]
    if n2_major:
        args += [t["gf"], t["f2"]]
        in_specs += [const(t["gf"].shape), const(t["f2"].shape)]
        scratch += [pltpu.VMEM((1, 2 * FFT_N1 * FFT_PITCH, LANES), F32)]
    else:
        args += [t["fcf"]]
        in_specs += [const(t["fcf"].shape)]
    return pl.pallas_call(
        functools.partial(_filter_kernel, seq=seq, n2_major=n2_major),
        grid=(n_tiles,),
        in_specs=in_specs,
        out_specs=pl.BlockSpec((4 * seq, LANES), lambda j: (0, j)),
        out_shape=jax.ShapeDtypeStruct((4 * seq, HY_ORDER * W_A), F32),
        scratch_shapes=scratch,
        compiler_params=_cparams("arbitrary"),
        name="hyena_filter" if n2_major else "hyena_ctx_filter",
    )(*args)


LC_OUT_CHUNKS = 8
LC_N2_PER_CHUNK = FFT_N2 // LC_OUT_CHUNKS


def _longconv_kernel(v_ref, gate_ref, kf_ref, g_ref, h_ref, f2_ref, f2i_ref, o_ref, s_ref):
    ct = v_ref.shape[2]
    t = pl.program_id(1)

    @pl.when(t == 0)
    def _():
        def stage1(n2, carry):
            r0 = pl.multiple_of(n2 * N1_PER_TILE, N1_PER_TILE)
            blk = [jnp.concatenate([v_ref[b, pl.ds(i * SEQ_TILE + r0, N1_PER_TILE), :]
                                    for i in range(SEQ // SEQ_TILE)], axis=0)
                   for b in range(BATCH)]
            x = jnp.concatenate([jnp.concatenate([blk[0], blk[2]], axis=1),
                                 jnp.concatenate([blk[1], blk[3]], axis=1)], axis=0)
            _scatter_n2(s_ref, n2, jnp.dot(g_ref[n2], x.astype(BF16), preferred_element_type=F32))
            return carry

        lax.fori_loop(0, FFT_N2, stage1, 0, unroll=8)

        def stage2(k1, carry):
            base, r = _spectrum_rows(s_ref, k1)
            z = jnp.dot(f2_ref[...], r.astype(BF16), preferred_element_type=F32)
            kf = kf_ref[pl.ds(pl.multiple_of(k1 * (2 * FFT_N2), 2 * FFT_N2), 2 * FFT_N2), :]
            kr = jnp.concatenate([kf[:FFT_N2]] * 2, axis=1)
            ki = jnp.concatenate([kf[FFT_N2:]] * 2, axis=1)
            zr, zi = z[:FFT_N2], z[FFT_N2:]
            p = jnp.concatenate([zr * kr - zi * ki, zr * ki + zi * kr], axis=0)
            q = jnp.dot(f2i_ref[...], p.astype(BF16), preferred_element_type=F32)
            for p in range(2):
                s_ref[p, pl.ds(base, FFT_N2), :] = q[:FFT_N2, p * ct:(p + 1) * ct]
                s_ref[p, pl.ds(base + FFT_PITCH, FFT_N2), :] = q[FFT_N2:, p * ct:(p + 1) * ct]
            return carry

        lax.fori_loop(0, FFT_N1, stage2, 0, unroll=4)

    def stage3(j, carry):
        n2 = t * LC_N2_PER_CHUNK + j
        rq = _gather_n2(s_ref, n2)
        y = jnp.dot(h_ref[n2], rq.astype(BF16), preferred_element_type=F32)
        r0 = pl.multiple_of(j * N1_PER_TILE, N1_PER_TILE)
        for b in range(BATCH):
            ri, pair = b % 2, b // 2
            for i in range(SEQ // SEQ_TILE):
                n1 = ri * FFT_HALF_N1 + i * N1_PER_TILE
                yb = y[n1:n1 + N1_PER_TILE, pair * ct:(pair + 1) * ct]
                o_ref[b, i, pl.ds(r0, N1_PER_TILE), :] = (
                    gate_ref[b, i, pl.ds(r0, N1_PER_TILE), :] * yb)
        return carry

    lax.fori_loop(0, LC_N2_PER_CHUNK, stage3, 0, unroll=8)


def _longconv(v, v_tile, gate, gate_tile, kf, kf_col):
    t = _dft_tables()
    ct = LANES
    n_seq_tiles = SEQ // SEQ_TILE
    rows = SEQ_TILE // LC_OUT_CHUNKS
    gate = gate.reshape(gate.shape[0], BATCH, n_seq_tiles, SEQ_TILE, ct)
    out = pl.pallas_call(
        _longconv_kernel,
        grid=(W_A // ct, LC_OUT_CHUNKS),
        in_specs=[
            _single_buffered((None, BATCH, SEQ, ct), lambda j, i: (v_tile + j, 0, 0, 0)),
            pl.BlockSpec((None, BATCH, n_seq_tiles, rows, ct),
                         lambda j, i: (gate_tile + j, 0, 0, i, 0)),
            _single_buffered((2 * FFT_N, ct), lambda j, i: (0, kf_col + j)),
            _single_buffered(t["g"].shape, lambda j, i: (0, 0, 0)),
            _single_buffered(t["h"].shape, lambda j, i: (0, 0, 0)),
            _single_buffered(t["f2"].shape, lambda j, i: (0, 0)),
            _single_buffered(t["f2i"].shape, lambda j, i: (0, 0)),
        ],
        out_specs=pl.BlockSpec((None, BATCH, n_seq_tiles, rows, ct),
                               lambda j, i: (j, 0, 0, i, 0)),
        out_shape=jax.ShapeDtypeStruct((W_A // ct, BATCH, n_seq_tiles, SEQ_TILE, ct), F32),
        scratch_shapes=[pltpu.VMEM((2, 2 * FFT_N1 * FFT_PITCH, ct), F32)],
        compiler_params=_cparams("arbitrary", "arbitrary"),
        name="hyena_longconv",
    )(v, gate, kf, t["g"], t["h"], t["f2"], t["f2i"])
    return out.reshape(W_A // ct, BATCH, SEQ, ct)


def _ctx_conv_kernel(v_ref, gate_ref, kf_ref, fc_ref, fci_ref, o_ref):
    ct = v_ref.shape[2]
    x = jnp.concatenate([jnp.concatenate([v_ref[0], v_ref[2]], axis=1),
                         jnp.concatenate([v_ref[1], v_ref[3]], axis=1)], axis=0)
    z = jnp.dot(fc_ref[...], x.astype(BF16), preferred_element_type=F32)
    kf = kf_ref[...]
    kr = jnp.concatenate([kf[:CTX_N]] * 2, axis=1)
    ki = jnp.concatenate([kf[CTX_N:]] * 2, axis=1)
    zr, zi = z[:CTX_N], z[CTX_N:]
    p = jnp.concatenate([zr * kr - zi * ki, zr * ki + zi * kr], axis=0)
    y = jnp.dot(fci_ref[...], p.astype(BF16), preferred_element_type=F32)
    for b in range(BATCH):
        ri, pair = b % 2, b // 2
        o_ref[b] = gate_ref[b] * y[ri * CTX_LEN:(ri + 1) * CTX_LEN, pair * ct:(pair + 1) * ct]


def _ctx_conv(v, v_tile, gate, gate_tile, kf, kf_col):
    t = _dft_tables()
    ct = LANES
    return pl.pallas_call(
        _ctx_conv_kernel,
        grid=(W_A // ct,),
        in_specs=[
            pl.BlockSpec((None, BATCH, CTX_LEN, ct), lambda j: (v_tile + j, 0, 0, 0)),
            pl.BlockSpec((None, BATCH, CTX_LEN, ct), lambda j: (gate_tile + j, 0, 0, 0)),
            pl.BlockSpec((2 * CTX_N, ct), lambda j: (0, kf_col + j)),
            pl.BlockSpec(t["fc"].shape, lambda j: (0, 0)),
            pl.BlockSpec(t["fci"].shape, lambda j: (0, 0)),
        ],
        out_specs=pl.BlockSpec((None, BATCH, CTX_LEN, ct), lambda j: (j, 0, 0, 0)),
        out_shape=jax.ShapeDtypeStruct((W_A // ct, BATCH, CTX_LEN, ct), F32),
        compiler_params=_cparams("arbitrary"),
        name="hyena_ctx_conv",
    )(v, gate, kf, t["fc"], t["fci"])


MOD_ROWS = 8
CTX_MOD_ROW = BATCH


def _ada_kernel(c_ref, w_ref, b_ref, o_ref):
    cv = c_ref[...]
    s = cv * jax.nn.sigmoid(cv)
    o_ref[0] = jnp.dot(s.astype(BF16), w_ref[0].astype(BF16),
                       preferred_element_type=F32) + b_ref[0]


def _ada_mods(c, c_ctx, w_ada, b_ada):
    cv = jnp.concatenate(
        [c, c_ctx[None], jnp.zeros((MOD_ROWS - BATCH - 1, D_MODEL), F32)], axis=0)
    n = N_MOD * D_MODEL
    tn = n // 4
    return pl.pallas_call(
        _ada_kernel,
        grid=(DEPTH, n // tn),
        in_specs=[
            pl.BlockSpec((MOD_ROWS, D_MODEL), lambda l, j: (0, 0)),
            pl.BlockSpec((1, D_MODEL, tn), lambda l, j: (l, 0, j)),
            pl.BlockSpec((1, 1, tn), lambda l, j: (l, 0, j)),
        ],
        out_specs=pl.BlockSpec((1, MOD_ROWS, tn), lambda l, j: (l, 0, j)),
        out_shape=jax.ShapeDtypeStruct((DEPTH, MOD_ROWS, n), F32),
        compiler_params=_cparams("arbitrary", "arbitrary"),
        name="ada_mods",
    )(cv, w_ada, b_ada.reshape(DEPTH, 1, n))


def _mod_spec(m, row_fn):
    return pl.BlockSpec((1, 1, D_MODEL), lambda i, *_: (row_fn(i), 0, m))


def _rms_mod(x, g, shift, scale):
    ms = jnp.mean(x * x, axis=-1, keepdims=True)
    return (x * lax.rsqrt(ms + NORM_EPS) * g) * (1.0 + scale) + shift


GELU_C = math.sqrt(2.0 / math.pi)


def _gelu(x):
    hx = 0.5 * x
    return hx + hx * jnp.tanh(x * (GELU_C + (GELU_C * 0.044715) * (x * x)))


HALO = 8


def _dwconv(z_all, tm, w, b, left, valid):
    n = z_all.shape[0]
    y = b + w[left:left + 1] * z_all[:tm]
    for k in range(w.shape[0]):
        d = k - left
        if d == 0:
            continue
        s = pltpu.roll(z_all, (-d) % n, 0)[:tm]
        if valid is not None:
            s = s * valid(d)
        y = y + s * w[k:k + 1]
    return y


def _halo_specs(tm, n_rows):
    per = tm // HALO
    last = n_rows // HALO - 1
    return [
        pl.BlockSpec((tm, D_MODEL), lambda i: (i, 0)),
        pl.BlockSpec((HALO, D_MODEL), lambda i: (jnp.maximum(i * per - 1, 0), 0)),
        pl.BlockSpec((HALO, D_MODEL), lambda i: (jnp.minimum((i + 1) * per, last), 0)),
    ]


def _normed_tile(x_ref, xp_ref, xn_ref, sh_ref, sc_ref, g_ref, seq_len):
    tm = x_ref.shape[0]
    g, shift, scale = g_ref[...], sh_ref[0], sc_ref[0]
    h = _rms_mod(x_ref[...], g, shift, scale)
    hn = _rms_mod(xn_ref[...], g, shift, scale)
    hp = _rms_mod(xp_ref[...], g, shift, scale)
    if seq_len >= tm:
        r0 = pl.program_id(0) * tm
        hp = hp * jnp.where((r0 & (seq_len - 1)) == 0, 0.0, 1.0)
        hn = hn * jnp.where(((r0 + tm) & (seq_len - 1)) == 0, 0.0, 1.0)
        valid = None
    else:
        hp, hn = jnp.zeros_like(hp), jnp.zeros_like(hn)
        pos = lax.broadcasted_iota(jnp.int32, (tm, 1), 0) & (seq_len - 1)
        valid = lambda d: jnp.where((pos + d >= 0) & (pos + d < seq_len), 1.0, 0.0)
    return jnp.concatenate([h, hn, hp], axis=0).astype(BF16), valid


EVEN_TM = SEQ_TILE
HY_COLS = 3 * W_A
HY_TILES = HY_COLS // LANES


def _in_even_kernel(x_ref, xp_ref, xn_ref, sh_ref, sc_ref, g_ref, w_ref, cw_ref, cb_ref,
                    lng_ref, sgw_ref, sgb_ref, za_ref, yb_ref, *, seq_len, n2_major):
    tm = x_ref.shape[0]
    h_all, valid = _normed_tile(x_ref, xp_ref, xn_ref, sh_ref, sc_ref, g_ref, seq_len)
    h = h_all[:tm]
    cw = 4 * LANES
    for cc in range(HY_COLS // cw):
        cols = slice(cc * cw, (cc + 1) * cw)
        z_all = jnp.dot(h_all, w_ref[:, cols], preferred_element_type=F32)
        y = _dwconv(z_all, tm, cw_ref[:, cols], cb_ref[:, cols], 1, valid)
        for c in range(cw // LANES):
            tile = cc * (cw // LANES) + c
            yc = y[:, c * LANES:(c + 1) * LANES]
            if n2_major:
                for j in range(N1_PER_TILE):
                    za_ref[tile, pl.ds(j, FFT_N2, stride=N1_PER_TILE), :] = (
                        yc[j * FFT_N2:(j + 1) * FFT_N2])
            else:
                za_ref[tile] = yc
    u = _gelu(jnp.dot(h, w_ref[:, HY_COLS:HY_COLS + W_B], preferred_element_type=F32))
    vb = _gelu(jnp.dot(h, w_ref[:, HY_COLS + W_B:], preferred_element_type=F32))
    vc = vb - jnp.mean(vb, axis=-1, keepdims=True)
    vn = vc * lax.rsqrt(jnp.mean(vc * vc, axis=-1, keepdims=True) + NORM_EPS) * lng_ref[...]
    vn = vn.astype(BF16)
    for ch in range(tm // CHUNK):
        rows = slice(ch * CHUNK, (ch + 1) * CHUNK)
        for q in range(SGU_GROUPS):
            cols = slice(q * SGU_DH, (q + 1) * SGU_DH)
            s = jnp.dot(sgw_ref[q], vn[rows, cols], preferred_element_type=F32) + sgb_ref[:, cols]
            yb_ref[rows, cols] = (u[rows, cols] * s).astype(BF16)


def _in_even(x, mods, mod_row, g, w, cw, cb, lng, sgw, sgb, seq_len, n2_major):
    n_rows = x.shape[0]
    tm = EVEN_TM
    za_shape = (HY_TILES, n_rows, LANES)
    za_spec = pl.BlockSpec((HY_TILES, tm, LANES), lambda i: (0, i, 0))
    const = lambda shape: _single_buffered(shape, lambda i: (0,) * len(shape))
    return pl.pallas_call(
        functools.partial(_in_even_kernel, seq_len=seq_len, n2_major=n2_major),
        grid=(n_rows // tm,),
        in_specs=_halo_specs(tm, n_rows) + [
            _mod_spec(0, mod_row), _mod_spec(1, mod_row),
            const((1, D_MODEL)), const(w.shape), const(cw.shape), const(cb.shape),
            const(lng.shape), const(sgw.shape), const(sgb.shape),
        ],
        out_specs=[za_spec, pl.BlockSpec((tm, W_B), lambda i: (i, 0))],
        out_shape=[jax.ShapeDtypeStruct(za_shape, F32),
                   jax.ShapeDtypeStruct((n_rows, W_B), BF16)],
        compiler_params=_cparams("arbitrary"),
        name="in_proj_even",
    )(x, x, x, mods, mods, g, w, cw, cb, lng, sgw, sgb)


def _out_even_kernel(x_ref, ya_ref, yb_ref, gt_ref, w_ref, o_ref, *, n2_major):
    acc = jnp.dot(yb_ref[...], w_ref[W_A:, :], preferred_element_type=F32)
    for c in range(W_A // LANES):
        if n2_major:
            yac = jnp.concatenate(
                [ya_ref[c, pl.ds(j, FFT_N2, stride=N1_PER_TILE), :]
                 for j in range(N1_PER_TILE)], axis=0)
        else:
            yac = ya_ref[c]
        acc = acc + jnp.dot(yac.astype(BF16), w_ref[c * LANES:(c + 1) * LANES, :],
                            preferred_element_type=F32)
    o_ref[...] = x_ref[...] + gt_ref[0] * acc


def _out_even(x, ya, yb, mods, mod_row, w, n2_major):
    n_rows = x.shape[0]
    tm = EVEN_TM
    nt = W_A // LANES
    ya = ya.reshape(nt, n_rows, LANES)
    ya_spec = pl.BlockSpec((nt, tm, LANES), lambda i: (0, i, 0))
    return pl.pallas_call(
        functools.partial(_out_even_kernel, n2_major=n2_major),
        grid=(n_rows // tm,),
        in_specs=[
            pl.BlockSpec((tm, D_MODEL), lambda i: (i, 0)),
            ya_spec,
            pl.BlockSpec((tm, W_B), lambda i: (i, 0)),
            _mod_spec(2, mod_row),
            _single_buffered(w.shape, lambda i: (0, 0)),
        ],
        out_specs=pl.BlockSpec((tm, D_MODEL), lambda i: (i, 0)),
        out_shape=jax.ShapeDtypeStruct((n_rows, D_MODEL), F32),
        compiler_params=_cparams("arbitrary"),
        name="out_proj_even",
    )(x, ya, yb, mods, w)


def _out_odd_kernel(x_ref, y_ref, gt_ref, w_ref, o_ref):
    acc = jnp.dot(y_ref[...], w_ref[...], preferred_element_type=F32)
    o_ref[...] = x_ref[...] + gt_ref[0] * acc


def _out_odd(x, y, mods, mod_row, w):
    n_rows = x.shape[0]
    tm = 512
    return pl.pallas_call(
        _out_odd_kernel,
        grid=(n_rows // tm,),
        in_specs=[
            pl.BlockSpec((tm, D_MODEL), lambda i: (i, 0)),
            pl.BlockSpec((tm, D_RNN), lambda i: (i, 0)),
            _mod_spec(2, mod_row),
            _single_buffered(w.shape, lambda i: (0, 0)),
        ],
        out_specs=pl.BlockSpec((tm, D_MODEL), lambda i: (i, 0)),
        out_shape=jax.ShapeDtypeStruct((n_rows, D_MODEL), F32),
        compiler_params=_cparams("arbitrary"),
        name="out_proj_odd",
    )(x, y, mods, w)


FFN_TM = 1024
FFN_CHUNKS = 2
FFN_CW = D_FF // FFN_CHUNKS
FFN_SUB = 256


def _ffn_kernel(x_ref, sh_ref, sc_ref, gt_ref, g_ref, w1_ref, w2_ref, wo_ref, fg_ref, o_ref,
                h_ref, acc_ref, *, final_norm):
    j = pl.program_id(1)

    @pl.when(j == 0)
    def _():
        h_ref[...] = _rms_mod(x_ref[...], g_ref[...], sh_ref[0], sc_ref[0]).astype(BF16)

    h = h_ref[...]
    acts = []
    for c0 in range(0, FFN_CW, FFN_SUB):
        cols = slice(c0, min(c0 + FFN_SUB, FFN_CW))
        hz = 0.5 * jnp.dot(h, w1_ref[:, cols], preferred_element_type=F32)
        z2 = jnp.dot(h, w2_ref[:, cols], preferred_element_type=F32)
        acts.append(((hz + hz * jnp.tanh(hz)) * z2).astype(BF16))
    part = jnp.dot(jnp.concatenate(acts, axis=1), wo_ref[...], preferred_element_type=F32)

    @pl.when(j == 0)
    def _():
        acc_ref[...] = part

    @pl.when(j > 0)
    def _():
        acc_ref[...] += part

    @pl.when(j == FFN_CHUNKS - 1)
    def _():
        y = x_ref[...] + gt_ref[0] * acc_ref[...]
        if final_norm:
            y = y * lax.rsqrt(jnp.mean(y * y, axis=-1, keepdims=True) + NORM_EPS) * fg_ref[...]
        o_ref[...] = y


def _ffn(x, mods, mod_row, g, w_in, w_out, final_g, final_norm):
    n_rows = x.shape[0]
    tm = FFN_TM
    return pl.pallas_call(
        functools.partial(_ffn_kernel, final_norm=final_norm),
        grid=(n_rows // tm, FFN_CHUNKS),
        in_specs=[
            pl.BlockSpec((tm, D_MODEL), lambda i, j: (i, 0)),
            _mod_spec(3, mod_row), _mod_spec(4, mod_row), _mod_spec(5, mod_row),
            pl.BlockSpec((1, D_MODEL), lambda i, j: (0, 0)),
            pl.BlockSpec((D_MODEL, FFN_CW), lambda i, j: (0, j)),
            pl.BlockSpec((D_MODEL, FFN_CW), lambda i, j: (0, FFN_CHUNKS + j)),
            pl.BlockSpec((FFN_CW, D_MODEL), lambda i, j: (j, 0)),
            pl.BlockSpec((1, D_MODEL), lambda i, j: (0, 0)),
        ],
        out_specs=pl.BlockSpec((tm, D_MODEL), lambda i, j: (i, 0)),
        out_shape=jax.ShapeDtypeStruct((n_rows, D_MODEL), F32),
        scratch_shapes=[pltpu.VMEM((tm, D_MODEL), BF16), pltpu.VMEM((tm, D_MODEL), F32)],
        compiler_params=_cparams("arbitrary", "arbitrary"),
        name="ffn",
    )(x, mods, mods, mods, g, w_in, w_in, w_out, final_g)


ODD_TM = 512
RNN_TILES = D_RNN // LANES


def _in_odd_kernel(x_ref, xp_ref, xn_ref, sh_ref, sc_ref, g_ref, w_ref, cw_ref, cb_ref,
                   gate_ref, xl_ref, *, seq_len, chunk):
    tm = x_ref.shape[0]
    h_all, valid = _normed_tile(x_ref, xp_ref, xn_ref, sh_ref, sc_ref, g_ref, seq_len)
    gate_ref[...] = _gelu(jnp.dot(h_all[:tm], w_ref[:, :D_RNN],
                                  preferred_element_type=F32)).astype(BF16)
    cw = 4 * LANES
    for c0 in range(0, D_RNN, cw):
        cols = slice(c0, min(c0 + cw, D_RNN))
        wcols = slice(D_RNN + cols.start, D_RNN + cols.stop)
        z_all = jnp.dot(h_all, w_ref[:, wcols], preferred_element_type=F32)
        y = _dwconv(z_all, tm, cw_ref[:, cols], cb_ref[:, cols], 2, valid)
        kg = chunk // RG_GROUPS
        for c in range((cols.stop - cols.start) // LANES):
            yc = y[:, c * LANES:(c + 1) * LANES]
            for r0 in range(0, tm, chunk):
                for gq in range(RG_GROUPS):
                    xl_ref[c0 // LANES + c, pl.ds(r0 + gq, kg, stride=RG_GROUPS), :] = (
                        yc[r0 + gq * kg:r0 + (gq + 1) * kg])


def _in_odd(x, mods, mod_row, g, w, cw, cb, seq_len, chunk):
    n_rows = x.shape[0]
    tm = ODD_TM
    const = lambda shape: _single_buffered(shape, lambda i: (0,) * len(shape))
    return pl.pallas_call(
        functools.partial(_in_odd_kernel, seq_len=seq_len, chunk=chunk),
        grid=(n_rows // tm,),
        in_specs=_halo_specs(tm, n_rows) + [
            _mod_spec(0, mod_row), _mod_spec(1, mod_row),
            const((1, D_MODEL)), const(w.shape), const(cw.shape), const(cb.shape),
        ],
        out_specs=[pl.BlockSpec((tm, D_RNN), lambda i: (i, 0)),
                   pl.BlockSpec((RNN_TILES, tm, LANES), lambda i: (0, i, 0))],
        out_shape=[jax.ShapeDtypeStruct((n_rows, D_RNN), BF16),
                   jax.ShapeDtypeStruct((RNN_TILES, n_rows, LANES), F32)],
        compiler_params=_cparams("arbitrary"),
        name="in_proj_odd",
    )(x, x, x, mods, mods, g, w, cw, cb)


RG_T = 512
RG_NCH = SEQ // RG_T
RG_GROUPS = 8
RG_PLANES_PER_PASS = 6
SQRT_GUARD = 1e-30
RG_WIN = 3 * LANES


def _rg_window_start(j):
    return min(max(LANES * (j - 1), 0), D_RNN - RG_WIN)


def _rg_chunk(x_ref, w_ref, ba_ref, bx_ref, lam_ref, a_pl, b_pl, carry_ref, reverse, emit):
    t_rows = x_ref.shape[1]
    kg = t_rows // RG_GROUPS
    xb = [x_ref[j].astype(BF16) for j in range(RNN_TILES)]
    for j in range(RNN_TILES):
        tile = slice(j * LANES, (j + 1) * LANES)
        wt = _rg_window_start(j) // LANES
        pre = jnp.dot(jnp.concatenate(xb[wt:wt + RG_WIN // LANES], axis=1), w_ref[j],
                      preferred_element_type=F32)
        lam = lam_ref[:, tile]
        softplus_neg = jnp.maximum(-lam, 0.0) + jnp.log1p(jnp.exp(-jnp.abs(lam)))
        th_r = jnp.tanh(pre[:, :LANES] + ba_ref[:, tile])
        th_i = jnp.tanh(pre[:, LANES:] + bx_ref[:, tile])
        c3 = (-0.5 * RG_C * math.log2(math.e)) * softplus_neg
        av = jnp.exp2(c3 + c3 * th_r)
        hx = 0.5 * x_ref[j]
        y = 1.0 - av * av
        a_pl[j, 0:t_rows, :] = av
        b_pl[j, 0:t_rows, :] = (y * lax.rsqrt(jnp.maximum(y, SQRT_GUARD))) * (hx + hx * th_i)

    def rows_k(pl_ref, j, k):
        return pl_ref[j, k * RG_GROUPS:(k + 1) * RG_GROUPS, :]

    order = list(range(kg))[::-1] if reverse else list(range(kg))
    groups = list(range(RG_GROUPS))[::-1] if reverse else list(range(RG_GROUPS))
    for j0 in range(0, RNN_TILES, RG_PLANES_PER_PASS):
        planes = range(j0, min(j0 + RG_PLANES_PER_PASS, RNN_TILES))
        big_a, big_b = {}, {}
        for n, k in enumerate(order):
            for j in planes:
                ak, bk = rows_k(a_pl, j, k), rows_k(b_pl, j, k)
                if n == 0:
                    big_a[j], big_b[j] = ak, bk
                else:
                    big_b[j] = ak * big_b[j] + bk
                    big_a[j] = ak * big_a[j]
        h = {}
        for j in planes:
            c = carry_ref[j, 0:1, :]
            rows = [None] * RG_GROUPS
            for gq in groups:
                rows[gq] = c
                c = big_a[j][gq:gq + 1] * c + big_b[j][gq:gq + 1]
            carry_ref[j, 0:1, :] = c
            h[j] = jnp.concatenate(rows, axis=0)
        for k in order:
            for j in planes:
                h[j] = rows_k(a_pl, j, k) * h[j] + rows_k(b_pl, j, k)
                b_pl[j, k * RG_GROUPS:(k + 1) * RG_GROUPS, :] = h[j]
        for j in planes:
            emit(j, jnp.concatenate(
                [b_pl[j, pl.ds(gq, kg, stride=RG_GROUPS), :] for gq in range(RG_GROUPS)], axis=0))


def _rg_kernel(*refs, reverse):
    if reverse:
        (xc_ref, xl_ref, w_ref, ba_ref, bx_ref, lam_ref, oc_ref, ol_ref,
         a_pl, b_pl, carry_ref) = refs
    else:
        (xc_ref, xl_ref, w_ref, ba_ref, bx_ref, lam_ref, gc_ref, gl_ref, hc_ref, hl_ref,
         oc_ref, ol_ref, a_pl, b_pl, carry_ref) = refs
    s = pl.program_id(1)

    def emitter(o_ref, g_ref, hb_ref):
        def emit(j, hcur):
            tile = slice(j * LANES, (j + 1) * LANES)
            if reverse:
                o_ref[:, tile] = hcur.astype(BF16)
            else:
                o_ref[:, tile] = (g_ref[:, tile].astype(F32)
                                  * (hcur + hb_ref[:, tile].astype(F32))).astype(BF16)
        return emit

    @pl.when(s == 0)
    def _():
        carry_ref[...] = jnp.zeros_like(carry_ref)
        _rg_chunk(xc_ref, w_ref, ba_ref, bx_ref, lam_ref, a_pl, b_pl, carry_ref, reverse,
                  emitter(oc_ref, None if reverse else gc_ref, None if reverse else hc_ref))

    @pl.when(s > 0)
    def _():
        _rg_chunk(xl_ref, w_ref, ba_ref, bx_ref, lam_ref, a_pl, b_pl, carry_ref, reverse,
                  emitter(ol_ref, None if reverse else gl_ref, None if reverse else hl_ref))


def _rg_scan(xc, xl, w, ba, bx, lam, reverse, gate_c=None, gate_l=None, hb_c=None, hb_l=None):
    if reverse:
        chunk = lambda s: RG_NCH - jnp.maximum(s, 1)
    else:
        chunk = lambda s: jnp.maximum(s - 1, 0)
    ctx_spec = pl.BlockSpec((None, CTX_LEN, D_RNN), lambda b, s: (b, 0, 0))
    lat_spec = pl.BlockSpec((None, RG_T, D_RNN), lambda b, s: (b, chunk(s), 0))
    const = lambda shape: _single_buffered(shape, lambda b, s: (0,) * len(shape))
    in_specs = [pl.BlockSpec((RNN_TILES, CTX_LEN, LANES), lambda b, s: (0, b, 0)),
                pl.BlockSpec((RNN_TILES, RG_T, LANES), lambda b, s: (0, b * RG_NCH + chunk(s), 0)),
                const(w.shape), const(ba.shape), const(bx.shape), const(lam.shape)]
    args = [xc, xl, w, ba, bx, lam]
    if not reverse:
        in_specs += [ctx_spec, lat_spec, ctx_spec, lat_spec]
        args += [gate_c, gate_l, hb_c, hb_l]
    return pl.pallas_call(
        functools.partial(_rg_kernel, reverse=reverse),
        grid=(BATCH, 1 + RG_NCH),
        in_specs=in_specs,
        out_specs=[ctx_spec, lat_spec],
        out_shape=[jax.ShapeDtypeStruct((BATCH, CTX_LEN, D_RNN), BF16),
                   jax.ShapeDtypeStruct((BATCH, SEQ, D_RNN), BF16)],
        scratch_shapes=[pltpu.VMEM((RNN_TILES, RG_T, LANES), F32),
                        pltpu.VMEM((RNN_TILES, RG_T, LANES), F32),
                        pltpu.VMEM((RNN_TILES, 8, LANES), F32)],
        compiler_params=_cparams("arbitrary", "arbitrary"),
        name="rglru_bwd" if reverse else "rglru_fwd",
    )(*args)


def _rg_gate_weights(wa, wx):
    def window(w, j):
        ws = _rg_window_start(j)
        win = None
        for h in range(RG_HEADS):
            c0, c1 = max(h * RG_DH, j * LANES), min((h + 1) * RG_DH, (j + 1) * LANES)
            if c0 >= c1:
                continue
            r0 = h * RG_DH - ws
            assert 0 <= r0 and r0 + RG_DH <= RG_WIN
            blk = jnp.pad(w[h, :, c0 - h * RG_DH:c1 - h * RG_DH],
                          ((r0, RG_WIN - RG_DH - r0), (c0 - j * LANES, (j + 1) * LANES - c1)))
            win = blk if win is None else win + blk
        return win

    wins = [jnp.concatenate([window(wa, j), window(wx, j)], axis=1) for j in range(RNN_TILES)]
    return (0.5 * jnp.stack(wins)).astype(BF16)


GRID_H = SEQ // GRID_W
GT_ROWS = 8


GT_PITCH = GRID_W + 8


def _grid_transpose_kernel(x_ref, o_ref, s_ref):
    nt = D_MODEL // LANES
    for t in range(nt):
        for j in range(GT_ROWS):
            s_ref[t, j * GT_PITCH:j * GT_PITCH + GRID_W, :] = (
                x_ref[j * GRID_W:(j + 1) * GRID_W, t * LANES:(t + 1) * LANES])
    for c in range(GRID_W):
        o_ref[c] = jnp.concatenate(
            [s_ref[t, pl.ds(c, GT_ROWS, stride=GT_PITCH), :] for t in range(nt)], axis=1)


def _grid_transpose(x):
    nblk = GRID_H // GT_ROWS
    out = pl.pallas_call(
        _grid_transpose_kernel,
        grid=(BATCH, nblk),
        in_specs=[pl.BlockSpec((GT_ROWS * GRID_W, D_MODEL), lambda b, i: (b * nblk + i, 0))],
        out_specs=pl.BlockSpec((None, GRID_W, GT_ROWS, D_MODEL), lambda b, i: (b, 0, i, 0)),
        out_shape=jax.ShapeDtypeStruct((BATCH, GRID_W, GRID_H, D_MODEL), F32),
        scratch_shapes=[pltpu.VMEM((D_MODEL // LANES, GT_ROWS * GT_PITCH, LANES), F32)],
        compiler_params=_cparams("arbitrary", "arbitrary"),
        name="grid_transpose",
    )(x)
    return out.reshape(BATCH * SEQ, D_MODEL)


def kernel(x, c, ctx, c_ctx, w_ada, b_ada, norm_mix_g, norm_ffn_g, w_in_even, w_out_even, hy_conv_w, hy_conv_b, hy_f1_w, hy_f1_b, hy_f2_w, hy_f2_b, hy_f3_w, hy_f3_b, hy_sin_freq, hy_skip, sgu_ln_g, sgu_w, sgu_b, w_in_odd, rg_conv_w, rg_conv_b, rg_wa, rg_ba, rg_wx, rg_bx, rg_lam, w_out_odd, w_ffn_in, w_ffn_out, final_norm_g):
    mods_all = _ada_mods(c, c_ctx, w_ada, b_ada)
    xs = x.reshape(BATCH * SEQ, D_MODEL)
    cs = ctx.reshape(BATCH * CTX_LEN, D_MODEL)
    lat_row_even = lambda i: i // (SEQ // EVEN_TM)
    lat_row_odd = lambda i: i // (SEQ // ODD_TM)
    lat_row_ffn = lambda i: i // (SEQ // FFN_TM)
    ctx_row = lambda i: CTX_MOD_ROW
    n_tiles = W_A // LANES
    for l in range(DEPTH):
        run_ctx = l < DEPTH - 1
        is_rec = l % 2 == 1
        i = l // 2
        mods = mods_all[l].reshape(MOD_ROWS, 1, N_MOD * D_MODEL)
        g_mix = norm_mix_g[l].reshape(1, D_MODEL)
        if is_rec:
            col_major = i % 2 == 1
            if col_major:
                xs = _grid_transpose(xs)
            w_in = w_in_odd[i].astype(BF16)
            cw, cb = rg_conv_w[i], rg_conv_b[i].reshape(1, D_RNN)
            gate_l, xl = _in_odd(xs, mods, lat_row_odd, g_mix, w_in, cw, cb, SEQ, RG_T)
            gate_c, xc = _in_odd(cs, mods, ctx_row, g_mix, w_in, cw, cb, CTX_LEN, CTX_LEN)
            gate_l = gate_l.reshape(BATCH, SEQ, D_RNN)
            gate_c = gate_c.reshape(BATCH, CTX_LEN, D_RNN)
            row = lambda v: v.reshape(1, D_RNN)
            half = lambda v: 0.5 * row(v)
            hb_c, hb_l = _rg_scan(xc, xl, _rg_gate_weights(rg_wa[i, 1], rg_wx[i, 1]),
                                  half(rg_ba[i, 1]), half(rg_bx[i, 1]), row(rg_lam[i, 1]), True)
            y_c, y_l = _rg_scan(xc, xl, _rg_gate_weights(rg_wa[i, 0], rg_wx[i, 0]),
                                half(rg_ba[i, 0]), half(rg_bx[i, 0]), row(rg_lam[i, 0]), False,
                                gate_c, gate_l, hb_c, hb_l)
            w_out = w_out_odd[i].astype(BF16)
            xs = _out_odd(xs, y_l.reshape(BATCH * SEQ, D_RNN), mods, lat_row_odd, w_out)
            if run_ctx:
                cs = _out_odd(cs, y_c.reshape(BATCH * CTX_LEN, D_RNN), mods, ctx_row, w_out)
        else:
            w_in = w_in_even[i].astype(BF16)
            w_out = w_out_even[i].astype(BF16)
            cw, cb = hy_conv_w[i], hy_conv_b[i].reshape(1, HY_COLS)
            lng = sgu_ln_g[i].reshape(1, W_B)
            sgw = sgu_w[i].astype(BF16)
            sgb = jnp.repeat(sgu_b[i].T, SGU_DH, axis=1)
            skip = hy_skip[i].reshape(1, HY_ORDER * W_A)
            fargs = (hy_f1_w[i], hy_f1_b[i], hy_f2_w[i], hy_f2_b[i], hy_f3_w[i], hy_f3_b[i],
                     hy_sin_freq[i])
            kf = _filter_spectrum(SEQ, *fargs, skip)
            za, yb = _in_even(xs, mods, lat_row_even, g_mix, w_in, cw, cb, lng, sgw, sgb, SEQ, True)
            za = za.reshape(HY_TILES, BATCH, SEQ, LANES)
            y1 = _longconv(za, 0, za, n_tiles, kf, 0)
            ya = _longconv(y1, 0, za, 2 * n_tiles, kf, n_tiles)
            xs = _out_even(xs, ya, yb, mods, lat_row_even, w_out, True)
            if run_ctx:
                kfc = _filter_spectrum(CTX_LEN, *fargs, skip)
                zc, ybc = _in_even(cs, mods, ctx_row, g_mix, w_in, cw, cb, lng, sgw, sgb,
                                   CTX_LEN, False)
                zc = zc.reshape(HY_TILES, BATCH, CTX_LEN, LANES)
                y1c = _ctx_conv(zc, 0, zc, n_tiles, kfc, 0)
                yac = _ctx_conv(y1c, 0, zc, 2 * n_tiles, kfc, n_tiles)
                cs = _out_even(cs, yac, ybc, mods, ctx_row, w_out, False)
        g_ffn = norm_ffn_g[l].reshape(1, D_MODEL)
        w1 = w_ffn_in[l].astype(BF16)
        w2 = w_ffn_out[l].astype(BF16)
        last = l == DEPTH - 1
        xs = _ffn(xs, mods, lat_row_ffn, g_ffn, w1, w2, final_norm_g.reshape(1, D_MODEL), last)
        if run_ctx:
            cs = _ffn(cs, mods, ctx_row, g_ffn, w1, w2, final_norm_g.reshape(1, D_MODEL), False)
        if is_rec and col_major:
            xs = _grid_transpose(xs)
    return xs.reshape(BATCH, SEQ, D_MODEL)
```

```python
import functools
import math

import numpy as np
import jax
import jax.numpy as jnp
from jax import lax
from jax.experimental import pallas as pl
from jax.experimental.pallas import tpu as pltpu

F32 = jnp.float32
BF16 = jnp.bfloat16

D_MODEL = 1024
BATCH = 4
SEQ = 4096
DEPTH = 4
GRID_W = 64
CTX_LEN = 256
N_MOD = 6
NORM_EPS = 1e-6
W_A = D_MODEL // 2
HY_ORDER = 2
HY_SHORT = 3
HY_BANDS = 16
HY_TARGET = 1e-2
HY_FAST_PCT = 0.3
HY_SLOW_PCT = 1.5
W_B = D_MODEL // 2
SGU_GROUPS = 4
SGU_DH = W_B // SGU_GROUPS
CHUNK = 128
D_RNN = ((4 * D_MODEL // 3 + 127) // 128) * 128
RG_HEADS = 16
RG_DH = D_RNN // RG_HEADS
RG_CONV = 4
RG_C = 8.0
D_FF = ((8 * D_MODEL // 3 + 255) // 256) * 256

VMEM_BYTES_V7X = 64 * 1024 * 1024
VMEM_LIMIT = VMEM_BYTES_V7X - 8 * 1024 * 1024
LANES = 128

FFT_N = 2 * SEQ
FFT_N1 = 64
FFT_N2 = 128
FFT_HALF_N1 = FFT_N1 // 2
FFT_PITCH = FFT_N2 + 8
CTX_N = 2 * CTX_LEN


def _cparams(*sem):
    return pltpu.CompilerParams(dimension_semantics=sem, vmem_limit_bytes=VMEM_LIMIT)


def _single_buffered(block_shape, index_map):
    return pl.BlockSpec(block_shape, index_map, pipeline_mode=pl.Buffered(1))


@functools.lru_cache(maxsize=None)
def _dft_tables_f32():
    n = FFT_N
    k1 = np.arange(FFT_N1)
    n2 = np.arange(FFT_N2)

    def angle(n1):
        m = (FFT_N2 * n1[None, None, :] + n2[:, None, None]) * k1[None, :, None]
        return 2.0 * np.pi * (m % n) / n

    a = angle(np.arange(FFT_HALF_N1))
    c, s = np.cos(a), np.sin(a)
    g = np.empty((FFT_N2, FFT_N1, 2, 2, FFT_HALF_N1))
    g[:, :, 0, 0], g[:, :, 0, 1] = c, s
    g[:, :, 1, 0], g[:, :, 1, 1] = -s, c
    g = g.reshape(FFT_N2, 2 * FFT_N1, 2 * FFT_HALF_N1)

    a = angle(np.arange(FFT_N1))
    gf = np.stack([np.cos(a), -np.sin(a)], axis=2).reshape(FFT_N2, 2 * FFT_N1, FFT_N1)

    a = np.transpose(angle(np.arange(FFT_HALF_N1)), (0, 2, 1))
    c, s = np.cos(a) / n, np.sin(a) / n
    h = np.empty((FFT_N2, 2, FFT_HALF_N1, FFT_N1, 2))
    h[:, 0, :, :, 0], h[:, 0, :, :, 1] = c, -s
    h[:, 1, :, :, 0], h[:, 1, :, :, 1] = s, c
    h = h.reshape(FFT_N2, 2 * FFT_HALF_N1, 2 * FFT_N1)

    a = 2.0 * np.pi * np.outer(n2, n2) / FFT_N2
    c, s = np.cos(a), np.sin(a)
    f2 = np.block([[c, s], [-s, c]])
    f2i = np.block([[c, -s], [s, c]])

    kk = np.arange(CTX_N)
    a = 2.0 * np.pi * np.outer(kk, np.arange(CTX_LEN)) / CTX_N
    c, s = np.cos(a), np.sin(a)
    fc = np.block([[c, s], [-s, c]])
    a = 2.0 * np.pi * np.outer(kk, np.arange(CTX_N)) / CTX_N
    fcf = np.concatenate([np.cos(a), -np.sin(a)], axis=0)
    a = 2.0 * np.pi * np.outer(np.arange(CTX_LEN), kk) / CTX_N
    c, s = np.cos(a) / CTX_N, np.sin(a) / CTX_N
    fci = np.block([[c, -s], [s, c]])

    tables = dict(g=g, gf=gf, h=h, f2=f2, f2i=f2i, fc=fc, fcf=fcf, fci=fci)
    return {k: np.asarray(v, dtype=np.float32) for k, v in tables.items()}


def _dft_tables():
    return {k: jnp.asarray(v).astype(BF16) for k, v in _dft_tables_f32().items()}


SEQ_TILE = 1024
N1_PER_TILE = SEQ_TILE // FFT_N2


def _scatter_n2(s_ref, n2, val):
    for p in range(s_ref.shape[0]):
        s_ref[p, pl.ds(n2, 2 * FFT_N1, stride=FFT_PITCH), :] = val[:, p * LANES:(p + 1) * LANES]


def _gather_n2(s_ref, n2):
    return jnp.concatenate(
        [s_ref[p, pl.ds(n2, 2 * FFT_N1, stride=FFT_PITCH), :] for p in range(s_ref.shape[0])],
        axis=1)


def _spectrum_rows(s_ref, k1):
    base = pl.multiple_of(k1 * (2 * FFT_PITCH), 8)
    return base, jnp.concatenate(
        [jnp.concatenate([s_ref[p, pl.ds(base + o, FFT_N2), :] for p in range(s_ref.shape[0])],
                         axis=1) for o in (0, FFT_PITCH)], axis=0)


HY_FFN = 64
FILT_ROWS = 1024
HIGHEST = lax.Precision.HIGHEST


def _tap_rows(row, seq, n2_major):
    if n2_major:
        n2, n1 = row >> int(math.log2(FFT_N1)), row & (FFT_N1 - 1)
        bwd = n1 >= FFT_HALF_N1
        m = FFT_N2 * (n1 - FFT_HALF_N1) + n2
        fwd_pos = FFT_N2 * n1 + n2
    else:
        bwd = row >= seq
        m = row - seq
        fwd_pos = row
    pos = jnp.where(bwd, seq - m, fwd_pos).astype(F32)
    bwd_f = jnp.where(bwd, 1.0, 0.0)
    keep_f = jnp.where(bwd & (m == 0), 0.0, 1.0)
    return pos, bwd_f, keep_f


def _filter_kernel(ec_ref, f1w_ref, f1b_ref, f2w_ref, f2b_ref, sf_ref, w3f_ref, w3b_ref, b3f_ref,
                   b3b_ref, dl_ref, skip_ref, *rest, seq, n2_major):
    if n2_major:
        gf_ref, f2_ref, o_ref, hid_ref, k_ref, s_ref = rest
    else:
        fcf_ref, o_ref, hid_ref, k_ref = rest
    n_rows = 2 * seq
    rb = min(FILT_ROWS, n_rows)

    def rows_of(i):
        r0 = pl.multiple_of(i * rb, rb)
        return r0, _tap_rows(r0 + lax.broadcasted_iota(jnp.int32, (rb, 1), 0), seq, n2_major)

    @pl.when(pl.program_id(0) == 0)
    def _():
        sf = sf_ref[...]

        def hidden(i, carry):
            r0, (pos, _, _) = rows_of(i)
            t = pos * (1.0 / (seq - 1))
            w = pos * (2.0 * math.pi / seq)
            emb = jnp.sin(w * ec_ref[0:1] + ec_ref[1:2]) + t * ec_ref[2:3]
            h = jnp.sin(sf * (jnp.dot(emb, f1w_ref[...], precision=HIGHEST,
                                      preferred_element_type=F32) + f1b_ref[...]))
            h = jnp.sin(sf * (jnp.dot(h.astype(BF16), f2w_ref[...].astype(BF16),
                                      preferred_element_type=F32) + f2b_ref[...]))
            hid_ref[pl.ds(r0, rb), :] = h
            return carry

        lax.fori_loop(0, n_rows // rb, hidden, 0)

    def taps(i, ssq):
        r0, (pos, bwd_f, keep_f) = rows_of(i)
        h = hid_ref[pl.ds(r0, rb), :].astype(BF16)
        kf = jnp.dot(h, w3f_ref[0].astype(BF16), preferred_element_type=F32) + b3f_ref[0]
        kb = jnp.dot(h, w3b_ref[0].astype(BF16), preferred_element_type=F32) + b3b_ref[0]
        window = jnp.exp(-(pos * (1.0 / (seq - 1))) * dl_ref[0])
        k = (kf + bwd_f * (kb - kf)) * window * keep_f
        k_ref[pl.ds(r0, rb), :] = k
        return ssq + jnp.sum(k * k, axis=0, keepdims=True)

    ssq = lax.fori_loop(0, n_rows // rb, taps, jnp.zeros((1, LANES), F32))
    scale = lax.rsqrt(ssq + NORM_EPS)
    skip = skip_ref[...]

    if n2_major:
        def stage1(n2, carry):
            x = k_ref[pl.ds(pl.multiple_of(n2 * FFT_N1, FFT_N1), FFT_N1), :] * scale
            _scatter_n2(s_ref, n2, jnp.dot(gf_ref[n2], x.astype(BF16),
                                           preferred_element_type=F32))
            return carry

        lax.fori_loop(0, FFT_N2, stage1, 0, unroll=8)

        def stage2(k1, carry):
            _, r = _spectrum_rows(s_ref, k1)
            z = jnp.dot(f2_ref[...], r.astype(BF16), preferred_element_type=F32)
            row = pl.multiple_of(k1 * (2 * FFT_N2), 2 * FFT_N2)
            o_ref[pl.ds(row, FFT_N2), :] = z[:FFT_N2] + skip
            o_ref[pl.ds(row + FFT_N2, FFT_N2), :] = z[FFT_N2:]
            return carry

        lax.fori_loop(0, FFT_N1, stage2, 0, unroll=8)
    else:
        z = jnp.dot(fcf_ref[...], (k_ref[...] * scale).astype(BF16), preferred_element_type=F32)
        o_ref[:n_rows, :] = z[:n_rows] + skip
        o_ref[n_rows:, :] = z[n_rows:]


def _filter_spectrum(seq, f1_w, f1_b, f2_w, f2_b, f3_w, f3_b, sin_freq, skip):
    n2_major = seq == SEQ
    t = _dft_tables()
    pad = lambda a, r, c: jnp.pad(a, ((0, r - a.shape[0]), (0, c - a.shape[1])))
    f = jnp.linspace(1e-4, HY_BANDS - 1, HY_BANDS, dtype=F32)
    zeros = lambda n: jnp.zeros((n,), F32)
    ones = lambda n: jnp.ones((n,), F32)
    rest = LANES - 1 - 2 * HY_BANDS
    ec = jnp.stack([
        jnp.concatenate([zeros(1), f, f, zeros(rest)]),
        jnp.concatenate([zeros(1), (math.pi / 2) * ones(HY_BANDS), math.pi * ones(HY_BANDS),
                         zeros(rest)]),
        jnp.concatenate([ones(1), zeros(LANES - 1)]),
    ] + [zeros(LANES)] * 5)
    row = lambda v: pad(v.reshape(1, -1), 1, LANES)
    n_tiles = HY_ORDER * W_A // LANES
    w3 = f3_w.reshape(HY_FFN, HY_ORDER, 2, W_A)
    b3 = f3_b.reshape(HY_ORDER, 2, W_A)
    tiles_w = lambda d: jnp.pad(
        w3[:, :, d, :].reshape(HY_FFN, n_tiles, LANES).transpose(1, 0, 2),
        ((0, 0), (0, LANES - HY_FFN), (0, 0)))
    tiles_b = lambda d: b3[:, d, :].reshape(n_tiles, 1, LANES)
    deltas = jnp.abs(jnp.linspace(math.log(HY_TARGET) / HY_SLOW_PCT,
                                  math.log(HY_TARGET) / HY_FAST_PCT, W_A, dtype=F32))
    dl = jnp.tile(deltas, HY_ORDER).reshape(n_tiles, 1, LANES)
    const = lambda shape: _single_buffered(shape, lambda j: (0,) * len(shape))
    tile = lambda shape: pl.BlockSpec((1,) + shape, lambda j: (j, 0, 0))
    args = [ec, pad(f1_w, LANES, LANES), row(f1_b), pad(f2_w, LANES, LANES), row(f2_b),
            row(sin_freq), tiles_w(0), tiles_w(1), tiles_b(0), tiles_b(1), dl, skip]
    in_specs = [const((8, LANES)), const((LANES, LANES)), const((1, LANES)),
                const((LANES, LANES)), const((1, LANES)), const((1, LANES)),
                tile((LANES, LANES)), tile((LANES, LANES)), tile((1, LANES)), tile((1, LANES)),
                tile((1, LANES)), pl.BlockSpec((1, LANES), lambda j: (0, j))]
    scratch = [pltpu.VMEM((2 * seq, LANES), F32), pltpu.VMEM((2 * seq, LANES), F32)]
    if n2_major:
        args += [t["gf"], t["f2"]]
        in_specs += [const(t["gf"].shape), const(t["f2"].shape)]
        scratch += [pltpu.VMEM((1, 2 * FFT_N1 * FFT_PITCH, LANES), F32)]
    else:
        args += [t["fcf"]]
        in_specs += [const(t["fcf"].shape)]
    return pl.pallas_call(
        functools.partial(_filter_kernel, seq=seq, n2_major=n2_major),
        grid=(n_tiles,),
        in_specs=in_specs,
        out_specs=pl.BlockSpec((4 * seq, LANES), lambda j: (0, j)),
        out_shape=jax.ShapeDtypeStruct((4 * seq, HY_ORDER * W_A), F32),
        scratch_shapes=scratch,
        compiler_params=_cparams("arbitrary"),
        name="hyena_filter" if n2_major else "hyena_ctx_filter",
    )(*args)


LC_OUT_CHUNKS = 8
LC_N2_PER_CHUNK = FFT_N2 // LC_OUT_CHUNKS


def _longconv_kernel(v_ref, gate_ref, kf_ref, g_ref, h_ref, f2_ref, f2i_ref, o_ref, s_ref):
    ct = v_ref.shape[2]
    t = pl.program_id(1)

    @pl.when(t == 0)
    def _():
        def stage1(n2, carry):
            r0 = pl.multiple_of(n2 * N1_PER_TILE, N1_PER_TILE)
            blk = [jnp.concatenate([v_ref[b, pl.ds(i * SEQ_TILE + r0, N1_PER_TILE), :]
                                    for i in range(SEQ // SEQ_TILE)], axis=0)
                   for b in range(BATCH)]
            x = jnp.concatenate([jnp.concatenate([blk[0], blk[2]], axis=1),
                                 jnp.concatenate([blk[1], blk[3]], axis=1)], axis=0)
            _scatter_n2(s_ref, n2, jnp.dot(g_ref[n2], x.astype(BF16), preferred_element_type=F32))
            return carry

        lax.fori_loop(0, FFT_N2, stage1, 0, unroll=8)

        def stage2(k1, carry):
            base, r = _spectrum_rows(s_ref, k1)
            z = jnp.dot(f2_ref[...], r.astype(BF16), preferred_element_type=F32)
            kf = kf_ref[pl.ds(pl.multiple_of(k1 * (2 * FFT_N2), 2 * FFT_N2), 2 * FFT_N2), :]
            kr = jnp.concatenate([kf[:FFT_N2]] * 2, axis=1)
            ki = jnp.concatenate([kf[FFT_N2:]] * 2, axis=1)
            zr, zi = z[:FFT_N2], z[FFT_N2:]
            p = jnp.concatenate([zr * kr - zi * ki, zr * ki + zi * kr], axis=0)
            q = jnp.dot(f2i_ref[...], p.astype(BF16), preferred_element_type=F32)
            for p in range(2):
                s_ref[p, pl.ds(base, FFT_N2), :] = q[:FFT_N2, p * ct:(p + 1) * ct]
                s_ref[p, pl.ds(base + FFT_PITCH, FFT_N2), :] = q[FFT_N2:, p * ct:(p + 1) * ct]
            return carry

        lax.fori_loop(0, FFT_N1, stage2, 0, unroll=4)

    def stage3(j, carry):
        n2 = t * LC_N2_PER_CHUNK + j
        rq = _gather_n2(s_ref, n2)
        y = jnp.dot(h_ref[n2], rq.astype(BF16), preferred_element_type=F32)
        r0 = pl.multiple_of(j * N1_PER_TILE, N1_PER_TILE)
        for b in range(BATCH):
            ri, pair = b % 2, b // 2
            for i in range(SEQ // SEQ_TILE):
                n1 = ri * FFT_HALF_N1 + i * N1_PER_TILE
                yb = y[n1:n1 + N1_PER_TILE, pair * ct:(pair + 1) * ct]
                o_ref[b, i, pl.ds(r0, N1_PER_TILE), :] = (
                    gate_ref[b, i, pl.ds(r0, N1_PER_TILE), :] * yb)
        return carry

    lax.fori_loop(0, LC_N2_PER_CHUNK, stage3, 0, unroll=8)


def _longconv(v, v_tile, gate, gate_tile, kf, kf_col):
    t = _dft_tables()
    ct = LANES
    n_seq_tiles = SEQ // SEQ_TILE
    rows = SEQ_TILE // LC_OUT_CHUNKS
    gate = gate.reshape(gate.shape[0], BATCH, n_seq_tiles, SEQ_TILE, ct)
    out = pl.pallas_call(
        _longconv_kernel,
        grid=(W_A // ct, LC_OUT_CHUNKS),
        in_specs=[
            _single_buffered((None, BATCH, SEQ, ct), lambda j, i: (v_tile + j, 0, 0, 0)),
            pl.BlockSpec((None, BATCH, n_seq_tiles, rows, ct),
                         lambda j, i: (gate_tile + j, 0, 0, i, 0)),
            _single_buffered((2 * FFT_N, ct), lambda j, i: (0, kf_col + j)),
            _single_buffered(t["g"].shape, lambda j, i: (0, 0, 0)),
            _single_buffered(t["h"].shape, lambda j, i: (0, 0, 0)),
            _single_buffered(t["f2"].shape, lambda j, i: (0, 0)),
            _single_buffered(t["f2i"].shape, lambda j, i: (0, 0)),
        ],
        out_specs=pl.BlockSpec((None, BATCH, n_seq_tiles, rows, ct),
                               lambda j, i: (j, 0, 0, i, 0)),
        out_shape=jax.ShapeDtypeStruct((W_A // ct, BATCH, n_seq_tiles, SEQ_TILE, ct), F32),
        scratch_shapes=[pltpu.VMEM((2, 2 * FFT_N1 * FFT_PITCH, ct), F32)],
        compiler_params=_cparams("arbitrary", "arbitrary"),
        name="hyena_longconv",
    )(v, gate, kf, t["g"], t["h"], t["f2"], t["f2i"])
    return out.reshape(W_A // ct, BATCH, SEQ, ct)


def _ctx_conv_kernel(v_ref, gate_ref, kf_ref, fc_ref, fci_ref, o_ref):
    ct = v_ref.shape[2]
    x = jnp.concatenate([jnp.concatenate([v_ref[0], v_ref[2]], axis=1),
                         jnp.concatenate([v_ref[1], v_ref[3]], axis=1)], axis=0)
    z = jnp.dot(fc_ref[...], x.astype(BF16), preferred_element_type=F32)
    kf = kf_ref[...]
    kr = jnp.concatenate([kf[:CTX_N]] * 2, axis=1)
    ki = jnp.concatenate([kf[CTX_N:]] * 2, axis=1)
    zr, zi = z[:CTX_N], z[CTX_N:]
    p = jnp.concatenate([zr * kr - zi * ki, zr * ki + zi * kr], axis=0)
    y = jnp.dot(fci_ref[...], p.astype(BF16), preferred_element_type=F32)
    for b in range(BATCH):
        ri, pair = b % 2, b // 2
        o_ref[b] = gate_ref[b] * y[ri * CTX_LEN:(ri + 1) * CTX_LEN, pair * ct:(pair + 1) * ct]


def _ctx_conv(v, v_tile, gate, gate_tile, kf, kf_col):
    t = _dft_tables()
    ct = LANES
    return pl.pallas_call(
        _ctx_conv_kernel,
        grid=(W_A // ct,),
        in_specs=[
            pl.BlockSpec((None, BATCH, CTX_LEN, ct), lambda j: (v_tile + j, 0, 0, 0)),
            pl.BlockSpec((None, BATCH, CTX_LEN, ct), lambda j: (gate_tile + j, 0, 0, 0)),
            pl.BlockSpec((2 * CTX_N, ct), lambda j: (0, kf_col + j)),
            pl.BlockSpec(t["fc"].shape, lambda j: (0, 0)),
            pl.BlockSpec(t["fci"].shape, lambda j: (0, 0)),
        ],
        out_specs=pl.BlockSpec((None, BATCH, CTX_LEN, ct), lambda j: (j, 0, 0, 0)),
        out_shape=jax.ShapeDtypeStruct((W_A // ct, BATCH, CTX_LEN, ct), F32),
        compiler_params=_cparams("arbitrary"),
        name="hyena_ctx_conv",
    )(v, gate, kf, t["fc"], t["fci"])


MOD_ROWS = 8
CTX_MOD_ROW = BATCH


def _ada_kernel(c_ref, w_ref, b_ref, o_ref):
    cv = c_ref[...]
    s = cv * jax.nn.sigmoid(cv)
    o_ref[0] = jnp.dot(s.astype(BF16), w_ref[0].astype(BF16),
                       preferred_element_type=F32) + b_ref[0]


def _ada_mods(c, c_ctx, w_ada, b_ada):
    cv = jnp.concatenate(
        [c, c_ctx[None], jnp.zeros((MOD_ROWS - BATCH - 1, D_MODEL), F32)], axis=0)
    n = N_MOD * D_MODEL
    tn = n // 4
    return pl.pallas_call(
        _ada_kernel,
        grid=(DEPTH, n // tn),
        in_specs=[
            pl.BlockSpec((MOD_ROWS, D_MODEL), lambda l, j: (0, 0)),
            pl.BlockSpec((1, D_MODEL, tn), lambda l, j: (l, 0, j)),
            pl.BlockSpec((1, 1, tn), lambda l, j: (l, 0, j)),
        ],
        out_specs=pl.BlockSpec((1, MOD_ROWS, tn), lambda l, j: (l, 0, j)),
        out_shape=jax.ShapeDtypeStruct((DEPTH, MOD_ROWS, n), F32),
        compiler_params=_cparams("arbitrary", "arbitrary"),
        name="ada_mods",
    )(cv, w_ada, b_ada.reshape(DEPTH, 1, n))


def _mod_spec(m, row_fn):
    return pl.BlockSpec((1, 1, D_MODEL), lambda i, *_: (row_fn(i), 0, m))


def _rms_mod(x, g, shift, scale):
    ms = jnp.mean(x * x, axis=-1, keepdims=True)
    return (x * lax.rsqrt(ms + NORM_EPS) * g) * (1.0 + scale) + shift


GELU_C = math.sqrt(2.0 / math.pi)


def _gelu(x):
    hx = 0.5 * x
    return hx + hx * jnp.tanh(x * (GELU_C + (GELU_C * 0.044715) * (x * x)))


HALO = 8


def _dwconv(z_all, tm, w, b, left, valid):
    n = z_all.shape[0]
    y = b + w[left:left + 1] * z_all[:tm]
    for k in range(w.shape[0]):
        d = k - left
        if d == 0:
            continue
        s = pltpu.roll(z_all, (-d) % n, 0)[:tm]
        if valid is not None:
            s = s * valid(d)
        y = y + s * w[k:k + 1]
    return y


def _halo_specs(tm, n_rows):
    per = tm // HALO
    last = n_rows // HALO - 1
    return [
        pl.BlockSpec((tm, D_MODEL), lambda i: (i, 0)),
        pl.BlockSpec((HALO, D_MODEL), lambda i: (jnp.maximum(i * per - 1, 0), 0)),
        pl.BlockSpec((HALO, D_MODEL), lambda i: (jnp.minimum((i + 1) * per, last), 0)),
    ]


def _normed_tile(x_ref, xp_ref, xn_ref, sh_ref, sc_ref, g_ref, seq_len):
    tm = x_ref.shape[0]
    g, shift, scale = g_ref[...], sh_ref[0], sc_ref[0]
    h = _rms_mod(x_ref[...], g, shift, scale)
    hn = _rms_mod(xn_ref[...], g, shift, scale)
    hp = _rms_mod(xp_ref[...], g, shift, scale)
    if seq_len >= tm:
        r0 = pl.program_id(0) * tm
        hp = hp * jnp.where((r0 & (seq_len - 1)) == 0, 0.0, 1.0)
        hn = hn * jnp.where(((r0 + tm) & (seq_len - 1)) == 0, 0.0, 1.0)
        valid = None
    else:
        hp, hn = jnp.zeros_like(hp), jnp.zeros_like(hn)
        pos = lax.broadcasted_iota(jnp.int32, (tm, 1), 0) & (seq_len - 1)
        valid = lambda d: jnp.where((pos + d >= 0) & (pos + d < seq_len), 1.0, 0.0)
    return jnp.concatenate([h, hn, hp], axis=0).astype(BF16), valid


EVEN_TM = SEQ_TILE
HY_COLS = 3 * W_A
HY_TILES = HY_COLS // LANES


def _in_even_kernel(x_ref, xp_ref, xn_ref, sh_ref, sc_ref, g_ref, w_ref, cw_ref, cb_ref,
                    lng_ref, sgw_ref, sgb_ref, za_ref, yb_ref, *, seq_len, n2_major):
    tm = x_ref.shape[0]
    h_all, valid = _normed_tile(x_ref, xp_ref, xn_ref, sh_ref, sc_ref, g_ref, seq_len)
    h = h_all[:tm]
    cw = 4 * LANES
    for cc in range(HY_COLS // cw):
        cols = slice(cc * cw, (cc + 1) * cw)
        z_all = jnp.dot(h_all, w_ref[:, cols], preferred_element_type=F32)
        y = _dwconv(z_all, tm, cw_ref[:, cols], cb_ref[:, cols], 1, valid)
        for c in range(cw // LANES):
            tile = cc * (cw // LANES) + c
            yc = y[:, c * LANES:(c + 1) * LANES]
            if n2_major:
                for j in range(N1_PER_TILE):
                    za_ref[tile, pl.ds(j, FFT_N2, stride=N1_PER_TILE), :] = (
                        yc[j * FFT_N2:(j + 1) * FFT_N2])
            else:
                za_ref[tile] = yc
    u = _gelu(jnp.dot(h, w_ref[:, HY_COLS:HY_COLS + W_B], preferred_element_type=F32))
    vb = _gelu(jnp.dot(h, w_ref[:, HY_COLS + W_B:], preferred_element_type=F32))
    vc = vb - jnp.mean(vb, axis=-1, keepdims=True)
    vn = vc * lax.rsqrt(jnp.mean(vc * vc, axis=-1, keepdims=True) + NORM_EPS) * lng_ref[...]
    vn = vn.astype(BF16)
    for ch in range(tm // CHUNK):
        rows = slice(ch * CHUNK, (ch + 1) * CHUNK)
        for q in range(SGU_GROUPS):
            cols = slice(q * SGU_DH, (q + 1) * SGU_DH)
            s = jnp.dot(sgw_ref[q], vn[rows, cols], preferred_element_type=F32) + sgb_ref[:, cols]
            yb_ref[rows, cols] = (u[rows, cols] * s).astype(BF16)


def _in_even(x, mods, mod_row, g, w, cw, cb, lng, sgw, sgb, seq_len, n2_major):
    n_rows = x.shape[0]
    tm = EVEN_TM
    za_shape = (HY_TILES, n_rows, LANES)
    za_spec = pl.BlockSpec((HY_TILES, tm, LANES), lambda i: (0, i, 0))
    const = lambda shape: _single_buffered(shape, lambda i: (0,) * len(shape))
    return pl.pallas_call(
        functools.partial(_in_even_kernel, seq_len=seq_len, n2_major=n2_major),
        grid=(n_rows // tm,),
        in_specs=_halo_specs(tm, n_rows) + [
            _mod_spec(0, mod_row), _mod_spec(1, mod_row),
            const((1, D_MODEL)), const(w.shape), const(cw.shape), const(cb.shape),
            const(lng.shape), const(sgw.shape), const(sgb.shape),
        ],
        out_specs=[za_spec, pl.BlockSpec((tm, W_B), lambda i: (i, 0))],
        out_shape=[jax.ShapeDtypeStruct(za_shape, F32),
                   jax.ShapeDtypeStruct((n_rows, W_B), BF16)],
        compiler_params=_cparams("arbitrary"),
        name="in_proj_even",
    )(x, x, x, mods, mods, g, w, cw, cb, lng, sgw, sgb)


def _out_even_kernel(x_ref, ya_ref, yb_ref, gt_ref, w_ref, o_ref, *, n2_major):
    acc = jnp.dot(yb_ref[...], w_ref[W_A:, :], preferred_element_type=F32)
    for c in range(W_A // LANES):
        if n2_major:
            yac = jnp.concatenate(
                [ya_ref[c, pl.ds(j, FFT_N2, stride=N1_PER_TILE), :]
                 for j in range(N1_PER_TILE)], axis=0)
        else:
            yac = ya_ref[c]
        acc = acc + jnp.dot(yac.astype(BF16), w_ref[c * LANES:(c + 1) * LANES, :],
                            preferred_element_type=F32)
    o_ref[...] = x_ref[...] + gt_ref[0] * acc


def _out_even(x, ya, yb, mods, mod_row, w, n2_major):
    n_rows = x.shape[0]
    tm = EVEN_TM
    nt = W_A // LANES
    ya = ya.reshape(nt, n_rows, LANES)
    ya_spec = pl.BlockSpec((nt, tm, LANES), lambda i: (0, i, 0))
    return pl.pallas_call(
        functools.partial(_out_even_kernel, n2_major=n2_major),
        grid=(n_rows // tm,),
        in_specs=[
            pl.BlockSpec((tm, D_MODEL), lambda i: (i, 0)),
            ya_spec,
            pl.BlockSpec((tm, W_B), lambda i: (i, 0)),
            _mod_spec(2, mod_row),
            _single_buffered(w.shape, lambda i: (0, 0)),
        ],
        out_specs=pl.BlockSpec((tm, D_MODEL), lambda i: (i, 0)),
        out_shape=jax.ShapeDtypeStruct((n_rows, D_MODEL), F32),
        compiler_params=_cparams("arbitrary"),
        name="out_proj_even",
    )(x, ya, yb, mods, w)


FFN_TM = 1024
FFN_SUB = 256


def _ffn_kernel(x_ref, sh_ref, sc_ref, gt_ref, g_ref, w_in_ref, wo_ref, fg_ref, o_ref, *,
                final_norm):
    x = x_ref[...]
    h = _rms_mod(x, g_ref[...], sh_ref[0], sc_ref[0]).astype(BF16)
    acts = []
    for c0 in range(0, D_FF, FFN_SUB):
        hz = 0.5 * jnp.dot(h, w_in_ref[:, c0:c0 + FFN_SUB], preferred_element_type=F32)
        z2 = jnp.dot(h, w_in_ref[:, D_FF + c0:D_FF + c0 + FFN_SUB], preferred_element_type=F32)
        acts.append(((hz + hz * jnp.tanh(hz)) * z2).astype(BF16))
    y = x + gt_ref[0] * jnp.dot(jnp.concatenate(acts, axis=1), wo_ref[...],
                                preferred_element_type=F32)
    if final_norm:
        y = y * lax.rsqrt(jnp.mean(y * y, axis=-1, keepdims=True) + NORM_EPS) * fg_ref[...]
    o_ref[...] = y


def _ffn(x, mods, mod_row, g, w_in, w_out, final_g, final_norm):
    n_rows = x.shape[0]
    tm = FFN_TM
    const = lambda shape: _single_buffered(shape, lambda i: (0,) * len(shape))
    return pl.pallas_call(
        functools.partial(_ffn_kernel, final_norm=final_norm),
        grid=(n_rows // tm,),
        in_specs=[
            pl.BlockSpec((tm, D_MODEL), lambda i: (i, 0)),
            _mod_spec(3, mod_row), _mod_spec(4, mod_row), _mod_spec(5, mod_row),
            const((1, D_MODEL)), const(w_in.shape), const(w_out.shape), const((1, D_MODEL)),
        ],
        out_specs=pl.BlockSpec((tm, D_MODEL), lambda i: (i, 0)),
        out_shape=jax.ShapeDtypeStruct((n_rows, D_MODEL), F32),
        compiler_params=_cparams("arbitrary"),
        name="ffn",
    )(x, mods, mods, mods, g, w_in, w_out, final_g)


ODD_TM = 512
RNN_TILES = D_RNN // LANES


def _in_odd_kernel(x_ref, xp_ref, xn_ref, sh_ref, sc_ref, g_ref, w_ref, cw_ref, cb_ref,
                   gate_ref, xl_ref, *, seq_len, chunk):
    tm = x_ref.shape[0]
    h_all, valid = _normed_tile(x_ref, xp_ref, xn_ref, sh_ref, sc_ref, g_ref, seq_len)
    gate_ref[...] = _gelu(jnp.dot(h_all[:tm], w_ref[:, :D_RNN],
                                  preferred_element_type=F32)).astype(BF16)
    cw = 4 * LANES
    for c0 in range(0, D_RNN, cw):
        cols = slice(c0, min(c0 + cw, D_RNN))
        wcols = slice(D_RNN + cols.start, D_RNN + cols.stop)
        z_all = jnp.dot(h_all, w_ref[:, wcols], preferred_element_type=F32)
        y = _dwconv(z_all, tm, cw_ref[:, cols], cb_ref[:, cols], 2, valid)
        kg = chunk // RG_GROUPS
        for c in range((cols.stop - cols.start) // LANES):
            yc = y[:, c * LANES:(c + 1) * LANES]
            for r0 in range(0, tm, chunk):
                for gq in range(RG_GROUPS):
                    xl_ref[c0 // LANES + c, pl.ds(r0 + gq, kg, stride=RG_GROUPS), :] = (
                        yc[r0 + gq * kg:r0 + (gq + 1) * kg])


def _in_odd(x, mods, mod_row, g, w, cw, cb, seq_len, chunk):
    n_rows = x.shape[0]
    tm = ODD_TM
    const = lambda shape: _single_buffered(shape, lambda i: (0,) * len(shape))
    return pl.pallas_call(
        functools.partial(_in_odd_kernel, seq_len=seq_len, chunk=chunk),
        grid=(n_rows // tm,),
        in_specs=_halo_specs(tm, n_rows) + [
            _mod_spec(0, mod_row), _mod_spec(1, mod_row),
            const((1, D_MODEL)), const(w.shape), const(cw.shape), const(cb.shape),
        ],
        out_specs=[pl.BlockSpec((tm, D_RNN), lambda i: (i, 0)),
                   pl.BlockSpec((RNN_TILES, tm, LANES), lambda i: (0, i, 0))],
        out_shape=[jax.ShapeDtypeStruct((n_rows, D_RNN), BF16),
                   jax.ShapeDtypeStruct((RNN_TILES, n_rows, LANES), F32)],
        compiler_params=_cparams("arbitrary"),
        name="in_proj_odd",
    )(x, x, x, mods, mods, g, w, cw, cb)


RG_T = 512
RG_NCH = SEQ // RG_T
RG_GROUPS = 8
RG_PLANES_PER_PASS = 6
SQRT_GUARD = 1e-30
RG_WIN = 3 * LANES


def _rg_window_start(j):
    return min(max(LANES * (j - 1), 0), D_RNN - RG_WIN)


def _rg_chunk(x_ref, w_ref, ba_ref, bx_ref, lam_ref, a_pl, b_pl, carry_ref, reverse, emit):
    t_rows = x_ref.shape[1]
    kg = t_rows // RG_GROUPS
    xb = [x_ref[j].astype(BF16) for j in range(RNN_TILES)]
    for j in range(RNN_TILES):
        tile = slice(j * LANES, (j + 1) * LANES)
        wt = _rg_window_start(j) // LANES
        pre = jnp.dot(jnp.concatenate(xb[wt:wt + RG_WIN // LANES], axis=1), w_ref[j],
                      preferred_element_type=F32)
        lam = lam_ref[:, tile]
        softplus_neg = jnp.maximum(-lam, 0.0) + jnp.log1p(jnp.exp(-jnp.abs(lam)))
        th_r = jnp.tanh(pre[:, :LANES] + ba_ref[:, tile])
        th_i = jnp.tanh(pre[:, LANES:] + bx_ref[:, tile])
        c3 = (-0.5 * RG_C * math.log2(math.e)) * softplus_neg
        av = jnp.exp2(c3 + c3 * th_r)
        hx = 0.5 * x_ref[j]
        y = 1.0 - av * av
        a_pl[j, 0:t_rows, :] = av
        b_pl[j, 0:t_rows, :] = (y * lax.rsqrt(jnp.maximum(y, SQRT_GUARD))) * (hx + hx * th_i)

    def rows_k(pl_ref, j, k):
        return pl_ref[j, k * RG_GROUPS:(k + 1) * RG_GROUPS, :]

    order = list(range(kg))[::-1] if reverse else list(range(kg))
    groups = list(range(RG_GROUPS))[::-1] if reverse else list(range(RG_GROUPS))
    for j0 in range(0, RNN_TILES, RG_PLANES_PER_PASS):
        planes = range(j0, min(j0 + RG_PLANES_PER_PASS, RNN_TILES))
        big_a, big_b = {}, {}
        for n, k in enumerate(order):
            for j in planes:
                ak, bk = rows_k(a_pl, j, k), rows_k(b_pl, j, k)
                if n == 0:
                    big_a[j], big_b[j] = ak, bk
                else:
                    big_b[j] = ak * big_b[j] + bk
                    big_a[j] = ak * big_a[j]
        h = {}
        for j in planes:
            c = carry_ref[j, 0:1, :]
            rows = [None] * RG_GROUPS
            for gq in groups:
                rows[gq] = c
                c = big_a[j][gq:gq + 1] * c + big_b[j][gq:gq + 1]
            carry_ref[j, 0:1, :] = c
            h[j] = jnp.concatenate(rows, axis=0)
        for k in order:
            for j in planes:
                h[j] = rows_k(a_pl, j, k) * h[j] + rows_k(b_pl, j, k)
                b_pl[j, k * RG_GROUPS:(k + 1) * RG_GROUPS, :] = h[j]
        for j in planes:
            emit(j, jnp.concatenate(
                [b_pl[j, pl.ds(gq, kg, stride=RG_GROUPS), :] for gq in range(RG_GROUPS)], axis=0))


def _rg_bwd_kernel(xc_ref, xl_ref, w_ref, ba_ref, bx_ref, lam_ref, oc_ref, ol_ref,
                   a_pl, b_pl, carry_ref):
    def emitter(o_ref):
        def emit(j, hcur):
            o_ref[:, j * LANES:(j + 1) * LANES] = hcur.astype(BF16)
        return emit

    @pl.when(pl.program_id(1) == 0)
    def _():
        carry_ref[...] = jnp.zeros_like(carry_ref)
        _rg_chunk(xc_ref, w_ref, ba_ref, bx_ref, lam_ref, a_pl, b_pl, carry_ref, True,
                  emitter(oc_ref))

    @pl.when(pl.program_id(1) > 0)
    def _():
        _rg_chunk(xl_ref, w_ref, ba_ref, bx_ref, lam_ref, a_pl, b_pl, carry_ref, True,
                  emitter(ol_ref))


def _rg_fwd_kernel(xc_ref, xl_ref, w_ref, ba_ref, bx_ref, lam_ref, gc_ref, gl_ref, hc_ref, hl_ref,
                   rc_ref, rl_ref, gtc_ref, gtl_ref, wo_ref, oc_ref, ol_ref,
                   a_pl, b_pl, carry_ref, y_ref):
    def run(x_ref, g_ref, hb_ref, res_ref, gt_ref, o_ref):
        t_rows = x_ref.shape[1]

        def emit(j, hcur):
            tile = slice(j * LANES, (j + 1) * LANES)
            y_ref[0:t_rows, tile] = (g_ref[:, tile].astype(F32)
                                     * (hcur + hb_ref[:, tile].astype(F32))).astype(BF16)

        _rg_chunk(x_ref, w_ref, ba_ref, bx_ref, lam_ref, a_pl, b_pl, carry_ref, False, emit)
        acc = jnp.dot(y_ref[0:t_rows, :], wo_ref[...], preferred_element_type=F32)
        o_ref[...] = res_ref[...] + gt_ref[0] * acc

    @pl.when(pl.program_id(1) == 0)
    def _():
        carry_ref[...] = jnp.zeros_like(carry_ref)
        run(xc_ref, gc_ref, hc_ref, rc_ref, gtc_ref, oc_ref)

    @pl.when(pl.program_id(1) > 0)
    def _():
        run(xl_ref, gl_ref, hl_ref, rl_ref, gtl_ref, ol_ref)


def _rg_scan(xc, xl, w, ba, bx, lam, reverse, fwd_args=None):
    if reverse:
        chunk = lambda s: RG_NCH - jnp.maximum(s, 1)
    else:
        chunk = lambda s: jnp.maximum(s - 1, 0)
    ctx_spec = lambda d: pl.BlockSpec((None, CTX_LEN, d), lambda b, s: (b, 0, 0))
    lat_spec = lambda d: pl.BlockSpec((None, RG_T, d), lambda b, s: (b, chunk(s), 0))
    const = lambda shape: _single_buffered(shape, lambda b, s: (0,) * len(shape))
    in_specs = [pl.BlockSpec((RNN_TILES, CTX_LEN, LANES), lambda b, s: (0, b, 0)),
                pl.BlockSpec((RNN_TILES, RG_T, LANES), lambda b, s: (0, b * RG_NCH + chunk(s), 0)),
                const(w.shape), const(ba.shape), const(bx.shape), const(lam.shape)]
    args = [xc, xl, w, ba, bx, lam]
    scratch = [pltpu.VMEM((RNN_TILES, RG_T, LANES), F32),
               pltpu.VMEM((RNN_TILES, RG_T, LANES), F32),
               pltpu.VMEM((RNN_TILES, 8, LANES), F32)]
    if reverse:
        body, name, width, dtype = _rg_bwd_kernel, "rglru_bwd", D_RNN, BF16
    else:
        gate_c, gate_l, hb_c, hb_l, res_c, res_l, mods, w_out = fwd_args
        in_specs += [ctx_spec(D_RNN), lat_spec(D_RNN), ctx_spec(D_RNN), lat_spec(D_RNN),
                     ctx_spec(D_MODEL), lat_spec(D_MODEL),
                     _mod_spec(2, lambda b: CTX_MOD_ROW), _mod_spec(2, lambda b: b),
                     const(w_out.shape)]
        args += [gate_c, gate_l, hb_c, hb_l, res_c, res_l, mods, mods, w_out]
        scratch += [pltpu.VMEM((RG_T, D_RNN), BF16)]
        body, name, width, dtype = _rg_fwd_kernel, "rglru_fwd", D_MODEL, F32
    return pl.pallas_call(
        body,
        grid=(BATCH, 1 + RG_NCH),
        in_specs=in_specs,
        out_specs=[ctx_spec(width), lat_spec(width)],
        out_shape=[jax.ShapeDtypeStruct((BATCH, CTX_LEN, width), dtype),
                   jax.ShapeDtypeStruct((BATCH, SEQ, width), dtype)],
        scratch_shapes=scratch,
        compiler_params=_cparams("arbitrary", "arbitrary"),
        name=name,
    )(*args)


def _rg_gate_weights(wa, wx):
    def window(w, j):
        ws = _rg_window_start(j)
        win = None
        for h in range(RG_HEADS):
            c0, c1 = max(h * RG_DH, j * LANES), min((h + 1) * RG_DH, (j + 1) * LANES)
            if c0 >= c1:
                continue
            r0 = h * RG_DH - ws
            assert 0 <= r0 and r0 + RG_DH <= RG_WIN
            blk = jnp.pad(w[h, :, c0 - h * RG_DH:c1 - h * RG_DH],
                          ((r0, RG_WIN - RG_DH - r0), (c0 - j * LANES, (j + 1) * LANES - c1)))
            win = blk if win is None else win + blk
        return win

    wins = [jnp.concatenate([window(wa, j), window(wx, j)], axis=1) for j in range(RNN_TILES)]
    return (0.5 * jnp.stack(wins)).astype(BF16)


GRID_H = SEQ // GRID_W
GT_ROWS = 8


GT_PITCH = GRID_W + 8


def _grid_transpose_kernel(x_ref, o_ref, s_ref):
    nt = D_MODEL // LANES
    for t in range(nt):
        for j in range(GT_ROWS):
            s_ref[t, j * GT_PITCH:j * GT_PITCH + GRID_W, :] = (
                x_ref[j * GRID_W:(j + 1) * GRID_W, t * LANES:(t + 1) * LANES])
    for c in range(GRID_W):
        o_ref[c] = jnp.concatenate(
            [s_ref[t, pl.ds(c, GT_ROWS, stride=GT_PITCH), :] for t in range(nt)], axis=1)


def _grid_transpose(x):
    nblk = GRID_H // GT_ROWS
    out = pl.pallas_call(
        _grid_transpose_kernel,
        grid=(BATCH, nblk),
        in_specs=[pl.BlockSpec((GT_ROWS * GRID_W, D_MODEL), lambda b, i: (b * nblk + i, 0))],
        out_specs=pl.BlockSpec((None, GRID_W, GT_ROWS, D_MODEL), lambda b, i: (b, 0, i, 0)),
        out_shape=jax.ShapeDtypeStruct((BATCH, GRID_W, GRID_H, D_MODEL), F32),
        scratch_shapes=[pltpu.VMEM((D_MODEL // LANES, GT_ROWS * GT_PITCH, LANES), F32)],
        compiler_params=_cparams("arbitrary", "arbitrary"),
        name="grid_transpose",
    )(x)
    return out.reshape(BATCH * SEQ, D_MODEL)


def kernel(x, c, ctx, c_ctx, w_ada, b_ada, norm_mix_g, norm_ffn_g, w_in_even, w_out_even, hy_conv_w, hy_conv_b, hy_f1_w, hy_f1_b, hy_f2_w, hy_f2_b, hy_f3_w, hy_f3_b, hy_sin_freq, hy_skip, sgu_ln_g, sgu_w, sgu_b, w_in_odd, rg_conv_w, rg_conv_b, rg_wa, rg_ba, rg_wx, rg_bx, rg_lam, w_out_odd, w_ffn_in, w_ffn_out, final_norm_g):
    mods_all = _ada_mods(c, c_ctx, w_ada, b_ada)
    xs = x.reshape(BATCH * SEQ, D_MODEL)
    cs = ctx.reshape(BATCH * CTX_LEN, D_MODEL)
    lat_row_even = lambda i: i // (SEQ // EVEN_TM)
    lat_row_odd = lambda i: i // (SEQ // ODD_TM)
    lat_row_ffn = lambda i: i // (SEQ // FFN_TM)
    ctx_row = lambda i: CTX_MOD_ROW
    n_tiles = W_A // LANES
    for l in range(DEPTH):
        run_ctx = l < DEPTH - 1
        is_rec = l % 2 == 1
        i = l // 2
        mods = mods_all[l].reshape(MOD_ROWS, 1, N_MOD * D_MODEL)
        g_mix = norm_mix_g[l].reshape(1, D_MODEL)
        if is_rec:
            col_major = i % 2 == 1
            if col_major:
                xs = _grid_transpose(xs)
            w_in = w_in_odd[i].astype(BF16)
            cw, cb = rg_conv_w[i], rg_conv_b[i].reshape(1, D_RNN)
            gate_l, xl = _in_odd(xs, mods, lat_row_odd, g_mix, w_in, cw, cb, SEQ, RG_T)
            gate_c, xc = _in_odd(cs, mods, ctx_row, g_mix, w_in, cw, cb, CTX_LEN, CTX_LEN)
            gate_l = gate_l.reshape(BATCH, SEQ, D_RNN)
            gate_c = gate_c.reshape(BATCH, CTX_LEN, D_RNN)
            row = lambda v: v.reshape(1, D_RNN)
            half = lambda v: 0.5 * row(v)
            hb_c, hb_l = _rg_scan(xc, xl, _rg_gate_weights(rg_wa[i, 1], rg_wx[i, 1]),
                                  half(rg_ba[i, 1]), half(rg_bx[i, 1]), row(rg_lam[i, 1]), True)
            cs_new, xs = _rg_scan(
                xc, xl, _rg_gate_weights(rg_wa[i, 0], rg_wx[i, 0]),
                half(rg_ba[i, 0]), half(rg_bx[i, 0]), row(rg_lam[i, 0]), False,
                (gate_c, gate_l, hb_c, hb_l, cs.reshape(BATCH, CTX_LEN, D_MODEL),
                 xs.reshape(BATCH, SEQ, D_MODEL), mods, w_out_odd[i].astype(BF16)))
            xs = xs.reshape(BATCH * SEQ, D_MODEL)
            if run_ctx:
                cs = cs_new.reshape(BATCH * CTX_LEN, D_MODEL)
        else:
            w_in = w_in_even[i].astype(BF16)
            w_out = w_out_even[i].astype(BF16)
            cw, cb = hy_conv_w[i], hy_conv_b[i].reshape(1, HY_COLS)
            lng = sgu_ln_g[i].reshape(1, W_B)
            sgw = sgu_w[i].astype(BF16)
            sgb = jnp.repeat(sgu_b[i].T, SGU_DH, axis=1)
            skip = hy_skip[i].reshape(1, HY_ORDER * W_A)
            fargs = (hy_f1_w[i], hy_f1_b[i], hy_f2_w[i], hy_f2_b[i], hy_f3_w[i], hy_f3_b[i],
                     hy_sin_freq[i])
            kf = _filter_spectrum(SEQ, *fargs, skip)
            za, yb = _in_even(xs, mods, lat_row_even, g_mix, w_in, cw, cb, lng, sgw, sgb, SEQ, True)
            za = za.reshape(HY_TILES, BATCH, SEQ, LANES)
            y1 = _longconv(za, 0, za, n_tiles, kf, 0)
            ya = _longconv(y1, 0, za, 2 * n_tiles, kf, n_tiles)
            xs = _out_even(xs, ya, yb, mods, lat_row_even, w_out, True)
            if run_ctx:
                kfc = _filter_spectrum(CTX_LEN, *fargs, skip)
                zc, ybc = _in_even(cs, mods, ctx_row, g_mix, w_in, cw, cb, lng, sgw, sgb,
                                   CTX_LEN, False)
                zc = zc.reshape(HY_TILES, BATCH, CTX_LEN, LANES)
                y1c = _ctx_conv(zc, 0, zc, n_tiles, kfc, 0)
                yac = _ctx_conv(y1c, 0, zc, 2 * n_tiles, kfc, n_tiles)
                cs = _out_even(cs, yac, ybc, mods, ctx_row, w_out, False)
        g_ffn = norm_ffn_g[l].reshape(1, D_MODEL)
        w1 = w_ffn_in[l].astype(BF16)
        w2 = w_ffn_out[l].astype(BF16)
        last = l == DEPTH - 1
        xs = _ffn(xs, mods, lat_row_ffn, g_ffn, w1, w2, final_norm_g.reshape(1, D_MODEL), last)
        if run_ctx:
            cs = _ffn(cs, mods, ctx_row, g_ffn, w1, w2, final_norm_g.reshape(1, D_MODEL), False)
        if is_rec and col_major:
            xs = _grid_transpose(xs)
    return xs.reshape(BATCH, SEQ, D_MODEL)
```

```python
import functools
import math

import numpy as np
import jax
import jax.numpy as jnp
from jax import lax
from jax.experimental import pallas as pl
from jax.experimental.pallas import tpu as pltpu

F32 = jnp.float32
BF16 = jnp.bfloat16

D_MODEL = 1024
BATCH = 4
SEQ = 4096
DEPTH = 4
GRID_W = 64
CTX_LEN = 256
N_MOD = 6
NORM_EPS = 1e-6
W_A = D_MODEL // 2
HY_ORDER = 2
HY_SHORT = 3
HY_BANDS = 16
HY_TARGET = 1e-2
HY_FAST_PCT = 0.3
HY_SLOW_PCT = 1.5
W_B = D_MODEL // 2
SGU_GROUPS = 4
SGU_DH = W_B // SGU_GROUPS
CHUNK = 128
D_RNN = ((4 * D_MODEL // 3 + 127) // 128) * 128
RG_HEADS = 16
RG_DH = D_RNN // RG_HEADS
RG_CONV = 4
RG_C = 8.0
D_FF = ((8 * D_MODEL // 3 + 255) // 256) * 256

VMEM_BYTES_V7X = 64 * 1024 * 1024
VMEM_LIMIT = VMEM_BYTES_V7X - 4 * 1024 * 1024
LANES = 128

FFT_N = 2 * SEQ
FFT_N1 = 64
FFT_N2 = 128
FFT_HALF_N1 = FFT_N1 // 2
FFT_PITCH = FFT_N2 + 8
CTX_N = 2 * CTX_LEN


def _cparams(*sem):
    return pltpu.CompilerParams(dimension_semantics=sem, vmem_limit_bytes=VMEM_LIMIT)


def _single_buffered(block_shape, index_map):
    return pl.BlockSpec(block_shape, index_map, pipeline_mode=pl.Buffered(1))


@functools.lru_cache(maxsize=None)
def _dft_tables_f32():
    n = FFT_N
    k1 = np.arange(FFT_N1)
    n2 = np.arange(FFT_N2)

    def angle(n1):
        m = (FFT_N2 * n1[None, None, :] + n2[:, None, None]) * k1[None, :, None]
        return 2.0 * np.pi * (m % n) / n

    a = angle(np.arange(FFT_HALF_N1))
    c, s = np.cos(a), np.sin(a)
    g = np.empty((FFT_N2, FFT_N1, 2, 2, FFT_HALF_N1))
    g[:, :, 0, 0], g[:, :, 0, 1] = c, s
    g[:, :, 1, 0], g[:, :, 1, 1] = -s, c
    g = g.reshape(FFT_N2, 2 * FFT_N1, 2 * FFT_HALF_N1)

    a = angle(np.arange(FFT_N1))
    gf = np.stack([np.cos(a), -np.sin(a)], axis=2).reshape(FFT_N2, 2 * FFT_N1, FFT_N1)

    a = np.transpose(angle(np.arange(FFT_HALF_N1)), (0, 2, 1))
    c, s = np.cos(a) / n, np.sin(a) / n
    h = np.empty((FFT_N2, 2, FFT_HALF_N1, FFT_N1, 2))
    h[:, 0, :, :, 0], h[:, 0, :, :, 1] = c, -s
    h[:, 1, :, :, 0], h[:, 1, :, :, 1] = s, c
    h = h.reshape(FFT_N2, 2 * FFT_HALF_N1, 2 * FFT_N1)

    a = 2.0 * np.pi * np.outer(n2, n2) / FFT_N2
    c, s = np.cos(a), np.sin(a)
    f2 = np.block([[c, s], [-s, c]])
    f2i = np.block([[c, -s], [s, c]])

    kk = np.arange(CTX_N)
    a = 2.0 * np.pi * np.outer(kk, np.arange(CTX_LEN)) / CTX_N
    c, s = np.cos(a), np.sin(a)
    fc = np.block([[c, s], [-s, c]])
    a = 2.0 * np.pi * np.outer(kk, np.arange(CTX_N)) / CTX_N
    fcf = np.concatenate([np.cos(a), -np.sin(a)], axis=0)
    a = 2.0 * np.pi * np.outer(np.arange(CTX_LEN), kk) / CTX_N
    c, s = np.cos(a) / CTX_N, np.sin(a) / CTX_N
    fci = np.block([[c, -s], [s, c]])

    tables = dict(g=g, gf=gf, h=h, f2=f2, f2i=f2i, fc=fc, fcf=fcf, fci=fci)
    return {k: np.asarray(v, dtype=np.float32) for k, v in tables.items()}


def _dft_tables():
    return {k: jnp.asarray(v).astype(BF16) for k, v in _dft_tables_f32().items()}


SEQ_TILE = 1024
N1_PER_TILE = SEQ_TILE // FFT_N2


def _scatter_n2(s_ref, n2, val):
    for p in range(s_ref.shape[0]):
        s_ref[p, pl.ds(n2, 2 * FFT_N1, stride=FFT_PITCH), :] = val[:, p * LANES:(p + 1) * LANES]


def _gather_n2(s_ref, n2):
    return jnp.concatenate(
        [s_ref[p, pl.ds(n2, 2 * FFT_N1, stride=FFT_PITCH), :] for p in range(s_ref.shape[0])],
        axis=1)


def _spectrum_rows(s_ref, k1):
    base = pl.multiple_of(k1 * (2 * FFT_PITCH), 8)
    return base, jnp.concatenate(
        [jnp.concatenate([s_ref[p, pl.ds(base + o, FFT_N2), :] for p in range(s_ref.shape[0])],
                         axis=1) for o in (0, FFT_PITCH)], axis=0)


HY_FFN = 64
FILT_ROWS = 1024
HIGHEST = lax.Precision.HIGHEST


def _tap_rows(row, seq, n2_major):
    if n2_major:
        n2, n1 = row >> int(math.log2(FFT_N1)), row & (FFT_N1 - 1)
        bwd = n1 >= FFT_HALF_N1
        m = FFT_N2 * (n1 - FFT_HALF_N1) + n2
        fwd_pos = FFT_N2 * n1 + n2
    else:
        bwd = row >= seq
        m = row - seq
        fwd_pos = row
    pos = jnp.where(bwd, seq - m, fwd_pos).astype(F32)
    bwd_f = jnp.where(bwd, 1.0, 0.0)
    keep_f = jnp.where(bwd & (m == 0), 0.0, 1.0)
    return pos, bwd_f, keep_f


def _filter_kernel(ec_ref, f1w_ref, f1b_ref, f2w_ref, f2b_ref, sf_ref, w3f_ref, w3b_ref, b3f_ref,
                   b3b_ref, dl_ref, skip_ref, *rest, seq, n2_major):
    if n2_major:
        gf_ref, f2_ref, o_ref, hid_ref, k_ref, s_ref = rest
    else:
        fcf_ref, o_ref, hid_ref, k_ref = rest
    n_rows = 2 * seq
    rb = min(FILT_ROWS, n_rows)

    def rows_of(i):
        r0 = pl.multiple_of(i * rb, rb)
        return r0, _tap_rows(r0 + lax.broadcasted_iota(jnp.int32, (rb, 1), 0), seq, n2_major)

    @pl.when(pl.program_id(0) == 0)
    def _():
        sf = sf_ref[...]

        def hidden(i, carry):
            r0, (pos, _, _) = rows_of(i)
            t = pos * (1.0 / (seq - 1))
            w = pos * (2.0 * math.pi / seq)
            emb = jnp.sin(w * ec_ref[0:1] + ec_ref[1:2]) + t * ec_ref[2:3]
            h = jnp.sin(sf * (jnp.dot(emb, f1w_ref[...], precision=HIGHEST,
                                      preferred_element_type=F32) + f1b_ref[...]))
            h = jnp.sin(sf * (jnp.dot(h.astype(BF16), f2w_ref[...].astype(BF16),
                                      preferred_element_type=F32) + f2b_ref[...]))
            hid_ref[pl.ds(r0, rb), :] = h
            return carry

        lax.fori_loop(0, n_rows // rb, hidden, 0)

    def taps(i, ssq):
        r0, (pos, bwd_f, keep_f) = rows_of(i)
        h = hid_ref[pl.ds(r0, rb), :].astype(BF16)
        kf = jnp.dot(h, w3f_ref[0].astype(BF16), preferred_element_type=F32) + b3f_ref[0]
        kb = jnp.dot(h, w3b_ref[0].astype(BF16), preferred_element_type=F32) + b3b_ref[0]
        window = jnp.exp(-(pos * (1.0 / (seq - 1))) * dl_ref[0])
        k = (kf + bwd_f * (kb - kf)) * window * keep_f
        k_ref[pl.ds(r0, rb), :] = k
        return ssq + jnp.sum(k * k, axis=0, keepdims=True)

    ssq = lax.fori_loop(0, n_rows // rb, taps, jnp.zeros((1, LANES), F32))
    scale = lax.rsqrt(ssq + NORM_EPS)
    skip = skip_ref[...]

    if n2_major:
        def stage1(n2, carry):
            x = k_ref[pl.ds(pl.multiple_of(n2 * FFT_N1, FFT_N1), FFT_N1), :] * scale
            _scatter_n2(s_ref, n2, jnp.dot(gf_ref[n2], x.astype(BF16),
                                           preferred_element_type=F32))
            return carry

        lax.fori_loop(0, FFT_N2, stage1, 0, unroll=8)

        def stage2(k1, carry):
            _, r = _spectrum_rows(s_ref, k1)
            z = jnp.dot(f2_ref[...], r.astype(BF16), preferred_element_type=F32)
            row = pl.multiple_of(k1 * (2 * FFT_N2), 2 * FFT_N2)
            o_ref[pl.ds(row, FFT_N2), :] = z[:FFT_N2] + skip
            o_ref[pl.ds(row + FFT_N2, FFT_N2), :] = z[FFT_N2:]
            return carry

        lax.fori_loop(0, FFT_N1, stage2, 0, unroll=8)
    else:
        z = jnp.dot(fcf_ref[...], (k_ref[...] * scale).astype(BF16), preferred_element_type=F32)
        o_ref[:n_rows, :] = z[:n_rows] + skip
        o_ref[n_rows:, :] = z[n_rows:]


def _filter_spectrum(seq, f1_w, f1_b, f2_w, f2_b, f3_w, f3_b, sin_freq, skip):
    n2_major = seq == SEQ
    t = _dft_tables()
    pad = lambda a, r, c: jnp.pad(a, ((0, r - a.shape[0]), (0, c - a.shape[1])))
    f = jnp.linspace(1e-4, HY_BANDS - 1, HY_BANDS, dtype=F32)
    zeros = lambda n: jnp.zeros((n,), F32)
    ones = lambda n: jnp.ones((n,), F32)
    rest = LANES - 1 - 2 * HY_BANDS
    ec = jnp.stack([
        jnp.concatenate([zeros(1), f, f, zeros(rest)]),
        jnp.concatenate([zeros(1), (math.pi / 2) * ones(HY_BANDS), math.pi * ones(HY_BANDS),
                         zeros(rest)]),
        jnp.concatenate([ones(1), zeros(LANES - 1)]),
    ] + [zeros(LANES)] * 5)
    row = lambda v: pad(v.reshape(1, -1), 1, LANES)
    n_tiles = HY_ORDER * W_A // LANES
    w3 = f3_w.reshape(HY_FFN, HY_ORDER, 2, W_A)
    b3 = f3_b.reshape(HY_ORDER, 2, W_A)
    tiles_w = lambda d: jnp.pad(
        w3[:, :, d, :].reshape(HY_FFN, n_tiles, LANES).transpose(1, 0, 2),
        ((0, 0), (0, LANES - HY_FFN), (0, 0)))
    tiles_b = lambda d: b3[:, d, :].reshape(n_tiles, 1, LANES)
    deltas = jnp.abs(jnp.linspace(math.log(HY_TARGET) / HY_SLOW_PCT,
                                  math.log(HY_TARGET) / HY_FAST_PCT, W_A, dtype=F32))
    dl = jnp.tile(deltas, HY_ORDER).reshape(n_tiles, 1, LANES)
    const = lambda shape: _single_buffered(shape, lambda j: (0,) * len(shape))
    tile = lambda shape: pl.BlockSpec((1,) + shape, lambda j: (j, 0, 0))
    args = [ec, pad(f1_w, LANES, LANES), row(f1_b), pad(f2_w, LANES, LANES), row(f2_b),
            row(sin_freq), tiles_w(0), tiles_w(1), tiles_b(0), tiles_b(1), dl, skip]
    in_specs = [const((8, LANES)), const((LANES, LANES)), const((1, LANES)),
                const((LANES, LANES)), const((1, LANES)), const((1, LANES)),
                tile((LANES, LANES)), tile((LANES, LANES)), tile((1, LANES)), tile((1, LANES)),
                tile((1, LANES)), pl.BlockSpec((1, LANES), lambda j: (0, j))]
    scratch = [pltpu.VMEM((2 * seq, LANES), F32), pltpu.VMEM((2 * seq, LANES), F32)]
    if n2_major:
        args += [t["gf"], t["f2"]]
        in_specs += [const(t["gf"].shape), const(t["f2"].shape)]
        scratch += [pltpu.VMEM((1, 2 * FFT_N1 * FFT_PITCH, LANES), F32)]
    else:
        args += [t["fcf"]]
        in_specs += [const(t["fcf"].shape)]
    return pl.pallas_call(
        functools.partial(_filter_kernel, seq=seq, n2_major=n2_major),
        grid=(n_tiles,),
        in_specs=in_specs,
        out_specs=pl.BlockSpec((4 * seq, LANES), lambda j: (0, j)),
        out_shape=jax.ShapeDtypeStruct((4 * seq, HY_ORDER * W_A), F32),
        scratch_shapes=scratch,
        compiler_params=_cparams("arbitrary"),
        name="hyena_filter" if n2_major else "hyena_ctx_filter",
    )(*args)


LC_OUT_CHUNKS = 8
LC_N2_PER_CHUNK = FFT_N2 // LC_OUT_CHUNKS


def _longconv_kernel(v_ref, gate_ref, kf_ref, g_ref, h_ref, f2_ref, f2i_ref, o_ref, s_ref):
    ct = v_ref.shape[2]
    t = pl.program_id(1)

    @pl.when(t == 0)
    def _():
        def stage1(n2, carry):
            r0 = pl.multiple_of(n2 * N1_PER_TILE, N1_PER_TILE)
            blk = [jnp.concatenate([v_ref[b, pl.ds(i * SEQ_TILE + r0, N1_PER_TILE), :]
                                    for i in range(SEQ // SEQ_TILE)], axis=0)
                   for b in range(BATCH)]
            x = jnp.concatenate([jnp.concatenate([blk[0], blk[2]], axis=1),
                                 jnp.concatenate([blk[1], blk[3]], axis=1)], axis=0)
            _scatter_n2(s_ref, n2, jnp.dot(g_ref[n2], x.astype(BF16), preferred_element_type=F32))
            return carry

        lax.fori_loop(0, FFT_N2, stage1, 0, unroll=8)

        def stage2(k1, carry):
            base, r = _spectrum_rows(s_ref, k1)
            z = jnp.dot(f2_ref[...], r.astype(BF16), preferred_element_type=F32)
            kf = kf_ref[pl.ds(pl.multiple_of(k1 * (2 * FFT_N2), 2 * FFT_N2), 2 * FFT_N2), :]
            kr = jnp.concatenate([kf[:FFT_N2]] * 2, axis=1)
            ki = jnp.concatenate([kf[FFT_N2:]] * 2, axis=1)
            zr, zi = z[:FFT_N2], z[FFT_N2:]
            p = jnp.concatenate([zr * kr - zi * ki, zr * ki + zi * kr], axis=0)
            q = jnp.dot(f2i_ref[...], p.astype(BF16), preferred_element_type=F32)
            for p in range(2):
                s_ref[p, pl.ds(base, FFT_N2), :] = q[:FFT_N2, p * ct:(p + 1) * ct]
                s_ref[p, pl.ds(base + FFT_PITCH, FFT_N2), :] = q[FFT_N2:, p * ct:(p + 1) * ct]
            return carry

        lax.fori_loop(0, FFT_N1, stage2, 0, unroll=4)

    def stage3(j, carry):
        n2 = t * LC_N2_PER_CHUNK + j
        rq = _gather_n2(s_ref, n2)
        y = jnp.dot(h_ref[n2], rq.astype(BF16), preferred_element_type=F32)
        r0 = pl.multiple_of(j * N1_PER_TILE, N1_PER_TILE)
        for b in range(BATCH):
            ri, pair = b % 2, b // 2
            for i in range(SEQ // SEQ_TILE):
                n1 = ri * FFT_HALF_N1 + i * N1_PER_TILE
                yb = y[n1:n1 + N1_PER_TILE, pair * ct:(pair + 1) * ct]
                o_ref[b, i, pl.ds(r0, N1_PER_TILE), :] = (
                    gate_ref[b, i, pl.ds(r0, N1_PER_TILE), :] * yb)
        return carry

    lax.fori_loop(0, LC_N2_PER_CHUNK, stage3, 0, unroll=8)


def _longconv(v, v_tile, gate, gate_tile, kf, kf_col):
    t = _dft_tables()
    ct = LANES
    n_seq_tiles = SEQ // SEQ_TILE
    rows = SEQ_TILE // LC_OUT_CHUNKS
    gate = gate.reshape(gate.shape[0], BATCH, n_seq_tiles, SEQ_TILE, ct)
    out = pl.pallas_call(
        _longconv_kernel,
        grid=(W_A // ct, LC_OUT_CHUNKS),
        in_specs=[
            _single_buffered((None, BATCH, SEQ, ct), lambda j, i: (v_tile + j, 0, 0, 0)),
            pl.BlockSpec((None, BATCH, n_seq_tiles, rows, ct),
                         lambda j, i: (gate_tile + j, 0, 0, i, 0)),
            _single_buffered((2 * FFT_N, ct), lambda j, i: (0, kf_col + j)),
            _single_buffered(t["g"].shape, lambda j, i: (0, 0, 0)),
            _single_buffered(t["h"].shape, lambda j, i: (0, 0, 0)),
            _single_buffered(t["f2"].shape, lambda j, i: (0, 0)),
            _single_buffered(t["f2i"].shape, lambda j, i: (0, 0)),
        ],
        out_specs=pl.BlockSpec((None, BATCH, n_seq_tiles, rows, ct),
                               lambda j, i: (j, 0, 0, i, 0)),
        out_shape=jax.ShapeDtypeStruct((W_A // ct, BATCH, n_seq_tiles, SEQ_TILE, ct), F32),
        scratch_shapes=[pltpu.VMEM((2, 2 * FFT_N1 * FFT_PITCH, ct), F32)],
        compiler_params=_cparams("arbitrary", "arbitrary"),
        name="hyena_longconv",
    )(v, gate, kf, t["g"], t["h"], t["f2"], t["f2i"])
    return out.reshape(W_A // ct, BATCH, SEQ, ct)


def _ctx_conv_kernel(v_ref, gate_ref, kf_ref, fc_ref, fci_ref, o_ref):
    ct = v_ref.shape[2]
    x = jnp.concatenate([jnp.concatenate([v_ref[0], v_ref[2]], axis=1),
                         jnp.concatenate([v_ref[1], v_ref[3]], axis=1)], axis=0)
    z = jnp.dot(fc_ref[...], x.astype(BF16), preferred_element_type=F32)
    kf = kf_ref[...]
    kr = jnp.concatenate([kf[:CTX_N]] * 2, axis=1)
    ki = jnp.concatenate([kf[CTX_N:]] * 2, axis=1)
    zr, zi = z[:CTX_N], z[CTX_N:]
    p = jnp.concatenate([zr * kr - zi * ki, zr * ki + zi * kr], axis=0)
    y = jnp.dot(fci_ref[...], p.astype(BF16), preferred_element_type=F32)
    for b in range(BATCH):
        ri, pair = b % 2, b // 2
        o_ref[b] = gate_ref[b] * y[ri * CTX_LEN:(ri + 1) * CTX_LEN, pair * ct:(pair + 1) * ct]


def _ctx_conv(v, v_tile, gate, gate_tile, kf, kf_col):
    t = _dft_tables()
    ct = LANES
    return pl.pallas_call(
        _ctx_conv_kernel,
        grid=(W_A // ct,),
        in_specs=[
            pl.BlockSpec((None, BATCH, CTX_LEN, ct), lambda j: (v_tile + j, 0, 0, 0)),
            pl.BlockSpec((None, BATCH, CTX_LEN, ct), lambda j: (gate_tile + j, 0, 0, 0)),
            pl.BlockSpec((2 * CTX_N, ct), lambda j: (0, kf_col + j)),
            pl.BlockSpec(t["fc"].shape, lambda j: (0, 0)),
            pl.BlockSpec(t["fci"].shape, lambda j: (0, 0)),
        ],
        out_specs=pl.BlockSpec((None, BATCH, CTX_LEN, ct), lambda j: (j, 0, 0, 0)),
        out_shape=jax.ShapeDtypeStruct((W_A // ct, BATCH, CTX_LEN, ct), F32),
        compiler_params=_cparams("arbitrary"),
        name="hyena_ctx_conv",
    )(v, gate, kf, t["fc"], t["fci"])


MOD_ROWS = 8
CTX_MOD_ROW = BATCH


def _ada_kernel(c_ref, w_ref, b_ref, o_ref):
    cv = c_ref[...]
    s = cv * jax.nn.sigmoid(cv)
    o_ref[0] = jnp.dot(s.astype(BF16), w_ref[0].astype(BF16),
                       preferred_element_type=F32) + b_ref[0]


def _ada_mods(c, c_ctx, w_ada, b_ada):
    cv = jnp.concatenate(
        [c, c_ctx[None], jnp.zeros((MOD_ROWS - BATCH - 1, D_MODEL), F32)], axis=0)
    n = N_MOD * D_MODEL
    tn = n // 4
    return pl.pallas_call(
        _ada_kernel,
        grid=(DEPTH, n // tn),
        in_specs=[
            pl.BlockSpec((MOD_ROWS, D_MODEL), lambda l, j: (0, 0)),
            pl.BlockSpec((1, D_MODEL, tn), lambda l, j: (l, 0, j)),
            pl.BlockSpec((1, 1, tn), lambda l, j: (l, 0, j)),
        ],
        out_specs=pl.BlockSpec((1, MOD_ROWS, tn), lambda l, j: (l, 0, j)),
        out_shape=jax.ShapeDtypeStruct((DEPTH, MOD_ROWS, n), F32),
        compiler_params=_cparams("arbitrary", "arbitrary"),
        name="ada_mods",
    )(cv, w_ada, b_ada.reshape(DEPTH, 1, n))


def _mod_spec(m, row_fn):
    return pl.BlockSpec((1, 1, D_MODEL), lambda i, *_: (row_fn(i), 0, m))


def _rms_mod(x, g, shift, scale):
    ms = jnp.mean(x * x, axis=-1, keepdims=True)
    return (x * lax.rsqrt(ms + NORM_EPS) * g) * (1.0 + scale) + shift


GELU_C = math.sqrt(2.0 / math.pi)


def _gelu(x):
    hx = 0.5 * x
    return hx + hx * jnp.tanh(x * (GELU_C + (GELU_C * 0.044715) * (x * x)))


HALO = 8


def _dwconv(z_all, tm, w, b, left, valid):
    n = z_all.shape[0]
    y = b + w[left:left + 1] * z_all[:tm]
    for k in range(w.shape[0]):
        d = k - left
        if d == 0:
            continue
        s = pltpu.roll(z_all, (-d) % n, 0)[:tm]
        if valid is not None:
            s = s * valid(d)
        y = y + s * w[k:k + 1]
    return y


def _halo_specs(tm, n_rows):
    per = tm // HALO
    last = n_rows // HALO - 1
    return [
        pl.BlockSpec((tm, D_MODEL), lambda i: (i, 0)),
        pl.BlockSpec((HALO, D_MODEL), lambda i: (jnp.maximum(i * per - 1, 0), 0)),
        pl.BlockSpec((HALO, D_MODEL), lambda i: (jnp.minimum((i + 1) * per, last), 0)),
    ]


def _normed_tile(x_ref, xp_ref, xn_ref, sh_ref, sc_ref, g_ref, seq_len):
    tm = x_ref.shape[0]
    g, shift, scale = g_ref[...], sh_ref[0], sc_ref[0]
    h = _rms_mod(x_ref[...], g, shift, scale)
    hn = _rms_mod(xn_ref[...], g, shift, scale)
    hp = _rms_mod(xp_ref[...], g, shift, scale)
    if seq_len >= tm:
        r0 = pl.program_id(0) * tm
        hp = hp * jnp.where((r0 & (seq_len - 1)) == 0, 0.0, 1.0)
        hn = hn * jnp.where(((r0 + tm) & (seq_len - 1)) == 0, 0.0, 1.0)
        valid = None
    else:
        hp, hn = jnp.zeros_like(hp), jnp.zeros_like(hn)
        pos = lax.broadcasted_iota(jnp.int32, (tm, 1), 0) & (seq_len - 1)
        valid = lambda d: jnp.where((pos + d >= 0) & (pos + d < seq_len), 1.0, 0.0)
    return jnp.concatenate([h, hn, hp], axis=0).astype(BF16), valid


EVEN_TM = SEQ_TILE
HY_COLS = 3 * W_A
HY_TILES = HY_COLS // LANES


def _in_even_kernel(x_ref, xp_ref, xn_ref, sh_ref, sc_ref, g_ref, w_ref, cw_ref, cb_ref,
                    lng_ref, sgw_ref, sgb_ref, za_ref, yb_ref, *, seq_len, n2_major):
    tm = x_ref.shape[0]
    h_all, valid = _normed_tile(x_ref, xp_ref, xn_ref, sh_ref, sc_ref, g_ref, seq_len)
    h = h_all[:tm]
    cw = 4 * LANES
    for cc in range(HY_COLS // cw):
        cols = slice(cc * cw, (cc + 1) * cw)
        z_all = jnp.dot(h_all, w_ref[:, cols], preferred_element_type=F32)
        y = _dwconv(z_all, tm, cw_ref[:, cols], cb_ref[:, cols], 1, valid)
        for c in range(cw // LANES):
            tile = cc * (cw // LANES) + c
            yc = y[:, c * LANES:(c + 1) * LANES]
            if n2_major:
                for j in range(N1_PER_TILE):
                    za_ref[tile, pl.ds(j, FFT_N2, stride=N1_PER_TILE), :] = (
                        yc[j * FFT_N2:(j + 1) * FFT_N2])
            else:
                za_ref[tile] = yc
    u = _gelu(jnp.dot(h, w_ref[:, HY_COLS:HY_COLS + W_B], preferred_element_type=F32))
    vb = _gelu(jnp.dot(h, w_ref[:, HY_COLS + W_B:], preferred_element_type=F32))
    vc = vb - jnp.mean(vb, axis=-1, keepdims=True)
    vn = vc * lax.rsqrt(jnp.mean(vc * vc, axis=-1, keepdims=True) + NORM_EPS) * lng_ref[...]
    vn = vn.astype(BF16)
    for ch in range(tm // CHUNK):
        rows = slice(ch * CHUNK, (ch + 1) * CHUNK)
        for q in range(SGU_GROUPS):
            cols = slice(q * SGU_DH, (q + 1) * SGU_DH)
            s = jnp.dot(sgw_ref[q], vn[rows, cols], preferred_element_type=F32) + sgb_ref[:, cols]
            yb_ref[rows, cols] = (u[rows, cols] * s).astype(BF16)


def _in_even(x, mods, mod_row, g, w, cw, cb, lng, sgw, sgb, seq_len, n2_major):
    n_rows = x.shape[0]
    tm = EVEN_TM
    za_shape = (HY_TILES, n_rows, LANES)
    za_spec = pl.BlockSpec((HY_TILES, tm, LANES), lambda i: (0, i, 0))
    const = lambda shape: _single_buffered(shape, lambda i: (0,) * len(shape))
    return pl.pallas_call(
        functools.partial(_in_even_kernel, seq_len=seq_len, n2_major=n2_major),
        grid=(n_rows // tm,),
        in_specs=_halo_specs(tm, n_rows) + [
            _mod_spec(0, mod_row), _mod_spec(1, mod_row),
            const((1, D_MODEL)), const(w.shape), const(cw.shape), const(cb.shape),
            const(lng.shape), const(sgw.shape), const(sgb.shape),
        ],
        out_specs=[za_spec, pl.BlockSpec((tm, W_B), lambda i: (i, 0))],
        out_shape=[jax.ShapeDtypeStruct(za_shape, F32),
                   jax.ShapeDtypeStruct((n_rows, W_B), BF16)],
        compiler_params=_cparams("arbitrary"),
        name="in_proj_even",
    )(x, x, x, mods, mods, g, w, cw, cb, lng, sgw, sgb)


FFN_TM = SEQ_TILE
FFN_SUB = 256
GRID_H = SEQ // GRID_W
GT_ROWS = FFN_TM // GRID_W
GT_PITCH = GRID_W + 8


def _even_mix(ya_ref, yb_ref, w_ref, n2_major):
    acc = jnp.dot(yb_ref[...], w_ref[W_A:, :], preferred_element_type=F32)
    for c in range(W_A // LANES):
        if n2_major:
            yac = jnp.concatenate(
                [ya_ref[c, pl.ds(j, FFT_N2, stride=N1_PER_TILE), :]
                 for j in range(N1_PER_TILE)], axis=0)
        else:
            yac = ya_ref[c]
        acc = acc + jnp.dot(yac.astype(BF16), w_ref[c * LANES:(c + 1) * LANES, :],
                            preferred_element_type=F32)
    return acc


def _ffn_kernel(*refs, final_norm, mix, n2_major, swap_grid):
    refs = list(refs)
    x_ref, sh_ref, sc_ref, gt_ref, g_ref, w_in_ref, wo_ref, fg_ref = refs[:8]
    del refs[:8]
    if mix:
        ya_ref, yb_ref, gm_ref, wm_ref = refs[:4]
        del refs[:4]
    o_ref = refs.pop(0)
    x = x_ref[...]
    if mix:
        x = x + gm_ref[0] * _even_mix(ya_ref, yb_ref, wm_ref, n2_major)
    h = _rms_mod(x, g_ref[...], sh_ref[0], sc_ref[0]).astype(BF16)
    acts = []
    for c0 in range(0, D_FF, FFN_SUB):
        hz = 0.5 * jnp.dot(h, w_in_ref[:, c0:c0 + FFN_SUB], preferred_element_type=F32)
        z2 = jnp.dot(h, w_in_ref[:, D_FF + c0:D_FF + c0 + FFN_SUB], preferred_element_type=F32)
        acts.append(((hz + hz * jnp.tanh(hz)) * z2).astype(BF16))
    y = x + gt_ref[0] * jnp.dot(jnp.concatenate(acts, axis=1), wo_ref[...],
                                preferred_element_type=F32)
    if final_norm:
        y = y * lax.rsqrt(jnp.mean(y * y, axis=-1, keepdims=True) + NORM_EPS) * fg_ref[...]
    if not swap_grid:
        o_ref[...] = y
        return
    s_ref = refs.pop(0)
    nt = D_MODEL // LANES
    for t in range(nt):
        for j in range(GT_ROWS):
            s_ref[t, j * GT_PITCH:j * GT_PITCH + GRID_W, :] = (
                y[j * GRID_W:(j + 1) * GRID_W, t * LANES:(t + 1) * LANES])
    for c in range(GRID_W):
        o_ref[c] = jnp.concatenate(
            [s_ref[t, pl.ds(c, GT_ROWS, stride=GT_PITCH), :] for t in range(nt)], axis=1)


def _ffn(x, mods, mod_row, g, w_in, w_out, final_g, final_norm, mix=None, swap_grid=False):
    n_rows = x.shape[0]
    tm = FFN_TM
    const = lambda shape: _single_buffered(shape, lambda i: (0,) * len(shape))
    in_specs = [
        pl.BlockSpec((tm, D_MODEL), lambda i: (i, 0)),
        _mod_spec(3, mod_row), _mod_spec(4, mod_row), _mod_spec(5, mod_row),
        const((1, D_MODEL)), const(w_in.shape), const(w_out.shape), const((1, D_MODEL)),
    ]
    args = [x, mods, mods, mods, g, w_in, w_out, final_g]
    n2_major = False
    if mix is not None:
        ya, yb, w_mix, n2_major = mix
        nt = W_A // LANES
        in_specs += [pl.BlockSpec((nt, tm, LANES), lambda i: (0, i, 0)),
                     pl.BlockSpec((tm, W_B), lambda i: (i, 0)),
                     _mod_spec(2, mod_row), const(w_mix.shape)]
        args += [ya.reshape(nt, n_rows, LANES), yb, mods, w_mix]
    scratch = []
    if swap_grid:
        per_seq = SEQ // tm
        out_spec = pl.BlockSpec((None, GRID_W, GT_ROWS, D_MODEL),
                                lambda i: (i // per_seq, 0, i % per_seq, 0))
        out_shape = jax.ShapeDtypeStruct((n_rows // SEQ, GRID_W, GRID_H, D_MODEL), F32)
        scratch = [pltpu.VMEM((D_MODEL // LANES, GT_ROWS * GT_PITCH, LANES), F32)]
    else:
        out_spec = pl.BlockSpec((tm, D_MODEL), lambda i: (i, 0))
        out_shape = jax.ShapeDtypeStruct((n_rows, D_MODEL), F32)
    out = pl.pallas_call(
        functools.partial(_ffn_kernel, final_norm=final_norm, mix=mix is not None,
                          n2_major=n2_major, swap_grid=swap_grid),
        grid=(n_rows // tm,),
        in_specs=in_specs,
        out_specs=out_spec,
        out_shape=out_shape,
        scratch_shapes=scratch,
        compiler_params=_cparams("arbitrary"),
        name="ffn",
    )(*args)
    return out.reshape(n_rows, D_MODEL)


ODD_TM = 512
RNN_TILES = D_RNN // LANES


def _in_odd_kernel(x_ref, xp_ref, xn_ref, sh_ref, sc_ref, g_ref, w_ref, cw_ref, cb_ref,
                   gate_ref, xl_ref, *, seq_len, chunk):
    tm = x_ref.shape[0]
    h_all, valid = _normed_tile(x_ref, xp_ref, xn_ref, sh_ref, sc_ref, g_ref, seq_len)
    gate_ref[...] = _gelu(jnp.dot(h_all[:tm], w_ref[:, :D_RNN],
                                  preferred_element_type=F32)).astype(BF16)
    cw = 4 * LANES
    for c0 in range(0, D_RNN, cw):
        cols = slice(c0, min(c0 + cw, D_RNN))
        wcols = slice(D_RNN + cols.start, D_RNN + cols.stop)
        z_all = jnp.dot(h_all, w_ref[:, wcols], preferred_element_type=F32)
        y = _dwconv(z_all, tm, cw_ref[:, cols], cb_ref[:, cols], 2, valid)
        kg = chunk // RG_GROUPS
        for c in range((cols.stop - cols.start) // LANES):
            yc = y[:, c * LANES:(c + 1) * LANES]
            for r0 in range(0, tm, chunk):
                for gq in range(RG_GROUPS):
                    xl_ref[c0 // LANES + c, pl.ds(r0 + gq, kg, stride=RG_GROUPS), :] = (
                        yc[r0 + gq * kg:r0 + (gq + 1) * kg])


def _in_odd(x, mods, mod_row, g, w, cw, cb, seq_len, chunk):
    n_rows = x.shape[0]
    tm = ODD_TM
    const = lambda shape: _single_buffered(shape, lambda i: (0,) * len(shape))
    return pl.pallas_call(
        functools.partial(_in_odd_kernel, seq_len=seq_len, chunk=chunk),
        grid=(n_rows // tm,),
        in_specs=_halo_specs(tm, n_rows) + [
            _mod_spec(0, mod_row), _mod_spec(1, mod_row),
            const((1, D_MODEL)), const(w.shape), const(cw.shape), const(cb.shape),
        ],
        out_specs=[pl.BlockSpec((tm, D_RNN), lambda i: (i, 0)),
                   pl.BlockSpec((RNN_TILES, tm, LANES), lambda i: (0, i, 0))],
        out_shape=[jax.ShapeDtypeStruct((n_rows, D_RNN), BF16),
                   jax.ShapeDtypeStruct((RNN_TILES, n_rows, LANES), F32)],
        compiler_params=_cparams("arbitrary"),
        name="in_proj_odd",
    )(x, x, x, mods, mods, g, w, cw, cb)


RG_T = 512
RG_NCH = SEQ // RG_T
RG_GROUPS = 8
RG_PLANES_PER_PASS = 6
SQRT_GUARD = 1e-30
RG_WIN = 3 * LANES


def _rg_window_start(j):
    return min(max(LANES * (j - 1), 0), D_RNN - RG_WIN)


def _rg_chunk(x_ref, w_ref, ba_ref, bx_ref, lam_ref, a_pl, b_pl, carry_ref, reverse, emit):
    t_rows = x_ref.shape[1]
    kg = t_rows // RG_GROUPS
    xb = [x_ref[j].astype(BF16) for j in range(RNN_TILES)]
    for j in range(RNN_TILES):
        tile = slice(j * LANES, (j + 1) * LANES)
        wt = _rg_window_start(j) // LANES
        pre = jnp.dot(jnp.concatenate(xb[wt:wt + RG_WIN // LANES], axis=1), w_ref[j],
                      preferred_element_type=F32)
        lam = lam_ref[:, tile]
        softplus_neg = jnp.maximum(-lam, 0.0) + jnp.log1p(jnp.exp(-jnp.abs(lam)))
        th_r = jnp.tanh(pre[:, :LANES] + ba_ref[:, tile])
        th_i = jnp.tanh(pre[:, LANES:] + bx_ref[:, tile])
        c3 = (-0.5 * RG_C * math.log2(math.e)) * softplus_neg
        av = jnp.exp2(c3 + c3 * th_r)
        hx = 0.5 * x_ref[j]
        y = 1.0 - av * av
        a_pl[j, 0:t_rows, :] = av
        b_pl[j, 0:t_rows, :] = (y * lax.rsqrt(jnp.maximum(y, SQRT_GUARD))) * (hx + hx * th_i)

    def rows_k(pl_ref, j, k):
        return pl_ref[j, k * RG_GROUPS:(k + 1) * RG_GROUPS, :]

    order = list(range(kg))[::-1] if reverse else list(range(kg))
    groups = list(range(RG_GROUPS))[::-1] if reverse else list(range(RG_GROUPS))
    for j0 in range(0, RNN_TILES, RG_PLANES_PER_PASS):
        planes = range(j0, min(j0 + RG_PLANES_PER_PASS, RNN_TILES))
        big_a, big_b = {}, {}
        for n, k in enumerate(order):
            for j in planes:
                ak, bk = rows_k(a_pl, j, k), rows_k(b_pl, j, k)
                if n == 0:
                    big_a[j], big_b[j] = ak, bk
                else:
                    big_b[j] = ak * big_b[j] + bk
                    big_a[j] = ak * big_a[j]
        h = {}
        for j in planes:
            c = carry_ref[j, 0:1, :]
            rows = [None] * RG_GROUPS
            for gq in groups:
                rows[gq] = c
                c = big_a[j][gq:gq + 1] * c + big_b[j][gq:gq + 1]
            carry_ref[j, 0:1, :] = c
            h[j] = jnp.concatenate(rows, axis=0)
        for k in order:
            for j in planes:
                h[j] = rows_k(a_pl, j, k) * h[j] + rows_k(b_pl, j, k)
                b_pl[j, k * RG_GROUPS:(k + 1) * RG_GROUPS, :] = h[j]
        for j in planes:
            emit(j, jnp.concatenate(
                [b_pl[j, pl.ds(gq, kg, stride=RG_GROUPS), :] for gq in range(RG_GROUPS)], axis=0))


def _rg_bwd_kernel(xc_ref, xl_ref, w_ref, ba_ref, bx_ref, lam_ref, oc_ref, ol_ref,
                   a_pl, b_pl, carry_ref):
    def emitter(o_ref):
        def emit(j, hcur):
            o_ref[:, j * LANES:(j + 1) * LANES] = hcur.astype(BF16)
        return emit

    @pl.when(pl.program_id(1) == 0)
    def _():
        carry_ref[...] = jnp.zeros_like(carry_ref)
        _rg_chunk(xc_ref, w_ref, ba_ref, bx_ref, lam_ref, a_pl, b_pl, carry_ref, True,
                  emitter(oc_ref))

    @pl.when(pl.program_id(1) > 0)
    def _():
        _rg_chunk(xl_ref, w_ref, ba_ref, bx_ref, lam_ref, a_pl, b_pl, carry_ref, True,
                  emitter(ol_ref))


def _rg_fwd_kernel(xc_ref, xl_ref, w_ref, ba_ref, bx_ref, lam_ref, gc_ref, gl_ref, hc_ref, hl_ref,
                   rc_ref, rl_ref, gtc_ref, gtl_ref, wo_ref, oc_ref, ol_ref,
                   a_pl, b_pl, carry_ref, y_ref):
    def run(x_ref, g_ref, hb_ref, res_ref, gt_ref, o_ref):
        t_rows = x_ref.shape[1]

        def emit(j, hcur):
            tile = slice(j * LANES, (j + 1) * LANES)
            y_ref[0:t_rows, tile] = (g_ref[:, tile].astype(F32)
                                     * (hcur + hb_ref[:, tile].astype(F32))).astype(BF16)

        _rg_chunk(x_ref, w_ref, ba_ref, bx_ref, lam_ref, a_pl, b_pl, carry_ref, False, emit)
        acc = jnp.dot(y_ref[0:t_rows, :], wo_ref[...], preferred_element_type=F32)
        o_ref[...] = res_ref[...] + gt_ref[0] * acc

    @pl.when(pl.program_id(1) == 0)
    def _():
        carry_ref[...] = jnp.zeros_like(carry_ref)
        run(xc_ref, gc_ref, hc_ref, rc_ref, gtc_ref, oc_ref)

    @pl.when(pl.program_id(1) > 0)
    def _():
        run(xl_ref, gl_ref, hl_ref, rl_ref, gtl_ref, ol_ref)


def _rg_scan(xc, xl, w, ba, bx, lam, reverse, fwd_args=None):
    if reverse:
        chunk = lambda s: RG_NCH - jnp.maximum(s, 1)
    else:
        chunk = lambda s: jnp.maximum(s - 1, 0)
    ctx_spec = lambda d: pl.BlockSpec((None, CTX_LEN, d), lambda b, s: (b, 0, 0))
    lat_spec = lambda d: pl.BlockSpec((None, RG_T, d), lambda b, s: (b, chunk(s), 0))
    const = lambda shape: _single_buffered(shape, lambda b, s: (0,) * len(shape))
    in_specs = [pl.BlockSpec((RNN_TILES, CTX_LEN, LANES), lambda b, s: (0, b, 0)),
                pl.BlockSpec((RNN_TILES, RG_T, LANES), lambda b, s: (0, b * RG_NCH + chunk(s), 0)),
                const(w.shape), const(ba.shape), const(bx.shape), const(lam.shape)]
    args = [xc, xl, w, ba, bx, lam]
    scratch = [pltpu.VMEM((RNN_TILES, RG_T, LANES), F32),
               pltpu.VMEM((RNN_TILES, RG_T, LANES), F32),
               pltpu.VMEM((RNN_TILES, 8, LANES), F32)]
    if reverse:
        body, name, width, dtype = _rg_bwd_kernel, "rglru_bwd", D_RNN, BF16
    else:
        gate_c, gate_l, hb_c, hb_l, res_c, res_l, mods, w_out = fwd_args
        in_specs += [ctx_spec(D_RNN), lat_spec(D_RNN), ctx_spec(D_RNN), lat_spec(D_RNN),
                     ctx_spec(D_MODEL), lat_spec(D_MODEL),
                     _mod_spec(2, lambda b: CTX_MOD_ROW), _mod_spec(2, lambda b: b),
                     const(w_out.shape)]
        args += [gate_c, gate_l, hb_c, hb_l, res_c, res_l, mods, mods, w_out]
        scratch += [pltpu.VMEM((RG_T, D_RNN), BF16)]
        body, name, width, dtype = _rg_fwd_kernel, "rglru_fwd", D_MODEL, F32
    return pl.pallas_call(
        body,
        grid=(BATCH, 1 + RG_NCH),
        in_specs=in_specs,
        out_specs=[ctx_spec(width), lat_spec(width)],
        out_shape=[jax.ShapeDtypeStruct((BATCH, CTX_LEN, width), dtype),
                   jax.ShapeDtypeStruct((BATCH, SEQ, width), dtype)],
        scratch_shapes=scratch,
        compiler_params=_cparams("arbitrary", "arbitrary"),
        name=name,
    )(*args)


def _rg_gate_weights(wa, wx):
    def window(w, j):
        ws = _rg_window_start(j)
        win = None
        for h in range(RG_HEADS):
            c0, c1 = max(h * RG_DH, j * LANES), min((h + 1) * RG_DH, (j + 1) * LANES)
            if c0 >= c1:
                continue
            r0 = h * RG_DH - ws
            assert 0 <= r0 and r0 + RG_DH <= RG_WIN
            blk = jnp.pad(w[h, :, c0 - h * RG_DH:c1 - h * RG_DH],
                          ((r0, RG_WIN - RG_DH - r0), (c0 - j * LANES, (j + 1) * LANES - c1)))
            win = blk if win is None else win + blk
        return win

    wins = [jnp.concatenate([window(wa, j), window(wx, j)], axis=1) for j in range(RNN_TILES)]
    return (0.5 * jnp.stack(wins)).astype(BF16)


def _col_major(layer):
    return layer < DEPTH and layer % 2 == 1 and (layer // 2) % 2 == 1


def kernel(x, c, ctx, c_ctx, w_ada, b_ada, norm_mix_g, norm_ffn_g, w_in_even, w_out_even, hy_conv_w, hy_conv_b, hy_f1_w, hy_f1_b, hy_f2_w, hy_f2_b, hy_f3_w, hy_f3_b, hy_sin_freq, hy_skip, sgu_ln_g, sgu_w, sgu_b, w_in_odd, rg_conv_w, rg_conv_b, rg_wa, rg_ba, rg_wx, rg_bx, rg_lam, w_out_odd, w_ffn_in, w_ffn_out, final_norm_g):
    mods_all = _ada_mods(c, c_ctx, w_ada, b_ada)
    xs = x.reshape(BATCH * SEQ, D_MODEL)
    cs = ctx.reshape(BATCH * CTX_LEN, D_MODEL)
    lat_row_even = lambda i: i // (SEQ // EVEN_TM)
    lat_row_odd = lambda i: i // (SEQ // ODD_TM)
    lat_row_ffn = lambda i: i // (SEQ // FFN_TM)
    ctx_row = lambda i: CTX_MOD_ROW
    n_tiles = W_A // LANES
    for l in range(DEPTH):
        run_ctx = l < DEPTH - 1
        is_rec = l % 2 == 1
        i = l // 2
        mods = mods_all[l].reshape(MOD_ROWS, 1, N_MOD * D_MODEL)
        g_mix = norm_mix_g[l].reshape(1, D_MODEL)
        mix_l = mix_c = None
        if is_rec:
            w_in = w_in_odd[i].astype(BF16)
            cw, cb = rg_conv_w[i], rg_conv_b[i].reshape(1, D_RNN)
            gate_l, xl = _in_odd(xs, mods, lat_row_odd, g_mix, w_in, cw, cb, SEQ, RG_T)
            gate_c, xc = _in_odd(cs, mods, ctx_row, g_mix, w_in, cw, cb, CTX_LEN, CTX_LEN)
            gate_l = gate_l.reshape(BATCH, SEQ, D_RNN)
            gate_c = gate_c.reshape(BATCH, CTX_LEN, D_RNN)
            row = lambda v: v.reshape(1, D_RNN)
            half = lambda v: 0.5 * row(v)
            hb_c, hb_l = _rg_scan(xc, xl, _rg_gate_weights(rg_wa[i, 1], rg_wx[i, 1]),
                                  half(rg_ba[i, 1]), half(rg_bx[i, 1]), row(rg_lam[i, 1]), True)
            cs_new, xs = _rg_scan(
                xc, xl, _rg_gate_weights(rg_wa[i, 0], rg_wx[i, 0]),
                half(rg_ba[i, 0]), half(rg_bx[i, 0]), row(rg_lam[i, 0]), False,
                (gate_c, gate_l, hb_c, hb_l, cs.reshape(BATCH, CTX_LEN, D_MODEL),
                 xs.reshape(BATCH, SEQ, D_MODEL), mods, w_out_odd[i].astype(BF16)))
            xs = xs.reshape(BATCH * SEQ, D_MODEL)
            if run_ctx:
                cs = cs_new.reshape(BATCH * CTX_LEN, D_MODEL)
        else:
            w_in = w_in_even[i].astype(BF16)
            w_out = w_out_even[i].astype(BF16)
            cw, cb = hy_conv_w[i], hy_conv_b[i].reshape(1, HY_COLS)
            lng = sgu_ln_g[i].reshape(1, W_B)
            sgw = sgu_w[i].astype(BF16)
            sgb = jnp.repeat(sgu_b[i].T, SGU_DH, axis=1)
            skip = hy_skip[i].reshape(1, HY_ORDER * W_A)
            fargs = (hy_f1_w[i], hy_f1_b[i], hy_f2_w[i], hy_f2_b[i], hy_f3_w[i], hy_f3_b[i],
                     hy_sin_freq[i])
            kf = _filter_spectrum(SEQ, *fargs, skip)
            za, yb = _in_even(xs, mods, lat_row_even, g_mix, w_in, cw, cb, lng, sgw, sgb, SEQ, True)
            za = za.reshape(HY_TILES, BATCH, SEQ, LANES)
            y1 = _longconv(za, 0, za, n_tiles, kf, 0)
            ya = _longconv(y1, 0, za, 2 * n_tiles, kf, n_tiles)
            mix_l = (ya, yb, w_out, True)
            if run_ctx:
                kfc = _filter_spectrum(CTX_LEN, *fargs, skip)
                zc, ybc = _in_even(cs, mods, ctx_row, g_mix, w_in, cw, cb, lng, sgw, sgb,
                                   CTX_LEN, False)
                zc = zc.reshape(HY_TILES, BATCH, CTX_LEN, LANES)
                y1c = _ctx_conv(zc, 0, zc, n_tiles, kfc, 0)
                yac = _ctx_conv(y1c, 0, zc, 2 * n_tiles, kfc, n_tiles)
                mix_c = (yac, ybc, w_out, False)
        g_ffn = norm_ffn_g[l].reshape(1, D_MODEL)
        w1 = w_ffn_in[l].astype(BF16)
        w2 = w_ffn_out[l].astype(BF16)
        fg = final_norm_g.reshape(1, D_MODEL)
        swap = _col_major(l) != _col_major(l + 1)
        xs = _ffn(xs, mods, lat_row_ffn, g_ffn, w1, w2, fg, l == DEPTH - 1, mix_l, swap)
        if run_ctx:
            cs = _ffn(cs, mods, ctx_row, g_ffn, w1, w2, fg, False, mix_c)
    return xs.reshape(BATCH, SEQ, D_MODEL)
```

```python
import functools
import math

import numpy as np
import jax
import jax.numpy as jnp
from jax import lax
from jax.experimental import pallas as pl
from jax.experimental.pallas import tpu as pltpu

F32 = jnp.float32
BF16 = jnp.bfloat16

D_MODEL = 1024
BATCH = 4
SEQ = 4096
DEPTH = 4
GRID_W = 64
CTX_LEN = 256
N_MOD = 6
NORM_EPS = 1e-6
W_A = D_MODEL // 2
HY_ORDER = 2
HY_SHORT = 3
HY_BANDS = 16
HY_TARGET = 1e-2
HY_FAST_PCT = 0.3
HY_SLOW_PCT = 1.5
W_B = D_MODEL // 2
SGU_GROUPS = 4
SGU_DH = W_B // SGU_GROUPS
CHUNK = 128
D_RNN = ((4 * D_MODEL // 3 + 127) // 128) * 128
RG_HEADS = 16
RG_DH = D_RNN // RG_HEADS
RG_CONV = 4
RG_C = 8.0
D_FF = ((8 * D_MODEL // 3 + 255) // 256) * 256

VMEM_BYTES_V7X = 64 * 1024 * 1024
VMEM_LIMIT = VMEM_BYTES_V7X - 4 * 1024 * 1024
LANES = 128

FFT_N = 2 * SEQ
FFT_N1 = 64
FFT_N2 = 128
FFT_HALF_N1 = FFT_N1 // 2
FFT_PITCH = FFT_N2 + 8
CTX_N = 2 * CTX_LEN


def _cparams(*sem):
    return pltpu.CompilerParams(dimension_semantics=sem, vmem_limit_bytes=VMEM_LIMIT)


def _single_buffered(block_shape, index_map):
    return pl.BlockSpec(block_shape, index_map, pipeline_mode=pl.Buffered(1))


def _layer_spec(stack, layer):
    rest = stack.shape[1:]
    return _single_buffered((None,) + rest, lambda *_: (layer,) + (0,) * len(rest))


@functools.lru_cache(maxsize=None)
def _dft_tables_f32():
    n = FFT_N
    k1 = np.arange(FFT_N1)
    n2 = np.arange(FFT_N2)

    def angle(n1):
        m = (FFT_N2 * n1[None, None, :] + n2[:, None, None]) * k1[None, :, None]
        return 2.0 * np.pi * (m % n) / n

    a = angle(np.arange(FFT_HALF_N1))
    c, s = np.cos(a), np.sin(a)
    g = np.empty((FFT_N2, FFT_N1, 2, 2, FFT_HALF_N1))
    g[:, :, 0, 0], g[:, :, 0, 1] = c, s
    g[:, :, 1, 0], g[:, :, 1, 1] = -s, c
    g = g.reshape(FFT_N2, 2 * FFT_N1, 2 * FFT_HALF_N1)

    a = angle(np.arange(FFT_N1))
    gf = np.stack([np.cos(a), -np.sin(a)], axis=2).reshape(FFT_N2, 2 * FFT_N1, FFT_N1)

    a = np.transpose(angle(np.arange(FFT_HALF_N1)), (0, 2, 1))
    c, s = np.cos(a) / n, np.sin(a) / n
    h = np.empty((FFT_N2, 2, FFT_HALF_N1, FFT_N1, 2))
    h[:, 0, :, :, 0], h[:, 0, :, :, 1] = c, -s
    h[:, 1, :, :, 0], h[:, 1, :, :, 1] = s, c
    h = h.reshape(FFT_N2, 2 * FFT_HALF_N1, 2 * FFT_N1)

    a = 2.0 * np.pi * np.outer(n2, n2) / FFT_N2
    c, s = np.cos(a), np.sin(a)
    f2 = np.block([[c, s], [-s, c]])
    f2i = np.block([[c, -s], [s, c]])

    kk = np.arange(CTX_N)
    a = 2.0 * np.pi * np.outer(kk, np.arange(CTX_LEN)) / CTX_N
    c, s = np.cos(a), np.sin(a)
    fc = np.block([[c, s], [-s, c]])
    a = 2.0 * np.pi * np.outer(kk, np.arange(CTX_N)) / CTX_N
    fcf = np.concatenate([np.cos(a), -np.sin(a)], axis=0)
    a = 2.0 * np.pi * np.outer(np.arange(CTX_LEN), kk) / CTX_N
    c, s = np.cos(a) / CTX_N, np.sin(a) / CTX_N
    fci = np.block([[c, -s], [s, c]])

    tables = dict(g=g, gf=gf, h=h, f2=f2, f2i=f2i, fc=fc, fcf=fcf, fci=fci)
    return {k: np.asarray(v, dtype=np.float32) for k, v in tables.items()}


def _dft_tables():
    return {k: jnp.asarray(v).astype(BF16) for k, v in _dft_tables_f32().items()}


SEQ_TILE = 1024
N1_PER_TILE = SEQ_TILE // FFT_N2


def _scatter_n2(s_ref, n2, val):
    for p in range(s_ref.shape[0]):
        s_ref[p, pl.ds(n2, 2 * FFT_N1, stride=FFT_PITCH), :] = val[:, p * LANES:(p + 1) * LANES]


def _gather_n2(s_ref, n2):
    return jnp.concatenate(
        [s_ref[p, pl.ds(n2, 2 * FFT_N1, stride=FFT_PITCH), :] for p in range(s_ref.shape[0])],
        axis=1)


def _spectrum_rows(s_ref, k1):
    base = pl.multiple_of(k1 * (2 * FFT_PITCH), 8)
    return base, jnp.concatenate(
        [jnp.concatenate([s_ref[p, pl.ds(base + o, FFT_N2), :] for p in range(s_ref.shape[0])],
                         axis=1) for o in (0, FFT_PITCH)], axis=0)


HY_FFN = 64
FILT_ROWS = 1024
HIGHEST = lax.Precision.HIGHEST


def _tap_rows(row, seq, n2_major):
    if n2_major:
        n2, n1 = row >> int(math.log2(FFT_N1)), row & (FFT_N1 - 1)
        bwd = n1 >= FFT_HALF_N1
        m = FFT_N2 * (n1 - FFT_HALF_N1) + n2
        fwd_pos = FFT_N2 * n1 + n2
    else:
        bwd = row >= seq
        m = row - seq
        fwd_pos = row
    pos = jnp.where(bwd, seq - m, fwd_pos).astype(F32)
    bwd_f = jnp.where(bwd, 1.0, 0.0)
    keep_f = jnp.where(bwd & (m == 0), 0.0, 1.0)
    return pos, bwd_f, keep_f


def _filter_kernel(ec_ref, f1w_ref, f1b_ref, f2w_ref, f2b_ref, sf_ref, w3f_ref, w3b_ref, b3f_ref,
                   b3b_ref, dl_ref, skip_ref, *rest, seq, n2_major):
    if n2_major:
        gf_ref, f2_ref, o_ref, hid_ref, k_ref, s_ref = rest
    else:
        fcf_ref, o_ref, hid_ref, k_ref = rest
    n_rows = 2 * seq
    rb = min(FILT_ROWS, n_rows)

    def rows_of(i):
        r0 = pl.multiple_of(i * rb, rb)
        return r0, _tap_rows(r0 + lax.broadcasted_iota(jnp.int32, (rb, 1), 0), seq, n2_major)

    @pl.when(pl.program_id(0) == 0)
    def _():
        sf = sf_ref[...]

        def hidden(i, carry):
            r0, (pos, _, _) = rows_of(i)
            t = pos * (1.0 / (seq - 1))
            w = pos * (2.0 * math.pi / seq)
            emb = jnp.sin(w * ec_ref[0:1] + ec_ref[1:2]) + t * ec_ref[2:3]
            h = jnp.sin(sf * (jnp.dot(emb, f1w_ref[...], precision=HIGHEST,
                                      preferred_element_type=F32) + f1b_ref[...]))
            h = jnp.sin(sf * (jnp.dot(h.astype(BF16), f2w_ref[...].astype(BF16),
                                      preferred_element_type=F32) + f2b_ref[...]))
            hid_ref[pl.ds(r0, rb), :] = h
            return carry

        lax.fori_loop(0, n_rows // rb, hidden, 0)

    def taps(i, ssq):
        r0, (pos, bwd_f, keep_f) = rows_of(i)
        h = hid_ref[pl.ds(r0, rb), :].astype(BF16)
        kf = jnp.dot(h, w3f_ref[0].astype(BF16), preferred_element_type=F32) + b3f_ref[0]
        kb = jnp.dot(h, w3b_ref[0].astype(BF16), preferred_element_type=F32) + b3b_ref[0]
        window = jnp.exp(-(pos * (1.0 / (seq - 1))) * dl_ref[0])
        k = (kf + bwd_f * (kb - kf)) * window * keep_f
        k_ref[pl.ds(r0, rb), :] = k
        return ssq + jnp.sum(k * k, axis=0, keepdims=True)

    ssq = lax.fori_loop(0, n_rows // rb, taps, jnp.zeros((1, LANES), F32))
    scale = lax.rsqrt(ssq + NORM_EPS)
    skip = skip_ref[...]

    if n2_major:
        def stage1(n2, carry):
            x = k_ref[pl.ds(pl.multiple_of(n2 * FFT_N1, FFT_N1), FFT_N1), :] * scale
            _scatter_n2(s_ref, n2, jnp.dot(gf_ref[n2], x.astype(BF16),
                                           preferred_element_type=F32))
            return carry

        lax.fori_loop(0, FFT_N2, stage1, 0, unroll=8)

        def stage2(k1, carry):
            _, r = _spectrum_rows(s_ref, k1)
            z = jnp.dot(f2_ref[...], r.astype(BF16), preferred_element_type=F32)
            row = pl.multiple_of(k1 * (2 * FFT_N2), 2 * FFT_N2)
            o_ref[pl.ds(row, FFT_N2), :] = (z[:FFT_N2] + skip).astype(BF16)
            o_ref[pl.ds(row + FFT_N2, FFT_N2), :] = z[FFT_N2:].astype(BF16)
            return carry

        lax.fori_loop(0, FFT_N1, stage2, 0, unroll=8)
    else:
        z = jnp.dot(fcf_ref[...], (k_ref[...] * scale).astype(BF16), preferred_element_type=F32)
        o_ref[:n_rows, :] = (z[:n_rows] + skip).astype(BF16)
        o_ref[n_rows:, :] = z[n_rows:].astype(BF16)


def _filter_spectrum(seq, f1_w, f1_b, f2_w, f2_b, f3_w, f3_b, sin_freq, skip):
    n2_major = seq == SEQ
    t = _dft_tables()
    pad = lambda a, r, c: jnp.pad(a, ((0, r - a.shape[0]), (0, c - a.shape[1])))
    f = jnp.linspace(1e-4, HY_BANDS - 1, HY_BANDS, dtype=F32)
    zeros = lambda n: jnp.zeros((n,), F32)
    ones = lambda n: jnp.ones((n,), F32)
    rest = LANES - 1 - 2 * HY_BANDS
    ec = jnp.stack([
        jnp.concatenate([zeros(1), f, f, zeros(rest)]),
        jnp.concatenate([zeros(1), (math.pi / 2) * ones(HY_BANDS), math.pi * ones(HY_BANDS),
                         zeros(rest)]),
        jnp.concatenate([ones(1), zeros(LANES - 1)]),
    ] + [zeros(LANES)] * 5)
    row = lambda v: pad(v.reshape(1, -1), 1, LANES)
    n_tiles = HY_ORDER * W_A // LANES
    w3 = f3_w.reshape(HY_FFN, HY_ORDER, 2, W_A)
    b3 = f3_b.reshape(HY_ORDER, 2, W_A)
    tiles_w = lambda d: jnp.pad(
        w3[:, :, d, :].reshape(HY_FFN, n_tiles, LANES).transpose(1, 0, 2),
        ((0, 0), (0, LANES - HY_FFN), (0, 0)))
    tiles_b = lambda d: b3[:, d, :].reshape(n_tiles, 1, LANES)
    deltas = jnp.abs(jnp.linspace(math.log(HY_TARGET) / HY_SLOW_PCT,
                                  math.log(HY_TARGET) / HY_FAST_PCT, W_A, dtype=F32))
    dl = jnp.tile(deltas, HY_ORDER).reshape(n_tiles, 1, LANES)
    const = lambda shape: _single_buffered(shape, lambda j: (0,) * len(shape))
    tile = lambda shape: pl.BlockSpec((1,) + shape, lambda j: (j, 0, 0))
    args = [ec, pad(f1_w, LANES, LANES), row(f1_b), pad(f2_w, LANES, LANES), row(f2_b),
            row(sin_freq), tiles_w(0), tiles_w(1), tiles_b(0), tiles_b(1), dl, skip]
    in_specs = [const((8, LANES)), const((LANES, LANES)), const((1, LANES)),
                const((LANES, LANES)), const((1, LANES)), const((1, LANES)),
                tile((LANES, LANES)), tile((LANES, LANES)), tile((1, LANES)), tile((1, LANES)),
                tile((1, LANES)), pl.BlockSpec((1, LANES), lambda j: (0, j))]
    scratch = [pltpu.VMEM((2 * seq, LANES), F32), pltpu.VMEM((2 * seq, LANES), F32)]
    if n2_major:
        args += [t["gf"], t["f2"]]
        in_specs += [const(t["gf"].shape), const(t["f2"].shape)]
        scratch += [pltpu.VMEM((1, 2 * FFT_N1 * FFT_PITCH, LANES), F32)]
    else:
        args += [t["fcf"]]
        in_specs += [const(t["fcf"].shape)]
    return pl.pallas_call(
        functools.partial(_filter_kernel, seq=seq, n2_major=n2_major),
        grid=(n_tiles,),
        in_specs=in_specs,
        out_specs=pl.BlockSpec((4 * seq, LANES), lambda j: (0, j)),
        out_shape=jax.ShapeDtypeStruct((4 * seq, HY_ORDER * W_A), BF16),
        scratch_shapes=scratch,
        compiler_params=_cparams("arbitrary"),
        name="hyena_filter" if n2_major else "hyena_ctx_filter",
    )(*args)


LC_OUT_CHUNKS = 8
LC_N2_PER_CHUNK = FFT_N2 // LC_OUT_CHUNKS


def _longconv_kernel(v_ref, gate_ref, kf_ref, g_ref, h_ref, f2_ref, f2i_ref, o_ref, s_ref):
    ct = v_ref.shape[2]
    t = pl.program_id(1)

    @pl.when(t == 0)
    def _():
        def stage1(n2, carry):
            r0 = pl.multiple_of(n2 * N1_PER_TILE, N1_PER_TILE)
            blk = [jnp.concatenate([v_ref[b, pl.ds(i * SEQ_TILE + r0, N1_PER_TILE), :]
                                    for i in range(SEQ // SEQ_TILE)], axis=0)
                   for b in range(BATCH)]
            x = jnp.concatenate([jnp.concatenate([blk[0], blk[2]], axis=1),
                                 jnp.concatenate([blk[1], blk[3]], axis=1)], axis=0)
            _scatter_n2(s_ref, n2, jnp.dot(g_ref[n2], x.astype(BF16), preferred_element_type=F32))
            return carry

        lax.fori_loop(0, FFT_N2, stage1, 0, unroll=8)

        def stage2(k1, carry):
            base, r = _spectrum_rows(s_ref, k1)
            z = jnp.dot(f2_ref[...], r.astype(BF16), preferred_element_type=F32)
            kf = kf_ref[pl.ds(pl.multiple_of(k1 * (2 * FFT_N2), 2 * FFT_N2), 2 * FFT_N2), :]
            kf = kf.astype(F32)
            kr = jnp.concatenate([kf[:FFT_N2]] * 2, axis=1)
            ki = jnp.concatenate([kf[FFT_N2:]] * 2, axis=1)
            zr, zi = z[:FFT_N2], z[FFT_N2:]
            p = jnp.concatenate([zr * kr - zi * ki, zr * ki + zi * kr], axis=0)
            q = jnp.dot(f2i_ref[...], p.astype(BF16), preferred_element_type=F32)
            for p in range(2):
                s_ref[p, pl.ds(base, FFT_N2), :] = q[:FFT_N2, p * ct:(p + 1) * ct]
                s_ref[p, pl.ds(base + FFT_PITCH, FFT_N2), :] = q[FFT_N2:, p * ct:(p + 1) * ct]
            return carry

        lax.fori_loop(0, FFT_N1, stage2, 0, unroll=4)

    def stage3(j, carry):
        n2 = t * LC_N2_PER_CHUNK + j
        rq = _gather_n2(s_ref, n2)
        y = jnp.dot(h_ref[n2], rq.astype(BF16), preferred_element_type=F32)
        r0 = pl.multiple_of(j * N1_PER_TILE, N1_PER_TILE)
        for b in range(BATCH):
            ri, pair = b % 2, b // 2
            for i in range(SEQ // SEQ_TILE):
                n1 = ri * FFT_HALF_N1 + i * N1_PER_TILE
                yb = y[n1:n1 + N1_PER_TILE, pair * ct:(pair + 1) * ct]
                o_ref[b, i, pl.ds(r0, N1_PER_TILE), :] = (
                    gate_ref[b, i, pl.ds(r0, N1_PER_TILE), :] * yb)
        return carry

    lax.fori_loop(0, LC_N2_PER_CHUNK, stage3, 0, unroll=8)


def _longconv(v, v_tile, gate, gate_tile, kf, kf_col):
    t = _dft_tables()
    ct = LANES
    n_seq_tiles = SEQ // SEQ_TILE
    rows = SEQ_TILE // LC_OUT_CHUNKS
    gate = gate.reshape(gate.shape[0], BATCH, n_seq_tiles, SEQ_TILE, ct)
    out = pl.pallas_call(
        _longconv_kernel,
        grid=(W_A // ct, LC_OUT_CHUNKS),
        in_specs=[
            _single_buffered((None, BATCH, SEQ, ct), lambda j, i: (v_tile + j, 0, 0, 0)),
            pl.BlockSpec((None, BATCH, n_seq_tiles, rows, ct),
                         lambda j, i: (gate_tile + j, 0, 0, i, 0)),
            pl.BlockSpec((2 * FFT_N, ct), lambda j, i: (0, kf_col + j)),
            _single_buffered(t["g"].shape, lambda j, i: (0, 0, 0)),
            _single_buffered(t["h"].shape, lambda j, i: (0, 0, 0)),
            _single_buffered(t["f2"].shape, lambda j, i: (0, 0)),
            _single_buffered(t["f2i"].shape, lambda j, i: (0, 0)),
        ],
        out_specs=pl.BlockSpec((None, BATCH, n_seq_tiles, rows, ct),
                               lambda j, i: (j, 0, 0, i, 0)),
        out_shape=jax.ShapeDtypeStruct((W_A // ct, BATCH, n_seq_tiles, SEQ_TILE, ct), F32),
        scratch_shapes=[pltpu.VMEM((2, 2 * FFT_N1 * FFT_PITCH, ct), F32)],
        compiler_params=_cparams("arbitrary", "arbitrary"),
        name="hyena_longconv",
    )(v, gate, kf, t["g"], t["h"], t["f2"], t["f2i"])
    return out.reshape(W_A // ct, BATCH, SEQ, ct)


def _ctx_conv_kernel(v_ref, gate_ref, kf_ref, fc_ref, fci_ref, o_ref):
    ct = v_ref.shape[2]
    x = jnp.concatenate([jnp.concatenate([v_ref[0], v_ref[2]], axis=1),
                         jnp.concatenate([v_ref[1], v_ref[3]], axis=1)], axis=0)
    z = jnp.dot(fc_ref[...], x.astype(BF16), preferred_element_type=F32)
    kf = kf_ref[...].astype(F32)
    kr = jnp.concatenate([kf[:CTX_N]] * 2, axis=1)
    ki = jnp.concatenate([kf[CTX_N:]] * 2, axis=1)
    zr, zi = z[:CTX_N], z[CTX_N:]
    p = jnp.concatenate([zr * kr - zi * ki, zr * ki + zi * kr], axis=0)
    y = jnp.dot(fci_ref[...], p.astype(BF16), preferred_element_type=F32)
    for b in range(BATCH):
        ri, pair = b % 2, b // 2
        o_ref[b] = gate_ref[b] * y[ri * CTX_LEN:(ri + 1) * CTX_LEN, pair * ct:(pair + 1) * ct]


def _ctx_conv(v, v_tile, gate, gate_tile, kf, kf_col):
    t = _dft_tables()
    ct = LANES
    return pl.pallas_call(
        _ctx_conv_kernel,
        grid=(W_A // ct,),
        in_specs=[
            pl.BlockSpec((None, BATCH, CTX_LEN, ct), lambda j: (v_tile + j, 0, 0, 0)),
            pl.BlockSpec((None, BATCH, CTX_LEN, ct), lambda j: (gate_tile + j, 0, 0, 0)),
            pl.BlockSpec((2 * CTX_N, ct), lambda j: (0, kf_col + j)),
            pl.BlockSpec(t["fc"].shape, lambda j: (0, 0)),
            pl.BlockSpec(t["fci"].shape, lambda j: (0, 0)),
        ],
        out_specs=pl.BlockSpec((None, BATCH, CTX_LEN, ct), lambda j: (j, 0, 0, 0)),
        out_shape=jax.ShapeDtypeStruct((W_A // ct, BATCH, CTX_LEN, ct), F32),
        compiler_params=_cparams("arbitrary"),
        name="hyena_ctx_conv",
    )(v, gate, kf, t["fc"], t["fci"])


MOD_ROWS = 8
CTX_MOD_ROW = BATCH


def _ada_kernel(c_ref, w_ref, b_ref, o_ref):
    cv = c_ref[...]
    s = cv * jax.nn.sigmoid(cv)
    o_ref[0] = jnp.dot(s.astype(BF16), w_ref[0].astype(BF16),
                       preferred_element_type=F32) + b_ref[0]


def _ada_mods(c, c_ctx, w_ada, b_ada):
    cv = jnp.concatenate(
        [c, c_ctx[None], jnp.zeros((MOD_ROWS - BATCH - 1, D_MODEL), F32)], axis=0)
    n = N_MOD * D_MODEL
    tn = n // 4
    return pl.pallas_call(
        _ada_kernel,
        grid=(DEPTH, n // tn),
        in_specs=[
            pl.BlockSpec((MOD_ROWS, D_MODEL), lambda l, j: (0, 0)),
            pl.BlockSpec((1, D_MODEL, tn), lambda l, j: (l, 0, j)),
            pl.BlockSpec((1, 1, tn), lambda l, j: (l, 0, j)),
        ],
        out_specs=pl.BlockSpec((1, MOD_ROWS, tn), lambda l, j: (l, 0, j)),
        out_shape=jax.ShapeDtypeStruct((DEPTH, MOD_ROWS, n), F32),
        compiler_params=_cparams("arbitrary", "arbitrary"),
        name="ada_mods",
    )(cv, w_ada, b_ada.reshape(DEPTH, 1, n))


def _mod_spec(m, row_fn):
    return pl.BlockSpec((1, 1, D_MODEL), lambda i, *_: (row_fn(i), 0, m))


def _rms_mod(x, g, shift, scale):
    ms = jnp.mean(x * x, axis=-1, keepdims=True)
    return (x * lax.rsqrt(ms + NORM_EPS) * g) * (1.0 + scale) + shift


GELU_C = math.sqrt(2.0 / math.pi)


def _gelu(x):
    hx = 0.5 * x
    return hx + hx * jnp.tanh(x * (GELU_C + (GELU_C * 0.044715) * (x * x)))


HALO = 8


def _dwconv(z_all, tm, w, b, left, valid):
    n = z_all.shape[0]
    y = b + w[left:left + 1] * z_all[:tm]
    for k in range(w.shape[0]):
        d = k - left
        if d == 0:
            continue
        s = pltpu.roll(z_all, (-d) % n, 0)[:tm]
        if valid is not None:
            s = s * valid(d)
        y = y + s * w[k:k + 1]
    return y


def _halo_specs(tm, n_rows):
    per = tm // HALO
    last = n_rows // HALO - 1
    return [
        pl.BlockSpec((tm, D_MODEL), lambda i: (i, 0)),
        pl.BlockSpec((HALO, D_MODEL), lambda i: (jnp.maximum(i * per - 1, 0), 0)),
        pl.BlockSpec((HALO, D_MODEL), lambda i: (jnp.minimum((i + 1) * per, last), 0)),
    ]


def _normed_tile(x_ref, xp_ref, xn_ref, sh_ref, sc_ref, g_ref, seq_len):
    tm = x_ref.shape[0]
    g, shift, scale = g_ref[...], sh_ref[0], sc_ref[0]
    h = _rms_mod(x_ref[...], g, shift, scale)
    hn = _rms_mod(xn_ref[...], g, shift, scale)
    hp = _rms_mod(xp_ref[...], g, shift, scale)
    if seq_len >= tm:
        r0 = pl.program_id(0) * tm
        hp = hp * jnp.where((r0 & (seq_len - 1)) == 0, 0.0, 1.0)
        hn = hn * jnp.where(((r0 + tm) & (seq_len - 1)) == 0, 0.0, 1.0)
        valid = None
    else:
        hp, hn = jnp.zeros_like(hp), jnp.zeros_like(hn)
        pos = lax.broadcasted_iota(jnp.int32, (tm, 1), 0) & (seq_len - 1)
        valid = lambda d: jnp.where((pos + d >= 0) & (pos + d < seq_len), 1.0, 0.0)
    return jnp.concatenate([h, hn, hp], axis=0).astype(BF16), valid


EVEN_TM = SEQ_TILE
HY_COLS = 3 * W_A
HY_TILES = HY_COLS // LANES


def _in_even_kernel(x_ref, xp_ref, xn_ref, sh_ref, sc_ref, g_ref, w_ref, cw_ref, cb_ref,
                    lng_ref, sgw_ref, sgb_ref, za_ref, yb_ref, *, seq_len, n2_major):
    tm = x_ref.shape[0]
    h_all, valid = _normed_tile(x_ref, xp_ref, xn_ref, sh_ref, sc_ref, g_ref, seq_len)
    h = h_all[:tm]
    cw = 4 * LANES
    for cc in range(HY_COLS // cw):
        cols = slice(cc * cw, (cc + 1) * cw)
        z_all = jnp.dot(h_all, w_ref[:, cols], preferred_element_type=F32)
        y = _dwconv(z_all, tm, cw_ref[:, cols], cb_ref[:, cols], 1, valid)
        for c in range(cw // LANES):
            tile = cc * (cw // LANES) + c
            yc = y[:, c * LANES:(c + 1) * LANES]
            if n2_major:
                for j in range(N1_PER_TILE):
                    za_ref[tile, pl.ds(j, FFT_N2, stride=N1_PER_TILE), :] = (
                        yc[j * FFT_N2:(j + 1) * FFT_N2])
            else:
                za_ref[tile] = yc
    u = _gelu(jnp.dot(h, w_ref[:, HY_COLS:HY_COLS + W_B], preferred_element_type=F32))
    vb = _gelu(jnp.dot(h, w_ref[:, HY_COLS + W_B:], preferred_element_type=F32))
    vc = vb - jnp.mean(vb, axis=-1, keepdims=True)
    vn = vc * lax.rsqrt(jnp.mean(vc * vc, axis=-1, keepdims=True) + NORM_EPS) * lng_ref[...]
    vn = vn.astype(BF16)
    for ch in range(tm // CHUNK):
        rows = slice(ch * CHUNK, (ch + 1) * CHUNK)
        for q in range(SGU_GROUPS):
            cols = slice(q * SGU_DH, (q + 1) * SGU_DH)
            s = jnp.dot(sgw_ref[q], vn[rows, cols], preferred_element_type=F32) + sgb_ref[:, cols]
            yb_ref[rows, cols] = (u[rows, cols] * s).astype(BF16)


def _in_even(x, mods, mod_row, g, w, layer, cw, cb, lng, sgw, sgb, seq_len, n2_major):
    n_rows = x.shape[0]
    tm = EVEN_TM
    za_shape = (HY_TILES, n_rows, LANES)
    za_spec = pl.BlockSpec((HY_TILES, tm, LANES), lambda i: (0, i, 0))
    const = lambda shape: _single_buffered(shape, lambda i: (0,) * len(shape))
    return pl.pallas_call(
        functools.partial(_in_even_kernel, seq_len=seq_len, n2_major=n2_major),
        grid=(n_rows // tm,),
        in_specs=_halo_specs(tm, n_rows) + [
            _mod_spec(0, mod_row), _mod_spec(1, mod_row),
            const((1, D_MODEL)), _layer_spec(w, layer), const(cw.shape), const(cb.shape),
            const(lng.shape), const(sgw.shape), const(sgb.shape),
        ],
        out_specs=[za_spec, pl.BlockSpec((tm, W_B), lambda i: (i, 0))],
        out_shape=[jax.ShapeDtypeStruct(za_shape, F32),
                   jax.ShapeDtypeStruct((n_rows, W_B), BF16)],
        compiler_params=_cparams("arbitrary"),
        name="in_proj_even",
    )(x, x, x, mods, mods, g, w, cw, cb, lng, sgw, sgb)


FFN_TM = SEQ_TILE
FFN_SUB = 256
GRID_H = SEQ // GRID_W
GT_ROWS = FFN_TM // GRID_W
GT_PITCH = GRID_W + 8


def _even_mix(ya_ref, yb_ref, w_ref, n2_major):
    acc = jnp.dot(yb_ref[...], w_ref[W_A:, :], preferred_element_type=F32)
    for c in range(W_A // LANES):
        if n2_major:
            yac = jnp.concatenate(
                [ya_ref[c, pl.ds(j, FFT_N2, stride=N1_PER_TILE), :]
                 for j in range(N1_PER_TILE)], axis=0)
        else:
            yac = ya_ref[c]
        acc = acc + jnp.dot(yac.astype(BF16), w_ref[c * LANES:(c + 1) * LANES, :],
                            preferred_element_type=F32)
    return acc


def _ffn_kernel(*refs, final_norm, mix, n2_major, swap_grid):
    refs = list(refs)
    x_ref, sh_ref, sc_ref, gt_ref, g_ref, w_in_ref, wo_ref, fg_ref = refs[:8]
    del refs[:8]
    if mix:
        ya_ref, yb_ref, gm_ref, wm_ref = refs[:4]
        del refs[:4]
    o_ref = refs.pop(0)
    x = x_ref[...]
    if mix:
        x = x + gm_ref[0] * _even_mix(ya_ref, yb_ref, wm_ref, n2_major)
    h = _rms_mod(x, g_ref[...], sh_ref[0], sc_ref[0]).astype(BF16)
    acts = []
    for c0 in range(0, D_FF, FFN_SUB):
        hz = 0.5 * jnp.dot(h, w_in_ref[:, c0:c0 + FFN_SUB], preferred_element_type=F32)
        z2 = jnp.dot(h, w_in_ref[:, D_FF + c0:D_FF + c0 + FFN_SUB], preferred_element_type=F32)
        acts.append(((hz + hz * jnp.tanh(hz)) * z2).astype(BF16))
    y = x + gt_ref[0] * jnp.dot(jnp.concatenate(acts, axis=1), wo_ref[...],
                                preferred_element_type=F32)
    if final_norm:
        y = y * lax.rsqrt(jnp.mean(y * y, axis=-1, keepdims=True) + NORM_EPS) * fg_ref[...]
    if not swap_grid:
        o_ref[...] = y
        return
    s_ref = refs.pop(0)
    nt = D_MODEL // LANES
    for t in range(nt):
        for j in range(GT_ROWS):
            s_ref[t, j * GT_PITCH:j * GT_PITCH + GRID_W, :] = (
                y[j * GRID_W:(j + 1) * GRID_W, t * LANES:(t + 1) * LANES])
    for c in range(GRID_W):
        o_ref[c] = jnp.concatenate(
            [s_ref[t, pl.ds(c, GT_ROWS, stride=GT_PITCH), :] for t in range(nt)], axis=1)


def _ffn(x, mods, mod_row, g, w_in, w_out, layer, final_g, final_norm, mix=None,
         swap_grid=False):
    n_rows = x.shape[0]
    tm = FFN_TM
    const = lambda shape: _single_buffered(shape, lambda i: (0,) * len(shape))
    in_specs = [
        pl.BlockSpec((tm, D_MODEL), lambda i: (i, 0)),
        _mod_spec(3, mod_row), _mod_spec(4, mod_row), _mod_spec(5, mod_row),
        const((1, D_MODEL)), _layer_spec(w_in, layer), _layer_spec(w_out, layer),
        const((1, D_MODEL)),
    ]
    args = [x, mods, mods, mods, g, w_in, w_out, final_g]
    n2_major = False
    if mix is not None:
        ya, yb, (w_mix, mix_layer), n2_major = mix
        nt = W_A // LANES
        in_specs += [pl.BlockSpec((nt, tm, LANES), lambda i: (0, i, 0)),
                     pl.BlockSpec((tm, W_B), lambda i: (i, 0)),
                     _mod_spec(2, mod_row), _layer_spec(w_mix, mix_layer)]
        args += [ya.reshape(nt, n_rows, LANES), yb, mods, w_mix]
    scratch = []
    if swap_grid:
        per_seq = SEQ // tm
        out_spec = pl.BlockSpec((None, GRID_W, GT_ROWS, D_MODEL),
                                lambda i: (i // per_seq, 0, i % per_seq, 0))
        out_shape = jax.ShapeDtypeStruct((n_rows // SEQ, GRID_W, GRID_H, D_MODEL), F32)
        scratch = [pltpu.VMEM((D_MODEL // LANES, GT_ROWS * GT_PITCH, LANES), F32)]
    else:
        out_spec = pl.BlockSpec((tm, D_MODEL), lambda i: (i, 0))
        out_shape = jax.ShapeDtypeStruct((n_rows, D_MODEL), F32)
    out = pl.pallas_call(
        functools.partial(_ffn_kernel, final_norm=final_norm, mix=mix is not None,
                          n2_major=n2_major, swap_grid=swap_grid),
        grid=(n_rows // tm,),
        in_specs=in_specs,
        out_specs=out_spec,
        out_shape=out_shape,
        scratch_shapes=scratch,
        compiler_params=_cparams("arbitrary"),
        name="ffn",
    )(*args)
    return out.reshape(n_rows, D_MODEL)


ODD_TM = 1024
RNN_TILES = D_RNN // LANES


def _in_odd_kernel(x_ref, xp_ref, xn_ref, sh_ref, sc_ref, g_ref, w_ref, cw_ref, cb_ref,
                   gate_ref, xl_ref, *, seq_len, chunk):
    tm = x_ref.shape[0]
    h_all, valid = _normed_tile(x_ref, xp_ref, xn_ref, sh_ref, sc_ref, g_ref, seq_len)
    gate_ref[...] = _gelu(jnp.dot(h_all[:tm], w_ref[:, :D_RNN],
                                  preferred_element_type=F32)).astype(BF16)
    cw = 4 * LANES
    for c0 in range(0, D_RNN, cw):
        cols = slice(c0, min(c0 + cw, D_RNN))
        wcols = slice(D_RNN + cols.start, D_RNN + cols.stop)
        z_all = jnp.dot(h_all, w_ref[:, wcols], preferred_element_type=F32)
        y = _dwconv(z_all, tm, cw_ref[:, cols], cb_ref[:, cols], 2, valid)
        kg = chunk // RG_GROUPS
        for c in range((cols.stop - cols.start) // LANES):
            yc = y[:, c * LANES:(c + 1) * LANES]
            for r0 in range(0, tm, chunk):
                for gq in range(RG_GROUPS):
                    xl_ref[c0 // LANES + c, pl.ds(r0 + gq, kg, stride=RG_GROUPS), :] = (
                        yc[r0 + gq * kg:r0 + (gq + 1) * kg])


def _in_odd(x, mods, mod_row, g, w, layer, cw, cb, seq_len, chunk):
    n_rows = x.shape[0]
    tm = ODD_TM
    const = lambda shape: _single_buffered(shape, lambda i: (0,) * len(shape))
    return pl.pallas_call(
        functools.partial(_in_odd_kernel, seq_len=seq_len, chunk=chunk),
        grid=(n_rows // tm,),
        in_specs=_halo_specs(tm, n_rows) + [
            _mod_spec(0, mod_row), _mod_spec(1, mod_row),
            const((1, D_MODEL)), _layer_spec(w, layer), const(cw.shape), const(cb.shape),
        ],
        out_specs=[pl.BlockSpec((tm, D_RNN), lambda i: (i, 0)),
                   pl.BlockSpec((RNN_TILES, tm, LANES), lambda i: (0, i, 0))],
        out_shape=[jax.ShapeDtypeStruct((n_rows, D_RNN), BF16),
                   jax.ShapeDtypeStruct((RNN_TILES, n_rows, LANES), F32)],
        compiler_params=_cparams("arbitrary"),
        name="in_proj_odd",
    )(x, x, x, mods, mods, g, w, cw, cb)


RG_T = 512
RG_NCH = SEQ // RG_T
RG_GROUPS = 8
RG_PLANES_PER_PASS = 6
SQRT_GUARD = 1e-30
RG_WIN = 3 * LANES


def _rg_window_start(j):
    return min(max(LANES * (j - 1), 0), D_RNN - RG_WIN)


def _rg_chunk(x_ref, w_ref, ba_ref, bx_ref, lam_ref, a_pl, b_pl, carry_ref, reverse, emit):
    t_rows = x_ref.shape[1]
    kg = t_rows // RG_GROUPS
    xb = [x_ref[j].astype(BF16) for j in range(RNN_TILES)]
    for j in range(RNN_TILES):
        tile = slice(j * LANES, (j + 1) * LANES)
        wt = _rg_window_start(j) // LANES
        pre = jnp.dot(jnp.concatenate(xb[wt:wt + RG_WIN // LANES], axis=1), w_ref[j],
                      preferred_element_type=F32)
        lam = lam_ref[:, tile]
        softplus_neg = jnp.maximum(-lam, 0.0) + jnp.log1p(jnp.exp(-jnp.abs(lam)))
        th_r = jnp.tanh(pre[:, :LANES] + ba_ref[:, tile])
        th_i = jnp.tanh(pre[:, LANES:] + bx_ref[:, tile])
        c3 = (-0.5 * RG_C * math.log2(math.e)) * softplus_neg
        av = jnp.exp2(c3 + c3 * th_r)
        hx = 0.5 * x_ref[j]
        y = 1.0 - av * av
        a_pl[j, 0:t_rows, :] = av
        b_pl[j, 0:t_rows, :] = (y * lax.rsqrt(jnp.maximum(y, SQRT_GUARD))) * (hx + hx * th_i)

    def rows_k(pl_ref, j, k):
        return pl_ref[j, k * RG_GROUPS:(k + 1) * RG_GROUPS, :]

    order = list(range(kg))[::-1] if reverse else list(range(kg))
    groups = list(range(RG_GROUPS))[::-1] if reverse else list(range(RG_GROUPS))
    for j0 in range(0, RNN_TILES, RG_PLANES_PER_PASS):
        planes = range(j0, min(j0 + RG_PLANES_PER_PASS, RNN_TILES))
        big_a, big_b = {}, {}
        for n, k in enumerate(order):
            for j in planes:
                ak, bk = rows_k(a_pl, j, k), rows_k(b_pl, j, k)
                if n == 0:
                    big_a[j], big_b[j] = ak, bk
                else:
                    big_b[j] = ak * big_b[j] + bk
                    big_a[j] = ak * big_a[j]
        h = {}
        for j in planes:
            c = carry_ref[j, 0:1, :]
            rows = [None] * RG_GROUPS
            for gq in groups:
                rows[gq] = c
                c = big_a[j][gq:gq + 1] * c + big_b[j][gq:gq + 1]
            carry_ref[j, 0:1, :] = c
            h[j] = jnp.concatenate(rows, axis=0)
        for k in order:
            for j in planes:
                h[j] = rows_k(a_pl, j, k) * h[j] + rows_k(b_pl, j, k)
                b_pl[j, k * RG_GROUPS:(k + 1) * RG_GROUPS, :] = h[j]
        for j in planes:
            emit(j, jnp.concatenate(
                [b_pl[j, pl.ds(gq, kg, stride=RG_GROUPS), :] for gq in range(RG_GROUPS)], axis=0))


def _rg_bwd_kernel(xc_ref, xl_ref, w_ref, ba_ref, bx_ref, lam_ref, oc_ref, ol_ref,
                   a_pl, b_pl, carry_ref):
    def emitter(o_ref):
        def emit(j, hcur):
            o_ref[:, j * LANES:(j + 1) * LANES] = hcur.astype(BF16)
        return emit

    @pl.when(pl.program_id(1) == 0)
    def _():
        carry_ref[...] = jnp.zeros_like(carry_ref)
        _rg_chunk(xc_ref, w_ref, ba_ref, bx_ref, lam_ref, a_pl, b_pl, carry_ref, True,
                  emitter(oc_ref))

    @pl.when(pl.program_id(1) > 0)
    def _():
        _rg_chunk(xl_ref, w_ref, ba_ref, bx_ref, lam_ref, a_pl, b_pl, carry_ref, True,
                  emitter(ol_ref))


def _rg_fwd_kernel(xc_ref, xl_ref, w_ref, ba_ref, bx_ref, lam_ref, gc_ref, gl_ref, hc_ref, hl_ref,
                   rc_ref, rl_ref, gtc_ref, gtl_ref, wo_ref, oc_ref, ol_ref,
                   a_pl, b_pl, carry_ref, y_ref):
    def run(x_ref, g_ref, hb_ref, res_ref, gt_ref, o_ref):
        t_rows = x_ref.shape[1]

        def emit(j, hcur):
            tile = slice(j * LANES, (j + 1) * LANES)
            y_ref[0:t_rows, tile] = (g_ref[:, tile].astype(F32)
                                     * (hcur + hb_ref[:, tile].astype(F32))).astype(BF16)

        _rg_chunk(x_ref, w_ref, ba_ref, bx_ref, lam_ref, a_pl, b_pl, carry_ref, False, emit)
        acc = jnp.dot(y_ref[0:t_rows, :], wo_ref[...], preferred_element_type=F32)
        o_ref[...] = res_ref[...] + gt_ref[0] * acc

    @pl.when(pl.program_id(1) == 0)
    def _():
        carry_ref[...] = jnp.zeros_like(carry_ref)
        run(xc_ref, gc_ref, hc_ref, rc_ref, gtc_ref, oc_ref)

    @pl.when(pl.program_id(1) > 0)
    def _():
        run(xl_ref, gl_ref, hl_ref, rl_ref, gtl_ref, ol_ref)


def _rg_scan(xc, xl, w, ba, bx, lam, reverse, fwd_args=None):
    if reverse:
        chunk = lambda s: RG_NCH - jnp.maximum(s, 1)
    else:
        chunk = lambda s: jnp.maximum(s - 1, 0)
    ctx_spec = lambda d: pl.BlockSpec((None, CTX_LEN, d), lambda b, s: (b, 0, 0))
    lat_spec = lambda d: pl.BlockSpec((None, RG_T, d), lambda b, s: (b, chunk(s), 0))
    const = lambda shape: _single_buffered(shape, lambda b, s: (0,) * len(shape))
    in_specs = [pl.BlockSpec((RNN_TILES, CTX_LEN, LANES), lambda b, s: (0, b, 0)),
                pl.BlockSpec((RNN_TILES, RG_T, LANES), lambda b, s: (0, b * RG_NCH + chunk(s), 0)),
                const(w.shape), const(ba.shape), const(bx.shape), const(lam.shape)]
    args = [xc, xl, w, ba, bx, lam]
    scratch = [pltpu.VMEM((RNN_TILES, RG_T, LANES), F32),
               pltpu.VMEM((RNN_TILES, RG_T, LANES), F32),
               pltpu.VMEM((RNN_TILES, 8, LANES), F32)]
    if reverse:
        body, name, width, dtype = _rg_bwd_kernel, "rglru_bwd", D_RNN, BF16
    else:
        gate_c, gate_l, hb_c, hb_l, res_c, res_l, mods, (w_out, out_layer) = fwd_args
        in_specs += [ctx_spec(D_RNN), lat_spec(D_RNN), ctx_spec(D_RNN), lat_spec(D_RNN),
                     ctx_spec(D_MODEL), lat_spec(D_MODEL),
                     _mod_spec(2, lambda b: CTX_MOD_ROW), _mod_spec(2, lambda b: b),
                     _layer_spec(w_out, out_layer)]
        args += [gate_c, gate_l, hb_c, hb_l, res_c, res_l, mods, mods, w_out]
        scratch += [pltpu.VMEM((RG_T, D_RNN), BF16)]
        body, name, width, dtype = _rg_fwd_kernel, "rglru_fwd", D_MODEL, F32
    return pl.pallas_call(
        body,
        grid=(BATCH, 1 + RG_NCH),
        in_specs=in_specs,
        out_specs=[ctx_spec(width), lat_spec(width)],
        out_shape=[jax.ShapeDtypeStruct((BATCH, CTX_LEN, width), dtype),
                   jax.ShapeDtypeStruct((BATCH, SEQ, width), dtype)],
        scratch_shapes=scratch,
        compiler_params=_cparams("arbitrary", "arbitrary"),
        name=name,
    )(*args)


def _rg_gate_weights(wa, wx):
    def window(w, j):
        ws = _rg_window_start(j)
        win = None
        for h in range(RG_HEADS):
            c0, c1 = max(h * RG_DH, j * LANES), min((h + 1) * RG_DH, (j + 1) * LANES)
            if c0 >= c1:
                continue
            r0 = h * RG_DH - ws
            assert 0 <= r0 and r0 + RG_DH <= RG_WIN
            blk = jnp.pad(w[h, :, c0 - h * RG_DH:c1 - h * RG_DH],
                          ((r0, RG_WIN - RG_DH - r0), (c0 - j * LANES, (j + 1) * LANES - c1)))
            win = blk if win is None else win + blk
        return win

    wins = [jnp.concatenate([window(wa, j), window(wx, j)], axis=1) for j in range(RNN_TILES)]
    return (0.5 * jnp.stack(wins)).astype(BF16)


def _col_major(layer):
    return layer < DEPTH and layer % 2 == 1 and (layer // 2) % 2 == 1


def kernel(x, c, ctx, c_ctx, w_ada, b_ada, norm_mix_g, norm_ffn_g, w_in_even, w_out_even, hy_conv_w, hy_conv_b, hy_f1_w, hy_f1_b, hy_f2_w, hy_f2_b, hy_f3_w, hy_f3_b, hy_sin_freq, hy_skip, sgu_ln_g, sgu_w, sgu_b, w_in_odd, rg_conv_w, rg_conv_b, rg_wa, rg_ba, rg_wx, rg_bx, rg_lam, w_out_odd, w_ffn_in, w_ffn_out, final_norm_g):
    mods_all = _ada_mods(c, c_ctx, w_ada, b_ada)
    xs = x.reshape(BATCH * SEQ, D_MODEL)
    cs = ctx.reshape(BATCH * CTX_LEN, D_MODEL)
    lat_row_even = lambda i: i // (SEQ // EVEN_TM)
    lat_row_odd = lambda i: i // (SEQ // ODD_TM)
    lat_row_ffn = lambda i: i // (SEQ // FFN_TM)
    ctx_row = lambda i: CTX_MOD_ROW
    n_tiles = W_A // LANES
    w_in_even, w_out_even, w_in_odd, w_out_odd, w_ffn_in, w_ffn_out = (
        w.astype(BF16) for w in (w_in_even, w_out_even, w_in_odd, w_out_odd, w_ffn_in, w_ffn_out))
    for l in range(DEPTH):
        run_ctx = l < DEPTH - 1
        is_rec = l % 2 == 1
        i = l // 2
        mods = mods_all[l].reshape(MOD_ROWS, 1, N_MOD * D_MODEL)
        g_mix = norm_mix_g[l].reshape(1, D_MODEL)
        mix_l = mix_c = None
        if is_rec:
            cw, cb = rg_conv_w[i], rg_conv_b[i].reshape(1, D_RNN)
            gate_l, xl = _in_odd(xs, mods, lat_row_odd, g_mix, w_in_odd, i, cw, cb, SEQ, RG_T)
            gate_c, xc = _in_odd(cs, mods, ctx_row, g_mix, w_in_odd, i, cw, cb, CTX_LEN, CTX_LEN)
            gate_l = gate_l.reshape(BATCH, SEQ, D_RNN)
            gate_c = gate_c.reshape(BATCH, CTX_LEN, D_RNN)
            row = lambda v: v.reshape(1, D_RNN)
            half = lambda v: 0.5 * row(v)
            hb_c, hb_l = _rg_scan(xc, xl, _rg_gate_weights(rg_wa[i, 1], rg_wx[i, 1]),
                                  half(rg_ba[i, 1]), half(rg_bx[i, 1]), row(rg_lam[i, 1]), True)
            cs_new, xs = _rg_scan(
                xc, xl, _rg_gate_weights(rg_wa[i, 0], rg_wx[i, 0]),
                half(rg_ba[i, 0]), half(rg_bx[i, 0]), row(rg_lam[i, 0]), False,
                (gate_c, gate_l, hb_c, hb_l, cs.reshape(BATCH, CTX_LEN, D_MODEL),
                 xs.reshape(BATCH, SEQ, D_MODEL), mods, (w_out_odd, i)))
            xs = xs.reshape(BATCH * SEQ, D_MODEL)
            if run_ctx:
                cs = cs_new.reshape(BATCH * CTX_LEN, D_MODEL)
        else:
            cw, cb = hy_conv_w[i], hy_conv_b[i].reshape(1, HY_COLS)
            lng = sgu_ln_g[i].reshape(1, W_B)
            sgw = sgu_w[i].astype(BF16)
            sgb = jnp.repeat(sgu_b[i].T, SGU_DH, axis=1)
            skip = hy_skip[i].reshape(1, HY_ORDER * W_A)
            fargs = (hy_f1_w[i], hy_f1_b[i], hy_f2_w[i], hy_f2_b[i], hy_f3_w[i], hy_f3_b[i],
                     hy_sin_freq[i])
            kf = _filter_spectrum(SEQ, *fargs, skip)
            za, yb = _in_even(xs, mods, lat_row_even, g_mix, w_in_even, i, cw, cb, lng, sgw, sgb,
                              SEQ, True)
            za = za.reshape(HY_TILES, BATCH, SEQ, LANES)
            y1 = _longconv(za, 0, za, n_tiles, kf, 0)
            ya = _longconv(y1, 0, za, 2 * n_tiles, kf, n_tiles)
            mix_l = (ya, yb, (w_out_even, i), True)
            if run_ctx:
                kfc = _filter_spectrum(CTX_LEN, *fargs, skip)
                zc, ybc = _in_even(cs, mods, ctx_row, g_mix, w_in_even, i, cw, cb, lng, sgw, sgb,
                                   CTX_LEN, False)
                zc = zc.reshape(HY_TILES, BATCH, CTX_LEN, LANES)
                y1c = _ctx_conv(zc, 0, zc, n_tiles, kfc, 0)
                yac = _ctx_conv(y1c, 0, zc, 2 * n_tiles, kfc, n_tiles)
                mix_c = (yac, ybc, (w_out_even, i), False)
        g_ffn = norm_ffn_g[l].reshape(1, D_MODEL)
        fg = final_norm_g.reshape(1, D_MODEL)
        swap = _col_major(l) != _col_major(l + 1)
        xs = _ffn(xs, mods, lat_row_ffn, g_ffn, w_ffn_in, w_ffn_out, l, fg, l == DEPTH - 1,
                  mix_l, swap)
        if run_ctx:
            cs = _ffn(cs, mods, ctx_row, g_ffn, w_ffn_in, w_ffn_out, l, fg, False, mix_c)
    return xs.reshape(BATCH, SEQ, D_MODEL)
```

```python
import functools
import math

import numpy as np
import jax
import jax.numpy as jnp
from jax import lax
from jax.experimental import pallas as pl
from jax.experimental.pallas import tpu as pltpu

F32 = jnp.float32
BF16 = jnp.bfloat16

D_MODEL = 1024
BATCH = 4
SEQ = 4096
DEPTH = 4
GRID_W = 64
CTX_LEN = 256
N_MOD = 6
NORM_EPS = 1e-6
W_A = D_MODEL // 2
HY_ORDER = 2
HY_SHORT = 3
HY_BANDS = 16
HY_TARGET = 1e-2
HY_FAST_PCT = 0.3
HY_SLOW_PCT = 1.5
W_B = D_MODEL // 2
SGU_GROUPS = 4
SGU_DH = W_B // SGU_GROUPS
CHUNK = 128
D_RNN = ((4 * D_MODEL // 3 + 127) // 128) * 128
RG_HEADS = 16
RG_DH = D_RNN // RG_HEADS
RG_CONV = 4
RG_C = 8.0
D_FF = ((8 * D_MODEL // 3 + 255) // 256) * 256

VMEM_BYTES_V7X = 64 * 1024 * 1024
VMEM_LIMIT = VMEM_BYTES_V7X - 4 * 1024 * 1024
LANES = 128

FFT_N = 2 * SEQ
FFT_N1 = 64
FFT_N2 = 128
FFT_HALF_N1 = FFT_N1 // 2
FFT_PITCH = FFT_N2 + 8
CTX_N = 2 * CTX_LEN


def _cparams(*sem):
    return pltpu.CompilerParams(dimension_semantics=sem, vmem_limit_bytes=VMEM_LIMIT)


def _single_buffered(block_shape, index_map):
    return pl.BlockSpec(block_shape, index_map, pipeline_mode=pl.Buffered(1))


def _layer_spec(stack, layer):
    rest = stack.shape[1:]
    return _single_buffered((None,) + rest, lambda *_: (layer,) + (0,) * len(rest))


@functools.lru_cache(maxsize=None)
def _dft_tables_f32():
    n = FFT_N
    k1 = np.arange(FFT_N1)
    n2 = np.arange(FFT_N2)

    def angle(n1):
        m = (FFT_N2 * n1[None, None, :] + n2[:, None, None]) * k1[None, :, None]
        return 2.0 * np.pi * (m % n) / n

    a = angle(np.arange(FFT_HALF_N1))
    c, s = np.cos(a), np.sin(a)
    g = np.empty((FFT_N2, FFT_N1, 2, 2, FFT_HALF_N1))
    g[:, :, 0, 0], g[:, :, 0, 1] = c, s
    g[:, :, 1, 0], g[:, :, 1, 1] = -s, c
    g = g.reshape(FFT_N2, 2 * FFT_N1, 2 * FFT_HALF_N1)

    a = angle(np.arange(FFT_N1))
    gf = np.stack([np.cos(a), -np.sin(a)], axis=2).reshape(FFT_N2, 2 * FFT_N1, FFT_N1)

    a = np.transpose(angle(np.arange(FFT_HALF_N1)), (0, 2, 1))
    c, s = np.cos(a) / n, np.sin(a) / n
    h = np.empty((FFT_N2, 2, FFT_HALF_N1, FFT_N1, 2))
    h[:, 0, :, :, 0], h[:, 0, :, :, 1] = c, -s
    h[:, 1, :, :, 0], h[:, 1, :, :, 1] = s, c
    h = h.reshape(FFT_N2, 2 * FFT_HALF_N1, 2 * FFT_N1)

    a = 2.0 * np.pi * np.outer(n2, n2) / FFT_N2
    c, s = np.cos(a), np.sin(a)
    f2 = np.block([[c, s], [-s, c]])
    f2i = np.block([[c, -s], [s, c]])

    kk = np.arange(CTX_N)
    a = 2.0 * np.pi * np.outer(kk, np.arange(CTX_LEN)) / CTX_N
    c, s = np.cos(a), np.sin(a)
    fc = np.block([[c, s], [-s, c]])
    a = 2.0 * np.pi * np.outer(kk, np.arange(CTX_N)) / CTX_N
    fcf = np.concatenate([np.cos(a), -np.sin(a)], axis=0)
    a = 2.0 * np.pi * np.outer(np.arange(CTX_LEN), kk) / CTX_N
    c, s = np.cos(a) / CTX_N, np.sin(a) / CTX_N
    fci = np.block([[c, -s], [s, c]])

    tables = dict(g=g, gf=gf, h=h, f2=f2, f2i=f2i, fc=fc, fcf=fcf, fci=fci)
    return {k: np.asarray(v, dtype=np.float32) for k, v in tables.items()}


def _dft_tables():
    return {k: jnp.asarray(v).astype(BF16) for k, v in _dft_tables_f32().items()}


SEQ_TILE = 1024
N1_PER_TILE = SEQ_TILE // FFT_N2


def _scatter_n2(s_ref, n2, val):
    for p in range(s_ref.shape[0]):
        s_ref[p, pl.ds(n2, 2 * FFT_N1, stride=FFT_PITCH), :] = val[:, p * LANES:(p + 1) * LANES]


def _gather_n2(s_ref, n2):
    return jnp.concatenate(
        [s_ref[p, pl.ds(n2, 2 * FFT_N1, stride=FFT_PITCH), :] for p in range(s_ref.shape[0])],
        axis=1)


def _spectrum_rows(s_ref, k1):
    base = pl.multiple_of(k1 * (2 * FFT_PITCH), 8)
    return base, jnp.concatenate(
        [jnp.concatenate([s_ref[p, pl.ds(base + o, FFT_N2), :] for p in range(s_ref.shape[0])],
                         axis=1) for o in (0, FFT_PITCH)], axis=0)


HY_FFN = 64
FILT_ROWS = 1024
FILT_TILES = 2
HIGHEST = lax.Precision.HIGHEST


def _tap_rows(row, seq, n2_major):
    if n2_major:
        n2, n1 = row >> int(math.log2(FFT_N1)), row & (FFT_N1 - 1)
        bwd = n1 >= FFT_HALF_N1
        m = FFT_N2 * (n1 - FFT_HALF_N1) + n2
        fwd_pos = FFT_N2 * n1 + n2
    else:
        bwd = row >= seq
        m = row - seq
        fwd_pos = row
    pos = jnp.where(bwd, seq - m, fwd_pos).astype(F32)
    bwd_f = jnp.where(bwd, 1.0, 0.0)
    keep_f = jnp.where(bwd & (m == 0), 0.0, 1.0)
    return pos, bwd_f, keep_f


def _filter_kernel(ec_ref, f1w_ref, f1b_ref, f2w_ref, f2b_ref, sf_ref, w3f_ref, w3b_ref, b3f_ref,
                   b3b_ref, dl_ref, skip_ref, *rest, seq, n2_major):
    if n2_major:
        gf_ref, f2_ref, o_ref, hid_ref, k_ref, s_ref = rest
    else:
        fcf_ref, o_ref, hid_ref, k_ref = rest
    n_rows = 2 * seq
    rb = min(FILT_ROWS, n_rows)

    def rows_of(i):
        r0 = pl.multiple_of(i * rb, rb)
        return r0, _tap_rows(r0 + lax.broadcasted_iota(jnp.int32, (rb, 1), 0), seq, n2_major)

    @pl.when(pl.program_id(0) == 0)
    def _():
        sf = sf_ref[...]

        def hidden(i, carry):
            r0, (pos, _, _) = rows_of(i)
            t = pos * (1.0 / (seq - 1))
            w = pos * (2.0 * math.pi / seq)
            emb = jnp.sin(w * ec_ref[0:1] + ec_ref[1:2]) + t * ec_ref[2:3]
            h = jnp.sin(sf * (jnp.dot(emb, f1w_ref[...], precision=HIGHEST,
                                      preferred_element_type=F32) + f1b_ref[...]))
            h = jnp.sin(sf * (jnp.dot(h.astype(BF16), f2w_ref[...].astype(BF16),
                                      preferred_element_type=F32) + f2b_ref[...]))
            hid_ref[pl.ds(r0, rb), :] = h
            return carry

        lax.fori_loop(0, n_rows // rb, hidden, 0)

    def taps(i, ssq):
        r0, (pos, bwd_f, keep_f) = rows_of(i)
        h = hid_ref[pl.ds(r0, rb), :].astype(BF16)
        neg_t = -(pos * (1.0 / (seq - 1)))
        ks = []
        for c in range(FILT_TILES):
            kf = jnp.dot(h, w3f_ref[c].astype(BF16), preferred_element_type=F32) + b3f_ref[c]
            kb = jnp.dot(h, w3b_ref[c].astype(BF16), preferred_element_type=F32) + b3b_ref[c]
            ks.append((kf + bwd_f * (kb - kf)) * jnp.exp(neg_t * dl_ref[c]) * keep_f)
        k = jnp.concatenate(ks, axis=1)
        k_ref[pl.ds(r0, rb), :] = k
        return ssq + jnp.sum(k * k, axis=0, keepdims=True)

    ssq = lax.fori_loop(0, n_rows // rb, taps, jnp.zeros((1, FILT_TILES * LANES), F32))
    scale = lax.rsqrt(ssq + NORM_EPS)
    skip = skip_ref[...]

    if n2_major:
        def stage1(n2, carry):
            x = k_ref[pl.ds(pl.multiple_of(n2 * FFT_N1, FFT_N1), FFT_N1), :] * scale
            _scatter_n2(s_ref, n2, jnp.dot(gf_ref[n2], x.astype(BF16),
                                           preferred_element_type=F32))
            return carry

        lax.fori_loop(0, FFT_N2, stage1, 0, unroll=8)

        def stage2(k1, carry):
            _, r = _spectrum_rows(s_ref, k1)
            z = jnp.dot(f2_ref[...], r.astype(BF16), preferred_element_type=F32)
            row = pl.multiple_of(k1 * (2 * FFT_N2), 2 * FFT_N2)
            o_ref[pl.ds(row, FFT_N2), :] = (z[:FFT_N2] + skip).astype(BF16)
            o_ref[pl.ds(row + FFT_N2, FFT_N2), :] = z[FFT_N2:].astype(BF16)
            return carry

        lax.fori_loop(0, FFT_N1, stage2, 0, unroll=8)
    else:
        z = jnp.dot(fcf_ref[...], (k_ref[...] * scale).astype(BF16), preferred_element_type=F32)
        o_ref[:n_rows, :] = (z[:n_rows] + skip).astype(BF16)
        o_ref[n_rows:, :] = z[n_rows:].astype(BF16)


def _filter_spectrum(seq, f1_w, f1_b, f2_w, f2_b, f3_w, f3_b, sin_freq, skip):
    n2_major = seq == SEQ
    t = _dft_tables()
    pad = lambda a, r, c: jnp.pad(a, ((0, r - a.shape[0]), (0, c - a.shape[1])))
    f = jnp.linspace(1e-4, HY_BANDS - 1, HY_BANDS, dtype=F32)
    zeros = lambda n: jnp.zeros((n,), F32)
    ones = lambda n: jnp.ones((n,), F32)
    rest = LANES - 1 - 2 * HY_BANDS
    ec = jnp.stack([
        jnp.concatenate([zeros(1), f, f, zeros(rest)]),
        jnp.concatenate([zeros(1), (math.pi / 2) * ones(HY_BANDS), math.pi * ones(HY_BANDS),
                         zeros(rest)]),
        jnp.concatenate([ones(1), zeros(LANES - 1)]),
    ] + [zeros(LANES)] * 5)
    row = lambda v: pad(v.reshape(1, -1), 1, LANES)
    n_tiles = HY_ORDER * W_A // LANES
    w3 = f3_w.reshape(HY_FFN, HY_ORDER, 2, W_A)
    b3 = f3_b.reshape(HY_ORDER, 2, W_A)
    tiles_w = lambda d: jnp.pad(
        w3[:, :, d, :].reshape(HY_FFN, n_tiles, LANES).transpose(1, 0, 2),
        ((0, 0), (0, LANES - HY_FFN), (0, 0)))
    tiles_b = lambda d: b3[:, d, :].reshape(n_tiles, 1, LANES)
    deltas = jnp.abs(jnp.linspace(math.log(HY_TARGET) / HY_SLOW_PCT,
                                  math.log(HY_TARGET) / HY_FAST_PCT, W_A, dtype=F32))
    dl = jnp.tile(deltas, HY_ORDER).reshape(n_tiles, 1, LANES)
    const = lambda shape: _single_buffered(shape, lambda j: (0,) * len(shape))
    tile = lambda shape: pl.BlockSpec((FILT_TILES,) + shape, lambda j: (j, 0, 0))
    ct = FILT_TILES * LANES
    args = [ec, pad(f1_w, LANES, LANES), row(f1_b), pad(f2_w, LANES, LANES), row(f2_b),
            row(sin_freq), tiles_w(0), tiles_w(1), tiles_b(0), tiles_b(1), dl, skip]
    in_specs = [const((8, LANES)), const((LANES, LANES)), const((1, LANES)),
                const((LANES, LANES)), const((1, LANES)), const((1, LANES)),
                tile((LANES, LANES)), tile((LANES, LANES)), tile((1, LANES)), tile((1, LANES)),
                tile((1, LANES)), pl.BlockSpec((1, ct), lambda j: (0, j))]
    scratch = [pltpu.VMEM((2 * seq, LANES), F32), pltpu.VMEM((2 * seq, ct), F32)]
    if n2_major:
        args += [t["gf"], t["f2"]]
        in_specs += [const(t["gf"].shape), const(t["f2"].shape)]
        scratch += [pltpu.VMEM((FILT_TILES, 2 * FFT_N1 * FFT_PITCH, LANES), F32)]
    else:
        args += [t["fcf"]]
        in_specs += [const(t["fcf"].shape)]
    return pl.pallas_call(
        functools.partial(_filter_kernel, seq=seq, n2_major=n2_major),
        grid=(n_tiles // FILT_TILES,),
        in_specs=in_specs,
        out_specs=pl.BlockSpec((4 * seq, ct), lambda j: (0, j)),
        out_shape=jax.ShapeDtypeStruct((4 * seq, HY_ORDER * W_A), BF16),
        scratch_shapes=scratch,
        compiler_params=_cparams("arbitrary"),
        name="hyena_filter" if n2_major else "hyena_ctx_filter",
    )(*args)


LC_OUT_CHUNKS = 8
LC_N2_PER_CHUNK = FFT_N2 // LC_OUT_CHUNKS


def _longconv_kernel(v_ref, gate_ref, kf_ref, g_ref, h_ref, f2_ref, f2i_ref, o_ref, s_ref):
    ct = v_ref.shape[2]
    t = pl.program_id(1)

    @pl.when(t == 0)
    def _():
        def stage1(n2, carry):
            r0 = pl.multiple_of(n2 * N1_PER_TILE, N1_PER_TILE)
            blk = [jnp.concatenate([v_ref[b, pl.ds(i * SEQ_TILE + r0, N1_PER_TILE), :]
                                    for i in range(SEQ // SEQ_TILE)], axis=0)
                   for b in range(BATCH)]
            x = jnp.concatenate([jnp.concatenate([blk[0], blk[2]], axis=1),
                                 jnp.concatenate([blk[1], blk[3]], axis=1)], axis=0)
            _scatter_n2(s_ref, n2, jnp.dot(g_ref[n2], x.astype(BF16), preferred_element_type=F32))
            return carry

        lax.fori_loop(0, FFT_N2, stage1, 0, unroll=8)

        def stage2(k1, carry):
            base, r = _spectrum_rows(s_ref, k1)
            z = jnp.dot(f2_ref[...], r.astype(BF16), preferred_element_type=F32)
            kf = kf_ref[pl.ds(pl.multiple_of(k1 * (2 * FFT_N2), 2 * FFT_N2), 2 * FFT_N2), :]
            kf = kf.astype(F32)
            kr = jnp.concatenate([kf[:FFT_N2]] * 2, axis=1)
            ki = jnp.concatenate([kf[FFT_N2:]] * 2, axis=1)
            zr, zi = z[:FFT_N2], z[FFT_N2:]
            p = jnp.concatenate([zr * kr - zi * ki, zr * ki + zi * kr], axis=0)
            q = jnp.dot(f2i_ref[...], p.astype(BF16), preferred_element_type=F32)
            for p in range(2):
                s_ref[p, pl.ds(base, FFT_N2), :] = q[:FFT_N2, p * ct:(p + 1) * ct]
                s_ref[p, pl.ds(base + FFT_PITCH, FFT_N2), :] = q[FFT_N2:, p * ct:(p + 1) * ct]
            return carry

        lax.fori_loop(0, FFT_N1, stage2, 0, unroll=8)

    def stage3(j, carry):
        n2 = t * LC_N2_PER_CHUNK + j
        rq = _gather_n2(s_ref, n2)
        y = jnp.dot(h_ref[n2], rq.astype(BF16), preferred_element_type=F32)
        r0 = pl.multiple_of(j * N1_PER_TILE, N1_PER_TILE)
        for b in range(BATCH):
            ri, pair = b % 2, b // 2
            for i in range(SEQ // SEQ_TILE):
                n1 = ri * FFT_HALF_N1 + i * N1_PER_TILE
                yb = y[n1:n1 + N1_PER_TILE, pair * ct:(pair + 1) * ct]
                o_ref[b, i, pl.ds(r0, N1_PER_TILE), :] = (
                    gate_ref[b, i, pl.ds(r0, N1_PER_TILE), :] * yb)
        return carry

    lax.fori_loop(0, LC_N2_PER_CHUNK, stage3, 0, unroll=8)


def _longconv(v, v_tile, gate, gate_tile, kf, kf_col):
    t = _dft_tables()
    ct = LANES
    n_seq_tiles = SEQ // SEQ_TILE
    rows = SEQ_TILE // LC_OUT_CHUNKS
    gate = gate.reshape(gate.shape[0], BATCH, n_seq_tiles, SEQ_TILE, ct)
    out = pl.pallas_call(
        _longconv_kernel,
        grid=(W_A // ct, LC_OUT_CHUNKS),
        in_specs=[
            _single_buffered((None, BATCH, SEQ, ct), lambda j, i: (v_tile + j, 0, 0, 0)),
            pl.BlockSpec((None, BATCH, n_seq_tiles, rows, ct),
                         lambda j, i: (gate_tile + j, 0, 0, i, 0)),
            pl.BlockSpec((2 * FFT_N, ct), lambda j, i: (0, kf_col + j)),
            _single_buffered(t["g"].shape, lambda j, i: (0, 0, 0)),
            _single_buffered(t["h"].shape, lambda j, i: (0, 0, 0)),
            _single_buffered(t["f2"].shape, lambda j, i: (0, 0)),
            _single_buffered(t["f2i"].shape, lambda j, i: (0, 0)),
        ],
        out_specs=pl.BlockSpec((None, BATCH, n_seq_tiles, rows, ct),
                               lambda j, i: (j, 0, 0, i, 0)),
        out_shape=jax.ShapeDtypeStruct((W_A // ct, BATCH, n_seq_tiles, SEQ_TILE, ct), F32),
        scratch_shapes=[pltpu.VMEM((2, 2 * FFT_N1 * FFT_PITCH, ct), F32)],
        compiler_params=_cparams("arbitrary", "arbitrary"),
        name="hyena_longconv",
    )(v, gate, kf, t["g"], t["h"], t["f2"], t["f2i"])
    return out.reshape(W_A // ct, BATCH, SEQ, ct)


def _ctx_conv_kernel(v_ref, gate_ref, kf_ref, fc_ref, fci_ref, o_ref):
    ct = v_ref.shape[2]
    x = jnp.concatenate([jnp.concatenate([v_ref[0], v_ref[2]], axis=1),
                         jnp.concatenate([v_ref[1], v_ref[3]], axis=1)], axis=0)
    z = jnp.dot(fc_ref[...], x.astype(BF16), preferred_element_type=F32)
    kf = kf_ref[...].astype(F32)
    kr = jnp.concatenate([kf[:CTX_N]] * 2, axis=1)
    ki = jnp.concatenate([kf[CTX_N:]] * 2, axis=1)
    zr, zi = z[:CTX_N], z[CTX_N:]
    p = jnp.concatenate([zr * kr - zi * ki, zr * ki + zi * kr], axis=0)
    y = jnp.dot(fci_ref[...], p.astype(BF16), preferred_element_type=F32)
    for b in range(BATCH):
        ri, pair = b % 2, b // 2
        o_ref[b] = gate_ref[b] * y[ri * CTX_LEN:(ri + 1) * CTX_LEN, pair * ct:(pair + 1) * ct]


def _ctx_conv(v, v_tile, gate, gate_tile, kf, kf_col):
    t = _dft_tables()
    ct = LANES
    return pl.pallas_call(
        _ctx_conv_kernel,
        grid=(W_A // ct,),
        in_specs=[
            pl.BlockSpec((None, BATCH, CTX_LEN, ct), lambda j: (v_tile + j, 0, 0, 0)),
            pl.BlockSpec((None, BATCH, CTX_LEN, ct), lambda j: (gate_tile + j, 0, 0, 0)),
            pl.BlockSpec((2 * CTX_N, ct), lambda j: (0, kf_col + j)),
            pl.BlockSpec(t["fc"].shape, lambda j: (0, 0)),
            pl.BlockSpec(t["fci"].shape, lambda j: (0, 0)),
        ],
        out_specs=pl.BlockSpec((None, BATCH, CTX_LEN, ct), lambda j: (j, 0, 0, 0)),
        out_shape=jax.ShapeDtypeStruct((W_A // ct, BATCH, CTX_LEN, ct), F32),
        compiler_params=_cparams("arbitrary"),
        name="hyena_ctx_conv",
    )(v, gate, kf, t["fc"], t["fci"])


MOD_ROWS = 8
CTX_MOD_ROW = BATCH


def _ada_kernel(c_ref, w_ref, b_ref, o_ref):
    cv = c_ref[...]
    s = cv * jax.nn.sigmoid(cv)
    o_ref[0] = jnp.dot(s.astype(BF16), w_ref[0].astype(BF16),
                       preferred_element_type=F32) + b_ref[0]


def _ada_mods(c, c_ctx, w_ada, b_ada):
    cv = jnp.concatenate(
        [c, c_ctx[None], jnp.zeros((MOD_ROWS - BATCH - 1, D_MODEL), F32)], axis=0)
    n = N_MOD * D_MODEL
    tn = n // 4
    return pl.pallas_call(
        _ada_kernel,
        grid=(DEPTH, n // tn),
        in_specs=[
            pl.BlockSpec((MOD_ROWS, D_MODEL), lambda l, j: (0, 0)),
            pl.BlockSpec((1, D_MODEL, tn), lambda l, j: (l, 0, j)),
            pl.BlockSpec((1, 1, tn), lambda l, j: (l, 0, j)),
        ],
        out_specs=pl.BlockSpec((1, MOD_ROWS, tn), lambda l, j: (l, 0, j)),
        out_shape=jax.ShapeDtypeStruct((DEPTH, MOD_ROWS, n), F32),
        compiler_params=_cparams("arbitrary", "arbitrary"),
        name="ada_mods",
    )(cv, w_ada, b_ada.reshape(DEPTH, 1, n))


def _mod_spec(m, row_fn):
    return pl.BlockSpec((1, 1, D_MODEL), lambda i, *_: (row_fn(i), 0, m))


def _rms_mod(x, g, shift, scale):
    ms = jnp.mean(x * x, axis=-1, keepdims=True)
    return (x * lax.rsqrt(ms + NORM_EPS) * g) * (1.0 + scale) + shift


GELU_C = math.sqrt(2.0 / math.pi)


def _gelu(x):
    hx = 0.5 * x
    return hx + hx * jnp.tanh(x * (GELU_C + (GELU_C * 0.044715) * (x * x)))


HALO = 8


def _dwconv(z_all, tm, w, b, left, valid):
    n = z_all.shape[0]
    y = b + w[left:left + 1] * z_all[:tm]
    for k in range(w.shape[0]):
        d = k - left
        if d == 0:
            continue
        s = pltpu.roll(z_all, (-d) % n, 0)[:tm]
        if valid is not None:
            s = s * valid(d)
        y = y + s * w[k:k + 1]
    return y


def _halo_specs(tm, n_rows):
    per = tm // HALO
    last = n_rows // HALO - 1
    return [
        pl.BlockSpec((tm, D_MODEL), lambda i: (i, 0)),
        pl.BlockSpec((HALO, D_MODEL), lambda i: (jnp.maximum(i * per - 1, 0), 0)),
        pl.BlockSpec((HALO, D_MODEL), lambda i: (jnp.minimum((i + 1) * per, last), 0)),
    ]


def _normed_tile(x_ref, xp_ref, xn_ref, sh_ref, sc_ref, g_ref, seq_len):
    tm = x_ref.shape[0]
    g, shift, scale = g_ref[...], sh_ref[0], sc_ref[0]
    h = _rms_mod(x_ref[...], g, shift, scale)
    hn = _rms_mod(xn_ref[...], g, shift, scale)
    hp = _rms_mod(xp_ref[...], g, shift, scale)
    if seq_len >= tm:
        r0 = pl.program_id(0) * tm
        hp = hp * jnp.where((r0 & (seq_len - 1)) == 0, 0.0, 1.0)
        hn = hn * jnp.where(((r0 + tm) & (seq_len - 1)) == 0, 0.0, 1.0)
        valid = None
    else:
        hp, hn = jnp.zeros_like(hp), jnp.zeros_like(hn)
        pos = lax.broadcasted_iota(jnp.int32, (tm, 1), 0) & (seq_len - 1)
        valid = lambda d: jnp.where((pos + d >= 0) & (pos + d < seq_len), 1.0, 0.0)
    return jnp.concatenate([h, hn, hp], axis=0).astype(BF16), valid


EVEN_TM = SEQ_TILE
HY_COLS = 3 * W_A
HY_TILES = HY_COLS // LANES


def _in_even_kernel(x_ref, xp_ref, xn_ref, sh_ref, sc_ref, g_ref, w_ref, cw_ref, cb_ref,
                    lng_ref, sgw_ref, sgb_ref, za_ref, yb_ref, *, seq_len, n2_major):
    tm = x_ref.shape[0]
    h_all, valid = _normed_tile(x_ref, xp_ref, xn_ref, sh_ref, sc_ref, g_ref, seq_len)
    h = h_all[:tm]
    cw = 4 * LANES
    for cc in range(HY_COLS // cw):
        cols = slice(cc * cw, (cc + 1) * cw)
        z_all = jnp.dot(h_all, w_ref[:, cols], preferred_element_type=F32)
        y = _dwconv(z_all, tm, cw_ref[:, cols], cb_ref[:, cols], 1, valid)
        for c in range(cw // LANES):
            tile = cc * (cw // LANES) + c
            yc = y[:, c * LANES:(c + 1) * LANES]
            if n2_major:
                for j in range(N1_PER_TILE):
                    za_ref[tile, pl.ds(j, FFT_N2, stride=N1_PER_TILE), :] = (
                        yc[j * FFT_N2:(j + 1) * FFT_N2])
            else:
                za_ref[tile] = yc
    u = _gelu(jnp.dot(h, w_ref[:, HY_COLS:HY_COLS + W_B], preferred_element_type=F32))
    vb = _gelu(jnp.dot(h, w_ref[:, HY_COLS + W_B:], preferred_element_type=F32))
    vc = vb - jnp.mean(vb, axis=-1, keepdims=True)
    vn = vc * lax.rsqrt(jnp.mean(vc * vc, axis=-1, keepdims=True) + NORM_EPS) * lng_ref[...]
    vn = vn.astype(BF16)
    for ch in range(tm // CHUNK):
        rows = slice(ch * CHUNK, (ch + 1) * CHUNK)
        for q in range(SGU_GROUPS):
            cols = slice(q * SGU_DH, (q + 1) * SGU_DH)
            s = jnp.dot(sgw_ref[q], vn[rows, cols], preferred_element_type=F32) + sgb_ref[:, cols]
            yb_ref[rows, cols] = (u[rows, cols] * s).astype(BF16)


def _in_even(x, mods, mod_row, g, w, layer, cw, cb, lng, sgw, sgb, seq_len, n2_major):
    n_rows = x.shape[0]
    tm = EVEN_TM
    za_shape = (HY_TILES, n_rows, LANES)
    za_spec = pl.BlockSpec((HY_TILES, tm, LANES), lambda i: (0, i, 0))
    const = lambda shape: _single_buffered(shape, lambda i: (0,) * len(shape))
    return pl.pallas_call(
        functools.partial(_in_even_kernel, seq_len=seq_len, n2_major=n2_major),
        grid=(n_rows // tm,),
        in_specs=_halo_specs(tm, n_rows) + [
            _mod_spec(0, mod_row), _mod_spec(1, mod_row),
            const((1, D_MODEL)), _layer_spec(w, layer), const(cw.shape), const(cb.shape),
            const(lng.shape), const(sgw.shape), const(sgb.shape),
        ],
        out_specs=[za_spec, pl.BlockSpec((tm, W_B), lambda i: (i, 0))],
        out_shape=[jax.ShapeDtypeStruct(za_shape, F32),
                   jax.ShapeDtypeStruct((n_rows, W_B), BF16)],
        compiler_params=_cparams("arbitrary"),
        name="in_proj_even",
    )(x, x, x, mods, mods, g, w, cw, cb, lng, sgw, sgb)


FFN_TM = SEQ_TILE
FFN_SUB = 256
GRID_H = SEQ // GRID_W
GT_ROWS = FFN_TM // GRID_W
GT_PITCH = GRID_W + 8


def _even_mix(ya_ref, yb_ref, w_ref, n2_major):
    acc = jnp.dot(yb_ref[...], w_ref[W_A:, :], preferred_element_type=F32)
    for c in range(W_A // LANES):
        if n2_major:
            yac = jnp.concatenate(
                [ya_ref[c, pl.ds(j, FFT_N2, stride=N1_PER_TILE), :]
                 for j in range(N1_PER_TILE)], axis=0)
        else:
            yac = ya_ref[c]
        acc = acc + jnp.dot(yac.astype(BF16), w_ref[c * LANES:(c + 1) * LANES, :],
                            preferred_element_type=F32)
    return acc


def _ffn_kernel(*refs, final_norm, mix, n2_major, swap_grid):
    refs = list(refs)
    x_ref, sh_ref, sc_ref, gt_ref, g_ref, w_in_ref, wo_ref, fg_ref = refs[:8]
    del refs[:8]
    if mix:
        ya_ref, yb_ref, gm_ref, wm_ref = refs[:4]
        del refs[:4]
    o_ref = refs.pop(0)
    x = x_ref[...]
    if mix:
        x = x + gm_ref[0] * _even_mix(ya_ref, yb_ref, wm_ref, n2_major)
    h = _rms_mod(x, g_ref[...], sh_ref[0], sc_ref[0]).astype(BF16)
    acts = []
    for c0 in range(0, D_FF, FFN_SUB):
        hz = 0.5 * jnp.dot(h, w_in_ref[:, c0:c0 + FFN_SUB], preferred_element_type=F32)
        z2 = jnp.dot(h, w_in_ref[:, D_FF + c0:D_FF + c0 + FFN_SUB], preferred_element_type=F32)
        acts.append(((hz + hz * jnp.tanh(hz)) * z2).astype(BF16))
    y = x + gt_ref[0] * jnp.dot(jnp.concatenate(acts, axis=1), wo_ref[...],
                                preferred_element_type=F32)
    if final_norm:
        y = y * lax.rsqrt(jnp.mean(y * y, axis=-1, keepdims=True) + NORM_EPS) * fg_ref[...]
    if not swap_grid:
        o_ref[...] = y
        return
    s_ref = refs.pop(0)
    nt = D_MODEL // LANES
    for t in range(nt):
        for j in range(GT_ROWS):
            s_ref[t, j * GT_PITCH:j * GT_PITCH + GRID_W, :] = (
                y[j * GRID_W:(j + 1) * GRID_W, t * LANES:(t + 1) * LANES])
    for c in range(GRID_W):
        o_ref[c] = jnp.concatenate(
            [s_ref[t, pl.ds(c, GT_ROWS, stride=GT_PITCH), :] for t in range(nt)], axis=1)


def _ffn(x, mods, mod_row, g, w_in, w_out, layer, final_g, final_norm, mix=None,
         swap_grid=False):
    n_rows = x.shape[0]
    tm = FFN_TM
    const = lambda shape: _single_buffered(shape, lambda i: (0,) * len(shape))
    in_specs = [
        pl.BlockSpec((tm, D_MODEL), lambda i: (i, 0)),
        _mod_spec(3, mod_row), _mod_spec(4, mod_row), _mod_spec(5, mod_row),
        const((1, D_MODEL)), _layer_spec(w_in, layer), _layer_spec(w_out, layer),
        const((1, D_MODEL)),
    ]
    args = [x, mods, mods, mods, g, w_in, w_out, final_g]
    n2_major = False
    if mix is not None:
        ya, yb, (w_mix, mix_layer), n2_major = mix
        nt = W_A // LANES
        in_specs += [pl.BlockSpec((nt, tm, LANES), lambda i: (0, i, 0)),
                     pl.BlockSpec((tm, W_B), lambda i: (i, 0)),
                     _mod_spec(2, mod_row), _layer_spec(w_mix, mix_layer)]
        args += [ya.reshape(nt, n_rows, LANES), yb, mods, w_mix]
    scratch = []
    if swap_grid:
        per_seq = SEQ // tm
        out_spec = pl.BlockSpec((None, GRID_W, GT_ROWS, D_MODEL),
                                lambda i: (i // per_seq, 0, i % per_seq, 0))
        out_shape = jax.ShapeDtypeStruct((n_rows // SEQ, GRID_W, GRID_H, D_MODEL), F32)
        scratch = [pltpu.VMEM((D_MODEL // LANES, GT_ROWS * GT_PITCH, LANES), F32)]
    else:
        out_spec = pl.BlockSpec((tm, D_MODEL), lambda i: (i, 0))
        out_shape = jax.ShapeDtypeStruct((n_rows, D_MODEL), F32)
    out = pl.pallas_call(
        functools.partial(_ffn_kernel, final_norm=final_norm, mix=mix is not None,
                          n2_major=n2_major, swap_grid=swap_grid),
        grid=(n_rows // tm,),
        in_specs=in_specs,
        out_specs=out_spec,
        out_shape=out_shape,
        scratch_shapes=scratch,
        compiler_params=_cparams("arbitrary"),
        name="ffn",
    )(*args)
    return out.reshape(n_rows, D_MODEL)


ODD_TM = 1024
RNN_TILES = D_RNN // LANES


def _in_odd_kernel(x_ref, xp_ref, xn_ref, sh_ref, sc_ref, g_ref, w_ref, cw_ref, cb_ref,
                   gate_ref, xl_ref, *, seq_len, chunk):
    tm = x_ref.shape[0]
    h_all, valid = _normed_tile(x_ref, xp_ref, xn_ref, sh_ref, sc_ref, g_ref, seq_len)
    gate_ref[...] = _gelu(jnp.dot(h_all[:tm], w_ref[:, :D_RNN],
                                  preferred_element_type=F32)).astype(BF16)
    cw = 4 * LANES
    for c0 in range(0, D_RNN, cw):
        cols = slice(c0, min(c0 + cw, D_RNN))
        wcols = slice(D_RNN + cols.start, D_RNN + cols.stop)
        z_all = jnp.dot(h_all, w_ref[:, wcols], preferred_element_type=F32)
        y = _dwconv(z_all, tm, cw_ref[:, cols], cb_ref[:, cols], 2, valid)
        kg = chunk // RG_GROUPS
        for c in range((cols.stop - cols.start) // LANES):
            yc = y[:, c * LANES:(c + 1) * LANES]
            for r0 in range(0, tm, chunk):
                for gq in range(RG_GROUPS):
                    xl_ref[c0 // LANES + c, pl.ds(r0 + gq, kg, stride=RG_GROUPS), :] = (
                        yc[r0 + gq * kg:r0 + (gq + 1) * kg])


def _in_odd(x, mods, mod_row, g, w, layer, cw, cb, seq_len, chunk):
    n_rows = x.shape[0]
    tm = ODD_TM
    const = lambda shape: _single_buffered(shape, lambda i: (0,) * len(shape))
    return pl.pallas_call(
        functools.partial(_in_odd_kernel, seq_len=seq_len, chunk=chunk),
        grid=(n_rows // tm,),
        in_specs=_halo_specs(tm, n_rows) + [
            _mod_spec(0, mod_row), _mod_spec(1, mod_row),
            const((1, D_MODEL)), _layer_spec(w, layer), const(cw.shape), const(cb.shape),
        ],
        out_specs=[pl.BlockSpec((tm, D_RNN), lambda i: (i, 0)),
                   pl.BlockSpec((RNN_TILES, tm, LANES), lambda i: (0, i, 0))],
        out_shape=[jax.ShapeDtypeStruct((n_rows, D_RNN), BF16),
                   jax.ShapeDtypeStruct((RNN_TILES, n_rows, LANES), F32)],
        compiler_params=_cparams("arbitrary"),
        name="in_proj_odd",
    )(x, x, x, mods, mods, g, w, cw, cb)


RG_T = 512
RG_NCH = SEQ // RG_T
RG_GROUPS = 8
RG_PLANES_PER_PASS = 6
SQRT_GUARD = 1e-30
RG_WIN = 3 * LANES


def _rg_window_start(j):
    return min(max(LANES * (j - 1), 0), D_RNN - RG_WIN)


def _rg_chunk(x_ref, w_ref, ba_ref, bx_ref, lam_ref, a_pl, b_pl, carry_ref, reverse, emit):
    t_rows = x_ref.shape[1]
    kg = t_rows // RG_GROUPS
    xb = [x_ref[j].astype(BF16) for j in range(RNN_TILES)]
    for j in range(RNN_TILES):
        tile = slice(j * LANES, (j + 1) * LANES)
        wt = _rg_window_start(j) // LANES
        pre = jnp.dot(jnp.concatenate(xb[wt:wt + RG_WIN // LANES], axis=1), w_ref[j],
                      preferred_element_type=F32)
        lam = lam_ref[:, tile]
        softplus_neg = jnp.maximum(-lam, 0.0) + jnp.log1p(jnp.exp(-jnp.abs(lam)))
        th_r = jnp.tanh(pre[:, :LANES] + ba_ref[:, tile])
        th_i = jnp.tanh(pre[:, LANES:] + bx_ref[:, tile])
        c3 = (-0.5 * RG_C * math.log2(math.e)) * softplus_neg
        av = jnp.exp2(c3 + c3 * th_r)
        hx = 0.5 * x_ref[j]
        y = 1.0 - av * av
        a_pl[j, 0:t_rows, :] = av
        b_pl[j, 0:t_rows, :] = (y * lax.rsqrt(jnp.maximum(y, SQRT_GUARD))) * (hx + hx * th_i)

    def rows_k(pl_ref, j, k):
        return pl_ref[j, k * RG_GROUPS:(k + 1) * RG_GROUPS, :]

    order = list(range(kg))[::-1] if reverse else list(range(kg))
    groups = list(range(RG_GROUPS))[::-1] if reverse else list(range(RG_GROUPS))
    for j0 in range(0, RNN_TILES, RG_PLANES_PER_PASS):
        planes = range(j0, min(j0 + RG_PLANES_PER_PASS, RNN_TILES))
        big_a, big_b = {}, {}
        for n, k in enumerate(order):
            for j in planes:
                ak, bk = rows_k(a_pl, j, k), rows_k(b_pl, j, k)
                if n == 0:
                    big_a[j], big_b[j] = ak, bk
                else:
                    big_b[j] = ak * big_b[j] + bk
                    big_a[j] = ak * big_a[j]
        h = {}
        for j in planes:
            c = carry_ref[j, 0:1, :]
            rows = [None] * RG_GROUPS
            for gq in groups:
                rows[gq] = c
                c = big_a[j][gq:gq + 1] * c + big_b[j][gq:gq + 1]
            carry_ref[j, 0:1, :] = c
            h[j] = jnp.concatenate(rows, axis=0)
        for k in order:
            for j in planes:
                h[j] = rows_k(a_pl, j, k) * h[j] + rows_k(b_pl, j, k)
                b_pl[j, k * RG_GROUPS:(k + 1) * RG_GROUPS, :] = h[j]
        for j in planes:
            emit(j, jnp.concatenate(
                [b_pl[j, pl.ds(gq, kg, stride=RG_GROUPS), :] for gq in range(RG_GROUPS)], axis=0))


def _rg_bwd_kernel(xc_ref, xl_ref, w_ref, ba_ref, bx_ref, lam_ref, oc_ref, ol_ref,
                   a_pl, b_pl, carry_ref):
    def emitter(o_ref):
        def emit(j, hcur):
            o_ref[:, j * LANES:(j + 1) * LANES] = hcur.astype(BF16)
        return emit

    @pl.when(pl.program_id(1) == 0)
    def _():
        carry_ref[...] = jnp.zeros_like(carry_ref)
        _rg_chunk(xc_ref, w_ref, ba_ref, bx_ref, lam_ref, a_pl, b_pl, carry_ref, True,
                  emitter(oc_ref))

    @pl.when(pl.program_id(1) > 0)
    def _():
        _rg_chunk(xl_ref, w_ref, ba_ref, bx_ref, lam_ref, a_pl, b_pl, carry_ref, True,
                  emitter(ol_ref))


def _rg_fwd_kernel(xc_ref, xl_ref, w_ref, ba_ref, bx_ref, lam_ref, gc_ref, gl_ref, hc_ref, hl_ref,
                   rc_ref, rl_ref, gtc_ref, gtl_ref, wo_ref, oc_ref, ol_ref,
                   a_pl, b_pl, carry_ref, y_ref):
    def run(x_ref, g_ref, hb_ref, res_ref, gt_ref, o_ref):
        t_rows = x_ref.shape[1]

        def emit(j, hcur):
            tile = slice(j * LANES, (j + 1) * LANES)
            y_ref[0:t_rows, tile] = (g_ref[:, tile].astype(F32)
                                     * (hcur + hb_ref[:, tile].astype(F32))).astype(BF16)

        _rg_chunk(x_ref, w_ref, ba_ref, bx_ref, lam_ref, a_pl, b_pl, carry_ref, False, emit)
        acc = jnp.dot(y_ref[0:t_rows, :], wo_ref[...], preferred_element_type=F32)
        o_ref[...] = res_ref[...] + gt_ref[0] * acc

    @pl.when(pl.program_id(1) == 0)
    def _():
        carry_ref[...] = jnp.zeros_like(carry_ref)
        run(xc_ref, gc_ref, hc_ref, rc_ref, gtc_ref, oc_ref)

    @pl.when(pl.program_id(1) > 0)
    def _():
        run(xl_ref, gl_ref, hl_ref, rl_ref, gtl_ref, ol_ref)


def _rg_scan(xc, xl, w, ba, bx, lam, reverse, fwd_args=None):
    if reverse:
        chunk = lambda s: RG_NCH - jnp.maximum(s, 1)
    else:
        chunk = lambda s: jnp.maximum(s - 1, 0)
    ctx_spec = lambda d: pl.BlockSpec((None, CTX_LEN, d), lambda b, s: (b, 0, 0))
    lat_spec = lambda d: pl.BlockSpec((None, RG_T, d), lambda b, s: (b, chunk(s), 0))
    const = lambda shape: _single_buffered(shape, lambda b, s: (0,) * len(shape))
    in_specs = [pl.BlockSpec((RNN_TILES, CTX_LEN, LANES), lambda b, s: (0, b, 0)),
                pl.BlockSpec((RNN_TILES, RG_T, LANES), lambda b, s: (0, b * RG_NCH + chunk(s), 0)),
                const(w.shape), const(ba.shape), const(bx.shape), const(lam.shape)]
    args = [xc, xl, w, ba, bx, lam]
    scratch = [pltpu.VMEM((RNN_TILES, RG_T, LANES), F32),
               pltpu.VMEM((RNN_TILES, RG_T, LANES), F32),
               pltpu.VMEM((RNN_TILES, 8, LANES), F32)]
    if reverse:
        body, name, width, dtype = _rg_bwd_kernel, "rglru_bwd", D_RNN, BF16
    else:
        gate_c, gate_l, hb_c, hb_l, res_c, res_l, mods, (w_out, out_layer) = fwd_args
        in_specs += [ctx_spec(D_RNN), lat_spec(D_RNN), ctx_spec(D_RNN), lat_spec(D_RNN),
                     ctx_spec(D_MODEL), lat_spec(D_MODEL),
                     _mod_spec(2, lambda b: CTX_MOD_ROW), _mod_spec(2, lambda b: b),
                     _layer_spec(w_out, out_layer)]
        args += [gate_c, gate_l, hb_c, hb_l, res_c, res_l, mods, mods, w_out]
        scratch += [pltpu.VMEM((RG_T, D_RNN), BF16)]
        body, name, width, dtype = _rg_fwd_kernel, "rglru_fwd", D_MODEL, F32
    return pl.pallas_call(
        body,
        grid=(BATCH, 1 + RG_NCH),
        in_specs=in_specs,
        out_specs=[ctx_spec(width), lat_spec(width)],
        out_shape=[jax.ShapeDtypeStruct((BATCH, CTX_LEN, width), dtype),
                   jax.ShapeDtypeStruct((BATCH, SEQ, width), dtype)],
        scratch_shapes=scratch,
        compiler_params=_cparams("arbitrary", "arbitrary"),
        name=name,
    )(*args)


def _rg_gate_weights(wa, wx):
    def window(w, j):
        ws = _rg_window_start(j)
        win = None
        for h in range(RG_HEADS):
            c0, c1 = max(h * RG_DH, j * LANES), min((h + 1) * RG_DH, (j + 1) * LANES)
            if c0 >= c1:
                continue
            r0 = h * RG_DH - ws
            assert 0 <= r0 and r0 + RG_DH <= RG_WIN
            blk = jnp.pad(w[h, :, c0 - h * RG_DH:c1 - h * RG_DH],
                          ((r0, RG_WIN - RG_DH - r0), (c0 - j * LANES, (j + 1) * LANES - c1)))
            win = blk if win is None else win + blk
        return win

    wins = [jnp.concatenate([window(wa, j), window(wx, j)], axis=1) for j in range(RNN_TILES)]
    return (0.5 * jnp.stack(wins)).astype(BF16)


def _col_major(layer):
    return layer < DEPTH and layer % 2 == 1 and (layer // 2) % 2 == 1


def kernel(x, c, ctx, c_ctx, w_ada, b_ada, norm_mix_g, norm_ffn_g, w_in_even, w_out_even, hy_conv_w, hy_conv_b, hy_f1_w, hy_f1_b, hy_f2_w, hy_f2_b, hy_f3_w, hy_f3_b, hy_sin_freq, hy_skip, sgu_ln_g, sgu_w, sgu_b, w_in_odd, rg_conv_w, rg_conv_b, rg_wa, rg_ba, rg_wx, rg_bx, rg_lam, w_out_odd, w_ffn_in, w_ffn_out, final_norm_g):
    mods_all = _ada_mods(c, c_ctx, w_ada, b_ada)
    xs = x.reshape(BATCH * SEQ, D_MODEL)
    cs = ctx.reshape(BATCH * CTX_LEN, D_MODEL)
    lat_row_even = lambda i: i // (SEQ // EVEN_TM)
    lat_row_odd = lambda i: i // (SEQ // ODD_TM)
    lat_row_ffn = lambda i: i // (SEQ // FFN_TM)
    ctx_row = lambda i: CTX_MOD_ROW
    n_tiles = W_A // LANES
    w_in_even, w_out_even, w_in_odd, w_out_odd, w_ffn_in, w_ffn_out = (
        w.astype(BF16) for w in (w_in_even, w_out_even, w_in_odd, w_out_odd, w_ffn_in, w_ffn_out))
    for l in range(DEPTH):
        run_ctx = l < DEPTH - 1
        is_rec = l % 2 == 1
        i = l // 2
        mods = mods_all[l].reshape(MOD_ROWS, 1, N_MOD * D_MODEL)
        g_mix = norm_mix_g[l].reshape(1, D_MODEL)
        mix_l = mix_c = None
        if is_rec:
            cw, cb = rg_conv_w[i], rg_conv_b[i].reshape(1, D_RNN)
            gate_l, xl = _in_odd(xs, mods, lat_row_odd, g_mix, w_in_odd, i, cw, cb, SEQ, RG_T)
            gate_c, xc = _in_odd(cs, mods, ctx_row, g_mix, w_in_odd, i, cw, cb, CTX_LEN, CTX_LEN)
            gate_l = gate_l.reshape(BATCH, SEQ, D_RNN)
            gate_c = gate_c.reshape(BATCH, CTX_LEN, D_RNN)
            row = lambda v: v.reshape(1, D_RNN)
            half = lambda v: 0.5 * row(v)
            hb_c, hb_l = _rg_scan(xc, xl, _rg_gate_weights(rg_wa[i, 1], rg_wx[i, 1]),
                                  half(rg_ba[i, 1]), half(rg_bx[i, 1]), row(rg_lam[i, 1]), True)
            cs_new, xs = _rg_scan(
                xc, xl, _rg_gate_weights(rg_wa[i, 0], rg_wx[i, 0]),
                half(rg_ba[i, 0]), half(rg_bx[i, 0]), row(rg_lam[i, 0]), False,
                (gate_c, gate_l, hb_c, hb_l, cs.reshape(BATCH, CTX_LEN, D_MODEL),
                 xs.reshape(BATCH, SEQ, D_MODEL), mods, (w_out_odd, i)))
            xs = xs.reshape(BATCH * SEQ, D_MODEL)
            if run_ctx:
                cs = cs_new.reshape(BATCH * CTX_LEN, D_MODEL)
        else:
            cw, cb = hy_conv_w[i], hy_conv_b[i].reshape(1, HY_COLS)
            lng = sgu_ln_g[i].reshape(1, W_B)
            sgw = sgu_w[i].astype(BF16)
            sgb = jnp.repeat(sgu_b[i].T, SGU_DH, axis=1)
            skip = hy_skip[i].reshape(1, HY_ORDER * W_A)
            fargs = (hy_f1_w[i], hy_f1_b[i], hy_f2_w[i], hy_f2_b[i], hy_f3_w[i], hy_f3_b[i],
                     hy_sin_freq[i])
            kf = _filter_spectrum(SEQ, *fargs, skip)
            za, yb = _in_even(xs, mods, lat_row_even, g_mix, w_in_even, i, cw, cb, lng, sgw, sgb,
                              SEQ, True)
            za = za.reshape(HY_TILES, BATCH, SEQ, LANES)
            y1 = _longconv(za, 0, za, n_tiles, kf, 0)
            ya = _longconv(y1, 0, za, 2 * n_tiles, kf, n_tiles)
            mix_l = (ya, yb, (w_out_even, i), True)
            if run_ctx:
                kfc = _filter_spectrum(CTX_LEN, *fargs, skip)
                zc, ybc = _in_even(cs, mods, ctx_row, g_mix, w_in_even, i, cw, cb, lng, sgw, sgb,
                                   CTX_LEN, False)
                zc = zc.reshape(HY_TILES, BATCH, CTX_LEN, LANES)
                y1c = _ctx_conv(zc, 0, zc, n_tiles, kfc, 0)
                yac = _ctx_conv(y1c, 0, zc, 2 * n_tiles, kfc, n_tiles)
                mix_c = (yac, ybc, (w_out_even, i), False)
        g_ffn = norm_ffn_g[l].reshape(1, D_MODEL)
        fg = final_norm_g.reshape(1, D_MODEL)
        swap = _col_major(l) != _col_major(l + 1)
        xs = _ffn(xs, mods, lat_row_ffn, g_ffn, w_ffn_in, w_ffn_out, l, fg, l == DEPTH - 1,
                  mix_l, swap)
        if run_ctx:
            cs = _ffn(cs, mods, ctx_row, g_ffn, w_ffn_in, w_ffn_out, l, fg, False, mix_c)
    return xs.reshape(BATCH, SEQ, D_MODEL)
```

```python
import functools
import math

import numpy as np
import jax
import jax.numpy as jnp
from jax import lax
from jax.experimental import pallas as pl
from jax.experimental.pallas import tpu as pltpu

F32 = jnp.float32
BF16 = jnp.bfloat16

D_MODEL = 1024
BATCH = 4
SEQ = 4096
DEPTH = 4
GRID_W = 64
CTX_LEN = 256
N_MOD = 6
NORM_EPS = 1e-6
W_A = D_MODEL // 2
HY_ORDER = 2
HY_SHORT = 3
HY_BANDS = 16
HY_TARGET = 1e-2
HY_FAST_PCT = 0.3
HY_SLOW_PCT = 1.5
W_B = D_MODEL // 2
SGU_GROUPS = 4
SGU_DH = W_B // SGU_GROUPS
CHUNK = 128
D_RNN = ((4 * D_MODEL // 3 + 127) // 128) * 128
RG_HEADS = 16
RG_DH = D_RNN // RG_HEADS
RG_CONV = 4
RG_C = 8.0
D_FF = ((8 * D_MODEL // 3 + 255) // 256) * 256

VMEM_BYTES_V7X = 64 * 1024 * 1024
VMEM_LIMIT = VMEM_BYTES_V7X - 4 * 1024 * 1024
LANES = 128

FFT_N = 2 * SEQ
FFT_N1 = 64
FFT_N2 = 128
FFT_HALF_N1 = FFT_N1 // 2
FFT_PITCH = FFT_N2 + 8
CTX_N = 2 * CTX_LEN


def _cparams(*sem):
    return pltpu.CompilerParams(dimension_semantics=sem, vmem_limit_bytes=VMEM_LIMIT)


def _single_buffered(block_shape, index_map):
    return pl.BlockSpec(block_shape, index_map, pipeline_mode=pl.Buffered(1))


def _layer_spec(stack, layer):
    rest = stack.shape[1:]
    return _single_buffered((None,) + rest, lambda *_: (layer,) + (0,) * len(rest))


@functools.lru_cache(maxsize=None)
def _dft_tables_f32():
    n = FFT_N
    k1 = np.arange(FFT_N1)
    n2 = np.arange(FFT_N2)

    def angle(n1):
        m = (FFT_N2 * n1[None, None, :] + n2[:, None, None]) * k1[None, :, None]
        return 2.0 * np.pi * (m % n) / n

    a = angle(np.arange(FFT_HALF_N1))
    c, s = np.cos(a), np.sin(a)
    g = np.empty((FFT_N2, FFT_N1, 2, 2, FFT_HALF_N1))
    g[:, :, 0, 0], g[:, :, 0, 1] = c, s
    g[:, :, 1, 0], g[:, :, 1, 1] = -s, c
    g = g.reshape(FFT_N2, 2 * FFT_N1, 2 * FFT_HALF_N1)

    a = angle(np.arange(FFT_N1))
    gf = np.stack([np.cos(a), -np.sin(a)], axis=2).reshape(FFT_N2, 2 * FFT_N1, FFT_N1)

    a = np.transpose(angle(np.arange(FFT_HALF_N1)), (0, 2, 1))
    c, s = np.cos(a) / n, np.sin(a) / n
    h = np.empty((FFT_N2, 2, FFT_HALF_N1, FFT_N1, 2))
    h[:, 0, :, :, 0], h[:, 0, :, :, 1] = c, -s
    h[:, 1, :, :, 0], h[:, 1, :, :, 1] = s, c
    h = h.reshape(FFT_N2, 2 * FFT_HALF_N1, 2 * FFT_N1)

    a = 2.0 * np.pi * np.outer(n2, n2) / FFT_N2
    c, s = np.cos(a), np.sin(a)
    f2 = np.block([[c, s], [-s, c]])
    f2i = np.block([[c, -s], [s, c]])

    kk = np.arange(CTX_N)
    a = 2.0 * np.pi * np.outer(kk, np.arange(CTX_LEN)) / CTX_N
    c, s = np.cos(a), np.sin(a)
    fc = np.block([[c, s], [-s, c]])
    a = 2.0 * np.pi * np.outer(kk, np.arange(CTX_N)) / CTX_N
    fcf = np.concatenate([np.cos(a), -np.sin(a)], axis=0)
    a = 2.0 * np.pi * np.outer(np.arange(CTX_LEN), kk) / CTX_N
    c, s = np.cos(a) / CTX_N, np.sin(a) / CTX_N
    fci = np.block([[c, -s], [s, c]])

    tables = dict(g=g, gf=gf, h=h, f2=f2, f2i=f2i, fc=fc, fcf=fcf, fci=fci)
    return {k: np.asarray(v, dtype=np.float32) for k, v in tables.items()}


def _dft_tables():
    return {k: jnp.asarray(v).astype(BF16) for k, v in _dft_tables_f32().items()}


SEQ_TILE = 1024
N1_PER_TILE = SEQ_TILE // FFT_N2


def _scatter_n2(s_ref, n2, val):
    for p in range(s_ref.shape[0]):
        s_ref[p, pl.ds(n2, 2 * FFT_N1, stride=FFT_PITCH), :] = val[:, p * LANES:(p + 1) * LANES]


def _gather_n2(s_ref, n2):
    return jnp.concatenate(
        [s_ref[p, pl.ds(n2, 2 * FFT_N1, stride=FFT_PITCH), :] for p in range(s_ref.shape[0])],
        axis=1)


def _spectrum_rows(s_ref, k1):
    base = pl.multiple_of(k1 * (2 * FFT_PITCH), 8)
    return base, jnp.concatenate(
        [jnp.concatenate([s_ref[p, pl.ds(base + o, FFT_N2), :] for p in range(s_ref.shape[0])],
                         axis=1) for o in (0, FFT_PITCH)], axis=0)


HY_FFN = 64
FILT_ROWS = 1024
FILT_TILES = 2
HIGHEST = lax.Precision.HIGHEST


def _tap_rows(row, seq, n2_major):
    if n2_major:
        n2, n1 = row >> int(math.log2(FFT_N1)), row & (FFT_N1 - 1)
        bwd = n1 >= FFT_HALF_N1
        m = FFT_N2 * (n1 - FFT_HALF_N1) + n2
        fwd_pos = FFT_N2 * n1 + n2
    else:
        bwd = row >= seq
        m = row - seq
        fwd_pos = row
    pos = jnp.where(bwd, seq - m, fwd_pos).astype(F32)
    bwd_f = jnp.where(bwd, 1.0, 0.0)
    keep_f = jnp.where(bwd & (m == 0), 0.0, 1.0)
    return pos, bwd_f, keep_f


def _filter_kernel(ec_ref, f1w_ref, f1b_ref, f2w_ref, f2b_ref, sf_ref, w3f_ref, w3b_ref, b3f_ref,
                   b3b_ref, dl_ref, skip_ref, *rest, seq, n2_major):
    if n2_major:
        gf_ref, f2_ref, o_ref, hid_ref, k_ref, s_ref = rest
    else:
        fcf_ref, o_ref, hid_ref, k_ref = rest
    n_rows = 2 * seq
    rb = min(FILT_ROWS, n_rows)

    def rows_of(i):
        r0 = pl.multiple_of(i * rb, rb)
        return r0, _tap_rows(r0 + lax.broadcasted_iota(jnp.int32, (rb, 1), 0), seq, n2_major)

    @pl.when(pl.program_id(0) == 0)
    def _():
        sf = sf_ref[...]

        def hidden(i, carry):
            r0, (pos, _, _) = rows_of(i)
            t = pos * (1.0 / (seq - 1))
            w = pos * (2.0 * math.pi / seq)
            emb = jnp.sin(w * ec_ref[0:1] + ec_ref[1:2]) + t * ec_ref[2:3]
            h = jnp.sin(sf * (jnp.dot(emb, f1w_ref[...], precision=HIGHEST,
                                      preferred_element_type=F32) + f1b_ref[...]))
            h = jnp.sin(sf * (jnp.dot(h.astype(BF16), f2w_ref[...].astype(BF16),
                                      preferred_element_type=F32) + f2b_ref[...]))
            hid_ref[pl.ds(r0, rb), :] = h
            return carry

        lax.fori_loop(0, n_rows // rb, hidden, 0)

    def taps(i, ssq):
        r0, (pos, bwd_f, keep_f) = rows_of(i)
        h = hid_ref[pl.ds(r0, rb), :].astype(BF16)
        neg_t = -(pos * (1.0 / (seq - 1)))
        ks = []
        for c in range(FILT_TILES):
            kf = jnp.dot(h, w3f_ref[c].astype(BF16), preferred_element_type=F32) + b3f_ref[c]
            kb = jnp.dot(h, w3b_ref[c].astype(BF16), preferred_element_type=F32) + b3b_ref[c]
            ks.append((kf + bwd_f * (kb - kf)) * jnp.exp(neg_t * dl_ref[c]) * keep_f)
        k = jnp.concatenate(ks, axis=1)
        k_ref[pl.ds(r0, rb), :] = k
        return ssq + jnp.sum(k * k, axis=0, keepdims=True)

    ssq = lax.fori_loop(0, n_rows // rb, taps, jnp.zeros((1, FILT_TILES * LANES), F32))
    scale = lax.rsqrt(ssq + NORM_EPS)
    skip = skip_ref[...]

    if n2_major:
        def stage1(n2, carry):
            x = k_ref[pl.ds(pl.multiple_of(n2 * FFT_N1, FFT_N1), FFT_N1), :] * scale
            _scatter_n2(s_ref, n2, jnp.dot(gf_ref[n2], x.astype(BF16),
                                           preferred_element_type=F32))
            return carry

        lax.fori_loop(0, FFT_N2, stage1, 0, unroll=8)

        def stage2(k1, carry):
            _, r = _spectrum_rows(s_ref, k1)
            z = jnp.dot(f2_ref[...], r.astype(BF16), preferred_element_type=F32)
            row = pl.multiple_of(k1 * (2 * FFT_N2), 2 * FFT_N2)
            o_ref[pl.ds(row, FFT_N2), :] = (z[:FFT_N2] + skip).astype(BF16)
            o_ref[pl.ds(row + FFT_N2, FFT_N2), :] = z[FFT_N2:].astype(BF16)
            return carry

        lax.fori_loop(0, FFT_N1, stage2, 0, unroll=8)
    else:
        z = jnp.dot(fcf_ref[...], (k_ref[...] * scale).astype(BF16), preferred_element_type=F32)
        o_ref[:n_rows, :] = (z[:n_rows] + skip).astype(BF16)
        o_ref[n_rows:, :] = z[n_rows:].astype(BF16)


def _filter_spectrum(seq, f1_w, f1_b, f2_w, f2_b, f3_w, f3_b, sin_freq, skip):
    n2_major = seq == SEQ
    t = _dft_tables()
    pad = lambda a, r, c: jnp.pad(a, ((0, r - a.shape[0]), (0, c - a.shape[1])))
    f = jnp.linspace(1e-4, HY_BANDS - 1, HY_BANDS, dtype=F32)
    zeros = lambda n: jnp.zeros((n,), F32)
    ones = lambda n: jnp.ones((n,), F32)
    rest = LANES - 1 - 2 * HY_BANDS
    ec = jnp.stack([
        jnp.concatenate([zeros(1), f, f, zeros(rest)]),
        jnp.concatenate([zeros(1), (math.pi / 2) * ones(HY_BANDS), math.pi * ones(HY_BANDS),
                         zeros(rest)]),
        jnp.concatenate([ones(1), zeros(LANES - 1)]),
    ] + [zeros(LANES)] * 5)
    row = lambda v: pad(v.reshape(1, -1), 1, LANES)
    n_tiles = HY_ORDER * W_A // LANES
    w3 = f3_w.reshape(HY_FFN, HY_ORDER, 2, W_A)
    b3 = f3_b.reshape(HY_ORDER, 2, W_A)
    tiles_w = lambda d: jnp.pad(
        w3[:, :, d, :].reshape(HY_FFN, n_tiles, LANES).transpose(1, 0, 2),
        ((0, 0), (0, LANES - HY_FFN), (0, 0)))
    tiles_b = lambda d: b3[:, d, :].reshape(n_tiles, 1, LANES)
    deltas = jnp.abs(jnp.linspace(math.log(HY_TARGET) / HY_SLOW_PCT,
                                  math.log(HY_TARGET) / HY_FAST_PCT, W_A, dtype=F32))
    dl = jnp.tile(deltas, HY_ORDER).reshape(n_tiles, 1, LANES)
    const = lambda shape: _single_buffered(shape, lambda j: (0,) * len(shape))
    tile = lambda shape: pl.BlockSpec((FILT_TILES,) + shape, lambda j: (j, 0, 0))
    ct = FILT_TILES * LANES
    args = [ec, pad(f1_w, LANES, LANES), row(f1_b), pad(f2_w, LANES, LANES), row(f2_b),
            row(sin_freq), tiles_w(0), tiles_w(1), tiles_b(0), tiles_b(1), dl, skip]
    in_specs = [const((8, LANES)), const((LANES, LANES)), const((1, LANES)),
                const((LANES, LANES)), const((1, LANES)), const((1, LANES)),
                tile((LANES, LANES)), tile((LANES, LANES)), tile((1, LANES)), tile((1, LANES)),
                tile((1, LANES)), pl.BlockSpec((1, ct), lambda j: (0, j))]
    scratch = [pltpu.VMEM((2 * seq, LANES), F32), pltpu.VMEM((2 * seq, ct), F32)]
    if n2_major:
        args += [t["gf"], t["f2"]]
        in_specs += [const(t["gf"].shape), const(t["f2"].shape)]
        scratch += [pltpu.VMEM((FILT_TILES, 2 * FFT_N1 * FFT_PITCH, LANES), F32)]
    else:
        args += [t["fcf"]]
        in_specs += [const(t["fcf"].shape)]
    return pl.pallas_call(
        functools.partial(_filter_kernel, seq=seq, n2_major=n2_major),
        grid=(n_tiles // FILT_TILES,),
        in_specs=in_specs,
        out_specs=pl.BlockSpec((4 * seq, ct), lambda j: (0, j)),
        out_shape=jax.ShapeDtypeStruct((4 * seq, HY_ORDER * W_A), BF16),
        scratch_shapes=scratch,
        compiler_params=_cparams("arbitrary"),
        name="hyena_filter" if n2_major else "hyena_ctx_filter",
    )(*args)


LC_OUT_CHUNKS = 4
LC_N2_PER_CHUNK = FFT_N2 // LC_OUT_CHUNKS


def _longconv_kernel(v_ref, gate_ref, kf_ref, g_ref, h_ref, f2_ref, f2i_ref, o_ref, s_ref):
    ct = v_ref.shape[2]
    t = pl.program_id(1)

    @pl.when(t == 0)
    def _():
        def stage1(n2, carry):
            r0 = pl.multiple_of(n2 * N1_PER_TILE, N1_PER_TILE)
            blk = [jnp.concatenate([v_ref[b, pl.ds(i * SEQ_TILE + r0, N1_PER_TILE), :]
                                    for i in range(SEQ // SEQ_TILE)], axis=0)
                   for b in range(BATCH)]
            x = jnp.concatenate([jnp.concatenate([blk[0], blk[2]], axis=1),
                                 jnp.concatenate([blk[1], blk[3]], axis=1)], axis=0)
            _scatter_n2(s_ref, n2, jnp.dot(g_ref[n2], x.astype(BF16), preferred_element_type=F32))
            return carry

        lax.fori_loop(0, FFT_N2, stage1, 0, unroll=8)

        def stage2(k1, carry):
            base, r = _spectrum_rows(s_ref, k1)
            z = jnp.dot(f2_ref[...], r.astype(BF16), preferred_element_type=F32)
            kf = kf_ref[pl.ds(pl.multiple_of(k1 * (2 * FFT_N2), 2 * FFT_N2), 2 * FFT_N2), :]
            kf = kf.astype(F32)
            kr = jnp.concatenate([kf[:FFT_N2]] * 2, axis=1)
            ki = jnp.concatenate([kf[FFT_N2:]] * 2, axis=1)
            zr, zi = z[:FFT_N2], z[FFT_N2:]
            p = jnp.concatenate([zr * kr - zi * ki, zr * ki + zi * kr], axis=0)
            q = jnp.dot(f2i_ref[...], p.astype(BF16), preferred_element_type=F32)
            for p in range(2):
                s_ref[p, pl.ds(base, FFT_N2), :] = q[:FFT_N2, p * ct:(p + 1) * ct]
                s_ref[p, pl.ds(base + FFT_PITCH, FFT_N2), :] = q[FFT_N2:, p * ct:(p + 1) * ct]
            return carry

        lax.fori_loop(0, FFT_N1, stage2, 0, unroll=8)

    def stage3(j, carry):
        n2 = t * LC_N2_PER_CHUNK + j
        rq = _gather_n2(s_ref, n2)
        y = jnp.dot(h_ref[n2], rq.astype(BF16), preferred_element_type=F32)
        r0 = pl.multiple_of(j * N1_PER_TILE, N1_PER_TILE)
        for b in range(BATCH):
            ri, pair = b % 2, b // 2
            for i in range(SEQ // SEQ_TILE):
                n1 = ri * FFT_HALF_N1 + i * N1_PER_TILE
                yb = y[n1:n1 + N1_PER_TILE, pair * ct:(pair + 1) * ct]
                o_ref[b, i, pl.ds(r0, N1_PER_TILE), :] = (
                    gate_ref[b, i, pl.ds(r0, N1_PER_TILE), :] * yb)
        return carry

    lax.fori_loop(0, LC_N2_PER_CHUNK, stage3, 0, unroll=8)


def _longconv(v, v_tile, gate, gate_tile, kf, kf_col):
    t = _dft_tables()
    ct = LANES
    n_seq_tiles = SEQ // SEQ_TILE
    rows = SEQ_TILE // LC_OUT_CHUNKS
    gate = gate.reshape(gate.shape[0], BATCH, n_seq_tiles, SEQ_TILE, ct)
    out = pl.pallas_call(
        _longconv_kernel,
        grid=(W_A // ct, LC_OUT_CHUNKS),
        in_specs=[
            _single_buffered((None, BATCH, SEQ, ct), lambda j, i: (v_tile + j, 0, 0, 0)),
            pl.BlockSpec((None, BATCH, n_seq_tiles, rows, ct),
                         lambda j, i: (gate_tile + j, 0, 0, i, 0)),
            pl.BlockSpec((2 * FFT_N, ct), lambda j, i: (0, kf_col + j)),
            _single_buffered(t["g"].shape, lambda j, i: (0, 0, 0)),
            _single_buffered(t["h"].shape, lambda j, i: (0, 0, 0)),
            _single_buffered(t["f2"].shape, lambda j, i: (0, 0)),
            _single_buffered(t["f2i"].shape, lambda j, i: (0, 0)),
        ],
        out_specs=pl.BlockSpec((None, BATCH, n_seq_tiles, rows, ct),
                               lambda j, i: (j, 0, 0, i, 0)),
        out_shape=jax.ShapeDtypeStruct((W_A // ct, BATCH, n_seq_tiles, SEQ_TILE, ct), F32),
        scratch_shapes=[pltpu.VMEM((2, 2 * FFT_N1 * FFT_PITCH, ct), F32)],
        compiler_params=_cparams("arbitrary", "arbitrary"),
        name="hyena_longconv",
    )(v, gate, kf, t["g"], t["h"], t["f2"], t["f2i"])
    return out.reshape(W_A // ct, BATCH, SEQ, ct)


def _ctx_conv_kernel(v_ref, gate_ref, kf_ref, fc_ref, fci_ref, o_ref):
    ct = v_ref.shape[2]
    x = jnp.concatenate([jnp.concatenate([v_ref[0], v_ref[2]], axis=1),
                         jnp.concatenate([v_ref[1], v_ref[3]], axis=1)], axis=0)
    z = jnp.dot(fc_ref[...], x.astype(BF16), preferred_element_type=F32)
    kf = kf_ref[...].astype(F32)
    kr = jnp.concatenate([kf[:CTX_N]] * 2, axis=1)
    ki = jnp.concatenate([kf[CTX_N:]] * 2, axis=1)
    zr, zi = z[:CTX_N], z[CTX_N:]
    p = jnp.concatenate([zr * kr - zi * ki, zr * ki + zi * kr], axis=0)
    y = jnp.dot(fci_ref[...], p.astype(BF16), preferred_element_type=F32)
    for b in range(BATCH):
        ri, pair = b % 2, b // 2
        o_ref[b] = gate_ref[b] * y[ri * CTX_LEN:(ri + 1) * CTX_LEN, pair * ct:(pair + 1) * ct]


def _ctx_conv(v, v_tile, gate, gate_tile, kf, kf_col):
    t = _dft_tables()
    ct = LANES
    return pl.pallas_call(
        _ctx_conv_kernel,
        grid=(W_A // ct,),
        in_specs=[
            pl.BlockSpec((None, BATCH, CTX_LEN, ct), lambda j: (v_tile + j, 0, 0, 0)),
            pl.BlockSpec((None, BATCH, CTX_LEN, ct), lambda j: (gate_tile + j, 0, 0, 0)),
            pl.BlockSpec((2 * CTX_N, ct), lambda j: (0, kf_col + j)),
            pl.BlockSpec(t["fc"].shape, lambda j: (0, 0)),
            pl.BlockSpec(t["fci"].shape, lambda j: (0, 0)),
        ],
        out_specs=pl.BlockSpec((None, BATCH, CTX_LEN, ct), lambda j: (j, 0, 0, 0)),
        out_shape=jax.ShapeDtypeStruct((W_A // ct, BATCH, CTX_LEN, ct), F32),
        compiler_params=_cparams("arbitrary"),
        name="hyena_ctx_conv",
    )(v, gate, kf, t["fc"], t["fci"])


MOD_ROWS = 8
CTX_MOD_ROW = BATCH


def _ada_kernel(c_ref, w_ref, b_ref, o_ref):
    cv = c_ref[...]
    s = cv * jax.nn.sigmoid(cv)
    o_ref[0] = jnp.dot(s.astype(BF16), w_ref[0].astype(BF16),
                       preferred_element_type=F32) + b_ref[0]


def _ada_mods(c, c_ctx, w_ada, b_ada):
    cv = jnp.concatenate(
        [c, c_ctx[None], jnp.zeros((MOD_ROWS - BATCH - 1, D_MODEL), F32)], axis=0)
    n = N_MOD * D_MODEL
    tn = n // 4
    return pl.pallas_call(
        _ada_kernel,
        grid=(DEPTH, n // tn),
        in_specs=[
            pl.BlockSpec((MOD_ROWS, D_MODEL), lambda l, j: (0, 0)),
            pl.BlockSpec((1, D_MODEL, tn), lambda l, j: (l, 0, j)),
            pl.BlockSpec((1, 1, tn), lambda l, j: (l, 0, j)),
        ],
        out_specs=pl.BlockSpec((1, MOD_ROWS, tn), lambda l, j: (l, 0, j)),
        out_shape=jax.ShapeDtypeStruct((DEPTH, MOD_ROWS, n), F32),
        compiler_params=_cparams("arbitrary", "arbitrary"),
        name="ada_mods",
    )(cv, w_ada, b_ada.reshape(DEPTH, 1, n))


def _mod_spec(m, row_fn):
    return pl.BlockSpec((1, 1, D_MODEL), lambda i, *_: (row_fn(i), 0, m))


def _rms_mod(x, g, shift, scale):
    ms = jnp.mean(x * x, axis=-1, keepdims=True)
    return (x * lax.rsqrt(ms + NORM_EPS)) * (g * (1.0 + scale)) + shift


GELU_C = math.sqrt(2.0 / math.pi)


def _gelu(x):
    hx = 0.5 * x
    return hx + hx * jnp.tanh(x * (GELU_C + (GELU_C * 0.044715) * (x * x)))


HALO = 8


def _dwconv(z_all, tm, w, b, left, valid):
    n = z_all.shape[0]
    y = b + w[left:left + 1] * z_all[:tm]
    for k in range(w.shape[0]):
        d = k - left
        if d == 0:
            continue
        s = pltpu.roll(z_all, (-d) % n, 0)[:tm]
        if valid is not None:
            s = s * valid(d)
        y = y + s * w[k:k + 1]
    return y


def _halo_specs(tm, n_rows):
    per = tm // HALO
    last = n_rows // HALO - 1
    return [
        pl.BlockSpec((tm, D_MODEL), lambda i: (i, 0)),
        pl.BlockSpec((HALO, D_MODEL), lambda i: (jnp.maximum(i * per - 1, 0), 0)),
        pl.BlockSpec((HALO, D_MODEL), lambda i: (jnp.minimum((i + 1) * per, last), 0)),
    ]


def _normed_tile(x_ref, xp_ref, xn_ref, sh_ref, sc_ref, g_ref, seq_len):
    tm = x_ref.shape[0]
    g, shift, scale = g_ref[...], sh_ref[0], sc_ref[0]
    h = _rms_mod(x_ref[...], g, shift, scale)
    hn = _rms_mod(xn_ref[...], g, shift, scale)
    hp = _rms_mod(xp_ref[...], g, shift, scale)
    if seq_len >= tm:
        r0 = pl.program_id(0) * tm
        hp = hp * jnp.where((r0 & (seq_len - 1)) == 0, 0.0, 1.0)
        hn = hn * jnp.where(((r0 + tm) & (seq_len - 1)) == 0, 0.0, 1.0)
        valid = None
    else:
        hp, hn = jnp.zeros_like(hp), jnp.zeros_like(hn)
        pos = lax.broadcasted_iota(jnp.int32, (tm, 1), 0) & (seq_len - 1)
        valid = lambda d: jnp.where((pos + d >= 0) & (pos + d < seq_len), 1.0, 0.0)
    return jnp.concatenate([h, hn, hp], axis=0).astype(BF16), valid


EVEN_TM = SEQ_TILE
HY_COLS = 3 * W_A
HY_TILES = HY_COLS // LANES


def _in_even_kernel(x_ref, xp_ref, xn_ref, sh_ref, sc_ref, g_ref, w_ref, cw_ref, cb_ref,
                    lng_ref, sgw_ref, sgb_ref, za_ref, yb_ref, *, seq_len, n2_major):
    tm = x_ref.shape[0]
    h_all, valid = _normed_tile(x_ref, xp_ref, xn_ref, sh_ref, sc_ref, g_ref, seq_len)
    h = h_all[:tm]
    cw = 4 * LANES
    for cc in range(HY_COLS // cw):
        cols = slice(cc * cw, (cc + 1) * cw)
        z_all = jnp.dot(h_all, w_ref[:, cols], preferred_element_type=F32)
        y = _dwconv(z_all, tm, cw_ref[:, cols], cb_ref[:, cols], 1, valid)
        for c in range(cw // LANES):
            tile = cc * (cw // LANES) + c
            yc = y[:, c * LANES:(c + 1) * LANES]
            if n2_major:
                for j in range(N1_PER_TILE):
                    za_ref[tile, pl.ds(j, FFT_N2, stride=N1_PER_TILE), :] = (
                        yc[j * FFT_N2:(j + 1) * FFT_N2])
            else:
                za_ref[tile] = yc
    u = _gelu(jnp.dot(h, w_ref[:, HY_COLS:HY_COLS + W_B], preferred_element_type=F32))
    vb = _gelu(jnp.dot(h, w_ref[:, HY_COLS + W_B:], preferred_element_type=F32))
    vc = vb - jnp.mean(vb, axis=-1, keepdims=True)
    vn = vc * lax.rsqrt(jnp.mean(vc * vc, axis=-1, keepdims=True) + NORM_EPS) * lng_ref[...]
    vn = vn.astype(BF16)
    for ch in range(tm // CHUNK):
        rows = slice(ch * CHUNK, (ch + 1) * CHUNK)
        for q in range(SGU_GROUPS):
            cols = slice(q * SGU_DH, (q + 1) * SGU_DH)
            s = jnp.dot(sgw_ref[q], vn[rows, cols], preferred_element_type=F32) + sgb_ref[:, cols]
            yb_ref[rows, cols] = (u[rows, cols] * s).astype(BF16)


def _in_even(x, mods, mod_row, g, w, layer, cw, cb, lng, sgw, sgb, seq_len, n2_major):
    n_rows = x.shape[0]
    tm = EVEN_TM
    za_shape = (HY_TILES, n_rows, LANES)
    za_spec = pl.BlockSpec((HY_TILES, tm, LANES), lambda i: (0, i, 0))
    const = lambda shape: _single_buffered(shape, lambda i: (0,) * len(shape))
    return pl.pallas_call(
        functools.partial(_in_even_kernel, seq_len=seq_len, n2_major=n2_major),
        grid=(n_rows // tm,),
        in_specs=_halo_specs(tm, n_rows) + [
            _mod_spec(0, mod_row), _mod_spec(1, mod_row),
            const((1, D_MODEL)), _layer_spec(w, layer), const(cw.shape), const(cb.shape),
            const(lng.shape), const(sgw.shape), const(sgb.shape),
        ],
        out_specs=[za_spec, pl.BlockSpec((tm, W_B), lambda i: (i, 0))],
        out_shape=[jax.ShapeDtypeStruct(za_shape, F32),
                   jax.ShapeDtypeStruct((n_rows, W_B), BF16)],
        compiler_params=_cparams("arbitrary"),
        name="in_proj_even",
    )(x, x, x, mods, mods, g, w, cw, cb, lng, sgw, sgb)


FFN_TM = SEQ_TILE
FFN_SUB = 256
GRID_H = SEQ // GRID_W
GT_ROWS = FFN_TM // GRID_W
GT_PITCH = GRID_W + 8


def _even_mix(ya_ref, yb_ref, w_ref, n2_major):
    parts = []
    for c in range(W_A // LANES):
        if n2_major:
            yac = jnp.concatenate(
                [ya_ref[c, pl.ds(j, FFT_N2, stride=N1_PER_TILE), :]
                 for j in range(N1_PER_TILE)], axis=0)
        else:
            yac = ya_ref[c]
        parts.append(yac.astype(BF16))
    y = jnp.concatenate(parts + [yb_ref[...]], axis=1)
    return jnp.dot(y, w_ref[...], preferred_element_type=F32)


def _ffn_kernel(*refs, final_norm, mix, n2_major, swap_grid):
    refs = list(refs)
    x_ref, sh_ref, sc_ref, gt_ref, g_ref, w_in_ref, wo_ref, fg_ref = refs[:8]
    del refs[:8]
    if mix:
        ya_ref, yb_ref, gm_ref, wm_ref = refs[:4]
        del refs[:4]
    o_ref = refs.pop(0)
    x = x_ref[...]
    if mix:
        x = x + gm_ref[0] * _even_mix(ya_ref, yb_ref, wm_ref, n2_major)
    h = _rms_mod(x, g_ref[...], sh_ref[0], sc_ref[0]).astype(BF16)
    acts = []
    for c0 in range(0, D_FF, FFN_SUB):
        hz = 0.5 * jnp.dot(h, w_in_ref[:, c0:c0 + FFN_SUB], preferred_element_type=F32)
        z2 = jnp.dot(h, w_in_ref[:, D_FF + c0:D_FF + c0 + FFN_SUB], preferred_element_type=F32)
        acts.append(((hz + hz * jnp.tanh(hz)) * z2).astype(BF16))
    y = x + gt_ref[0] * jnp.dot(jnp.concatenate(acts, axis=1), wo_ref[...],
                                preferred_element_type=F32)
    if final_norm:
        y = y * lax.rsqrt(jnp.mean(y * y, axis=-1, keepdims=True) + NORM_EPS) * fg_ref[...]
    if not swap_grid:
        o_ref[...] = y
        return
    s_ref = refs.pop(0)
    nt = D_MODEL // LANES
    for t in range(nt):
        for j in range(GT_ROWS):
            s_ref[t, j * GT_PITCH:j * GT_PITCH + GRID_W, :] = (
                y[j * GRID_W:(j + 1) * GRID_W, t * LANES:(t + 1) * LANES])
    for c in range(GRID_W):
        o_ref[c] = jnp.concatenate(
            [s_ref[t, pl.ds(c, GT_ROWS, stride=GT_PITCH), :] for t in range(nt)], axis=1)


def _ffn(x, mods, mod_row, g, w_in, w_out, layer, final_g, final_norm, mix=None,
         swap_grid=False):
    n_rows = x.shape[0]
    tm = FFN_TM
    const = lambda shape: _single_buffered(shape, lambda i: (0,) * len(shape))
    in_specs = [
        pl.BlockSpec((tm, D_MODEL), lambda i: (i, 0)),
        _mod_spec(3, mod_row), _mod_spec(4, mod_row), _mod_spec(5, mod_row),
        const((1, D_MODEL)), _layer_spec(w_in, layer), _layer_spec(w_out, layer),
        const((1, D_MODEL)),
    ]
    args = [x, mods, mods, mods, g, w_in, w_out, final_g]
    n2_major = False
    if mix is not None:
        ya, yb, (w_mix, mix_layer), n2_major = mix
        nt = W_A // LANES
        in_specs += [pl.BlockSpec((nt, tm, LANES), lambda i: (0, i, 0)),
                     pl.BlockSpec((tm, W_B), lambda i: (i, 0)),
                     _mod_spec(2, mod_row), _layer_spec(w_mix, mix_layer)]
        args += [ya.reshape(nt, n_rows, LANES), yb, mods, w_mix]
    scratch = []
    if swap_grid:
        per_seq = SEQ // tm
        out_spec = pl.BlockSpec((None, GRID_W, GT_ROWS, D_MODEL),
                                lambda i: (i // per_seq, 0, i % per_seq, 0))
        out_shape = jax.ShapeDtypeStruct((n_rows // SEQ, GRID_W, GRID_H, D_MODEL), F32)
        scratch = [pltpu.VMEM((D_MODEL // LANES, GT_ROWS * GT_PITCH, LANES), F32)]
    else:
        out_spec = pl.BlockSpec((tm, D_MODEL), lambda i: (i, 0))
        out_shape = jax.ShapeDtypeStruct((n_rows, D_MODEL), F32)
    out = pl.pallas_call(
        functools.partial(_ffn_kernel, final_norm=final_norm, mix=mix is not None,
                          n2_major=n2_major, swap_grid=swap_grid),
        grid=(n_rows // tm,),
        in_specs=in_specs,
        out_specs=out_spec,
        out_shape=out_shape,
        scratch_shapes=scratch,
        compiler_params=_cparams("arbitrary"),
        name="ffn",
    )(*args)
    return out.reshape(n_rows, D_MODEL)


ODD_TM = 1024
RNN_TILES = D_RNN // LANES


def _in_odd_kernel(x_ref, xp_ref, xn_ref, sh_ref, sc_ref, g_ref, w_ref, cw_ref, cb_ref,
                   gate_ref, xl_ref, *, seq_len, chunk):
    tm = x_ref.shape[0]
    h_all, valid = _normed_tile(x_ref, xp_ref, xn_ref, sh_ref, sc_ref, g_ref, seq_len)
    gate_ref[...] = _gelu(jnp.dot(h_all[:tm], w_ref[:, :D_RNN],
                                  preferred_element_type=F32)).astype(BF16)
    cw = 4 * LANES
    for c0 in range(0, D_RNN, cw):
        cols = slice(c0, min(c0 + cw, D_RNN))
        wcols = slice(D_RNN + cols.start, D_RNN + cols.stop)
        z_all = jnp.dot(h_all, w_ref[:, wcols], preferred_element_type=F32)
        y = _dwconv(z_all, tm, cw_ref[:, cols], cb_ref[:, cols], 2, valid)
        kg = chunk // RG_GROUPS
        for c in range((cols.stop - cols.start) // LANES):
            yc = y[:, c * LANES:(c + 1) * LANES]
            for r0 in range(0, tm, chunk):
                for gq in range(RG_GROUPS):
                    xl_ref[c0 // LANES + c, pl.ds(r0 + gq, kg, stride=RG_GROUPS), :] = (
                        yc[r0 + gq * kg:r0 + (gq + 1) * kg])


def _in_odd(x, mods, mod_row, g, w, layer, cw, cb, seq_len, chunk):
    n_rows = x.shape[0]
    tm = ODD_TM
    const = lambda shape: _single_buffered(shape, lambda i: (0,) * len(shape))
    return pl.pallas_call(
        functools.partial(_in_odd_kernel, seq_len=seq_len, chunk=chunk),
        grid=(n_rows // tm,),
        in_specs=_halo_specs(tm, n_rows) + [
            _mod_spec(0, mod_row), _mod_spec(1, mod_row),
            const((1, D_MODEL)), _layer_spec(w, layer), const(cw.shape), const(cb.shape),
        ],
        out_specs=[pl.BlockSpec((tm, D_RNN), lambda i: (i, 0)),
                   pl.BlockSpec((RNN_TILES, tm, LANES), lambda i: (0, i, 0))],
        out_shape=[jax.ShapeDtypeStruct((n_rows, D_RNN), BF16),
                   jax.ShapeDtypeStruct((RNN_TILES, n_rows, LANES), F32)],
        compiler_params=_cparams("arbitrary"),
        name="in_proj_odd",
    )(x, x, x, mods, mods, g, w, cw, cb)


RG_T = 512
RG_NCH = SEQ // RG_T
RG_GROUPS = 8
RG_PLANES_PER_PASS = 6
SQRT_GUARD = 1e-30
RG_WIN = 3 * LANES


def _rg_window_start(j):
    return min(max(LANES * (j - 1), 0), D_RNN - RG_WIN)


def _rg_chunk(x_ref, w_ref, ba_ref, bx_ref, lam_ref, a_pl, b_pl, carry_ref, reverse, emit):
    t_rows = x_ref.shape[1]
    kg = t_rows // RG_GROUPS
    xb = [x_ref[j].astype(BF16) for j in range(RNN_TILES)]
    for j in range(RNN_TILES):
        tile = slice(j * LANES, (j + 1) * LANES)
        wt = _rg_window_start(j) // LANES
        pre = jnp.dot(jnp.concatenate(xb[wt:wt + RG_WIN // LANES], axis=1), w_ref[j],
                      preferred_element_type=F32)
        lam = lam_ref[:, tile]
        softplus_neg = jnp.maximum(-lam, 0.0) + jnp.log1p(jnp.exp(-jnp.abs(lam)))
        th_r = jnp.tanh(pre[:, :LANES] + ba_ref[:, tile])
        th_i = jnp.tanh(pre[:, LANES:] + bx_ref[:, tile])
        c3 = (-0.5 * RG_C * math.log2(math.e)) * softplus_neg
        av = jnp.exp2(c3 + c3 * th_r)
        hx = 0.5 * x_ref[j]
        y = 1.0 - av * av
        a_pl[j, 0:t_rows, :] = av
        b_pl[j, 0:t_rows, :] = (y * lax.rsqrt(jnp.maximum(y, SQRT_GUARD))) * (hx + hx * th_i)

    def rows_k(pl_ref, j, k):
        return pl_ref[j, k * RG_GROUPS:(k + 1) * RG_GROUPS, :]

    order = list(range(kg))[::-1] if reverse else list(range(kg))
    groups = list(range(RG_GROUPS))[::-1] if reverse else list(range(RG_GROUPS))
    for j0 in range(0, RNN_TILES, RG_PLANES_PER_PASS):
        planes = range(j0, min(j0 + RG_PLANES_PER_PASS, RNN_TILES))
        big_a, big_b = {}, {}
        for n, k in enumerate(order):
            for j in planes:
                ak, bk = rows_k(a_pl, j, k), rows_k(b_pl, j, k)
                if n == 0:
                    big_a[j], big_b[j] = ak, bk
                else:
                    big_b[j] = ak * big_b[j] + bk
                    big_a[j] = ak * big_a[j]
        h = {}
        for j in planes:
            c = carry_ref[j, 0:1, :]
            rows = [None] * RG_GROUPS
            for gq in groups:
                rows[gq] = c
                c = big_a[j][gq:gq + 1] * c + big_b[j][gq:gq + 1]
            carry_ref[j, 0:1, :] = c
            h[j] = jnp.concatenate(rows, axis=0)
        for k in order:
            for j in planes:
                h[j] = rows_k(a_pl, j, k) * h[j] + rows_k(b_pl, j, k)
                b_pl[j, k * RG_GROUPS:(k + 1) * RG_GROUPS, :] = h[j]
        for j in planes:
            emit(j, jnp.concatenate(
                [b_pl[j, pl.ds(gq, kg, stride=RG_GROUPS), :] for gq in range(RG_GROUPS)], axis=0))


def _rg_bwd_kernel(xc_ref, xl_ref, w_ref, ba_ref, bx_ref, lam_ref, oc_ref, ol_ref,
                   a_pl, b_pl, carry_ref):
    def emitter(o_ref):
        def emit(j, hcur):
            o_ref[:, j * LANES:(j + 1) * LANES] = hcur.astype(BF16)
        return emit

    @pl.when(pl.program_id(1) == 0)
    def _():
        carry_ref[...] = jnp.zeros_like(carry_ref)
        _rg_chunk(xc_ref, w_ref, ba_ref, bx_ref, lam_ref, a_pl, b_pl, carry_ref, True,
                  emitter(oc_ref))

    @pl.when(pl.program_id(1) > 0)
    def _():
        _rg_chunk(xl_ref, w_ref, ba_ref, bx_ref, lam_ref, a_pl, b_pl, carry_ref, True,
                  emitter(ol_ref))


def _rg_fwd_kernel(xc_ref, xl_ref, w_ref, ba_ref, bx_ref, lam_ref, gc_ref, gl_ref, hc_ref, hl_ref,
                   rc_ref, rl_ref, gtc_ref, gtl_ref, wo_ref, oc_ref, ol_ref,
                   a_pl, b_pl, carry_ref, y_ref):
    def run(x_ref, g_ref, hb_ref, res_ref, gt_ref, o_ref):
        t_rows = x_ref.shape[1]

        def emit(j, hcur):
            tile = slice(j * LANES, (j + 1) * LANES)
            y_ref[0:t_rows, tile] = (g_ref[:, tile].astype(F32)
                                     * (hcur + hb_ref[:, tile].astype(F32))).astype(BF16)

        _rg_chunk(x_ref, w_ref, ba_ref, bx_ref, lam_ref, a_pl, b_pl, carry_ref, False, emit)
        acc = jnp.dot(y_ref[0:t_rows, :], wo_ref[...], preferred_element_type=F32)
        o_ref[...] = res_ref[...] + gt_ref[0] * acc

    @pl.when(pl.program_id(1) == 0)
    def _():
        carry_ref[...] = jnp.zeros_like(carry_ref)
        run(xc_ref, gc_ref, hc_ref, rc_ref, gtc_ref, oc_ref)

    @pl.when(pl.program_id(1) > 0)
    def _():
        run(xl_ref, gl_ref, hl_ref, rl_ref, gtl_ref, ol_ref)


def _rg_scan(xc, xl, w, ba, bx, lam, reverse, fwd_args=None):
    if reverse:
        chunk = lambda s: RG_NCH - jnp.maximum(s, 1)
    else:
        chunk = lambda s: jnp.maximum(s - 1, 0)
    ctx_spec = lambda d: pl.BlockSpec((None, CTX_LEN, d), lambda b, s: (b, 0, 0))
    lat_spec = lambda d: pl.BlockSpec((None, RG_T, d), lambda b, s: (b, chunk(s), 0))
    const = lambda shape: _single_buffered(shape, lambda b, s: (0,) * len(shape))
    in_specs = [pl.BlockSpec((RNN_TILES, CTX_LEN, LANES), lambda b, s: (0, b, 0)),
                pl.BlockSpec((RNN_TILES, RG_T, LANES), lambda b, s: (0, b * RG_NCH + chunk(s), 0)),
                const(w.shape), const(ba.shape), const(bx.shape), const(lam.shape)]
    args = [xc, xl, w, ba, bx, lam]
    scratch = [pltpu.VMEM((RNN_TILES, RG_T, LANES), F32),
               pltpu.VMEM((RNN_TILES, RG_T, LANES), F32),
               pltpu.VMEM((RNN_TILES, 8, LANES), F32)]
    if reverse:
        body, name, width, dtype = _rg_bwd_kernel, "rglru_bwd", D_RNN, BF16
    else:
        gate_c, gate_l, hb_c, hb_l, res_c, res_l, mods, (w_out, out_layer) = fwd_args
        in_specs += [ctx_spec(D_RNN), lat_spec(D_RNN), ctx_spec(D_RNN), lat_spec(D_RNN),
                     ctx_spec(D_MODEL), lat_spec(D_MODEL),
                     _mod_spec(2, lambda b: CTX_MOD_ROW), _mod_spec(2, lambda b: b),
                     _layer_spec(w_out, out_layer)]
        args += [gate_c, gate_l, hb_c, hb_l, res_c, res_l, mods, mods, w_out]
        scratch += [pltpu.VMEM((RG_T, D_RNN), BF16)]
        body, name, width, dtype = _rg_fwd_kernel, "rglru_fwd", D_MODEL, F32
    return pl.pallas_call(
        body,
        grid=(BATCH, 1 + RG_NCH),
        in_specs=in_specs,
        out_specs=[ctx_spec(width), lat_spec(width)],
        out_shape=[jax.ShapeDtypeStruct((BATCH, CTX_LEN, width), dtype),
                   jax.ShapeDtypeStruct((BATCH, SEQ, width), dtype)],
        scratch_shapes=scratch,
        compiler_params=_cparams("arbitrary", "arbitrary"),
        name=name,
    )(*args)


def _rg_gate_weights(wa, wx):
    def window(w, j):
        ws = _rg_window_start(j)
        win = None
        for h in range(RG_HEADS):
            c0, c1 = max(h * RG_DH, j * LANES), min((h + 1) * RG_DH, (j + 1) * LANES)
            if c0 >= c1:
                continue
            r0 = h * RG_DH - ws
            assert 0 <= r0 and r0 + RG_DH <= RG_WIN
            blk = jnp.pad(w[h, :, c0 - h * RG_DH:c1 - h * RG_DH],
                          ((r0, RG_WIN - RG_DH - r0), (c0 - j * LANES, (j + 1) * LANES - c1)))
            win = blk if win is None else win + blk
        return win

    wins = [jnp.concatenate([window(wa, j), window(wx, j)], axis=1) for j in range(RNN_TILES)]
    return (0.5 * jnp.stack(wins)).astype(BF16)


def _col_major(layer):
    return layer < DEPTH and layer % 2 == 1 and (layer // 2) % 2 == 1


def kernel(x, c, ctx, c_ctx, w_ada, b_ada, norm_mix_g, norm_ffn_g, w_in_even, w_out_even, hy_conv_w, hy_conv_b, hy_f1_w, hy_f1_b, hy_f2_w, hy_f2_b, hy_f3_w, hy_f3_b, hy_sin_freq, hy_skip, sgu_ln_g, sgu_w, sgu_b, w_in_odd, rg_conv_w, rg_conv_b, rg_wa, rg_ba, rg_wx, rg_bx, rg_lam, w_out_odd, w_ffn_in, w_ffn_out, final_norm_g):
    mods_all = _ada_mods(c, c_ctx, w_ada, b_ada)
    xs = x.reshape(BATCH * SEQ, D_MODEL)
    cs = ctx.reshape(BATCH * CTX_LEN, D_MODEL)
    lat_row_even = lambda i: i // (SEQ // EVEN_TM)
    lat_row_odd = lambda i: i // (SEQ // ODD_TM)
    lat_row_ffn = lambda i: i // (SEQ // FFN_TM)
    ctx_row = lambda i: CTX_MOD_ROW
    n_tiles = W_A // LANES
    w_in_even, w_out_even, w_in_odd, w_out_odd, w_ffn_in, w_ffn_out = (
        w.astype(BF16) for w in (w_in_even, w_out_even, w_in_odd, w_out_odd, w_ffn_in, w_ffn_out))
    for l in range(DEPTH):
        run_ctx = l < DEPTH - 1
        is_rec = l % 2 == 1
        i = l // 2
        mods = mods_all[l].reshape(MOD_ROWS, 1, N_MOD * D_MODEL)
        g_mix = norm_mix_g[l].reshape(1, D_MODEL)
        mix_l = mix_c = None
        if is_rec:
            cw, cb = rg_conv_w[i], rg_conv_b[i].reshape(1, D_RNN)
            gate_l, xl = _in_odd(xs, mods, lat_row_odd, g_mix, w_in_odd, i, cw, cb, SEQ, RG_T)
            gate_c, xc = _in_odd(cs, mods, ctx_row, g_mix, w_in_odd, i, cw, cb, CTX_LEN, CTX_LEN)
            gate_l = gate_l.reshape(BATCH, SEQ, D_RNN)
            gate_c = gate_c.reshape(BATCH, CTX_LEN, D_RNN)
            row = lambda v: v.reshape(1, D_RNN)
            half = lambda v: 0.5 * row(v)
            hb_c, hb_l = _rg_scan(xc, xl, _rg_gate_weights(rg_wa[i, 1], rg_wx[i, 1]),
                                  half(rg_ba[i, 1]), half(rg_bx[i, 1]), row(rg_lam[i, 1]), True)
            cs_new, xs = _rg_scan(
                xc, xl, _rg_gate_weights(rg_wa[i, 0], rg_wx[i, 0]),
                half(rg_ba[i, 0]), half(rg_bx[i, 0]), row(rg_lam[i, 0]), False,
                (gate_c, gate_l, hb_c, hb_l, cs.reshape(BATCH, CTX_LEN, D_MODEL),
                 xs.reshape(BATCH, SEQ, D_MODEL), mods, (w_out_odd, i)))
            xs = xs.reshape(BATCH * SEQ, D_MODEL)
            if run_ctx:
                cs = cs_new.reshape(BATCH * CTX_LEN, D_MODEL)
        else:
            cw, cb = hy_conv_w[i], hy_conv_b[i].reshape(1, HY_COLS)
            lng = sgu_ln_g[i].reshape(1, W_B)
            sgw = sgu_w[i].astype(BF16)
            sgb = jnp.repeat(sgu_b[i].T, SGU_DH, axis=1)
            skip = hy_skip[i].reshape(1, HY_ORDER * W_A)
            fargs = (hy_f1_w[i], hy_f1_b[i], hy_f2_w[i], hy_f2_b[i], hy_f3_w[i], hy_f3_b[i],
                     hy_sin_freq[i])
            kf = _filter_spectrum(SEQ, *fargs, skip)
            za, yb = _in_even(xs, mods, lat_row_even, g_mix, w_in_even, i, cw, cb, lng, sgw, sgb,
                              SEQ, True)
            za = za.reshape(HY_TILES, BATCH, SEQ, LANES)
            y1 = _longconv(za, 0, za, n_tiles, kf, 0)
            ya = _longconv(y1, 0, za, 2 * n_tiles, kf, n_tiles)
            mix_l = (ya, yb, (w_out_even, i), True)
            if run_ctx:
                kfc = _filter_spectrum(CTX_LEN, *fargs, skip)
                zc, ybc = _in_even(cs, mods, ctx_row, g_mix, w_in_even, i, cw, cb, lng, sgw, sgb,
                                   CTX_LEN, False)
                zc = zc.reshape(HY_TILES, BATCH, CTX_LEN, LANES)
                y1c = _ctx_conv(zc, 0, zc, n_tiles, kfc, 0)
                yac = _ctx_conv(y1c, 0, zc, 2 * n_tiles, kfc, n_tiles)
                mix_c = (yac, ybc, (w_out_even, i), False)
        g_ffn = norm_ffn_g[l].reshape(1, D_MODEL)
        fg = final_norm_g.reshape(1, D_MODEL)
        swap = _col_major(l) != _col_major(l + 1)
        xs = _ffn(xs, mods, lat_row_ffn, g_ffn, w_ffn_in, w_ffn_out, l, fg, l == DEPTH - 1,
                  mix_l, swap)
        if run_ctx:
            cs = _ffn(cs, mods, ctx_row, g_ffn, w_ffn_in, w_ffn_out, l, fg, False, mix_c)
    return xs.reshape(BATCH, SEQ, D_MODEL)
```

```python
import functools
import math

import numpy as np
import jax
import jax.numpy as jnp
from jax import lax
from jax.experimental import pallas as pl
from jax.experimental.pallas import tpu as pltpu

F32 = jnp.float32
BF16 = jnp.bfloat16

D_MODEL = 1024
BATCH = 4
SEQ = 4096
DEPTH = 4
GRID_W = 64
CTX_LEN = 256
N_MOD = 6
NORM_EPS = 1e-6
W_A = D_MODEL // 2
HY_ORDER = 2
HY_SHORT = 3
HY_BANDS = 16
HY_TARGET = 1e-2
HY_FAST_PCT = 0.3
HY_SLOW_PCT = 1.5
W_B = D_MODEL // 2
SGU_GROUPS = 4
SGU_DH = W_B // SGU_GROUPS
CHUNK = 128
D_RNN = ((4 * D_MODEL // 3 + 127) // 128) * 128
RG_HEADS = 16
RG_DH = D_RNN // RG_HEADS
RG_CONV = 4
RG_C = 8.0
D_FF = ((8 * D_MODEL // 3 + 255) // 256) * 256

VMEM_BYTES_V7X = 64 * 1024 * 1024
VMEM_LIMIT = VMEM_BYTES_V7X - 4 * 1024 * 1024
LANES = 128

FFT_N = 2 * SEQ
FFT_N1 = 64
FFT_N2 = 128
FFT_HALF_N1 = FFT_N1 // 2
FFT_PITCH = FFT_N2 + 8
CTX_N = 2 * CTX_LEN


def _cparams(*sem):
    return pltpu.CompilerParams(dimension_semantics=sem, vmem_limit_bytes=VMEM_LIMIT)


def _single_buffered(block_shape, index_map):
    return pl.BlockSpec(block_shape, index_map, pipeline_mode=pl.Buffered(1))


def _layer_spec(stack, layer):
    rest = stack.shape[1:]
    return _single_buffered((None,) + rest, lambda *_: (layer,) + (0,) * len(rest))


@functools.lru_cache(maxsize=None)
def _dft_tables_f32():
    n = FFT_N
    k1 = np.arange(FFT_N1)
    n2 = np.arange(FFT_N2)

    def angle(n1):
        m = (FFT_N2 * n1[None, None, :] + n2[:, None, None]) * k1[None, :, None]
        return 2.0 * np.pi * (m % n) / n

    a = angle(np.arange(FFT_HALF_N1))
    c, s = np.cos(a), np.sin(a)
    g = np.empty((FFT_N2, FFT_N1, 2, 2, FFT_HALF_N1))
    g[:, :, 0, 0], g[:, :, 0, 1] = c, s
    g[:, :, 1, 0], g[:, :, 1, 1] = -s, c
    g = g.reshape(FFT_N2, 2 * FFT_N1, 2 * FFT_HALF_N1)

    a = angle(np.arange(FFT_N1))
    gf = np.stack([np.cos(a), -np.sin(a)], axis=2).reshape(FFT_N2, 2 * FFT_N1, FFT_N1)

    a = np.transpose(angle(np.arange(FFT_HALF_N1)), (0, 2, 1))
    c, s = np.cos(a) / n, np.sin(a) / n
    h = np.empty((FFT_N2, 2, FFT_HALF_N1, FFT_N1, 2))
    h[:, 0, :, :, 0], h[:, 0, :, :, 1] = c, -s
    h[:, 1, :, :, 0], h[:, 1, :, :, 1] = s, c
    h = h.reshape(FFT_N2, 2 * FFT_HALF_N1, 2 * FFT_N1)

    a = 2.0 * np.pi * np.outer(n2, n2) / FFT_N2
    c, s = np.cos(a), np.sin(a)
    f2 = np.block([[c, s], [-s, c]])
    f2i = np.block([[c, -s], [s, c]])

    kk = np.arange(CTX_N)
    a = 2.0 * np.pi * np.outer(kk, np.arange(CTX_LEN)) / CTX_N
    c, s = np.cos(a), np.sin(a)
    fc = np.block([[c, s], [-s, c]])
    a = 2.0 * np.pi * np.outer(kk, np.arange(CTX_N)) / CTX_N
    fcf = np.concatenate([np.cos(a), -np.sin(a)], axis=0)
    a = 2.0 * np.pi * np.outer(np.arange(CTX_LEN), kk) / CTX_N
    c, s = np.cos(a) / CTX_N, np.sin(a) / CTX_N
    fci = np.block([[c, -s], [s, c]])

    tables = dict(g=g, gf=gf, h=h, f2=f2, f2i=f2i, fc=fc, fcf=fcf, fci=fci)
    return {k: np.asarray(v, dtype=np.float32) for k, v in tables.items()}


def _dft_tables():
    return {k: jnp.asarray(v).astype(BF16) for k, v in _dft_tables_f32().items()}


SEQ_TILE = 1024
N1_PER_TILE = SEQ_TILE // FFT_N2


def _scatter_n2(s_ref, n2, val):
    for p in range(s_ref.shape[0]):
        s_ref[p, pl.ds(n2, 2 * FFT_N1, stride=FFT_PITCH), :] = val[:, p * LANES:(p + 1) * LANES]


def _gather_n2(s_ref, n2):
    return jnp.concatenate(
        [s_ref[p, pl.ds(n2, 2 * FFT_N1, stride=FFT_PITCH), :] for p in range(s_ref.shape[0])],
        axis=1)


def _spectrum_rows(s_ref, k1):
    base = pl.multiple_of(k1 * (2 * FFT_PITCH), 8)
    return base, jnp.concatenate(
        [jnp.concatenate([s_ref[p, pl.ds(base + o, FFT_N2), :] for p in range(s_ref.shape[0])],
                         axis=1) for o in (0, FFT_PITCH)], axis=0)


HY_FFN = 64
FILT_ROWS = 1024
FILT_TILES = 2
HIGHEST = lax.Precision.HIGHEST


def _tap_rows(row, seq, n2_major):
    if n2_major:
        n2, n1 = row >> int(math.log2(FFT_N1)), row & (FFT_N1 - 1)
        bwd = n1 >= FFT_HALF_N1
        m = FFT_N2 * (n1 - FFT_HALF_N1) + n2
        fwd_pos = FFT_N2 * n1 + n2
    else:
        bwd = row >= seq
        m = row - seq
        fwd_pos = row
    pos = jnp.where(bwd, seq - m, fwd_pos).astype(F32)
    bwd_f = jnp.where(bwd, 1.0, 0.0)
    keep_f = jnp.where(bwd & (m == 0), 0.0, 1.0)
    return pos, bwd_f, keep_f


def _filter_kernel(ec_ref, f1w_ref, f1b_ref, f2w_ref, f2b_ref, sf_ref, w3f_ref, w3b_ref, b3f_ref,
                   b3b_ref, dl_ref, skip_ref, *rest, seq, n2_major):
    if n2_major:
        gf_ref, f2_ref, o_ref, hid_ref, k_ref, s_ref = rest
    else:
        fcf_ref, o_ref, hid_ref, k_ref = rest
    n_rows = 2 * seq
    rb = min(FILT_ROWS, n_rows)

    def rows_of(i):
        r0 = pl.multiple_of(i * rb, rb)
        return r0, _tap_rows(r0 + lax.broadcasted_iota(jnp.int32, (rb, 1), 0), seq, n2_major)

    @pl.when(pl.program_id(0) == 0)
    def _():
        sf = sf_ref[...]

        def hidden(i, carry):
            r0, (pos, _, _) = rows_of(i)
            t = pos * (1.0 / (seq - 1))
            w = pos * (2.0 * math.pi / seq)
            emb = jnp.sin(w * ec_ref[0:1] + ec_ref[1:2]) + t * ec_ref[2:3]
            h = jnp.sin(sf * (jnp.dot(emb, f1w_ref[...], precision=HIGHEST,
                                      preferred_element_type=F32) + f1b_ref[...]))
            h = jnp.sin(sf * (jnp.dot(h.astype(BF16), f2w_ref[...].astype(BF16),
                                      preferred_element_type=F32) + f2b_ref[...]))
            hid_ref[pl.ds(r0, rb), :] = h
            return carry

        lax.fori_loop(0, n_rows // rb, hidden, 0)

    def taps(i, ssq):
        r0, (pos, bwd_f, keep_f) = rows_of(i)
        h = hid_ref[pl.ds(r0, rb), :].astype(BF16)
        neg_t = -(pos * (1.0 / (seq - 1)))
        ks = []
        for c in range(FILT_TILES):
            kf = jnp.dot(h, w3f_ref[c].astype(BF16), preferred_element_type=F32) + b3f_ref[c]
            kb = jnp.dot(h, w3b_ref[c].astype(BF16), preferred_element_type=F32) + b3b_ref[c]
            ks.append((kf + bwd_f * (kb - kf)) * jnp.exp(neg_t * dl_ref[c]) * keep_f)
        k = jnp.concatenate(ks, axis=1)
        k_ref[pl.ds(r0, rb), :] = k
        return ssq + jnp.sum(k * k, axis=0, keepdims=True)

    ssq = lax.fori_loop(0, n_rows // rb, taps, jnp.zeros((1, FILT_TILES * LANES), F32))
    scale = lax.rsqrt(ssq + NORM_EPS)
    skip = skip_ref[...]

    if n2_major:
        def stage1(n2, carry):
            x = k_ref[pl.ds(pl.multiple_of(n2 * FFT_N1, FFT_N1), FFT_N1), :] * scale
            _scatter_n2(s_ref, n2, jnp.dot(gf_ref[n2], x.astype(BF16),
                                           preferred_element_type=F32))
            return carry

        lax.fori_loop(0, FFT_N2, stage1, 0, unroll=8)

        def stage2(k1, carry):
            _, r = _spectrum_rows(s_ref, k1)
            z = jnp.dot(f2_ref[...], r.astype(BF16), preferred_element_type=F32)
            row = pl.multiple_of(k1 * (2 * FFT_N2), 2 * FFT_N2)
            o_ref[pl.ds(row, FFT_N2), :] = (z[:FFT_N2] + skip).astype(BF16)
            o_ref[pl.ds(row + FFT_N2, FFT_N2), :] = z[FFT_N2:].astype(BF16)
            return carry

        lax.fori_loop(0, FFT_N1, stage2, 0, unroll=8)
    else:
        z = jnp.dot(fcf_ref[...], (k_ref[...] * scale).astype(BF16), preferred_element_type=F32)
        o_ref[:n_rows, :] = (z[:n_rows] + skip).astype(BF16)
        o_ref[n_rows:, :] = z[n_rows:].astype(BF16)


def _filter_spectrum(seq, f1_w, f1_b, f2_w, f2_b, f3_w, f3_b, sin_freq, skip):
    n2_major = seq == SEQ
    t = _dft_tables()
    pad = lambda a, r, c: jnp.pad(a, ((0, r - a.shape[0]), (0, c - a.shape[1])))
    f = jnp.linspace(1e-4, HY_BANDS - 1, HY_BANDS, dtype=F32)
    zeros = lambda n: jnp.zeros((n,), F32)
    ones = lambda n: jnp.ones((n,), F32)
    rest = LANES - 1 - 2 * HY_BANDS
    ec = jnp.stack([
        jnp.concatenate([zeros(1), f, f, zeros(rest)]),
        jnp.concatenate([zeros(1), (math.pi / 2) * ones(HY_BANDS), math.pi * ones(HY_BANDS),
                         zeros(rest)]),
        jnp.concatenate([ones(1), zeros(LANES - 1)]),
    ] + [zeros(LANES)] * 5)
    row = lambda v: pad(v.reshape(1, -1), 1, LANES)
    n_tiles = HY_ORDER * W_A // LANES
    w3 = f3_w.reshape(HY_FFN, HY_ORDER, 2, W_A)
    b3 = f3_b.reshape(HY_ORDER, 2, W_A)
    tiles_w = lambda d: jnp.pad(
        w3[:, :, d, :].reshape(HY_FFN, n_tiles, LANES).transpose(1, 0, 2),
        ((0, 0), (0, LANES - HY_FFN), (0, 0)))
    tiles_b = lambda d: b3[:, d, :].reshape(n_tiles, 1, LANES)
    deltas = jnp.abs(jnp.linspace(math.log(HY_TARGET) / HY_SLOW_PCT,
                                  math.log(HY_TARGET) / HY_FAST_PCT, W_A, dtype=F32))
    dl = jnp.tile(deltas, HY_ORDER).reshape(n_tiles, 1, LANES)
    const = lambda shape: _single_buffered(shape, lambda j: (0,) * len(shape))
    tile = lambda shape: pl.BlockSpec((FILT_TILES,) + shape, lambda j: (j, 0, 0))
    ct = FILT_TILES * LANES
    args = [ec, pad(f1_w, LANES, LANES), row(f1_b), pad(f2_w, LANES, LANES), row(f2_b),
            row(sin_freq), tiles_w(0), tiles_w(1), tiles_b(0), tiles_b(1), dl, skip]
    in_specs = [const((8, LANES)), const((LANES, LANES)), const((1, LANES)),
                const((LANES, LANES)), const((1, LANES)), const((1, LANES)),
                tile((LANES, LANES)), tile((LANES, LANES)), tile((1, LANES)), tile((1, LANES)),
                tile((1, LANES)), pl.BlockSpec((1, ct), lambda j: (0, j))]
    scratch = [pltpu.VMEM((2 * seq, LANES), F32), pltpu.VMEM((2 * seq, ct), F32)]
    if n2_major:
        args += [t["gf"], t["f2"]]
        in_specs += [const(t["gf"].shape), const(t["f2"].shape)]
        scratch += [pltpu.VMEM((FILT_TILES, 2 * FFT_N1 * FFT_PITCH, LANES), F32)]
    else:
        args += [t["fcf"]]
        in_specs += [const(t["fcf"].shape)]
    return pl.pallas_call(
        functools.partial(_filter_kernel, seq=seq, n2_major=n2_major),
        grid=(n_tiles // FILT_TILES,),
        in_specs=in_specs,
        out_specs=pl.BlockSpec((4 * seq, ct), lambda j: (0, j)),
        out_shape=jax.ShapeDtypeStruct((4 * seq, HY_ORDER * W_A), BF16),
        scratch_shapes=scratch,
        compiler_params=_cparams("arbitrary"),
        name="hyena_filter" if n2_major else "hyena_ctx_filter",
    )(*args)


LC_OUT_CHUNKS = 4
LC_N2_PER_CHUNK = FFT_N2 // LC_OUT_CHUNKS


def _longconv_kernel(v_ref, gate_ref, kf_ref, g_ref, h_ref, f2_ref, f2i_ref, o_ref, s_ref):
    ct = v_ref.shape[2]
    t = pl.program_id(1)

    @pl.when(t == 0)
    def _():
        def stage1(n2, carry):
            r0 = pl.multiple_of(n2 * N1_PER_TILE, N1_PER_TILE)
            blk = [jnp.concatenate([v_ref[b, pl.ds(i * SEQ_TILE + r0, N1_PER_TILE), :]
                                    for i in range(SEQ // SEQ_TILE)], axis=0)
                   for b in range(BATCH)]
            x = jnp.concatenate([jnp.concatenate([blk[0], blk[2]], axis=1),
                                 jnp.concatenate([blk[1], blk[3]], axis=1)], axis=0)
            _scatter_n2(s_ref, n2, jnp.dot(g_ref[n2], x.astype(BF16), preferred_element_type=F32))
            return carry

        lax.fori_loop(0, FFT_N2, stage1, 0, unroll=16)

        def stage2(k1, carry):
            base, r = _spectrum_rows(s_ref, k1)
            z = jnp.dot(f2_ref[...], r.astype(BF16), preferred_element_type=F32)
            kf = kf_ref[pl.ds(pl.multiple_of(k1 * (2 * FFT_N2), 2 * FFT_N2), 2 * FFT_N2), :]
            kf = kf.astype(F32)
            kr = jnp.concatenate([kf[:FFT_N2]] * 2, axis=1)
            ki = jnp.concatenate([kf[FFT_N2:]] * 2, axis=1)
            zr, zi = z[:FFT_N2], z[FFT_N2:]
            p = jnp.concatenate([zr * kr - zi * ki, zr * ki + zi * kr], axis=0)
            q = jnp.dot(f2i_ref[...], p.astype(BF16), preferred_element_type=F32)
            for p in range(2):
                s_ref[p, pl.ds(base, FFT_N2), :] = q[:FFT_N2, p * ct:(p + 1) * ct]
                s_ref[p, pl.ds(base + FFT_PITCH, FFT_N2), :] = q[FFT_N2:, p * ct:(p + 1) * ct]
            return carry

        lax.fori_loop(0, FFT_N1, stage2, 0, unroll=8)

    def stage3(j, carry):
        n2 = t * LC_N2_PER_CHUNK + j
        rq = _gather_n2(s_ref, n2)
        y = jnp.dot(h_ref[n2], rq.astype(BF16), preferred_element_type=F32)
        r0 = pl.multiple_of(j * N1_PER_TILE, N1_PER_TILE)
        for b in range(BATCH):
            ri, pair = b % 2, b // 2
            for i in range(SEQ // SEQ_TILE):
                n1 = ri * FFT_HALF_N1 + i * N1_PER_TILE
                yb = y[n1:n1 + N1_PER_TILE, pair * ct:(pair + 1) * ct]
                o_ref[b, i, pl.ds(r0, N1_PER_TILE), :] = (
                    gate_ref[b, i, pl.ds(r0, N1_PER_TILE), :] * yb)
        return carry

    lax.fori_loop(0, LC_N2_PER_CHUNK, stage3, 0, unroll=16)


def _longconv(v, v_tile, gate, gate_tile, kf, kf_col):
    t = _dft_tables()
    ct = LANES
    n_seq_tiles = SEQ // SEQ_TILE
    rows = SEQ_TILE // LC_OUT_CHUNKS
    gate = gate.reshape(gate.shape[0], BATCH, n_seq_tiles, SEQ_TILE, ct)
    out = pl.pallas_call(
        _longconv_kernel,
        grid=(W_A // ct, LC_OUT_CHUNKS),
        in_specs=[
            _single_buffered((None, BATCH, SEQ, ct), lambda j, i: (v_tile + j, 0, 0, 0)),
            pl.BlockSpec((None, BATCH, n_seq_tiles, rows, ct),
                         lambda j, i: (gate_tile + j, 0, 0, i, 0)),
            pl.BlockSpec((2 * FFT_N, ct), lambda j, i: (0, kf_col + j)),
            _single_buffered(t["g"].shape, lambda j, i: (0, 0, 0)),
            _single_buffered(t["h"].shape, lambda j, i: (0, 0, 0)),
            _single_buffered(t["f2"].shape, lambda j, i: (0, 0)),
            _single_buffered(t["f2i"].shape, lambda j, i: (0, 0)),
        ],
        out_specs=pl.BlockSpec((None, BATCH, n_seq_tiles, rows, ct),
                               lambda j, i: (j, 0, 0, i, 0)),
        out_shape=jax.ShapeDtypeStruct((W_A // ct, BATCH, n_seq_tiles, SEQ_TILE, ct), F32),
        scratch_shapes=[pltpu.VMEM((2, 2 * FFT_N1 * FFT_PITCH, ct), F32)],
        compiler_params=_cparams("arbitrary", "arbitrary"),
        name="hyena_longconv",
    )(v, gate, kf, t["g"], t["h"], t["f2"], t["f2i"])
    return out.reshape(W_A // ct, BATCH, SEQ, ct)


def _ctx_conv_kernel(v_ref, gate_ref, kf_ref, fc_ref, fci_ref, o_ref):
    ct = v_ref.shape[2]
    x = jnp.concatenate([jnp.concatenate([v_ref[0], v_ref[2]], axis=1),
                         jnp.concatenate([v_ref[1], v_ref[3]], axis=1)], axis=0)
    z = jnp.dot(fc_ref[...], x.astype(BF16), preferred_element_type=F32)
    kf = kf_ref[...].astype(F32)
    kr = jnp.concatenate([kf[:CTX_N]] * 2, axis=1)
    ki = jnp.concatenate([kf[CTX_N:]] * 2, axis=1)
    zr, zi = z[:CTX_N], z[CTX_N:]
    p = jnp.concatenate([zr * kr - zi * ki, zr * ki + zi * kr], axis=0)
    y = jnp.dot(fci_ref[...], p.astype(BF16), preferred_element_type=F32)
    for b in range(BATCH):
        ri, pair = b % 2, b // 2
        o_ref[b] = gate_ref[b] * y[ri * CTX_LEN:(ri + 1) * CTX_LEN, pair * ct:(pair + 1) * ct]


def _ctx_conv(v, v_tile, gate, gate_tile, kf, kf_col):
    t = _dft_tables()
    ct = LANES
    return pl.pallas_call(
        _ctx_conv_kernel,
        grid=(W_A // ct,),
        in_specs=[
            pl.BlockSpec((None, BATCH, CTX_LEN, ct), lambda j: (v_tile + j, 0, 0, 0)),
            pl.BlockSpec((None, BATCH, CTX_LEN, ct), lambda j: (gate_tile + j, 0, 0, 0)),
            pl.BlockSpec((2 * CTX_N, ct), lambda j: (0, kf_col + j)),
            pl.BlockSpec(t["fc"].shape, lambda j: (0, 0)),
            pl.BlockSpec(t["fci"].shape, lambda j: (0, 0)),
        ],
        out_specs=pl.BlockSpec((None, BATCH, CTX_LEN, ct), lambda j: (j, 0, 0, 0)),
        out_shape=jax.ShapeDtypeStruct((W_A // ct, BATCH, CTX_LEN, ct), F32),
        compiler_params=_cparams("arbitrary"),
        name="hyena_ctx_conv",
    )(v, gate, kf, t["fc"], t["fci"])


MOD_ROWS = 8
CTX_MOD_ROW = BATCH


def _ada_kernel(c_ref, w_ref, b_ref, o_ref):
    cv = c_ref[...]
    s = cv * jax.nn.sigmoid(cv)
    o_ref[0] = jnp.dot(s.astype(BF16), w_ref[0].astype(BF16),
                       preferred_element_type=F32) + b_ref[0]


def _ada_mods(c, c_ctx, w_ada, b_ada):
    cv = jnp.concatenate(
        [c, c_ctx[None], jnp.zeros((MOD_ROWS - BATCH - 1, D_MODEL), F32)], axis=0)
    n = N_MOD * D_MODEL
    tn = n // 4
    return pl.pallas_call(
        _ada_kernel,
        grid=(DEPTH, n // tn),
        in_specs=[
            pl.BlockSpec((MOD_ROWS, D_MODEL), lambda l, j: (0, 0)),
            pl.BlockSpec((1, D_MODEL, tn), lambda l, j: (l, 0, j)),
            pl.BlockSpec((1, 1, tn), lambda l, j: (l, 0, j)),
        ],
        out_specs=pl.BlockSpec((1, MOD_ROWS, tn), lambda l, j: (l, 0, j)),
        out_shape=jax.ShapeDtypeStruct((DEPTH, MOD_ROWS, n), F32),
        compiler_params=_cparams("arbitrary", "arbitrary"),
        name="ada_mods",
    )(cv, w_ada, b_ada.reshape(DEPTH, 1, n))


def _mod_spec(m, row_fn):
    return pl.BlockSpec((1, 1, D_MODEL), lambda i, *_: (row_fn(i), 0, m))


def _rms_mod(x, g, shift, scale):
    ms = jnp.mean(x * x, axis=-1, keepdims=True)
    return (x * lax.rsqrt(ms + NORM_EPS)) * (g * (1.0 + scale)) + shift


GELU_C = math.sqrt(2.0 / math.pi)


def _gelu(x):
    hx = 0.5 * x
    return hx + hx * jnp.tanh(x * (GELU_C + (GELU_C * 0.044715) * (x * x)))


HALO = 8


def _dwconv(z_all, tm, w, b, left, valid):
    n = z_all.shape[0]
    y = b + w[left:left + 1] * z_all[:tm]
    for k in range(w.shape[0]):
        d = k - left
        if d == 0:
            continue
        s = pltpu.roll(z_all, (-d) % n, 0)[:tm]
        if valid is not None:
            s = s * valid(d)
        y = y + s * w[k:k + 1]
    return y


def _halo_specs(tm, n_rows):
    per = tm // HALO
    last = n_rows // HALO - 1
    return [
        pl.BlockSpec((tm, D_MODEL), lambda i: (i, 0)),
        pl.BlockSpec((HALO, D_MODEL), lambda i: (jnp.maximum(i * per - 1, 0), 0)),
        pl.BlockSpec((HALO, D_MODEL), lambda i: (jnp.minimum((i + 1) * per, last), 0)),
    ]


def _normed_tile(x_ref, xp_ref, xn_ref, sh_ref, sc_ref, g_ref, seq_len):
    tm = x_ref.shape[0]
    g, shift, scale = g_ref[...], sh_ref[0], sc_ref[0]
    h0 = _rms_mod(x_ref[:tm // 2, :], g, shift, scale)
    h = _rms_mod(x_ref[tm // 2:, :], g, shift, scale)
    hn = _rms_mod(xn_ref[...], g, shift, scale)
    hp = _rms_mod(xp_ref[...], g, shift, scale)
    if seq_len >= tm:
        r0 = pl.program_id(0) * tm
        hp = hp * jnp.where((r0 & (seq_len - 1)) == 0, 0.0, 1.0)
        hn = hn * jnp.where(((r0 + tm) & (seq_len - 1)) == 0, 0.0, 1.0)
        valid = None
    else:
        hp, hn = jnp.zeros_like(hp), jnp.zeros_like(hn)
        pos = lax.broadcasted_iota(jnp.int32, (tm, 1), 0) & (seq_len - 1)
        valid = lambda d: jnp.where((pos + d >= 0) & (pos + d < seq_len), 1.0, 0.0)
    return [h0.astype(BF16), jnp.concatenate([h, hn, hp], axis=0).astype(BF16)], valid


def _dot_rows(pieces, w):
    return jnp.concatenate([jnp.dot(p, w, preferred_element_type=F32) for p in pieces], axis=0)


EVEN_TM = SEQ_TILE
HY_COLS = 3 * W_A
HY_TILES = HY_COLS // LANES


def _in_even_kernel(x_ref, xp_ref, xn_ref, sh_ref, sc_ref, g_ref, w_ref, cw_ref, cb_ref,
                    lng_ref, sgw_ref, sgb_ref, za_ref, yb_ref, *, seq_len, n2_major):
    tm = x_ref.shape[0]
    pieces, valid = _normed_tile(x_ref, xp_ref, xn_ref, sh_ref, sc_ref, g_ref, seq_len)
    h = jnp.concatenate([pieces[0], pieces[1][:tm // 2]], axis=0)
    cw = 4 * LANES
    for cc in range(HY_COLS // cw):
        cols = slice(cc * cw, (cc + 1) * cw)
        z_all = _dot_rows(pieces, w_ref[:, cols])
        y = _dwconv(z_all, tm, cw_ref[:, cols], cb_ref[:, cols], 1, valid)
        for c in range(cw // LANES):
            tile = cc * (cw // LANES) + c
            yc = y[:, c * LANES:(c + 1) * LANES]
            if n2_major:
                for j in range(N1_PER_TILE):
                    za_ref[tile, pl.ds(j, FFT_N2, stride=N1_PER_TILE), :] = (
                        yc[j * FFT_N2:(j + 1) * FFT_N2])
            else:
                za_ref[tile] = yc
    u = _gelu(jnp.dot(h, w_ref[:, HY_COLS:HY_COLS + W_B], preferred_element_type=F32))
    vb = _gelu(jnp.dot(h, w_ref[:, HY_COLS + W_B:], preferred_element_type=F32))
    vc = vb - jnp.mean(vb, axis=-1, keepdims=True)
    vn = vc * lax.rsqrt(jnp.mean(vc * vc, axis=-1, keepdims=True) + NORM_EPS) * lng_ref[...]
    vn = vn.astype(BF16)
    n_chunks = tm // CHUNK
    for q in range(SGU_GROUPS):
        cols = slice(q * SGU_DH, (q + 1) * SGU_DH)
        rhs = jnp.concatenate(
            [vn[ch * CHUNK:(ch + 1) * CHUNK, cols] for ch in range(n_chunks)], axis=1)
        s_all = jnp.dot(sgw_ref[q], rhs, preferred_element_type=F32)
        for ch in range(n_chunks):
            rows = slice(ch * CHUNK, (ch + 1) * CHUNK)
            s = s_all[:, ch * SGU_DH:(ch + 1) * SGU_DH] + sgb_ref[:, cols]
            yb_ref[rows, cols] = (u[rows, cols] * s).astype(BF16)


def _in_even(x, mods, mod_row, g, w, layer, cw, cb, lng, sgw, sgb, seq_len, n2_major):
    n_rows = x.shape[0]
    tm = EVEN_TM
    za_shape = (HY_TILES, n_rows, LANES)
    za_spec = pl.BlockSpec((HY_TILES, tm, LANES), lambda i: (0, i, 0))
    const = lambda shape: _single_buffered(shape, lambda i: (0,) * len(shape))
    return pl.pallas_call(
        functools.partial(_in_even_kernel, seq_len=seq_len, n2_major=n2_major),
        grid=(n_rows // tm,),
        in_specs=_halo_specs(tm, n_rows) + [
            _mod_spec(0, mod_row), _mod_spec(1, mod_row),
            const((1, D_MODEL)), _layer_spec(w, layer), const(cw.shape), const(cb.shape),
            const(lng.shape), const(sgw.shape), const(sgb.shape),
        ],
        out_specs=[za_spec, pl.BlockSpec((tm, W_B), lambda i: (i, 0))],
        out_shape=[jax.ShapeDtypeStruct(za_shape, F32),
                   jax.ShapeDtypeStruct((n_rows, W_B), BF16)],
        compiler_params=_cparams("arbitrary"),
        name="in_proj_even",
    )(x, x, x, mods, mods, g, w, cw, cb, lng, sgw, sgb)


FFN_TM = SEQ_TILE
FFN_SUB = 256
GRID_H = SEQ // GRID_W
GT_ROWS = FFN_TM // GRID_W
GT_PITCH = GRID_W + 8


def _even_mix(ya_ref, yb_ref, w_ref, n2_major):
    parts = []
    for c in range(W_A // LANES):
        if n2_major:
            yac = jnp.concatenate(
                [ya_ref[c, pl.ds(j, FFT_N2, stride=N1_PER_TILE), :]
                 for j in range(N1_PER_TILE)], axis=0)
        else:
            yac = ya_ref[c]
        parts.append(yac.astype(BF16))
    y = jnp.concatenate(parts + [yb_ref[...]], axis=1)
    return jnp.dot(y, w_ref[...], preferred_element_type=F32)


def _ffn_kernel(*refs, final_norm, mix, n2_major, swap_grid):
    refs = list(refs)
    x_ref, sh_ref, sc_ref, gt_ref, g_ref, w_in_ref, wo_ref, fg_ref = refs[:8]
    del refs[:8]
    if mix:
        ya_ref, yb_ref, gm_ref, wm_ref = refs[:4]
        del refs[:4]
    o_ref = refs.pop(0)
    x = x_ref[...]
    if mix:
        x = x + gm_ref[0] * _even_mix(ya_ref, yb_ref, wm_ref, n2_major)
    h = _rms_mod(x, g_ref[...], sh_ref[0], sc_ref[0]).astype(BF16)
    acts = []
    for c0 in range(0, D_FF, FFN_SUB):
        c1 = min(c0 + FFN_SUB, D_FF)
        hz = 0.5 * jnp.dot(h, w_in_ref[:, c0:c1], preferred_element_type=F32)
        z2 = jnp.dot(h, w_in_ref[:, D_FF + c0:D_FF + c1], preferred_element_type=F32)
        acts.append(((hz + hz * jnp.tanh(hz)) * z2).astype(BF16))
    y = x + gt_ref[0] * jnp.dot(jnp.concatenate(acts, axis=1), wo_ref[...],
                                preferred_element_type=F32)
    if final_norm:
        y = y * lax.rsqrt(jnp.mean(y * y, axis=-1, keepdims=True) + NORM_EPS) * fg_ref[...]
    if not swap_grid:
        o_ref[...] = y
        return
    s_ref = refs.pop(0)
    nt = D_MODEL // LANES
    for t in range(nt):
        for j in range(GT_ROWS):
            s_ref[t, j * GT_PITCH:j * GT_PITCH + GRID_W, :] = (
                y[j * GRID_W:(j + 1) * GRID_W, t * LANES:(t + 1) * LANES])
    for c in range(GRID_W):
        o_ref[c] = jnp.concatenate(
            [s_ref[t, pl.ds(c, GT_ROWS, stride=GT_PITCH), :] for t in range(nt)], axis=1)


def _ffn(x, mods, mod_row, g, w_in, w_out, layer, final_g, final_norm, mix=None,
         swap_grid=False):
    n_rows = x.shape[0]
    tm = FFN_TM
    const = lambda shape: _single_buffered(shape, lambda i: (0,) * len(shape))
    in_specs = [
        pl.BlockSpec((tm, D_MODEL), lambda i: (i, 0)),
        _mod_spec(3, mod_row), _mod_spec(4, mod_row), _mod_spec(5, mod_row),
        const((1, D_MODEL)), _layer_spec(w_in, layer), _layer_spec(w_out, layer),
        const((1, D_MODEL)),
    ]
    args = [x, mods, mods, mods, g, w_in, w_out, final_g]
    n2_major = False
    if mix is not None:
        ya, yb, (w_mix, mix_layer), n2_major = mix
        nt = W_A // LANES
        in_specs += [pl.BlockSpec((nt, tm, LANES), lambda i: (0, i, 0)),
                     pl.BlockSpec((tm, W_B), lambda i: (i, 0)),
                     _mod_spec(2, mod_row), _layer_spec(w_mix, mix_layer)]
        args += [ya.reshape(nt, n_rows, LANES), yb, mods, w_mix]
    scratch = []
    if swap_grid:
        per_seq = SEQ // tm
        out_spec = pl.BlockSpec((None, GRID_W, GT_ROWS, D_MODEL),
                                lambda i: (i // per_seq, 0, i % per_seq, 0))
        out_shape = jax.ShapeDtypeStruct((n_rows // SEQ, GRID_W, GRID_H, D_MODEL), F32)
        scratch = [pltpu.VMEM((D_MODEL // LANES, GT_ROWS * GT_PITCH, LANES), F32)]
    else:
        out_spec = pl.BlockSpec((tm, D_MODEL), lambda i: (i, 0))
        out_shape = jax.ShapeDtypeStruct((n_rows, D_MODEL), F32)
    out = pl.pallas_call(
        functools.partial(_ffn_kernel, final_norm=final_norm, mix=mix is not None,
                          n2_major=n2_major, swap_grid=swap_grid),
        grid=(n_rows // tm,),
        in_specs=in_specs,
        out_specs=out_spec,
        out_shape=out_shape,
        scratch_shapes=scratch,
        compiler_params=_cparams("arbitrary"),
        name="ffn",
    )(*args)
    return out.reshape(n_rows, D_MODEL)


ODD_TM = 1024
RNN_TILES = D_RNN // LANES


def _in_odd_kernel(x_ref, xp_ref, xn_ref, sh_ref, sc_ref, g_ref, w_ref, cw_ref, cb_ref,
                   gate_ref, xl_ref, *, seq_len, chunk):
    tm = x_ref.shape[0]
    pieces, valid = _normed_tile(x_ref, xp_ref, xn_ref, sh_ref, sc_ref, g_ref, seq_len)
    half = tm // 2
    gate_ref[:half, :] = _gelu(jnp.dot(pieces[0], w_ref[:, :D_RNN],
                                       preferred_element_type=F32)).astype(BF16)
    gate_ref[half:, :] = _gelu(jnp.dot(pieces[1][:half], w_ref[:, :D_RNN],
                                       preferred_element_type=F32)).astype(BF16)
    cw = 4 * LANES
    for c0 in range(0, D_RNN, cw):
        cols = slice(c0, min(c0 + cw, D_RNN))
        wcols = slice(D_RNN + cols.start, D_RNN + cols.stop)
        z_all = _dot_rows(pieces, w_ref[:, wcols])
        y = _dwconv(z_all, tm, cw_ref[:, cols], cb_ref[:, cols], 2, valid)
        kg = chunk // RG_GROUPS
        for c in range((cols.stop - cols.start) // LANES):
            yc = y[:, c * LANES:(c + 1) * LANES]
            for r0 in range(0, tm, chunk):
                for gq in range(RG_GROUPS):
                    xl_ref[c0 // LANES + c, pl.ds(r0 + gq, kg, stride=RG_GROUPS), :] = (
                        yc[r0 + gq * kg:r0 + (gq + 1) * kg])


def _in_odd(x, mods, mod_row, g, w, layer, cw, cb, seq_len, chunk):
    n_rows = x.shape[0]
    tm = ODD_TM
    const = lambda shape: _single_buffered(shape, lambda i: (0,) * len(shape))
    return pl.pallas_call(
        functools.partial(_in_odd_kernel, seq_len=seq_len, chunk=chunk),
        grid=(n_rows // tm,),
        in_specs=_halo_specs(tm, n_rows) + [
            _mod_spec(0, mod_row), _mod_spec(1, mod_row),
            const((1, D_MODEL)), _layer_spec(w, layer), const(cw.shape), const(cb.shape),
        ],
        out_specs=[pl.BlockSpec((tm, D_RNN), lambda i: (i, 0)),
                   pl.BlockSpec((RNN_TILES, tm, LANES), lambda i: (0, i, 0))],
        out_shape=[jax.ShapeDtypeStruct((n_rows, D_RNN), BF16),
                   jax.ShapeDtypeStruct((RNN_TILES, n_rows, LANES), F32)],
        compiler_params=_cparams("arbitrary"),
        name="in_proj_odd",
    )(x, x, x, mods, mods, g, w, cw, cb)


RG_T = 512
RG_NCH = SEQ // RG_T
RG_GROUPS = 8
RG_PLANES_PER_PASS = 6
SQRT_GUARD = 1e-30
RG_WIN = 3 * LANES


def _rg_window_start(j):
    return min(max(LANES * (j - 1), 0), D_RNN - RG_WIN)


def _rg_chunk(x_ref, w_ref, ba_ref, bx_ref, lam_ref, a_pl, b_pl, carry_ref, reverse, emit):
    t_rows = x_ref.shape[1]
    kg = t_rows // RG_GROUPS
    xb = [x_ref[j].astype(BF16) for j in range(RNN_TILES)]
    for j in range(RNN_TILES):
        tile = slice(j * LANES, (j + 1) * LANES)
        wt = _rg_window_start(j) // LANES
        pre = jnp.dot(jnp.concatenate(xb[wt:wt + RG_WIN // LANES], axis=1), w_ref[j],
                      preferred_element_type=F32)
        lam = lam_ref[:, tile]
        softplus_neg = jnp.maximum(-lam, 0.0) + jnp.log1p(jnp.exp(-jnp.abs(lam)))
        th_r = jnp.tanh(pre[:, :LANES] + ba_ref[:, tile])
        th_i = jnp.tanh(pre[:, LANES:] + bx_ref[:, tile])
        c3 = (-0.5 * RG_C * math.log2(math.e)) * softplus_neg
        av = jnp.exp2(c3 + c3 * th_r)
        hx = 0.5 * x_ref[j]
        y = 1.0 - av * av
        a_pl[j, 0:t_rows, :] = av
        b_pl[j, 0:t_rows, :] = (y * lax.rsqrt(jnp.maximum(y, SQRT_GUARD))) * (hx + hx * th_i)

    def rows_k(pl_ref, j, k):
        return pl_ref[j, k * RG_GROUPS:(k + 1) * RG_GROUPS, :]

    order = list(range(kg))[::-1] if reverse else list(range(kg))
    groups = list(range(RG_GROUPS))[::-1] if reverse else list(range(RG_GROUPS))
    for j0 in range(0, RNN_TILES, RG_PLANES_PER_PASS):
        planes = range(j0, min(j0 + RG_PLANES_PER_PASS, RNN_TILES))
        big_a, big_b = {}, {}
        for n, k in enumerate(order):
            for j in planes:
                ak, bk = rows_k(a_pl, j, k), rows_k(b_pl, j, k)
                if n == 0:
                    big_a[j], big_b[j] = ak, bk
                else:
                    big_b[j] = ak * big_b[j] + bk
                    big_a[j] = ak * big_a[j]
        h = {}
        for j in planes:
            c = carry_ref[j, 0:1, :]
            rows = [None] * RG_GROUPS
            for gq in groups:
                rows[gq] = c
                c = big_a[j][gq:gq + 1] * c + big_b[j][gq:gq + 1]
            carry_ref[j, 0:1, :] = c
            h[j] = jnp.concatenate(rows, axis=0)
        for k in order:
            for j in planes:
                h[j] = rows_k(a_pl, j, k) * h[j] + rows_k(b_pl, j, k)
                b_pl[j, k * RG_GROUPS:(k + 1) * RG_GROUPS, :] = h[j]
        for j in planes:
            emit(j, jnp.concatenate(
                [b_pl[j, pl.ds(gq, kg, stride=RG_GROUPS), :] for gq in range(RG_GROUPS)], axis=0))


def _rg_bwd_kernel(xc_ref, xl_ref, w_ref, ba_ref, bx_ref, lam_ref, oc_ref, ol_ref,
                   a_pl, b_pl, carry_ref):
    def emitter(o_ref):
        def emit(j, hcur):
            o_ref[:, j * LANES:(j + 1) * LANES] = hcur.astype(BF16)
        return emit

    @pl.when(pl.program_id(1) == 0)
    def _():
        carry_ref[...] = jnp.zeros_like(carry_ref)
        _rg_chunk(xc_ref, w_ref, ba_ref, bx_ref, lam_ref, a_pl, b_pl, carry_ref, True,
                  emitter(oc_ref))

    @pl.when(pl.program_id(1) > 0)
    def _():
        _rg_chunk(xl_ref, w_ref, ba_ref, bx_ref, lam_ref, a_pl, b_pl, carry_ref, True,
                  emitter(ol_ref))


def _rg_fwd_kernel(xc_ref, xl_ref, w_ref, ba_ref, bx_ref, lam_ref, gc_ref, gl_ref, hc_ref, hl_ref,
                   rc_ref, rl_ref, gtc_ref, gtl_ref, wo_ref, oc_ref, ol_ref,
                   a_pl, b_pl, carry_ref, y_ref):
    def run(x_ref, g_ref, hb_ref, res_ref, gt_ref, o_ref):
        t_rows = x_ref.shape[1]

        def emit(j, hcur):
            tile = slice(j * LANES, (j + 1) * LANES)
            y_ref[0:t_rows, tile] = (g_ref[:, tile].astype(F32)
                                     * (hcur + hb_ref[:, tile].astype(F32))).astype(BF16)

        _rg_chunk(x_ref, w_ref, ba_ref, bx_ref, lam_ref, a_pl, b_pl, carry_ref, False, emit)
        acc = jnp.dot(y_ref[0:t_rows, :], wo_ref[...], preferred_element_type=F32)
        o_ref[...] = res_ref[...] + gt_ref[0] * acc

    @pl.when(pl.program_id(1) == 0)
    def _():
        carry_ref[...] = jnp.zeros_like(carry_ref)
        run(xc_ref, gc_ref, hc_ref, rc_ref, gtc_ref, oc_ref)

    @pl.when(pl.program_id(1) > 0)
    def _():
        run(xl_ref, gl_ref, hl_ref, rl_ref, gtl_ref, ol_ref)


def _rg_scan(xc, xl, w, ba, bx, lam, reverse, fwd_args=None):
    if reverse:
        chunk = lambda s: RG_NCH - jnp.maximum(s, 1)
    else:
        chunk = lambda s: jnp.maximum(s - 1, 0)
    ctx_spec = lambda d: pl.BlockSpec((None, CTX_LEN, d), lambda b, s: (b, 0, 0))
    lat_spec = lambda d: pl.BlockSpec((None, RG_T, d), lambda b, s: (b, chunk(s), 0))
    const = lambda shape: _single_buffered(shape, lambda b, s: (0,) * len(shape))
    in_specs = [pl.BlockSpec((RNN_TILES, CTX_LEN, LANES), lambda b, s: (0, b, 0)),
                pl.BlockSpec((RNN_TILES, RG_T, LANES), lambda b, s: (0, b * RG_NCH + chunk(s), 0)),
                const(w.shape), const(ba.shape), const(bx.shape), const(lam.shape)]
    args = [xc, xl, w, ba, bx, lam]
    scratch = [pltpu.VMEM((RNN_TILES, RG_T, LANES), F32),
               pltpu.VMEM((RNN_TILES, RG_T, LANES), F32),
               pltpu.VMEM((RNN_TILES, 8, LANES), F32)]
    if reverse:
        body, name, width, dtype = _rg_bwd_kernel, "rglru_bwd", D_RNN, BF16
    else:
        gate_c, gate_l, hb_c, hb_l, res_c, res_l, mods, (w_out, out_layer) = fwd_args
        in_specs += [ctx_spec(D_RNN), lat_spec(D_RNN), ctx_spec(D_RNN), lat_spec(D_RNN),
                     ctx_spec(D_MODEL), lat_spec(D_MODEL),
                     _mod_spec(2, lambda b: CTX_MOD_ROW), _mod_spec(2, lambda b: b),
                     _layer_spec(w_out, out_layer)]
        args += [gate_c, gate_l, hb_c, hb_l, res_c, res_l, mods, mods, w_out]
        scratch += [pltpu.VMEM((RG_T, D_RNN), BF16)]
        body, name, width, dtype = _rg_fwd_kernel, "rglru_fwd", D_MODEL, F32
    return pl.pallas_call(
        body,
        grid=(BATCH, 1 + RG_NCH),
        in_specs=in_specs,
        out_specs=[ctx_spec(width), lat_spec(width)],
        out_shape=[jax.ShapeDtypeStruct((BATCH, CTX_LEN, width), dtype),
                   jax.ShapeDtypeStruct((BATCH, SEQ, width), dtype)],
        scratch_shapes=scratch,
        compiler_params=_cparams("arbitrary", "arbitrary"),
        name=name,
    )(*args)


def _rg_gate_weights(wa, wx):
    def window(w, j):
        ws = _rg_window_start(j)
        win = None
        for h in range(RG_HEADS):
            c0, c1 = max(h * RG_DH, j * LANES), min((h + 1) * RG_DH, (j + 1) * LANES)
            if c0 >= c1:
                continue
            r0 = h * RG_DH - ws
            assert 0 <= r0 and r0 + RG_DH <= RG_WIN
            blk = jnp.pad(w[h, :, c0 - h * RG_DH:c1 - h * RG_DH],
                          ((r0, RG_WIN - RG_DH - r0), (c0 - j * LANES, (j + 1) * LANES - c1)))
            win = blk if win is None else win + blk
        return win

    wins = [jnp.concatenate([window(wa, j), window(wx, j)], axis=1) for j in range(RNN_TILES)]
    return (0.5 * jnp.stack(wins)).astype(BF16)


def _col_major(layer):
    return layer < DEPTH and layer % 2 == 1 and (layer // 2) % 2 == 1


def kernel(x, c, ctx, c_ctx, w_ada, b_ada, norm_mix_g, norm_ffn_g, w_in_even, w_out_even, hy_conv_w, hy_conv_b, hy_f1_w, hy_f1_b, hy_f2_w, hy_f2_b, hy_f3_w, hy_f3_b, hy_sin_freq, hy_skip, sgu_ln_g, sgu_w, sgu_b, w_in_odd, rg_conv_w, rg_conv_b, rg_wa, rg_ba, rg_wx, rg_bx, rg_lam, w_out_odd, w_ffn_in, w_ffn_out, final_norm_g):
    mods_all = _ada_mods(c, c_ctx, w_ada, b_ada)
    xs = x.reshape(BATCH * SEQ, D_MODEL)
    cs = ctx.reshape(BATCH * CTX_LEN, D_MODEL)
    lat_row_even = lambda i: i // (SEQ // EVEN_TM)
    lat_row_odd = lambda i: i // (SEQ // ODD_TM)
    lat_row_ffn = lambda i: i // (SEQ // FFN_TM)
    ctx_row = lambda i: CTX_MOD_ROW
    n_tiles = W_A // LANES
    w_in_even, w_out_even, w_in_odd, w_out_odd, w_ffn_in, w_ffn_out = (
        w.astype(BF16) for w in (w_in_even, w_out_even, w_in_odd, w_out_odd, w_ffn_in, w_ffn_out))
    for l in range(DEPTH):
        run_ctx = l < DEPTH - 1
        is_rec = l % 2 == 1
        i = l // 2
        mods = mods_all[l].reshape(MOD_ROWS, 1, N_MOD * D_MODEL)
        g_mix = norm_mix_g[l].reshape(1, D_MODEL)
        mix_l = mix_c = None
        if is_rec:
            cw, cb = rg_conv_w[i], rg_conv_b[i].reshape(1, D_RNN)
            gate_l, xl = _in_odd(xs, mods, lat_row_odd, g_mix, w_in_odd, i, cw, cb, SEQ, RG_T)
            gate_c, xc = _in_odd(cs, mods, ctx_row, g_mix, w_in_odd, i, cw, cb, CTX_LEN, CTX_LEN)
            gate_l = gate_l.reshape(BATCH, SEQ, D_RNN)
            gate_c = gate_c.reshape(BATCH, CTX_LEN, D_RNN)
            row = lambda v: v.reshape(1, D_RNN)
            half = lambda v: 0.5 * row(v)
            hb_c, hb_l = _rg_scan(xc, xl, _rg_gate_weights(rg_wa[i, 1], rg_wx[i, 1]),
                                  half(rg_ba[i, 1]), half(rg_bx[i, 1]), row(rg_lam[i, 1]), True)
            cs_new, xs = _rg_scan(
                xc, xl, _rg_gate_weights(rg_wa[i, 0], rg_wx[i, 0]),
                half(rg_ba[i, 0]), half(rg_bx[i, 0]), row(rg_lam[i, 0]), False,
                (gate_c, gate_l, hb_c, hb_l, cs.reshape(BATCH, CTX_LEN, D_MODEL),
                 xs.reshape(BATCH, SEQ, D_MODEL), mods, (w_out_odd, i)))
            xs = xs.reshape(BATCH * SEQ, D_MODEL)
            if run_ctx:
                cs = cs_new.reshape(BATCH * CTX_LEN, D_MODEL)
        else:
            cw, cb = hy_conv_w[i], hy_conv_b[i].reshape(1, HY_COLS)
            lng = sgu_ln_g[i].reshape(1, W_B)
            sgw = sgu_w[i].astype(BF16)
            sgb = jnp.repeat(sgu_b[i].T, SGU_DH, axis=1)
            skip = hy_skip[i].reshape(1, HY_ORDER * W_A)
            fargs = (hy_f1_w[i], hy_f1_b[i], hy_f2_w[i], hy_f2_b[i], hy_f3_w[i], hy_f3_b[i],
                     hy_sin_freq[i])
            kf = _filter_spectrum(SEQ, *fargs, skip)
            za, yb = _in_even(xs, mods, lat_row_even, g_mix, w_in_even, i, cw, cb, lng, sgw, sgb,
                              SEQ, True)
            za = za.reshape(HY_TILES, BATCH, SEQ, LANES)
            y1 = _longconv(za, 0, za, n_tiles, kf, 0)
            ya = _longconv(y1, 0, za, 2 * n_tiles, kf, n_tiles)
            mix_l = (ya, yb, (w_out_even, i), True)
            if run_ctx:
                kfc = _filter_spectrum(CTX_LEN, *fargs, skip)
                zc, ybc = _in_even(cs, mods, ctx_row, g_mix, w_in_even, i, cw, cb, lng, sgw, sgb,
                                   CTX_LEN, False)
                zc = zc.reshape(HY_TILES, BATCH, CTX_LEN, LANES)
                y1c = _ctx_conv(zc, 0, zc, n_tiles, kfc, 0)
                yac = _ctx_conv(y1c, 0, zc, 2 * n_tiles, kfc, n_tiles)
                mix_c = (yac, ybc, (w_out_even, i), False)
        g_ffn = norm_ffn_g[l].reshape(1, D_MODEL)
        fg = final_norm_g.reshape(1, D_MODEL)
        swap = _col_major(l) != _col_major(l + 1)
        xs = _ffn(xs, mods, lat_row_ffn, g_ffn, w_ffn_in, w_ffn_out, l, fg, l == DEPTH - 1,
                  mix_l, swap)
        if run_ctx:
            cs = _ffn(cs, mods, ctx_row, g_ffn, w_ffn_in, w_ffn_out, l, fg, False, mix_c)
    return xs.reshape(BATCH, SEQ, D_MODEL)
```

```python
import functools
import math

import numpy as np
import jax
import jax.numpy as jnp
from jax import lax
from jax.experimental import pallas as pl
from jax.experimental.pallas import tpu as pltpu

F32 = jnp.float32
BF16 = jnp.bfloat16

D_MODEL = 1024
BATCH = 4
SEQ = 4096
DEPTH = 4
GRID_W = 64
CTX_LEN = 256
N_MOD = 6
NORM_EPS = 1e-6
W_A = D_MODEL // 2
HY_ORDER = 2
HY_SHORT = 3
HY_BANDS = 16
HY_TARGET = 1e-2
HY_FAST_PCT = 0.3
HY_SLOW_PCT = 1.5
W_B = D_MODEL // 2
SGU_GROUPS = 4
SGU_DH = W_B // SGU_GROUPS
CHUNK = 128
D_RNN = ((4 * D_MODEL // 3 + 127) // 128) * 128
RG_HEADS = 16
RG_DH = D_RNN // RG_HEADS
RG_CONV = 4
RG_C = 8.0
D_FF = ((8 * D_MODEL // 3 + 255) // 256) * 256

VMEM_BYTES_V7X = 64 * 1024 * 1024
VMEM_LIMIT = VMEM_BYTES_V7X - 4 * 1024 * 1024
LANES = 128

FFT_N = 2 * SEQ
FFT_N1 = 64
FFT_N2 = 128
FFT_HALF_N1 = FFT_N1 // 2
FFT_PITCH = FFT_N2 + 8
CTX_N = 2 * CTX_LEN


def _cparams(*sem):
    return pltpu.CompilerParams(dimension_semantics=sem, vmem_limit_bytes=VMEM_LIMIT)


def _single_buffered(block_shape, index_map):
    return pl.BlockSpec(block_shape, index_map, pipeline_mode=pl.Buffered(1))


def _layer_spec(stack, layer):
    rest = stack.shape[1:]
    return _single_buffered((None,) + rest, lambda *_: (layer,) + (0,) * len(rest))


@functools.lru_cache(maxsize=None)
def _dft_tables_f32():
    n = FFT_N
    k1 = np.arange(FFT_N1)
    n2 = np.arange(FFT_N2)

    def angle(n1):
        m = (FFT_N2 * n1[None, None, :] + n2[:, None, None]) * k1[None, :, None]
        return 2.0 * np.pi * (m % n) / n

    a = angle(np.arange(FFT_HALF_N1))
    c, s = np.cos(a), np.sin(a)
    g = np.empty((FFT_N2, FFT_N1, 2, 2, FFT_HALF_N1))
    g[:, :, 0, 0], g[:, :, 0, 1] = c, s
    g[:, :, 1, 0], g[:, :, 1, 1] = -s, c
    g = g.reshape(FFT_N2, 2 * FFT_N1, 2 * FFT_HALF_N1)

    a = angle(np.arange(FFT_N1))
    gf = np.stack([np.cos(a), -np.sin(a)], axis=2).reshape(FFT_N2, 2 * FFT_N1, FFT_N1)

    a = np.transpose(angle(np.arange(FFT_HALF_N1)), (0, 2, 1))
    c, s = np.cos(a) / n, np.sin(a) / n
    h = np.empty((FFT_N2, 2, FFT_HALF_N1, FFT_N1, 2))
    h[:, 0, :, :, 0], h[:, 0, :, :, 1] = c, -s
    h[:, 1, :, :, 0], h[:, 1, :, :, 1] = s, c
    h = h.reshape(FFT_N2, 2 * FFT_HALF_N1, 2 * FFT_N1)

    a = 2.0 * np.pi * np.outer(n2, n2) / FFT_N2
    c, s = np.cos(a), np.sin(a)
    f2 = np.block([[c, s], [-s, c]])
    f2i = np.block([[c, -s], [s, c]])

    kk = np.arange(CTX_N)
    a = 2.0 * np.pi * np.outer(kk, np.arange(CTX_LEN)) / CTX_N
    c, s = np.cos(a), np.sin(a)
    fc = np.block([[c, s], [-s, c]])
    a = 2.0 * np.pi * np.outer(kk, np.arange(CTX_N)) / CTX_N
    fcf = np.concatenate([np.cos(a), -np.sin(a)], axis=0)
    a = 2.0 * np.pi * np.outer(np.arange(CTX_LEN), kk) / CTX_N
    c, s = np.cos(a) / CTX_N, np.sin(a) / CTX_N
    fci = np.block([[c, -s], [s, c]])

    tables = dict(g=g, gf=gf, h=h, f2=f2, f2i=f2i, fc=fc, fcf=fcf, fci=fci)
    return {k: np.asarray(v, dtype=np.float32) for k, v in tables.items()}


def _dft_tables():
    return {k: jnp.asarray(v).astype(BF16) for k, v in _dft_tables_f32().items()}


SEQ_TILE = 1024
N1_PER_TILE = SEQ_TILE // FFT_N2


def _scatter_n2(s_ref, n2, val):
    for p in range(s_ref.shape[0]):
        s_ref[p, pl.ds(n2, 2 * FFT_N1, stride=FFT_PITCH), :] = val[:, p * LANES:(p + 1) * LANES]


def _gather_n2(s_ref, n2):
    return jnp.concatenate(
        [s_ref[p, pl.ds(n2, 2 * FFT_N1, stride=FFT_PITCH), :] for p in range(s_ref.shape[0])],
        axis=1)


def _spectrum_rows(s_ref, k1):
    base = pl.multiple_of(k1 * (2 * FFT_PITCH), 8)
    return base, jnp.concatenate(
        [jnp.concatenate([s_ref[p, pl.ds(base + o, FFT_N2), :] for p in range(s_ref.shape[0])],
                         axis=1) for o in (0, FFT_PITCH)], axis=0)


HY_FFN = 64
FILT_ROWS = 1024
FILT_TILES = 2
HIGHEST = lax.Precision.HIGHEST


def _tap_rows(row, seq, n2_major):
    if n2_major:
        n2, n1 = row >> int(math.log2(FFT_N1)), row & (FFT_N1 - 1)
        bwd = n1 >= FFT_HALF_N1
        m = FFT_N2 * (n1 - FFT_HALF_N1) + n2
        fwd_pos = FFT_N2 * n1 + n2
    else:
        bwd = row >= seq
        m = row - seq
        fwd_pos = row
    pos = jnp.where(bwd, seq - m, fwd_pos).astype(F32)
    bwd_f = jnp.where(bwd, 1.0, 0.0)
    keep_f = jnp.where(bwd & (m == 0), 0.0, 1.0)
    return pos, bwd_f, keep_f


def _filter_kernel(ec_ref, f1w_ref, f1b_ref, f2w_ref, f2b_ref, sf_ref, w3f_ref, w3b_ref, b3f_ref,
                   b3b_ref, dl_ref, skip_ref, *rest, seq, n2_major):
    if n2_major:
        gf_ref, f2_ref, o_ref, hid_ref, k_ref, s_ref = rest
    else:
        fcf_ref, o_ref, hid_ref, k_ref = rest
    n_rows = 2 * seq
    rb = min(FILT_ROWS, n_rows)

    def rows_of(i):
        r0 = pl.multiple_of(i * rb, rb)
        return r0, _tap_rows(r0 + lax.broadcasted_iota(jnp.int32, (rb, 1), 0), seq, n2_major)

    @pl.when(pl.program_id(0) == 0)
    def _():
        sf = sf_ref[...]

        def hidden(i, carry):
            r0, (pos, _, _) = rows_of(i)
            t = pos * (1.0 / (seq - 1))
            w = pos * (2.0 * math.pi / seq)
            emb = jnp.sin(w * ec_ref[0:1] + ec_ref[1:2]) + t * ec_ref[2:3]
            h = jnp.sin(sf * (jnp.dot(emb, f1w_ref[...], precision=HIGHEST,
                                      preferred_element_type=F32) + f1b_ref[...]))
            h = jnp.sin(sf * (jnp.dot(h.astype(BF16), f2w_ref[...].astype(BF16),
                                      preferred_element_type=F32) + f2b_ref[...]))
            hid_ref[pl.ds(r0, rb), :] = h
            return carry

        lax.fori_loop(0, n_rows // rb, hidden, 0)

    def taps(i, ssq):
        r0, (pos, bwd_f, keep_f) = rows_of(i)
        h = hid_ref[pl.ds(r0, rb), :].astype(BF16)
        neg_t = -(pos * (1.0 / (seq - 1)))
        ks = []
        for c in range(FILT_TILES):
            kf = jnp.dot(h, w3f_ref[c].astype(BF16), preferred_element_type=F32) + b3f_ref[c]
            kb = jnp.dot(h, w3b_ref[c].astype(BF16), preferred_element_type=F32) + b3b_ref[c]
            ks.append((kf + bwd_f * (kb - kf)) * jnp.exp(neg_t * dl_ref[c]) * keep_f)
        k = jnp.concatenate(ks, axis=1)
        k_ref[pl.ds(r0, rb), :] = k
        return ssq + jnp.sum(k * k, axis=0, keepdims=True)

    ssq = lax.fori_loop(0, n_rows // rb, taps, jnp.zeros((1, FILT_TILES * LANES), F32))
    scale = lax.rsqrt(ssq + NORM_EPS)
    skip = skip_ref[...]

    if n2_major:
        def stage1(n2, carry):
            x = k_ref[pl.ds(pl.multiple_of(n2 * FFT_N1, FFT_N1), FFT_N1), :] * scale
            _scatter_n2(s_ref, n2, jnp.dot(gf_ref[n2], x.astype(BF16),
                                           preferred_element_type=F32))
            return carry

        lax.fori_loop(0, FFT_N2, stage1, 0, unroll=8)

        def stage2(k1, carry):
            _, r = _spectrum_rows(s_ref, k1)
            z = jnp.dot(f2_ref[...], r.astype(BF16), preferred_element_type=F32)
            row = pl.multiple_of(k1 * (2 * FFT_N2), 2 * FFT_N2)
            o_ref[pl.ds(row, FFT_N2), :] = (z[:FFT_N2] + skip).astype(BF16)
            o_ref[pl.ds(row + FFT_N2, FFT_N2), :] = z[FFT_N2:].astype(BF16)
            return carry

        lax.fori_loop(0, FFT_N1, stage2, 0, unroll=8)
    else:
        z = jnp.dot(fcf_ref[...], (k_ref[...] * scale).astype(BF16), preferred_element_type=F32)
        o_ref[:n_rows, :] = (z[:n_rows] + skip).astype(BF16)
        o_ref[n_rows:, :] = z[n_rows:].astype(BF16)


def _filter_spectrum(seq, f1_w, f1_b, f2_w, f2_b, f3_w, f3_b, sin_freq, skip):
    n2_major = seq == SEQ
    t = _dft_tables()
    pad = lambda a, r, c: jnp.pad(a, ((0, r - a.shape[0]), (0, c - a.shape[1])))
    f = jnp.linspace(1e-4, HY_BANDS - 1, HY_BANDS, dtype=F32)
    zeros = lambda n: jnp.zeros((n,), F32)
    ones = lambda n: jnp.ones((n,), F32)
    rest = LANES - 1 - 2 * HY_BANDS
    ec = jnp.stack([
        jnp.concatenate([zeros(1), f, f, zeros(rest)]),
        jnp.concatenate([zeros(1), (math.pi / 2) * ones(HY_BANDS), math.pi * ones(HY_BANDS),
                         zeros(rest)]),
        jnp.concatenate([ones(1), zeros(LANES - 1)]),
    ] + [zeros(LANES)] * 5)
    row = lambda v: pad(v.reshape(1, -1), 1, LANES)
    n_tiles = HY_ORDER * W_A // LANES
    w3 = f3_w.reshape(HY_FFN, HY_ORDER, 2, W_A)
    b3 = f3_b.reshape(HY_ORDER, 2, W_A)
    tiles_w = lambda d: jnp.pad(
        w3[:, :, d, :].reshape(HY_FFN, n_tiles, LANES).transpose(1, 0, 2),
        ((0, 0), (0, LANES - HY_FFN), (0, 0)))
    tiles_b = lambda d: b3[:, d, :].reshape(n_tiles, 1, LANES)
    deltas = jnp.abs(jnp.linspace(math.log(HY_TARGET) / HY_SLOW_PCT,
                                  math.log(HY_TARGET) / HY_FAST_PCT, W_A, dtype=F32))
    dl = jnp.tile(deltas, HY_ORDER).reshape(n_tiles, 1, LANES)
    const = lambda shape: _single_buffered(shape, lambda j: (0,) * len(shape))
    tile = lambda shape: pl.BlockSpec((FILT_TILES,) + shape, lambda j: (j, 0, 0))
    ct = FILT_TILES * LANES
    args = [ec, pad(f1_w, LANES, LANES), row(f1_b), pad(f2_w, LANES, LANES), row(f2_b),
            row(sin_freq), tiles_w(0), tiles_w(1), tiles_b(0), tiles_b(1), dl, skip]
    in_specs = [const((8, LANES)), const((LANES, LANES)), const((1, LANES)),
                const((LANES, LANES)), const((1, LANES)), const((1, LANES)),
                tile((LANES, LANES)), tile((LANES, LANES)), tile((1, LANES)), tile((1, LANES)),
                tile((1, LANES)), pl.BlockSpec((1, ct), lambda j: (0, j))]
    scratch = [pltpu.VMEM((2 * seq, LANES), F32), pltpu.VMEM((2 * seq, ct), F32)]
    if n2_major:
        args += [t["gf"], t["f2"]]
        in_specs += [const(t["gf"].shape), const(t["f2"].shape)]
        scratch += [pltpu.VMEM((FILT_TILES, 2 * FFT_N1 * FFT_PITCH, LANES), F32)]
    else:
        args += [t["fcf"]]
        in_specs += [const(t["fcf"].shape)]
    return pl.pallas_call(
        functools.partial(_filter_kernel, seq=seq, n2_major=n2_major),
        grid=(n_tiles // FILT_TILES,),
        in_specs=in_specs,
        out_specs=pl.BlockSpec((4 * seq, ct), lambda j: (0, j)),
        out_shape=jax.ShapeDtypeStruct((4 * seq, HY_ORDER * W_A), BF16),
        scratch_shapes=scratch,
        compiler_params=_cparams("arbitrary"),
        name="hyena_filter" if n2_major else "hyena_ctx_filter",
    )(*args)


LC_OUT_CHUNKS = 4
LC_N2_PER_CHUNK = FFT_N2 // LC_OUT_CHUNKS


def _longconv_kernel(v_ref, gate_ref, kf_ref, g_ref, h_ref, f2_ref, f2i_ref, o_ref, s_ref, *,
                     flat_in, flat_out):
    ct = v_ref.shape[2]
    t = pl.program_id(1)

    @pl.when(t == 0)
    def _():
        def stage1(n2, carry):
            if flat_in:
                r0 = pl.multiple_of(n2 * FFT_HALF_N1, FFT_HALF_N1)
                blk = [v_ref[b, pl.ds(r0, FFT_HALF_N1), :] for b in range(BATCH)]
            else:
                r0 = pl.multiple_of(n2 * N1_PER_TILE, N1_PER_TILE)
                blk = [jnp.concatenate([v_ref[b, pl.ds(i * SEQ_TILE + r0, N1_PER_TILE), :]
                                        for i in range(SEQ // SEQ_TILE)], axis=0)
                       for b in range(BATCH)]
            x = jnp.concatenate([jnp.concatenate([blk[0], blk[2]], axis=1),
                                 jnp.concatenate([blk[1], blk[3]], axis=1)], axis=0)
            _scatter_n2(s_ref, n2, jnp.dot(g_ref[n2], x.astype(BF16), preferred_element_type=F32))
            return carry

        lax.fori_loop(0, FFT_N2, stage1, 0, unroll=16)

        def stage2(k1, carry):
            base, r = _spectrum_rows(s_ref, k1)
            z = jnp.dot(f2_ref[...], r.astype(BF16), preferred_element_type=F32)
            kf = kf_ref[pl.ds(pl.multiple_of(k1 * (2 * FFT_N2), 2 * FFT_N2), 2 * FFT_N2), :]
            kf = kf.astype(F32)
            kr = jnp.concatenate([kf[:FFT_N2]] * 2, axis=1)
            ki = jnp.concatenate([kf[FFT_N2:]] * 2, axis=1)
            zr, zi = z[:FFT_N2], z[FFT_N2:]
            p = jnp.concatenate([zr * kr - zi * ki, zr * ki + zi * kr], axis=0)
            q = jnp.dot(f2i_ref[...], p.astype(BF16), preferred_element_type=F32)
            for p in range(2):
                s_ref[p, pl.ds(base, FFT_N2), :] = q[:FFT_N2, p * ct:(p + 1) * ct]
                s_ref[p, pl.ds(base + FFT_PITCH, FFT_N2), :] = q[FFT_N2:, p * ct:(p + 1) * ct]
            return carry

        lax.fori_loop(0, FFT_N1, stage2, 0, unroll=8)

    def stage3(j, carry):
        n2 = t * LC_N2_PER_CHUNK + j
        rq = _gather_n2(s_ref, n2)
        y = jnp.dot(h_ref[n2], rq.astype(BF16), preferred_element_type=F32)
        r0 = pl.multiple_of(j * N1_PER_TILE, N1_PER_TILE)
        for b in range(BATCH):
            ri, pair = b % 2, b // 2
            gated = [gate_ref[b, i, pl.ds(r0, N1_PER_TILE), :]
                     * y[ri * FFT_HALF_N1 + i * N1_PER_TILE:ri * FFT_HALF_N1 + (i + 1) * N1_PER_TILE,
                         pair * ct:(pair + 1) * ct]
                     for i in range(SEQ // SEQ_TILE)]
            if flat_out:
                o_ref[b, pl.ds(pl.multiple_of(j * FFT_HALF_N1, FFT_HALF_N1), FFT_HALF_N1), :] = (
                    jnp.concatenate(gated, axis=0).astype(o_ref.dtype))
            else:
                for i in range(SEQ // SEQ_TILE):
                    o_ref[b, i, pl.ds(r0, N1_PER_TILE), :] = gated[i]
        return carry

    lax.fori_loop(0, LC_N2_PER_CHUNK, stage3, 0, unroll=16)


def _longconv(v, v_tile, gate, gate_tile, kf, kf_col, flat_in=False, flat_out=False):
    t = _dft_tables()
    ct = LANES
    n_seq_tiles = SEQ // SEQ_TILE
    rows = SEQ_TILE // LC_OUT_CHUNKS
    gate = gate.reshape(gate.shape[0], BATCH, n_seq_tiles, SEQ_TILE, ct)
    v_index = lambda j, i: (v_tile + j, 0, 0, 0)
    if flat_in:
        v_spec = pl.BlockSpec((None, BATCH, SEQ, ct), v_index)
    else:
        v_spec = _single_buffered((None, BATCH, SEQ, ct), v_index)
    if flat_out:
        out_spec = pl.BlockSpec((None, BATCH, SEQ // LC_OUT_CHUNKS, ct), lambda j, i: (j, 0, i, 0))
        out_shape = jax.ShapeDtypeStruct((W_A // ct, BATCH, SEQ, ct), BF16)
    else:
        out_spec = pl.BlockSpec((None, BATCH, n_seq_tiles, rows, ct),
                                lambda j, i: (j, 0, 0, i, 0))
        out_shape = jax.ShapeDtypeStruct((W_A // ct, BATCH, n_seq_tiles, SEQ_TILE, ct), F32)
    out = pl.pallas_call(
        functools.partial(_longconv_kernel, flat_in=flat_in, flat_out=flat_out),
        grid=(W_A // ct, LC_OUT_CHUNKS),
        in_specs=[
            v_spec,
            pl.BlockSpec((None, BATCH, n_seq_tiles, rows, ct),
                         lambda j, i: (gate_tile + j, 0, 0, i, 0)),
            pl.BlockSpec((2 * FFT_N, ct), lambda j, i: (0, kf_col + j)),
            _single_buffered(t["g"].shape, lambda j, i: (0, 0, 0)),
            _single_buffered(t["h"].shape, lambda j, i: (0, 0, 0)),
            _single_buffered(t["f2"].shape, lambda j, i: (0, 0)),
            _single_buffered(t["f2i"].shape, lambda j, i: (0, 0)),
        ],
        out_specs=out_spec,
        out_shape=out_shape,
        scratch_shapes=[pltpu.VMEM((2, 2 * FFT_N1 * FFT_PITCH, ct), F32)],
        compiler_params=_cparams("arbitrary", "arbitrary"),
        name="hyena_longconv",
    )(v, gate, kf, t["g"], t["h"], t["f2"], t["f2i"])
    return out.reshape(W_A // ct, BATCH, SEQ, ct)


def _ctx_conv_kernel(v_ref, gate_ref, kf_ref, fc_ref, fci_ref, o_ref):
    ct = v_ref.shape[2]
    x = jnp.concatenate([jnp.concatenate([v_ref[0], v_ref[2]], axis=1),
                         jnp.concatenate([v_ref[1], v_ref[3]], axis=1)], axis=0)
    z = jnp.dot(fc_ref[...], x.astype(BF16), preferred_element_type=F32)
    kf = kf_ref[...].astype(F32)
    kr = jnp.concatenate([kf[:CTX_N]] * 2, axis=1)
    ki = jnp.concatenate([kf[CTX_N:]] * 2, axis=1)
    zr, zi = z[:CTX_N], z[CTX_N:]
    p = jnp.concatenate([zr * kr - zi * ki, zr * ki + zi * kr], axis=0)
    y = jnp.dot(fci_ref[...], p.astype(BF16), preferred_element_type=F32)
    for b in range(BATCH):
        ri, pair = b % 2, b // 2
        o_ref[b] = gate_ref[b] * y[ri * CTX_LEN:(ri + 1) * CTX_LEN, pair * ct:(pair + 1) * ct]


def _ctx_conv(v, v_tile, gate, gate_tile, kf, kf_col):
    t = _dft_tables()
    ct = LANES
    return pl.pallas_call(
        _ctx_conv_kernel,
        grid=(W_A // ct,),
        in_specs=[
            pl.BlockSpec((None, BATCH, CTX_LEN, ct), lambda j: (v_tile + j, 0, 0, 0)),
            pl.BlockSpec((None, BATCH, CTX_LEN, ct), lambda j: (gate_tile + j, 0, 0, 0)),
            pl.BlockSpec((2 * CTX_N, ct), lambda j: (0, kf_col + j)),
            pl.BlockSpec(t["fc"].shape, lambda j: (0, 0)),
            pl.BlockSpec(t["fci"].shape, lambda j: (0, 0)),
        ],
        out_specs=pl.BlockSpec((None, BATCH, CTX_LEN, ct), lambda j: (j, 0, 0, 0)),
        out_shape=jax.ShapeDtypeStruct((W_A // ct, BATCH, CTX_LEN, ct), F32),
        compiler_params=_cparams("arbitrary"),
        name="hyena_ctx_conv",
    )(v, gate, kf, t["fc"], t["fci"])


MOD_ROWS = 8
CTX_MOD_ROW = BATCH


def _ada_kernel(c_ref, w_ref, b_ref, o_ref):
    cv = c_ref[...]
    s = cv * jax.nn.sigmoid(cv)
    o_ref[0] = jnp.dot(s.astype(BF16), w_ref[0].astype(BF16),
                       preferred_element_type=F32) + b_ref[0]


def _ada_mods(c, c_ctx, w_ada, b_ada):
    cv = jnp.concatenate(
        [c, c_ctx[None], jnp.zeros((MOD_ROWS - BATCH - 1, D_MODEL), F32)], axis=0)
    n = N_MOD * D_MODEL
    tn = n // 4
    return pl.pallas_call(
        _ada_kernel,
        grid=(DEPTH, n // tn),
        in_specs=[
            pl.BlockSpec((MOD_ROWS, D_MODEL), lambda l, j: (0, 0)),
            pl.BlockSpec((1, D_MODEL, tn), lambda l, j: (l, 0, j)),
            pl.BlockSpec((1, 1, tn), lambda l, j: (l, 0, j)),
        ],
        out_specs=pl.BlockSpec((1, MOD_ROWS, tn), lambda l, j: (l, 0, j)),
        out_shape=jax.ShapeDtypeStruct((DEPTH, MOD_ROWS, n), F32),
        compiler_params=_cparams("arbitrary", "arbitrary"),
        name="ada_mods",
    )(cv, w_ada, b_ada.reshape(DEPTH, 1, n))


def _mod_spec(m, row_fn):
    return pl.BlockSpec((1, 1, D_MODEL), lambda i, *_: (row_fn(i), 0, m))


def _rms_mod(x, g, shift, scale):
    ms = jnp.mean(x * x, axis=-1, keepdims=True)
    return (x * lax.rsqrt(ms + NORM_EPS)) * (g * (1.0 + scale)) + shift


GELU_C = math.sqrt(2.0 / math.pi)


def _gelu(x):
    hx = 0.5 * x
    return hx + hx * jnp.tanh(x * (GELU_C + (GELU_C * 0.044715) * (x * x)))


HALO = 8


def _dwconv(z_all, tm, w, b, left, valid):
    n = z_all.shape[0]
    y = b + w[left:left + 1] * z_all[:tm]
    for k in range(w.shape[0]):
        d = k - left
        if d == 0:
            continue
        s = pltpu.roll(z_all, (-d) % n, 0)[:tm]
        if valid is not None:
            s = s * valid(d)
        y = y + s * w[k:k + 1]
    return y


def _halo_specs(tm, n_rows):
    per = tm // HALO
    last = n_rows // HALO - 1
    return [
        pl.BlockSpec((tm, D_MODEL), lambda i: (i, 0)),
        pl.BlockSpec((HALO, D_MODEL), lambda i: (jnp.maximum(i * per - 1, 0), 0)),
        pl.BlockSpec((HALO, D_MODEL), lambda i: (jnp.minimum((i + 1) * per, last), 0)),
    ]


def _normed_tile(x_ref, xp_ref, xn_ref, sh_ref, sc_ref, g_ref, seq_len):
    tm = x_ref.shape[0]
    g, shift, scale = g_ref[...], sh_ref[0], sc_ref[0]
    h0 = _rms_mod(x_ref[:tm // 2, :], g, shift, scale)
    h = _rms_mod(x_ref[tm // 2:, :], g, shift, scale)
    hn = _rms_mod(xn_ref[...], g, shift, scale)
    hp = _rms_mod(xp_ref[...], g, shift, scale)
    if seq_len >= tm:
        r0 = pl.program_id(0) * tm
        hp = hp * jnp.where((r0 & (seq_len - 1)) == 0, 0.0, 1.0)
        hn = hn * jnp.where(((r0 + tm) & (seq_len - 1)) == 0, 0.0, 1.0)
        valid = None
    else:
        hp, hn = jnp.zeros_like(hp), jnp.zeros_like(hn)
        pos = lax.broadcasted_iota(jnp.int32, (tm, 1), 0) & (seq_len - 1)
        valid = lambda d: jnp.where((pos + d >= 0) & (pos + d < seq_len), 1.0, 0.0)
    return [h0.astype(BF16), jnp.concatenate([h, hn, hp], axis=0).astype(BF16)], valid


def _dot_rows(pieces, w):
    return jnp.concatenate([jnp.dot(p, w, preferred_element_type=F32) for p in pieces], axis=0)


EVEN_TM = SEQ_TILE
HY_COLS = 3 * W_A
HY_TILES = HY_COLS // LANES


def _in_even_kernel(x_ref, xp_ref, xn_ref, sh_ref, sc_ref, g_ref, w_ref, cw_ref, cb_ref,
                    lng_ref, sgw_ref, sgb_ref, za_ref, yb_ref, *, seq_len, n2_major):
    tm = x_ref.shape[0]
    pieces, valid = _normed_tile(x_ref, xp_ref, xn_ref, sh_ref, sc_ref, g_ref, seq_len)
    h = jnp.concatenate([pieces[0], pieces[1][:tm // 2]], axis=0)
    cw = 4 * LANES
    for cc in range(HY_COLS // cw):
        cols = slice(cc * cw, (cc + 1) * cw)
        z_all = _dot_rows(pieces, w_ref[:, cols])
        y = _dwconv(z_all, tm, cw_ref[:, cols], cb_ref[:, cols], 1, valid)
        for c in range(cw // LANES):
            tile = cc * (cw // LANES) + c
            yc = y[:, c * LANES:(c + 1) * LANES]
            if n2_major:
                for j in range(N1_PER_TILE):
                    za_ref[tile, pl.ds(j, FFT_N2, stride=N1_PER_TILE), :] = (
                        yc[j * FFT_N2:(j + 1) * FFT_N2])
            else:
                za_ref[tile] = yc
    u = _gelu(jnp.dot(h, w_ref[:, HY_COLS:HY_COLS + W_B], preferred_element_type=F32))
    vb = _gelu(jnp.dot(h, w_ref[:, HY_COLS + W_B:], preferred_element_type=F32))
    vc = vb - jnp.mean(vb, axis=-1, keepdims=True)
    vn = vc * lax.rsqrt(jnp.mean(vc * vc, axis=-1, keepdims=True) + NORM_EPS) * lng_ref[...]
    vn = vn.astype(BF16)
    n_chunks = tm // CHUNK
    for q in range(SGU_GROUPS):
        cols = slice(q * SGU_DH, (q + 1) * SGU_DH)
        rhs = jnp.concatenate(
            [vn[ch * CHUNK:(ch + 1) * CHUNK, cols] for ch in range(n_chunks)], axis=1)
        s_all = jnp.dot(sgw_ref[q], rhs, preferred_element_type=F32)
        for ch in range(n_chunks):
            rows = slice(ch * CHUNK, (ch + 1) * CHUNK)
            s = s_all[:, ch * SGU_DH:(ch + 1) * SGU_DH] + sgb_ref[:, cols]
            yb_ref[rows, cols] = (u[rows, cols] * s).astype(BF16)


def _in_even(x, mods, mod_row, g, w, layer, cw, cb, lng, sgw, sgb, seq_len, n2_major):
    n_rows = x.shape[0]
    tm = EVEN_TM
    za_shape = (HY_TILES, n_rows, LANES)
    za_spec = pl.BlockSpec((HY_TILES, tm, LANES), lambda i: (0, i, 0))
    const = lambda shape: _single_buffered(shape, lambda i: (0,) * len(shape))
    return pl.pallas_call(
        functools.partial(_in_even_kernel, seq_len=seq_len, n2_major=n2_major),
        grid=(n_rows // tm,),
        in_specs=_halo_specs(tm, n_rows) + [
            _mod_spec(0, mod_row), _mod_spec(1, mod_row),
            const((1, D_MODEL)), _layer_spec(w, layer), const(cw.shape), const(cb.shape),
            const(lng.shape), const(sgw.shape), const(sgb.shape),
        ],
        out_specs=[za_spec, pl.BlockSpec((tm, W_B), lambda i: (i, 0))],
        out_shape=[jax.ShapeDtypeStruct(za_shape, F32),
                   jax.ShapeDtypeStruct((n_rows, W_B), BF16)],
        compiler_params=_cparams("arbitrary"),
        name="in_proj_even",
    )(x, x, x, mods, mods, g, w, cw, cb, lng, sgw, sgb)


FFN_TM = SEQ_TILE
FFN_SUB = 256
GRID_H = SEQ // GRID_W
GT_ROWS = FFN_TM // GRID_W
GT_PITCH = GRID_W + 8


def _even_mix(ya_ref, yb_ref, w_ref, n2_major):
    parts = []
    for c in range(W_A // LANES):
        if n2_major:
            yac = jnp.concatenate(
                [ya_ref[c, pl.ds(j, FFT_N2, stride=N1_PER_TILE), :]
                 for j in range(N1_PER_TILE)], axis=0)
        else:
            yac = ya_ref[c]
        parts.append(yac.astype(BF16))
    y = jnp.concatenate(parts + [yb_ref[...]], axis=1)
    return jnp.dot(y, w_ref[...], preferred_element_type=F32)


def _ffn_kernel(*refs, final_norm, mix, n2_major, swap_grid):
    refs = list(refs)
    x_ref, sh_ref, sc_ref, gt_ref, g_ref, w_in_ref, wo_ref, fg_ref = refs[:8]
    del refs[:8]
    if mix:
        ya_ref, yb_ref, gm_ref, wm_ref = refs[:4]
        del refs[:4]
    o_ref = refs.pop(0)
    x = x_ref[...]
    if mix:
        x = x + gm_ref[0] * _even_mix(ya_ref, yb_ref, wm_ref, n2_major)
    h = _rms_mod(x, g_ref[...], sh_ref[0], sc_ref[0]).astype(BF16)
    acts = []
    for c0 in range(0, D_FF, FFN_SUB):
        c1 = min(c0 + FFN_SUB, D_FF)
        hz = 0.5 * jnp.dot(h, w_in_ref[:, c0:c1], preferred_element_type=F32)
        z2 = jnp.dot(h, w_in_ref[:, D_FF + c0:D_FF + c1], preferred_element_type=F32)
        acts.append(((hz + hz * jnp.tanh(hz)) * z2).astype(BF16))
    y = x + gt_ref[0] * jnp.dot(jnp.concatenate(acts, axis=1), wo_ref[...],
                                preferred_element_type=F32)
    if final_norm:
        y = y * lax.rsqrt(jnp.mean(y * y, axis=-1, keepdims=True) + NORM_EPS) * fg_ref[...]
    if not swap_grid:
        o_ref[...] = y
        return
    s_ref = refs.pop(0)
    nt = D_MODEL // LANES
    for t in range(nt):
        for j in range(GT_ROWS):
            s_ref[t, j * GT_PITCH:j * GT_PITCH + GRID_W, :] = (
                y[j * GRID_W:(j + 1) * GRID_W, t * LANES:(t + 1) * LANES])
    for c in range(GRID_W):
        o_ref[c] = jnp.concatenate(
            [s_ref[t, pl.ds(c, GT_ROWS, stride=GT_PITCH), :] for t in range(nt)], axis=1)


def _ffn(x, mods, mod_row, g, w_in, w_out, layer, final_g, final_norm, mix=None,
         swap_grid=False):
    n_rows = x.shape[0]
    tm = FFN_TM
    const = lambda shape: _single_buffered(shape, lambda i: (0,) * len(shape))
    in_specs = [
        pl.BlockSpec((tm, D_MODEL), lambda i: (i, 0)),
        _mod_spec(3, mod_row), _mod_spec(4, mod_row), _mod_spec(5, mod_row),
        const((1, D_MODEL)), _layer_spec(w_in, layer), _layer_spec(w_out, layer),
        const((1, D_MODEL)),
    ]
    args = [x, mods, mods, mods, g, w_in, w_out, final_g]
    n2_major = False
    if mix is not None:
        ya, yb, (w_mix, mix_layer), n2_major = mix
        nt = W_A // LANES
        in_specs += [pl.BlockSpec((nt, tm, LANES), lambda i: (0, i, 0)),
                     pl.BlockSpec((tm, W_B), lambda i: (i, 0)),
                     _mod_spec(2, mod_row), _layer_spec(w_mix, mix_layer)]
        args += [ya.reshape(nt, n_rows, LANES), yb, mods, w_mix]
    scratch = []
    if swap_grid:
        per_seq = SEQ // tm
        out_spec = pl.BlockSpec((None, GRID_W, GT_ROWS, D_MODEL),
                                lambda i: (i // per_seq, 0, i % per_seq, 0))
        out_shape = jax.ShapeDtypeStruct((n_rows // SEQ, GRID_W, GRID_H, D_MODEL), F32)
        scratch = [pltpu.VMEM((D_MODEL // LANES, GT_ROWS * GT_PITCH, LANES), F32)]
    else:
        out_spec = pl.BlockSpec((tm, D_MODEL), lambda i: (i, 0))
        out_shape = jax.ShapeDtypeStruct((n_rows, D_MODEL), F32)
    out = pl.pallas_call(
        functools.partial(_ffn_kernel, final_norm=final_norm, mix=mix is not None,
                          n2_major=n2_major, swap_grid=swap_grid),
        grid=(n_rows // tm,),
        in_specs=in_specs,
        out_specs=out_spec,
        out_shape=out_shape,
        scratch_shapes=scratch,
        compiler_params=_cparams("arbitrary"),
        name="ffn",
    )(*args)
    return out.reshape(n_rows, D_MODEL)


ODD_TM = 1024
RNN_TILES = D_RNN // LANES


def _in_odd_kernel(x_ref, xp_ref, xn_ref, sh_ref, sc_ref, g_ref, w_ref, cw_ref, cb_ref,
                   gate_ref, xl_ref, *, seq_len, chunk):
    tm = x_ref.shape[0]
    pieces, valid = _normed_tile(x_ref, xp_ref, xn_ref, sh_ref, sc_ref, g_ref, seq_len)
    half = tm // 2
    gate_ref[:half, :] = _gelu(jnp.dot(pieces[0], w_ref[:, :D_RNN],
                                       preferred_element_type=F32)).astype(BF16)
    gate_ref[half:, :] = _gelu(jnp.dot(pieces[1][:half], w_ref[:, :D_RNN],
                                       preferred_element_type=F32)).astype(BF16)
    cw = 4 * LANES
    for c0 in range(0, D_RNN, cw):
        cols = slice(c0, min(c0 + cw, D_RNN))
        wcols = slice(D_RNN + cols.start, D_RNN + cols.stop)
        z_all = _dot_rows(pieces, w_ref[:, wcols])
        y = _dwconv(z_all, tm, cw_ref[:, cols], cb_ref[:, cols], 2, valid)
        kg = chunk // RG_GROUPS
        for c in range((cols.stop - cols.start) // LANES):
            yc = y[:, c * LANES:(c + 1) * LANES]
            for r0 in range(0, tm, chunk):
                for gq in range(RG_GROUPS):
                    xl_ref[c0 // LANES + c, pl.ds(r0 + gq, kg, stride=RG_GROUPS), :] = (
                        yc[r0 + gq * kg:r0 + (gq + 1) * kg])


def _in_odd(x, mods, mod_row, g, w, layer, cw, cb, seq_len, chunk):
    n_rows = x.shape[0]
    tm = ODD_TM
    const = lambda shape: _single_buffered(shape, lambda i: (0,) * len(shape))
    return pl.pallas_call(
        functools.partial(_in_odd_kernel, seq_len=seq_len, chunk=chunk),
        grid=(n_rows // tm,),
        in_specs=_halo_specs(tm, n_rows) + [
            _mod_spec(0, mod_row), _mod_spec(1, mod_row),
            const((1, D_MODEL)), _layer_spec(w, layer), const(cw.shape), const(cb.shape),
        ],
        out_specs=[pl.BlockSpec((tm, D_RNN), lambda i: (i, 0)),
                   pl.BlockSpec((RNN_TILES, tm, LANES), lambda i: (0, i, 0))],
        out_shape=[jax.ShapeDtypeStruct((n_rows, D_RNN), BF16),
                   jax.ShapeDtypeStruct((RNN_TILES, n_rows, LANES), F32)],
        compiler_params=_cparams("arbitrary"),
        name="in_proj_odd",
    )(x, x, x, mods, mods, g, w, cw, cb)


RG_T = 512
RG_NCH = SEQ // RG_T
RG_GROUPS = 8
RG_PLANES_PER_PASS = 6
SQRT_GUARD = 1e-30
RG_WIN = 3 * LANES


def _rg_window_start(j):
    return min(max(LANES * (j - 1), 0), D_RNN - RG_WIN)


def _rg_chunk(x_ref, w_ref, ba_ref, bx_ref, lam_ref, a_pl, b_pl, carry_ref, reverse, emit):
    t_rows = x_ref.shape[1]
    kg = t_rows // RG_GROUPS
    xb = [x_ref[j].astype(BF16) for j in range(RNN_TILES)]
    for j in range(RNN_TILES):
        tile = slice(j * LANES, (j + 1) * LANES)
        wt = _rg_window_start(j) // LANES
        pre = jnp.dot(jnp.concatenate(xb[wt:wt + RG_WIN // LANES], axis=1), w_ref[j],
                      preferred_element_type=F32)
        lam = lam_ref[:, tile]
        softplus_neg = jnp.maximum(-lam, 0.0) + jnp.log1p(jnp.exp(-jnp.abs(lam)))
        th_r = jnp.tanh(pre[:, :LANES] + ba_ref[:, tile])
        th_i = jnp.tanh(pre[:, LANES:] + bx_ref[:, tile])
        c3 = (-0.5 * RG_C * math.log2(math.e)) * softplus_neg
        av = jnp.exp2(c3 + c3 * th_r)
        hx = 0.5 * x_ref[j]
        y = 1.0 - av * av
        a_pl[j, 0:t_rows, :] = av
        b_pl[j, 0:t_rows, :] = (y * lax.rsqrt(jnp.maximum(y, SQRT_GUARD))) * (hx + hx * th_i)

    def rows_k(pl_ref, j, k):
        return pl_ref[j, k * RG_GROUPS:(k + 1) * RG_GROUPS, :]

    order = list(range(kg))[::-1] if reverse else list(range(kg))
    groups = list(range(RG_GROUPS))[::-1] if reverse else list(range(RG_GROUPS))
    for j0 in range(0, RNN_TILES, RG_PLANES_PER_PASS):
        planes = range(j0, min(j0 + RG_PLANES_PER_PASS, RNN_TILES))
        big_a, big_b = {}, {}
        for n, k in enumerate(order):
            for j in planes:
                ak, bk = rows_k(a_pl, j, k), rows_k(b_pl, j, k)
                if n == 0:
                    big_a[j], big_b[j] = ak, bk
                else:
                    big_b[j] = ak * big_b[j] + bk
                    big_a[j] = ak * big_a[j]
        h = {}
        for j in planes:
            c = carry_ref[j, 0:1, :]
            rows = [None] * RG_GROUPS
            for gq in groups:
                rows[gq] = c
                c = big_a[j][gq:gq + 1] * c + big_b[j][gq:gq + 1]
            carry_ref[j, 0:1, :] = c
            h[j] = jnp.concatenate(rows, axis=0)
        for k in order:
            for j in planes:
                h[j] = rows_k(a_pl, j, k) * h[j] + rows_k(b_pl, j, k)
                b_pl[j, k * RG_GROUPS:(k + 1) * RG_GROUPS, :] = h[j]
        for j in planes:
            emit(j, jnp.concatenate(
                [b_pl[j, pl.ds(gq, kg, stride=RG_GROUPS), :] for gq in range(RG_GROUPS)], axis=0))


def _rg_bwd_kernel(xc_ref, xl_ref, w_ref, ba_ref, bx_ref, lam_ref, oc_ref, ol_ref,
                   a_pl, b_pl, carry_ref):
    def emitter(o_ref):
        def emit(j, hcur):
            o_ref[:, j * LANES:(j + 1) * LANES] = hcur.astype(BF16)
        return emit

    @pl.when(pl.program_id(1) == 0)
    def _():
        carry_ref[...] = jnp.zeros_like(carry_ref)
        _rg_chunk(xc_ref, w_ref, ba_ref, bx_ref, lam_ref, a_pl, b_pl, carry_ref, True,
                  emitter(oc_ref))

    @pl.when(pl.program_id(1) > 0)
    def _():
        _rg_chunk(xl_ref, w_ref, ba_ref, bx_ref, lam_ref, a_pl, b_pl, carry_ref, True,
                  emitter(ol_ref))


def _rg_fwd_kernel(xc_ref, xl_ref, w_ref, ba_ref, bx_ref, lam_ref, gc_ref, gl_ref, hc_ref, hl_ref,
                   rc_ref, rl_ref, gtc_ref, gtl_ref, wo_ref, oc_ref, ol_ref,
                   a_pl, b_pl, carry_ref, y_ref):
    def run(x_ref, g_ref, hb_ref, res_ref, gt_ref, o_ref):
        t_rows = x_ref.shape[1]

        def emit(j, hcur):
            tile = slice(j * LANES, (j + 1) * LANES)
            y_ref[0:t_rows, tile] = (g_ref[:, tile].astype(F32)
                                     * (hcur + hb_ref[:, tile].astype(F32))).astype(BF16)

        _rg_chunk(x_ref, w_ref, ba_ref, bx_ref, lam_ref, a_pl, b_pl, carry_ref, False, emit)
        acc = jnp.dot(y_ref[0:t_rows, :], wo_ref[...], preferred_element_type=F32)
        o_ref[...] = res_ref[...] + gt_ref[0] * acc

    @pl.when(pl.program_id(1) == 0)
    def _():
        carry_ref[...] = jnp.zeros_like(carry_ref)
        run(xc_ref, gc_ref, hc_ref, rc_ref, gtc_ref, oc_ref)

    @pl.when(pl.program_id(1) > 0)
    def _():
        run(xl_ref, gl_ref, hl_ref, rl_ref, gtl_ref, ol_ref)


def _rg_scan(xc, xl, w, ba, bx, lam, reverse, fwd_args=None):
    if reverse:
        chunk = lambda s: RG_NCH - jnp.maximum(s, 1)
    else:
        chunk = lambda s: jnp.maximum(s - 1, 0)
    ctx_spec = lambda d: pl.BlockSpec((None, CTX_LEN, d), lambda b, s: (b, 0, 0))
    lat_spec = lambda d: pl.BlockSpec((None, RG_T, d), lambda b, s: (b, chunk(s), 0))
    const = lambda shape: _single_buffered(shape, lambda b, s: (0,) * len(shape))
    in_specs = [pl.BlockSpec((RNN_TILES, CTX_LEN, LANES), lambda b, s: (0, b, 0)),
                pl.BlockSpec((RNN_TILES, RG_T, LANES), lambda b, s: (0, b * RG_NCH + chunk(s), 0)),
                const(w.shape), const(ba.shape), const(bx.shape), const(lam.shape)]
    args = [xc, xl, w, ba, bx, lam]
    scratch = [pltpu.VMEM((RNN_TILES, RG_T, LANES), F32),
               pltpu.VMEM((RNN_TILES, RG_T, LANES), F32),
               pltpu.VMEM((RNN_TILES, 8, LANES), F32)]
    if reverse:
        body, name, width, dtype = _rg_bwd_kernel, "rglru_bwd", D_RNN, BF16
    else:
        gate_c, gate_l, hb_c, hb_l, res_c, res_l, mods, (w_out, out_layer) = fwd_args
        in_specs += [ctx_spec(D_RNN), lat_spec(D_RNN), ctx_spec(D_RNN), lat_spec(D_RNN),
                     ctx_spec(D_MODEL), lat_spec(D_MODEL),
                     _mod_spec(2, lambda b: CTX_MOD_ROW), _mod_spec(2, lambda b: b),
                     _layer_spec(w_out, out_layer)]
        args += [gate_c, gate_l, hb_c, hb_l, res_c, res_l, mods, mods, w_out]
        scratch += [pltpu.VMEM((RG_T, D_RNN), BF16)]
        body, name, width, dtype = _rg_fwd_kernel, "rglru_fwd", D_MODEL, F32
    return pl.pallas_call(
        body,
        grid=(BATCH, 1 + RG_NCH),
        in_specs=in_specs,
        out_specs=[ctx_spec(width), lat_spec(width)],
        out_shape=[jax.ShapeDtypeStruct((BATCH, CTX_LEN, width), dtype),
                   jax.ShapeDtypeStruct((BATCH, SEQ, width), dtype)],
        scratch_shapes=scratch,
        compiler_params=_cparams("arbitrary", "arbitrary"),
        name=name,
    )(*args)


def _rg_gate_weights(wa, wx):
    def window(w, j):
        ws = _rg_window_start(j)
        win = None
        for h in range(RG_HEADS):
            c0, c1 = max(h * RG_DH, j * LANES), min((h + 1) * RG_DH, (j + 1) * LANES)
            if c0 >= c1:
                continue
            r0 = h * RG_DH - ws
            assert 0 <= r0 and r0 + RG_DH <= RG_WIN
            blk = jnp.pad(w[h, :, c0 - h * RG_DH:c1 - h * RG_DH],
                          ((r0, RG_WIN - RG_DH - r0), (c0 - j * LANES, (j + 1) * LANES - c1)))
            win = blk if win is None else win + blk
        return win

    wins = [jnp.concatenate([window(wa, j), window(wx, j)], axis=1) for j in range(RNN_TILES)]
    return (0.5 * jnp.stack(wins)).astype(BF16)


def _col_major(layer):
    return layer < DEPTH and layer % 2 == 1 and (layer // 2) % 2 == 1


def kernel(x, c, ctx, c_ctx, w_ada, b_ada, norm_mix_g, norm_ffn_g, w_in_even, w_out_even, hy_conv_w, hy_conv_b, hy_f1_w, hy_f1_b, hy_f2_w, hy_f2_b, hy_f3_w, hy_f3_b, hy_sin_freq, hy_skip, sgu_ln_g, sgu_w, sgu_b, w_in_odd, rg_conv_w, rg_conv_b, rg_wa, rg_ba, rg_wx, rg_bx, rg_lam, w_out_odd, w_ffn_in, w_ffn_out, final_norm_g):
    mods_all = _ada_mods(c, c_ctx, w_ada, b_ada)
    xs = x.reshape(BATCH * SEQ, D_MODEL)
    cs = ctx.reshape(BATCH * CTX_LEN, D_MODEL)
    lat_row_even = lambda i: i // (SEQ // EVEN_TM)
    lat_row_odd = lambda i: i // (SEQ // ODD_TM)
    lat_row_ffn = lambda i: i // (SEQ // FFN_TM)
    ctx_row = lambda i: CTX_MOD_ROW
    n_tiles = W_A // LANES
    w_in_even, w_out_even, w_in_odd, w_out_odd, w_ffn_in, w_ffn_out = (
        w.astype(BF16) for w in (w_in_even, w_out_even, w_in_odd, w_out_odd, w_ffn_in, w_ffn_out))
    for l in range(DEPTH):
        run_ctx = l < DEPTH - 1
        is_rec = l % 2 == 1
        i = l // 2
        mods = mods_all[l].reshape(MOD_ROWS, 1, N_MOD * D_MODEL)
        g_mix = norm_mix_g[l].reshape(1, D_MODEL)
        mix_l = mix_c = None
        if is_rec:
            cw, cb = rg_conv_w[i], rg_conv_b[i].reshape(1, D_RNN)
            gate_l, xl = _in_odd(xs, mods, lat_row_odd, g_mix, w_in_odd, i, cw, cb, SEQ, RG_T)
            gate_c, xc = _in_odd(cs, mods, ctx_row, g_mix, w_in_odd, i, cw, cb, CTX_LEN, CTX_LEN)
            gate_l = gate_l.reshape(BATCH, SEQ, D_RNN)
            gate_c = gate_c.reshape(BATCH, CTX_LEN, D_RNN)
            row = lambda v: v.reshape(1, D_RNN)
            half = lambda v: 0.5 * row(v)
            hb_c, hb_l = _rg_scan(xc, xl, _rg_gate_weights(rg_wa[i, 1], rg_wx[i, 1]),
                                  half(rg_ba[i, 1]), half(rg_bx[i, 1]), row(rg_lam[i, 1]), True)
            cs_new, xs = _rg_scan(
                xc, xl, _rg_gate_weights(rg_wa[i, 0], rg_wx[i, 0]),
                half(rg_ba[i, 0]), half(rg_bx[i, 0]), row(rg_lam[i, 0]), False,
                (gate_c, gate_l, hb_c, hb_l, cs.reshape(BATCH, CTX_LEN, D_MODEL),
                 xs.reshape(BATCH, SEQ, D_MODEL), mods, (w_out_odd, i)))
            xs = xs.reshape(BATCH * SEQ, D_MODEL)
            if run_ctx:
                cs = cs_new.reshape(BATCH * CTX_LEN, D_MODEL)
        else:
            cw, cb = hy_conv_w[i], hy_conv_b[i].reshape(1, HY_COLS)
            lng = sgu_ln_g[i].reshape(1, W_B)
            sgw = sgu_w[i].astype(BF16)
            sgb = jnp.repeat(sgu_b[i].T, SGU_DH, axis=1)
            skip = hy_skip[i].reshape(1, HY_ORDER * W_A)
            fargs = (hy_f1_w[i], hy_f1_b[i], hy_f2_w[i], hy_f2_b[i], hy_f3_w[i], hy_f3_b[i],
                     hy_sin_freq[i])
            kf = _filter_spectrum(SEQ, *fargs, skip)
            za, yb = _in_even(xs, mods, lat_row_even, g_mix, w_in_even, i, cw, cb, lng, sgw, sgb,
                              SEQ, True)
            za = za.reshape(HY_TILES, BATCH, SEQ, LANES)
            y1 = _longconv(za, 0, za, n_tiles, kf, 0, flat_out=True)
            ya = _longconv(y1, 0, za, 2 * n_tiles, kf, n_tiles, flat_in=True)
            mix_l = (ya, yb, (w_out_even, i), True)
            if run_ctx:
                kfc = _filter_spectrum(CTX_LEN, *fargs, skip)
                zc, ybc = _in_even(cs, mods, ctx_row, g_mix, w_in_even, i, cw, cb, lng, sgw, sgb,
                                   CTX_LEN, False)
                zc = zc.reshape(HY_TILES, BATCH, CTX_LEN, LANES)
                y1c = _ctx_conv(zc, 0, zc, n_tiles, kfc, 0)
                yac = _ctx_conv(y1c, 0, zc, 2 * n_tiles, kfc, n_tiles)
                mix_c = (yac, ybc, (w_out_even, i), False)
        g_ffn = norm_ffn_g[l].reshape(1, D_MODEL)
        fg = final_norm_g.reshape(1, D_MODEL)
        swap = _col_major(l) != _col_major(l + 1)
        xs = _ffn(xs, mods, lat_row_ffn, g_ffn, w_ffn_in, w_ffn_out, l, fg, l == DEPTH - 1,
                  mix_l, swap)
        if run_ctx:
            cs = _ffn(cs, mods, ctx_row, g_ffn, w_ffn_in, w_ffn_out, l, fg, False, mix_c)
    return xs.reshape(BATCH, SEQ, D_MODEL)
```

```python
import functools
import math

import numpy as np
import jax
import jax.numpy as jnp
from jax import lax
from jax.experimental import pallas as pl
from jax.experimental.pallas import tpu as pltpu

F32 = jnp.float32
BF16 = jnp.bfloat16

D_MODEL = 1024
BATCH = 4
SEQ = 4096
DEPTH = 4
GRID_W = 64
CTX_LEN = 256
N_MOD = 6
NORM_EPS = 1e-6
W_A = D_MODEL // 2
HY_ORDER = 2
HY_SHORT = 3
HY_BANDS = 16
HY_TARGET = 1e-2
HY_FAST_PCT = 0.3
HY_SLOW_PCT = 1.5
W_B = D_MODEL // 2
SGU_GROUPS = 4
SGU_DH = W_B // SGU_GROUPS
CHUNK = 128
D_RNN = ((4 * D_MODEL // 3 + 127) // 128) * 128
RG_HEADS = 16
RG_DH = D_RNN // RG_HEADS
RG_CONV = 4
RG_C = 8.0
D_FF = ((8 * D_MODEL // 3 + 255) // 256) * 256

VMEM_BYTES_V7X = 64 * 1024 * 1024
VMEM_LIMIT = VMEM_BYTES_V7X - 4 * 1024 * 1024
LANES = 128

FFT_N = 2 * SEQ
FFT_N1 = 64
FFT_N2 = 128
FFT_HALF_N1 = FFT_N1 // 2
FFT_PITCH = FFT_N2 + 8
CTX_N = 2 * CTX_LEN


def _cparams(*sem):
    return pltpu.CompilerParams(dimension_semantics=sem, vmem_limit_bytes=VMEM_LIMIT)


def _single_buffered(block_shape, index_map):
    return pl.BlockSpec(block_shape, index_map, pipeline_mode=pl.Buffered(1))


def _layer_spec(stack, layer):
    rest = stack.shape[1:]
    return _single_buffered((None,) + rest, lambda *_: (layer,) + (0,) * len(rest))


@functools.lru_cache(maxsize=None)
def _dft_tables_f32():
    n = FFT_N
    k1 = np.arange(FFT_N1)
    n2 = np.arange(FFT_N2)

    def angle(n1):
        m = (FFT_N2 * n1[None, None, :] + n2[:, None, None]) * k1[None, :, None]
        return 2.0 * np.pi * (m % n) / n

    a = angle(np.arange(FFT_HALF_N1))
    c, s = np.cos(a), np.sin(a)
    g = np.empty((FFT_N2, FFT_N1, 2, 2, FFT_HALF_N1))
    g[:, :, 0, 0], g[:, :, 0, 1] = c, s
    g[:, :, 1, 0], g[:, :, 1, 1] = -s, c
    g = g.reshape(FFT_N2, 2 * FFT_N1, 2 * FFT_HALF_N1)

    a = angle(np.arange(FFT_N1))
    gf = np.stack([np.cos(a), -np.sin(a)], axis=2).reshape(FFT_N2, 2 * FFT_N1, FFT_N1)

    a = np.transpose(angle(np.arange(FFT_HALF_N1)), (0, 2, 1))
    c, s = np.cos(a) / n, np.sin(a) / n
    h = np.empty((FFT_N2, 2, FFT_HALF_N1, FFT_N1, 2))
    h[:, 0, :, :, 0], h[:, 0, :, :, 1] = c, -s
    h[:, 1, :, :, 0], h[:, 1, :, :, 1] = s, c
    h = h.reshape(FFT_N2, 2 * FFT_HALF_N1, 2 * FFT_N1)

    a = 2.0 * np.pi * np.outer(n2, n2) / FFT_N2
    c, s = np.cos(a), np.sin(a)
    f2 = np.block([[c, s], [-s, c]])
    f2i = np.block([[c, -s], [s, c]])

    kk = np.arange(CTX_N)
    a = 2.0 * np.pi * np.outer(kk, np.arange(CTX_LEN)) / CTX_N
    c, s = np.cos(a), np.sin(a)
    fc = np.block([[c, s], [-s, c]])
    a = 2.0 * np.pi * np.outer(kk, np.arange(CTX_N)) / CTX_N
    fcf = np.concatenate([np.cos(a), -np.sin(a)], axis=0)
    a = 2.0 * np.pi * np.outer(np.arange(CTX_LEN), kk) / CTX_N
    c, s = np.cos(a) / CTX_N, np.sin(a) / CTX_N
    fci = np.block([[c, -s], [s, c]])

    tables = dict(g=g, gf=gf, h=h, f2=f2, f2i=f2i, fc=fc, fcf=fcf, fci=fci)
    return {k: np.asarray(v, dtype=np.float32) for k, v in tables.items()}


def _dft_tables():
    return {k: jnp.asarray(v).astype(BF16) for k, v in _dft_tables_f32().items()}


SEQ_TILE = 1024
N1_PER_TILE = SEQ_TILE // FFT_N2


def _scatter_n2(s_ref, n2, val):
    for p in range(s_ref.shape[0]):
        s_ref[p, pl.ds(n2, 2 * FFT_N1, stride=FFT_PITCH), :] = val[:, p * LANES:(p + 1) * LANES]


def _gather_n2(s_ref, n2):
    return jnp.concatenate(
        [s_ref[p, pl.ds(n2, 2 * FFT_N1, stride=FFT_PITCH), :] for p in range(s_ref.shape[0])],
        axis=1)


def _spectrum_rows(s_ref, k1):
    base = pl.multiple_of(k1 * (2 * FFT_PITCH), 8)
    return base, jnp.concatenate(
        [jnp.concatenate([s_ref[p, pl.ds(base + o, FFT_N2), :] for p in range(s_ref.shape[0])],
                         axis=1) for o in (0, FFT_PITCH)], axis=0)


HY_FFN = 64
FILT_ROWS = 1024
FILT_TILES = 2
HIGHEST = lax.Precision.HIGHEST


def _tap_rows(row, seq, n2_major):
    if n2_major:
        n2, n1 = row >> int(math.log2(FFT_N1)), row & (FFT_N1 - 1)
        bwd = n1 >= FFT_HALF_N1
        m = FFT_N2 * (n1 - FFT_HALF_N1) + n2
        fwd_pos = FFT_N2 * n1 + n2
    else:
        bwd = row >= seq
        m = row - seq
        fwd_pos = row
    pos = jnp.where(bwd, seq - m, fwd_pos).astype(F32)
    bwd_f = jnp.where(bwd, 1.0, 0.0)
    keep_f = jnp.where(bwd & (m == 0), 0.0, 1.0)
    return pos, bwd_f, keep_f


def _filter_kernel(ec_ref, f1w_ref, f1b_ref, f2w_ref, f2b_ref, sf_ref, w3f_ref, w3b_ref, b3f_ref,
                   b3b_ref, dl_ref, skip_ref, *rest, seq, n2_major):
    if n2_major:
        gf_ref, f2_ref, o_ref, hid_ref, k_ref, s_ref = rest
    else:
        fcf_ref, o_ref, hid_ref, k_ref = rest
    n_rows = 2 * seq
    rb = min(FILT_ROWS, n_rows)

    def rows_of(i):
        r0 = pl.multiple_of(i * rb, rb)
        return r0, _tap_rows(r0 + lax.broadcasted_iota(jnp.int32, (rb, 1), 0), seq, n2_major)

    @pl.when(pl.program_id(0) == 0)
    def _():
        sf = sf_ref[...]

        def hidden(i, carry):
            r0, (pos, _, _) = rows_of(i)
            t = pos * (1.0 / (seq - 1))
            w = pos * (2.0 * math.pi / seq)
            emb = jnp.sin(w * ec_ref[0:1] + ec_ref[1:2]) + t * ec_ref[2:3]
            h = jnp.sin(sf * (jnp.dot(emb, f1w_ref[...], precision=HIGHEST,
                                      preferred_element_type=F32) + f1b_ref[...]))
            h = jnp.sin(sf * (jnp.dot(h.astype(BF16), f2w_ref[...].astype(BF16),
                                      preferred_element_type=F32) + f2b_ref[...]))
            hid_ref[pl.ds(r0, rb), :] = h
            return carry

        lax.fori_loop(0, n_rows // rb, hidden, 0)

    def taps(i, ssq):
        r0, (pos, bwd_f, keep_f) = rows_of(i)
        h = hid_ref[pl.ds(r0, rb), :].astype(BF16)
        neg_t = -(pos * (1.0 / (seq - 1)))
        ks = []
        for c in range(FILT_TILES):
            kf = jnp.dot(h, w3f_ref[c].astype(BF16), preferred_element_type=F32) + b3f_ref[c]
            kb = jnp.dot(h, w3b_ref[c].astype(BF16), preferred_element_type=F32) + b3b_ref[c]
            ks.append((kf + bwd_f * (kb - kf)) * jnp.exp(neg_t * dl_ref[c]) * keep_f)
        k = jnp.concatenate(ks, axis=1)
        k_ref[pl.ds(r0, rb), :] = k
        return ssq + jnp.sum(k * k, axis=0, keepdims=True)

    ssq = lax.fori_loop(0, n_rows // rb, taps, jnp.zeros((1, FILT_TILES * LANES), F32))
    scale = lax.rsqrt(ssq + NORM_EPS)
    skip = skip_ref[...]

    if n2_major:
        def stage1(n2, carry):
            x = k_ref[pl.ds(pl.multiple_of(n2 * FFT_N1, FFT_N1), FFT_N1), :] * scale
            _scatter_n2(s_ref, n2, jnp.dot(gf_ref[n2], x.astype(BF16),
                                           preferred_element_type=F32))
            return carry

        lax.fori_loop(0, FFT_N2, stage1, 0, unroll=8)

        def stage2(k1, carry):
            _, r = _spectrum_rows(s_ref, k1)
            z = jnp.dot(f2_ref[...], r.astype(BF16), preferred_element_type=F32)
            row = pl.multiple_of(k1 * (2 * FFT_N2), 2 * FFT_N2)
            o_ref[pl.ds(row, FFT_N2), :] = (z[:FFT_N2] + skip).astype(BF16)
            o_ref[pl.ds(row + FFT_N2, FFT_N2), :] = z[FFT_N2:].astype(BF16)
            return carry

        lax.fori_loop(0, FFT_N1, stage2, 0, unroll=8)
    else:
        z = jnp.dot(fcf_ref[...], (k_ref[...] * scale).astype(BF16), preferred_element_type=F32)
        o_ref[:n_rows, :] = (z[:n_rows] + skip).astype(BF16)
        o_ref[n_rows:, :] = z[n_rows:].astype(BF16)


def _filter_spectrum(seq, f1_w, f1_b, f2_w, f2_b, f3_w, f3_b, sin_freq, skip):
    n2_major = seq == SEQ
    t = _dft_tables()
    pad = lambda a, r, c: jnp.pad(a, ((0, r - a.shape[0]), (0, c - a.shape[1])))
    f = jnp.linspace(1e-4, HY_BANDS - 1, HY_BANDS, dtype=F32)
    zeros = lambda n: jnp.zeros((n,), F32)
    ones = lambda n: jnp.ones((n,), F32)
    rest = LANES - 1 - 2 * HY_BANDS
    ec = jnp.stack([
        jnp.concatenate([zeros(1), f, f, zeros(rest)]),
        jnp.concatenate([zeros(1), (math.pi / 2) * ones(HY_BANDS), math.pi * ones(HY_BANDS),
                         zeros(rest)]),
        jnp.concatenate([ones(1), zeros(LANES - 1)]),
    ] + [zeros(LANES)] * 5)
    row = lambda v: pad(v.reshape(1, -1), 1, LANES)
    n_tiles = HY_ORDER * W_A // LANES
    w3 = f3_w.reshape(HY_FFN, HY_ORDER, 2, W_A)
    b3 = f3_b.reshape(HY_ORDER, 2, W_A)
    tiles_w = lambda d: jnp.pad(
        w3[:, :, d, :].reshape(HY_FFN, n_tiles, LANES).transpose(1, 0, 2),
        ((0, 0), (0, LANES - HY_FFN), (0, 0)))
    tiles_b = lambda d: b3[:, d, :].reshape(n_tiles, 1, LANES)
    deltas = jnp.abs(jnp.linspace(math.log(HY_TARGET) / HY_SLOW_PCT,
                                  math.log(HY_TARGET) / HY_FAST_PCT, W_A, dtype=F32))
    dl = jnp.tile(deltas, HY_ORDER).reshape(n_tiles, 1, LANES)
    const = lambda shape: _single_buffered(shape, lambda j: (0,) * len(shape))
    tile = lambda shape: pl.BlockSpec((FILT_TILES,) + shape, lambda j: (j, 0, 0))
    ct = FILT_TILES * LANES
    args = [ec, pad(f1_w, LANES, LANES), row(f1_b), pad(f2_w, LANES, LANES), row(f2_b),
            row(sin_freq), tiles_w(0), tiles_w(1), tiles_b(0), tiles_b(1), dl, skip]
    in_specs = [const((8, LANES)), const((LANES, LANES)), const((1, LANES)),
                const((LANES, LANES)), const((1, LANES)), const((1, LANES)),
                tile((LANES, LANES)), tile((LANES, LANES)), tile((1, LANES)), tile((1, LANES)),
                tile((1, LANES)), pl.BlockSpec((1, ct), lambda j: (0, j))]
    scratch = [pltpu.VMEM((2 * seq, LANES), F32), pltpu.VMEM((2 * seq, ct), F32)]
    if n2_major:
        args += [t["gf"], t["f2"]]
        in_specs += [const(t["gf"].shape), const(t["f2"].shape)]
        scratch += [pltpu.VMEM((FILT_TILES, 2 * FFT_N1 * FFT_PITCH, LANES), F32)]
    else:
        args += [t["fcf"]]
        in_specs += [const(t["fcf"].shape)]
    return pl.pallas_call(
        functools.partial(_filter_kernel, seq=seq, n2_major=n2_major),
        grid=(n_tiles // FILT_TILES,),
        in_specs=in_specs,
        out_specs=pl.BlockSpec((4 * seq, ct), lambda j: (0, j)),
        out_shape=jax.ShapeDtypeStruct((4 * seq, HY_ORDER * W_A), BF16),
        scratch_shapes=scratch,
        compiler_params=_cparams("arbitrary"),
        name="hyena_filter" if n2_major else "hyena_ctx_filter",
    )(*args)


LC_OUT_CHUNKS = 4
LC_N2_PER_CHUNK = FFT_N2 // LC_OUT_CHUNKS


def _longconv_kernel(v_ref, gate_ref, kf_ref, g_ref, h_ref, f2_ref, f2i_ref, o_ref, s_ref):
    ct = v_ref.shape[2]
    t = pl.program_id(1)

    @pl.when(t == 0)
    def _():
        def stage1(n2, carry):
            r0 = pl.multiple_of(n2 * N1_PER_TILE, N1_PER_TILE)
            blk = [jnp.concatenate([v_ref[b, pl.ds(i * SEQ_TILE + r0, N1_PER_TILE), :]
                                    for i in range(SEQ // SEQ_TILE)], axis=0)
                   for b in range(BATCH)]
            x = jnp.concatenate([jnp.concatenate([blk[0], blk[2]], axis=1),
                                 jnp.concatenate([blk[1], blk[3]], axis=1)], axis=0)
            _scatter_n2(s_ref, n2, jnp.dot(g_ref[n2], x.astype(BF16), preferred_element_type=F32))
            return carry

        lax.fori_loop(0, FFT_N2, stage1, 0, unroll=16)

        def stage2(k1, carry):
            base, r = _spectrum_rows(s_ref, k1)
            z = jnp.dot(f2_ref[...], r.astype(BF16), preferred_element_type=F32)
            kf = kf_ref[pl.ds(pl.multiple_of(k1 * (2 * FFT_N2), 2 * FFT_N2), 2 * FFT_N2), :]
            kf = kf.astype(F32)
            kr = jnp.concatenate([kf[:FFT_N2]] * 2, axis=1)
            ki = jnp.concatenate([kf[FFT_N2:]] * 2, axis=1)
            zr, zi = z[:FFT_N2], z[FFT_N2:]
            p = jnp.concatenate([zr * kr - zi * ki, zr * ki + zi * kr], axis=0)
            q = jnp.dot(f2i_ref[...], p.astype(BF16), preferred_element_type=F32)
            for p in range(2):
                s_ref[p, pl.ds(base, FFT_N2), :] = q[:FFT_N2, p * ct:(p + 1) * ct]
                s_ref[p, pl.ds(base + FFT_PITCH, FFT_N2), :] = q[FFT_N2:, p * ct:(p + 1) * ct]
            return carry

        lax.fori_loop(0, FFT_N1, stage2, 0, unroll=8)

    def stage3(j, carry):
        n2 = t * LC_N2_PER_CHUNK + j
        rq = _gather_n2(s_ref, n2)
        y = jnp.dot(h_ref[n2], rq.astype(BF16), preferred_element_type=F32)
        r0 = pl.multiple_of(j * N1_PER_TILE, N1_PER_TILE)
        for b in range(BATCH):
            ri, pair = b % 2, b // 2
            for i in range(SEQ // SEQ_TILE):
                n1 = ri * FFT_HALF_N1 + i * N1_PER_TILE
                yb = y[n1:n1 + N1_PER_TILE, pair * ct:(pair + 1) * ct]
                o_ref[b, i, pl.ds(r0, N1_PER_TILE), :] = (
                    gate_ref[b, i, pl.ds(r0, N1_PER_TILE), :] * yb)
        return carry

    lax.fori_loop(0, LC_N2_PER_CHUNK, stage3, 0, unroll=16)


def _longconv(v, v_tile, gate, gate_tile, kf, kf_col):
    t = _dft_tables()
    ct = LANES
    n_seq_tiles = SEQ // SEQ_TILE
    rows = SEQ_TILE // LC_OUT_CHUNKS
    gate = gate.reshape(gate.shape[0], BATCH, n_seq_tiles, SEQ_TILE, ct)
    out = pl.pallas_call(
        _longconv_kernel,
        grid=(W_A // ct, LC_OUT_CHUNKS),
        in_specs=[
            _single_buffered((None, BATCH, SEQ, ct), lambda j, i: (v_tile + j, 0, 0, 0)),
            pl.BlockSpec((None, BATCH, n_seq_tiles, rows, ct),
                         lambda j, i: (gate_tile + j, 0, 0, i, 0)),
            pl.BlockSpec((2 * FFT_N, ct), lambda j, i: (0, kf_col + j)),
            _single_buffered(t["g"].shape, lambda j, i: (0, 0, 0)),
            _single_buffered(t["h"].shape, lambda j, i: (0, 0, 0)),
            _single_buffered(t["f2"].shape, lambda j, i: (0, 0)),
            _single_buffered(t["f2i"].shape, lambda j, i: (0, 0)),
        ],
        out_specs=pl.BlockSpec((None, BATCH, n_seq_tiles, rows, ct),
                               lambda j, i: (j, 0, 0, i, 0)),
        out_shape=jax.ShapeDtypeStruct((W_A // ct, BATCH, n_seq_tiles, SEQ_TILE, ct), F32),
        scratch_shapes=[pltpu.VMEM((2, 2 * FFT_N1 * FFT_PITCH, ct), F32)],
        compiler_params=_cparams("arbitrary", "arbitrary"),
        name="hyena_longconv",
    )(v, gate, kf, t["g"], t["h"], t["f2"], t["f2i"])
    return out.reshape(W_A // ct, BATCH, SEQ, ct)


def _ctx_conv_kernel(v_ref, gate_ref, kf_ref, fc_ref, fci_ref, o_ref):
    ct = v_ref.shape[2]
    x = jnp.concatenate([jnp.concatenate([v_ref[0], v_ref[2]], axis=1),
                         jnp.concatenate([v_ref[1], v_ref[3]], axis=1)], axis=0)
    z = jnp.dot(fc_ref[...], x.astype(BF16), preferred_element_type=F32)
    kf = kf_ref[...].astype(F32)
    kr = jnp.concatenate([kf[:CTX_N]] * 2, axis=1)
    ki = jnp.concatenate([kf[CTX_N:]] * 2, axis=1)
    zr, zi = z[:CTX_N], z[CTX_N:]
    p = jnp.concatenate([zr * kr - zi * ki, zr * ki + zi * kr], axis=0)
    y = jnp.dot(fci_ref[...], p.astype(BF16), preferred_element_type=F32)
    for b in range(BATCH):
        ri, pair = b % 2, b // 2
        o_ref[b] = gate_ref[b] * y[ri * CTX_LEN:(ri + 1) * CTX_LEN, pair * ct:(pair + 1) * ct]


def _ctx_conv(v, v_tile, gate, gate_tile, kf, kf_col):
    t = _dft_tables()
    ct = LANES
    return pl.pallas_call(
        _ctx_conv_kernel,
        grid=(W_A // ct,),
        in_specs=[
            pl.BlockSpec((None, BATCH, CTX_LEN, ct), lambda j: (v_tile + j, 0, 0, 0)),
            pl.BlockSpec((None, BATCH, CTX_LEN, ct), lambda j: (gate_tile + j, 0, 0, 0)),
            pl.BlockSpec((2 * CTX_N, ct), lambda j: (0, kf_col + j)),
            pl.BlockSpec(t["fc"].shape, lambda j: (0, 0)),
            pl.BlockSpec(t["fci"].shape, lambda j: (0, 0)),
        ],
        out_specs=pl.BlockSpec((None, BATCH, CTX_LEN, ct), lambda j: (j, 0, 0, 0)),
        out_shape=jax.ShapeDtypeStruct((W_A // ct, BATCH, CTX_LEN, ct), F32),
        compiler_params=_cparams("arbitrary"),
        name="hyena_ctx_conv",
    )(v, gate, kf, t["fc"], t["fci"])


MOD_ROWS = 8
CTX_MOD_ROW = BATCH


def _ada_kernel(c_ref, w_ref, b_ref, o_ref):
    cv = c_ref[...]
    s = cv * jax.nn.sigmoid(cv)
    o_ref[0] = jnp.dot(s.astype(BF16), w_ref[0].astype(BF16),
                       preferred_element_type=F32) + b_ref[0]


def _ada_mods(c, c_ctx, w_ada, b_ada):
    cv = jnp.concatenate(
        [c, c_ctx[None], jnp.zeros((MOD_ROWS - BATCH - 1, D_MODEL), F32)], axis=0)
    n = N_MOD * D_MODEL
    tn = n // 4
    return pl.pallas_call(
        _ada_kernel,
        grid=(DEPTH, n // tn),
        in_specs=[
            pl.BlockSpec((MOD_ROWS, D_MODEL), lambda l, j: (0, 0)),
            pl.BlockSpec((1, D_MODEL, tn), lambda l, j: (l, 0, j)),
            pl.BlockSpec((1, 1, tn), lambda l, j: (l, 0, j)),
        ],
        out_specs=pl.BlockSpec((1, MOD_ROWS, tn), lambda l, j: (l, 0, j)),
        out_shape=jax.ShapeDtypeStruct((DEPTH, MOD_ROWS, n), F32),
        compiler_params=_cparams("arbitrary", "arbitrary"),
        name="ada_mods",
    )(cv, w_ada, b_ada.reshape(DEPTH, 1, n))


def _mod_spec(m, row_fn):
    return pl.BlockSpec((1, 1, D_MODEL), lambda i, *_: (row_fn(i), 0, m))


def _rms_mod(x, g, shift, scale):
    ms = jnp.mean(x * x, axis=-1, keepdims=True)
    return (x * lax.rsqrt(ms + NORM_EPS)) * (g * (1.0 + scale)) + shift


GELU_C = math.sqrt(2.0 / math.pi)


def _gelu(x):
    hx = 0.5 * x
    return hx + hx * jnp.tanh(x * (GELU_C + (GELU_C * 0.044715) * (x * x)))


HALO = 8
PROJ_COLS = 2 * LANES


def _dwconv(z_all, tm, w, b, left, valid):
    n = z_all.shape[0]
    y = b + w[left:left + 1] * z_all[:tm]
    for k in range(w.shape[0]):
        d = k - left
        if d == 0:
            continue
        s = pltpu.roll(z_all, (-d) % n, 0)[:tm]
        if valid is not None:
            s = s * valid(d)
        y = y + s * w[k:k + 1]
    return y


def _halo_specs(tm, n_rows):
    per = tm // HALO
    last = n_rows // HALO - 1
    return [
        pl.BlockSpec((tm, D_MODEL), lambda i: (i, 0)),
        pl.BlockSpec((HALO, D_MODEL), lambda i: (jnp.maximum(i * per - 1, 0), 0)),
        pl.BlockSpec((HALO, D_MODEL), lambda i: (jnp.minimum((i + 1) * per, last), 0)),
    ]


def _normed_tile(x_ref, xp_ref, xn_ref, sh_ref, sc_ref, g_ref, seq_len):
    tm = x_ref.shape[0]
    g, shift, scale = g_ref[...], sh_ref[0], sc_ref[0]
    h0 = _rms_mod(x_ref[:tm // 2, :], g, shift, scale)
    h = _rms_mod(x_ref[tm // 2:, :], g, shift, scale)
    hn = _rms_mod(xn_ref[...], g, shift, scale)
    hp = _rms_mod(xp_ref[...], g, shift, scale)
    if seq_len >= tm:
        r0 = pl.program_id(0) * tm
        hp = hp * jnp.where((r0 & (seq_len - 1)) == 0, 0.0, 1.0)
        hn = hn * jnp.where(((r0 + tm) & (seq_len - 1)) == 0, 0.0, 1.0)
        valid = None
    else:
        hp, hn = jnp.zeros_like(hp), jnp.zeros_like(hn)
        pos = lax.broadcasted_iota(jnp.int32, (tm, 1), 0) & (seq_len - 1)
        valid = lambda d: jnp.where((pos + d >= 0) & (pos + d < seq_len), 1.0, 0.0)
    return [h0.astype(BF16), jnp.concatenate([h, hn, hp], axis=0).astype(BF16)], valid


def _dot_rows(pieces, w):
    return jnp.concatenate([jnp.dot(p, w, preferred_element_type=F32) for p in pieces], axis=0)


EVEN_TM = SEQ_TILE
HY_COLS = 3 * W_A
HY_TILES = HY_COLS // LANES


def _in_even_kernel(x_ref, xp_ref, xn_ref, sh_ref, sc_ref, g_ref, w_ref, cw_ref, cb_ref,
                    lng_ref, sgw_ref, sgb_ref, za_ref, yb_ref, *, seq_len, n2_major):
    tm = x_ref.shape[0]
    pieces, valid = _normed_tile(x_ref, xp_ref, xn_ref, sh_ref, sc_ref, g_ref, seq_len)
    h = jnp.concatenate([pieces[0], pieces[1][:tm // 2]], axis=0)
    cw = PROJ_COLS
    for cc in range(HY_COLS // cw):
        cols = slice(cc * cw, (cc + 1) * cw)
        z_all = _dot_rows(pieces, w_ref[:, cols])
        y = _dwconv(z_all, tm, cw_ref[:, cols], cb_ref[:, cols], 1, valid)
        for c in range(cw // LANES):
            tile = cc * (cw // LANES) + c
            yc = y[:, c * LANES:(c + 1) * LANES]
            if n2_major:
                for j in range(N1_PER_TILE):
                    za_ref[tile, pl.ds(j, FFT_N2, stride=N1_PER_TILE), :] = (
                        yc[j * FFT_N2:(j + 1) * FFT_N2])
            else:
                za_ref[tile] = yc
    u = _gelu(jnp.dot(h, w_ref[:, HY_COLS:HY_COLS + W_B], preferred_element_type=F32))
    vb = _gelu(jnp.dot(h, w_ref[:, HY_COLS + W_B:], preferred_element_type=F32))
    vc = vb - jnp.mean(vb, axis=-1, keepdims=True)
    vn = vc * lax.rsqrt(jnp.mean(vc * vc, axis=-1, keepdims=True) + NORM_EPS) * lng_ref[...]
    vn = vn.astype(BF16)
    n_chunks = tm // CHUNK
    for q in range(SGU_GROUPS):
        cols = slice(q * SGU_DH, (q + 1) * SGU_DH)
        rhs = jnp.concatenate(
            [vn[ch * CHUNK:(ch + 1) * CHUNK, cols] for ch in range(n_chunks)], axis=1)
        s_all = jnp.dot(sgw_ref[q], rhs, preferred_element_type=F32)
        for ch in range(n_chunks):
            rows = slice(ch * CHUNK, (ch + 1) * CHUNK)
            s = s_all[:, ch * SGU_DH:(ch + 1) * SGU_DH] + sgb_ref[:, cols]
            yb_ref[rows, cols] = (u[rows, cols] * s).astype(BF16)


def _in_even(x, mods, mod_row, g, w, layer, cw, cb, lng, sgw, sgb, seq_len, n2_major):
    n_rows = x.shape[0]
    tm = EVEN_TM
    za_shape = (HY_TILES, n_rows, LANES)
    za_spec = pl.BlockSpec((HY_TILES, tm, LANES), lambda i: (0, i, 0))
    const = lambda shape: _single_buffered(shape, lambda i: (0,) * len(shape))
    return pl.pallas_call(
        functools.partial(_in_even_kernel, seq_len=seq_len, n2_major=n2_major),
        grid=(n_rows // tm,),
        in_specs=_halo_specs(tm, n_rows) + [
            _mod_spec(0, mod_row), _mod_spec(1, mod_row),
            const((1, D_MODEL)), _layer_spec(w, layer), const(cw.shape), const(cb.shape),
            const(lng.shape), const(sgw.shape), const(sgb.shape),
        ],
        out_specs=[za_spec, pl.BlockSpec((tm, W_B), lambda i: (i, 0))],
        out_shape=[jax.ShapeDtypeStruct(za_shape, F32),
                   jax.ShapeDtypeStruct((n_rows, W_B), BF16)],
        compiler_params=_cparams("arbitrary"),
        name="in_proj_even",
    )(x, x, x, mods, mods, g, w, cw, cb, lng, sgw, sgb)


FFN_TM = SEQ_TILE
FFN_SUB = 256
GRID_H = SEQ // GRID_W
GT_ROWS = FFN_TM // GRID_W
GT_PITCH = GRID_W + 8


def _even_mix(ya_ref, yb_ref, w_ref, n2_major):
    parts = []
    for c in range(W_A // LANES):
        if n2_major:
            yac = jnp.concatenate(
                [ya_ref[c, pl.ds(j, FFT_N2, stride=N1_PER_TILE), :]
                 for j in range(N1_PER_TILE)], axis=0)
        else:
            yac = ya_ref[c]
        parts.append(yac.astype(BF16))
    y = jnp.concatenate(parts + [yb_ref[...]], axis=1)
    return jnp.dot(y, w_ref[...], preferred_element_type=F32)


def _ffn_kernel(*refs, final_norm, mix, n2_major, swap_grid):
    refs = list(refs)
    x_ref, sh_ref, sc_ref, gt_ref, g_ref, w_in_ref, wo_ref, fg_ref = refs[:8]
    del refs[:8]
    if mix:
        ya_ref, yb_ref, gm_ref, wm_ref = refs[:4]
        del refs[:4]
    o_ref = refs.pop(0)
    x = x_ref[...]
    if mix:
        x = x + gm_ref[0] * _even_mix(ya_ref, yb_ref, wm_ref, n2_major)
    h = _rms_mod(x, g_ref[...], sh_ref[0], sc_ref[0]).astype(BF16)
    acts = []
    for c0 in range(0, D_FF, FFN_SUB):
        c1 = min(c0 + FFN_SUB, D_FF)
        hz = 0.5 * jnp.dot(h, w_in_ref[:, c0:c1], preferred_element_type=F32)
        z2 = jnp.dot(h, w_in_ref[:, D_FF + c0:D_FF + c1], preferred_element_type=F32)
        acts.append(((hz + hz * jnp.tanh(hz)) * z2).astype(BF16))
    y = x + gt_ref[0] * jnp.dot(jnp.concatenate(acts, axis=1), wo_ref[...],
                                preferred_element_type=F32)
    if final_norm:
        y = y * lax.rsqrt(jnp.mean(y * y, axis=-1, keepdims=True) + NORM_EPS) * fg_ref[...]
    if not swap_grid:
        o_ref[...] = y
        return
    s_ref = refs.pop(0)
    nt = D_MODEL // LANES
    for t in range(nt):
        for j in range(GT_ROWS):
            s_ref[t, j * GT_PITCH:j * GT_PITCH + GRID_W, :] = (
                y[j * GRID_W:(j + 1) * GRID_W, t * LANES:(t + 1) * LANES])
    for c in range(GRID_W):
        o_ref[c] = jnp.concatenate(
            [s_ref[t, pl.ds(c, GT_ROWS, stride=GT_PITCH), :] for t in range(nt)], axis=1)


def _ffn(x, mods, mod_row, g, w_in, w_out, layer, final_g, final_norm, mix=None,
         swap_grid=False):
    n_rows = x.shape[0]
    tm = FFN_TM
    const = lambda shape: _single_buffered(shape, lambda i: (0,) * len(shape))
    in_specs = [
        pl.BlockSpec((tm, D_MODEL), lambda i: (i, 0)),
        _mod_spec(3, mod_row), _mod_spec(4, mod_row), _mod_spec(5, mod_row),
        const((1, D_MODEL)), _layer_spec(w_in, layer), _layer_spec(w_out, layer),
        const((1, D_MODEL)),
    ]
    args = [x, mods, mods, mods, g, w_in, w_out, final_g]
    n2_major = False
    if mix is not None:
        ya, yb, (w_mix, mix_layer), n2_major = mix
        nt = W_A // LANES
        in_specs += [pl.BlockSpec((nt, tm, LANES), lambda i: (0, i, 0)),
                     pl.BlockSpec((tm, W_B), lambda i: (i, 0)),
                     _mod_spec(2, mod_row), _layer_spec(w_mix, mix_layer)]
        args += [ya.reshape(nt, n_rows, LANES), yb, mods, w_mix]
    scratch = []
    if swap_grid:
        per_seq = SEQ // tm
        out_spec = pl.BlockSpec((None, GRID_W, GT_ROWS, D_MODEL),
                                lambda i: (i // per_seq, 0, i % per_seq, 0))
        out_shape = jax.ShapeDtypeStruct((n_rows // SEQ, GRID_W, GRID_H, D_MODEL), F32)
        scratch = [pltpu.VMEM((D_MODEL // LANES, GT_ROWS * GT_PITCH, LANES), F32)]
    else:
        out_spec = pl.BlockSpec((tm, D_MODEL), lambda i: (i, 0))
        out_shape = jax.ShapeDtypeStruct((n_rows, D_MODEL), F32)
    out = pl.pallas_call(
        functools.partial(_ffn_kernel, final_norm=final_norm, mix=mix is not None,
                          n2_major=n2_major, swap_grid=swap_grid),
        grid=(n_rows // tm,),
        in_specs=in_specs,
        out_specs=out_spec,
        out_shape=out_shape,
        scratch_shapes=scratch,
        compiler_params=_cparams("arbitrary"),
        name="ffn",
    )(*args)
    return out.reshape(n_rows, D_MODEL)


ODD_TM = 1024
RNN_TILES = D_RNN // LANES


def _in_odd_kernel(x_ref, xp_ref, xn_ref, sh_ref, sc_ref, g_ref, w_ref, cw_ref, cb_ref,
                   gate_ref, xl_ref, *, seq_len, chunk):
    tm = x_ref.shape[0]
    pieces, valid = _normed_tile(x_ref, xp_ref, xn_ref, sh_ref, sc_ref, g_ref, seq_len)
    half = tm // 2
    gate_ref[:half, :] = _gelu(jnp.dot(pieces[0], w_ref[:, :D_RNN],
                                       preferred_element_type=F32)).astype(BF16)
    gate_ref[half:, :] = _gelu(jnp.dot(pieces[1][:half], w_ref[:, :D_RNN],
                                       preferred_element_type=F32)).astype(BF16)
    cw = PROJ_COLS
    for c0 in range(0, D_RNN, cw):
        cols = slice(c0, min(c0 + cw, D_RNN))
        wcols = slice(D_RNN + cols.start, D_RNN + cols.stop)
        z_all = _dot_rows(pieces, w_ref[:, wcols])
        y = _dwconv(z_all, tm, cw_ref[:, cols], cb_ref[:, cols], 2, valid)
        kg = chunk // RG_GROUPS
        for c in range((cols.stop - cols.start) // LANES):
            yc = y[:, c * LANES:(c + 1) * LANES]
            for r0 in range(0, tm, chunk):
                for gq in range(RG_GROUPS):
                    xl_ref[c0 // LANES + c, pl.ds(r0 + gq, kg, stride=RG_GROUPS), :] = (
                        yc[r0 + gq * kg:r0 + (gq + 1) * kg])


def _in_odd(x, mods, mod_row, g, w, layer, cw, cb, seq_len, chunk):
    n_rows = x.shape[0]
    tm = ODD_TM
    const = lambda shape: _single_buffered(shape, lambda i: (0,) * len(shape))
    return pl.pallas_call(
        functools.partial(_in_odd_kernel, seq_len=seq_len, chunk=chunk),
        grid=(n_rows // tm,),
        in_specs=_halo_specs(tm, n_rows) + [
            _mod_spec(0, mod_row), _mod_spec(1, mod_row),
            const((1, D_MODEL)), _layer_spec(w, layer), const(cw.shape), const(cb.shape),
        ],
        out_specs=[pl.BlockSpec((tm, D_RNN), lambda i: (i, 0)),
                   pl.BlockSpec((RNN_TILES, tm, LANES), lambda i: (0, i, 0))],
        out_shape=[jax.ShapeDtypeStruct((n_rows, D_RNN), BF16),
                   jax.ShapeDtypeStruct((RNN_TILES, n_rows, LANES), F32)],
        compiler_params=_cparams("arbitrary"),
        name="in_proj_odd",
    )(x, x, x, mods, mods, g, w, cw, cb)


RG_T = 512
RG_NCH = SEQ // RG_T
RG_GROUPS = 8
RG_PLANES_PER_PASS = 6
SQRT_GUARD = 1e-30
RG_WIN = 3 * LANES


def _rg_window_start(j):
    return min(max(LANES * (j - 1), 0), D_RNN - RG_WIN)


def _rg_chunk(x_ref, w_ref, ba_ref, bx_ref, lam_ref, a_pl, b_pl, carry_ref, reverse, emit):
    t_rows = x_ref.shape[1]
    kg = t_rows // RG_GROUPS
    xb = [x_ref[j].astype(BF16) for j in range(RNN_TILES)]
    for j in range(RNN_TILES):
        tile = slice(j * LANES, (j + 1) * LANES)
        wt = _rg_window_start(j) // LANES
        pre = jnp.dot(jnp.concatenate(xb[wt:wt + RG_WIN // LANES], axis=1), w_ref[j],
                      preferred_element_type=F32)
        lam = lam_ref[:, tile]
        softplus_neg = jnp.maximum(-lam, 0.0) + jnp.log1p(jnp.exp(-jnp.abs(lam)))
        th_r = jnp.tanh(pre[:, :LANES] + ba_ref[:, tile])
        th_i = jnp.tanh(pre[:, LANES:] + bx_ref[:, tile])
        c3 = (-0.5 * RG_C * math.log2(math.e)) * softplus_neg
        av = jnp.exp2(c3 + c3 * th_r)
        hx = 0.5 * x_ref[j]
        y = 1.0 - av * av
        a_pl[j, 0:t_rows, :] = av
        b_pl[j, 0:t_rows, :] = (y * lax.rsqrt(jnp.maximum(y, SQRT_GUARD))) * (hx + hx * th_i)

    def rows_k(pl_ref, j, k):
        return pl_ref[j, k * RG_GROUPS:(k + 1) * RG_GROUPS, :]

    order = list(range(kg))[::-1] if reverse else list(range(kg))
    groups = list(range(RG_GROUPS))[::-1] if reverse else list(range(RG_GROUPS))
    for j0 in range(0, RNN_TILES, RG_PLANES_PER_PASS):
        planes = range(j0, min(j0 + RG_PLANES_PER_PASS, RNN_TILES))
        big_a, big_b = {}, {}
        for n, k in enumerate(order):
            for j in planes:
                ak, bk = rows_k(a_pl, j, k), rows_k(b_pl, j, k)
                if n == 0:
                    big_a[j], big_b[j] = ak, bk
                else:
                    big_b[j] = ak * big_b[j] + bk
                    big_a[j] = ak * big_a[j]
        h = {}
        for j in planes:
            c = carry_ref[j, 0:1, :]
            rows = [None] * RG_GROUPS
            for gq in groups:
                rows[gq] = c
                c = big_a[j][gq:gq + 1] * c + big_b[j][gq:gq + 1]
            carry_ref[j, 0:1, :] = c
            h[j] = jnp.concatenate(rows, axis=0)
        for k in order:
            for j in planes:
                h[j] = rows_k(a_pl, j, k) * h[j] + rows_k(b_pl, j, k)
                b_pl[j, k * RG_GROUPS:(k + 1) * RG_GROUPS, :] = h[j]
        for j in planes:
            emit(j, jnp.concatenate(
                [b_pl[j, pl.ds(gq, kg, stride=RG_GROUPS), :] for gq in range(RG_GROUPS)], axis=0))


def _rg_bwd_kernel(xc_ref, xl_ref, w_ref, ba_ref, bx_ref, lam_ref, oc_ref, ol_ref,
                   a_pl, b_pl, carry_ref):
    def emitter(o_ref):
        def emit(j, hcur):
            o_ref[:, j * LANES:(j + 1) * LANES] = hcur.astype(BF16)
        return emit

    @pl.when(pl.program_id(1) == 0)
    def _():
        carry_ref[...] = jnp.zeros_like(carry_ref)
        _rg_chunk(xc_ref, w_ref, ba_ref, bx_ref, lam_ref, a_pl, b_pl, carry_ref, True,
                  emitter(oc_ref))

    @pl.when(pl.program_id(1) > 0)
    def _():
        _rg_chunk(xl_ref, w_ref, ba_ref, bx_ref, lam_ref, a_pl, b_pl, carry_ref, True,
                  emitter(ol_ref))


def _rg_fwd_kernel(xc_ref, xl_ref, w_ref, ba_ref, bx_ref, lam_ref, gc_ref, gl_ref, hc_ref, hl_ref,
                   rc_ref, rl_ref, gtc_ref, gtl_ref, wo_ref, oc_ref, ol_ref,
                   a_pl, b_pl, carry_ref, y_ref):
    def run(x_ref, g_ref, hb_ref, res_ref, gt_ref, o_ref):
        t_rows = x_ref.shape[1]

        def emit(j, hcur):
            tile = slice(j * LANES, (j + 1) * LANES)
            y_ref[0:t_rows, tile] = (g_ref[:, tile].astype(F32)
                                     * (hcur + hb_ref[:, tile].astype(F32))).astype(BF16)

        _rg_chunk(x_ref, w_ref, ba_ref, bx_ref, lam_ref, a_pl, b_pl, carry_ref, False, emit)
        acc = jnp.dot(y_ref[0:t_rows, :], wo_ref[...], preferred_element_type=F32)
        o_ref[...] = res_ref[...] + gt_ref[0] * acc

    @pl.when(pl.program_id(1) == 0)
    def _():
        carry_ref[...] = jnp.zeros_like(carry_ref)
        run(xc_ref, gc_ref, hc_ref, rc_ref, gtc_ref, oc_ref)

    @pl.when(pl.program_id(1) > 0)
    def _():
        run(xl_ref, gl_ref, hl_ref, rl_ref, gtl_ref, ol_ref)


def _rg_scan(xc, xl, w, ba, bx, lam, reverse, fwd_args=None):
    if reverse:
        chunk = lambda s: RG_NCH - jnp.maximum(s, 1)
    else:
        chunk = lambda s: jnp.maximum(s - 1, 0)
    ctx_spec = lambda d: pl.BlockSpec((None, CTX_LEN, d), lambda b, s: (b, 0, 0))
    lat_spec = lambda d: pl.BlockSpec((None, RG_T, d), lambda b, s: (b, chunk(s), 0))
    const = lambda shape: _single_buffered(shape, lambda b, s: (0,) * len(shape))
    in_specs = [pl.BlockSpec((RNN_TILES, CTX_LEN, LANES), lambda b, s: (0, b, 0)),
                pl.BlockSpec((RNN_TILES, RG_T, LANES), lambda b, s: (0, b * RG_NCH + chunk(s), 0)),
                const(w.shape), const(ba.shape), const(bx.shape), const(lam.shape)]
    args = [xc, xl, w, ba, bx, lam]
    scratch = [pltpu.VMEM((RNN_TILES, RG_T, LANES), F32),
               pltpu.VMEM((RNN_TILES, RG_T, LANES), F32),
               pltpu.VMEM((RNN_TILES, 8, LANES), F32)]
    if reverse:
        body, name, width, dtype = _rg_bwd_kernel, "rglru_bwd", D_RNN, BF16
    else:
        gate_c, gate_l, hb_c, hb_l, res_c, res_l, mods, (w_out, out_layer) = fwd_args
        in_specs += [ctx_spec(D_RNN), lat_spec(D_RNN), ctx_spec(D_RNN), lat_spec(D_RNN),
                     ctx_spec(D_MODEL), lat_spec(D_MODEL),
                     _mod_spec(2, lambda b: CTX_MOD_ROW), _mod_spec(2, lambda b: b),
                     _layer_spec(w_out, out_layer)]
        args += [gate_c, gate_l, hb_c, hb_l, res_c, res_l, mods, mods, w_out]
        scratch += [pltpu.VMEM((RG_T, D_RNN), BF16)]
        body, name, width, dtype = _rg_fwd_kernel, "rglru_fwd", D_MODEL, F32
    return pl.pallas_call(
        body,
        grid=(BATCH, 1 + RG_NCH),
        in_specs=in_specs,
        out_specs=[ctx_spec(width), lat_spec(width)],
        out_shape=[jax.ShapeDtypeStruct((BATCH, CTX_LEN, width), dtype),
                   jax.ShapeDtypeStruct((BATCH, SEQ, width), dtype)],
        scratch_shapes=scratch,
        compiler_params=_cparams("arbitrary", "arbitrary"),
        name=name,
    )(*args)


def _rg_gate_weights(wa, wx):
    def window(w, j):
        ws = _rg_window_start(j)
        win = None
        for h in range(RG_HEADS):
            c0, c1 = max(h * RG_DH, j * LANES), min((h + 1) * RG_DH, (j + 1) * LANES)
            if c0 >= c1:
                continue
            r0 = h * RG_DH - ws
            assert 0 <= r0 and r0 + RG_DH <= RG_WIN
            blk = jnp.pad(w[h, :, c0 - h * RG_DH:c1 - h * RG_DH],
                          ((r0, RG_WIN - RG_DH - r0), (c0 - j * LANES, (j + 1) * LANES - c1)))
            win = blk if win is None else win + blk
        return win

    wins = [jnp.concatenate([window(wa, j), window(wx, j)], axis=1) for j in range(RNN_TILES)]
    return (0.5 * jnp.stack(wins)).astype(BF16)


def _col_major(layer):
    return layer < DEPTH and layer % 2 == 1 and (layer // 2) % 2 == 1


def kernel(x, c, ctx, c_ctx, w_ada, b_ada, norm_mix_g, norm_ffn_g, w_in_even, w_out_even, hy_conv_w, hy_conv_b, hy_f1_w, hy_f1_b, hy_f2_w, hy_f2_b, hy_f3_w, hy_f3_b, hy_sin_freq, hy_skip, sgu_ln_g, sgu_w, sgu_b, w_in_odd, rg_conv_w, rg_conv_b, rg_wa, rg_ba, rg_wx, rg_bx, rg_lam, w_out_odd, w_ffn_in, w_ffn_out, final_norm_g):
    mods_all = _ada_mods(c, c_ctx, w_ada, b_ada)
    xs = x.reshape(BATCH * SEQ, D_MODEL)
    cs = ctx.reshape(BATCH * CTX_LEN, D_MODEL)
    lat_row_even = lambda i: i // (SEQ // EVEN_TM)
    lat_row_odd = lambda i: i // (SEQ // ODD_TM)
    lat_row_ffn = lambda i: i // (SEQ // FFN_TM)
    ctx_row = lambda i: CTX_MOD_ROW
    n_tiles = W_A // LANES
    w_in_even, w_out_even, w_in_odd, w_out_odd, w_ffn_in, w_ffn_out = (
        w.astype(BF16) for w in (w_in_even, w_out_even, w_in_odd, w_out_odd, w_ffn_in, w_ffn_out))
    for l in range(DEPTH):
        run_ctx = l < DEPTH - 1
        is_rec = l % 2 == 1
        i = l // 2
        mods = mods_all[l].reshape(MOD_ROWS, 1, N_MOD * D_MODEL)
        g_mix = norm_mix_g[l].reshape(1, D_MODEL)
        mix_l = mix_c = None
        if is_rec:
            cw, cb = rg_conv_w[i], rg_conv_b[i].reshape(1, D_RNN)
            gate_l, xl = _in_odd(xs, mods, lat_row_odd, g_mix, w_in_odd, i, cw, cb, SEQ, RG_T)
            gate_c, xc = _in_odd(cs, mods, ctx_row, g_mix, w_in_odd, i, cw, cb, CTX_LEN, CTX_LEN)
            gate_l = gate_l.reshape(BATCH, SEQ, D_RNN)
            gate_c = gate_c.reshape(BATCH, CTX_LEN, D_RNN)
            row = lambda v: v.reshape(1, D_RNN)
            half = lambda v: 0.5 * row(v)
            hb_c, hb_l = _rg_scan(xc, xl, _rg_gate_weights(rg_wa[i, 1], rg_wx[i, 1]),
                                  half(rg_ba[i, 1]), half(rg_bx[i, 1]), row(rg_lam[i, 1]), True)
            cs_new, xs = _rg_scan(
                xc, xl, _rg_gate_weights(rg_wa[i, 0], rg_wx[i, 0]),
                half(rg_ba[i, 0]), half(rg_bx[i, 0]), row(rg_lam[i, 0]), False,
                (gate_c, gate_l, hb_c, hb_l, cs.reshape(BATCH, CTX_LEN, D_MODEL),
                 xs.reshape(BATCH, SEQ, D_MODEL), mods, (w_out_odd, i)))
            xs = xs.reshape(BATCH * SEQ, D_MODEL)
            if run_ctx:
                cs = cs_new.reshape(BATCH * CTX_LEN, D_MODEL)
        else:
            cw, cb = hy_conv_w[i], hy_conv_b[i].reshape(1, HY_COLS)
            lng = sgu_ln_g[i].reshape(1, W_B)
            sgw = sgu_w[i].astype(BF16)
            sgb = jnp.repeat(sgu_b[i].T, SGU_DH, axis=1)
            skip = hy_skip[i].reshape(1, HY_ORDER * W_A)
            fargs = (hy_f1_w[i], hy_f1_b[i], hy_f2_w[i], hy_f2_b[i], hy_f3_w[i], hy_f3_b[i],
                     hy_sin_freq[i])
            kf = _filter_spectrum(SEQ, *fargs, skip)
            za, yb = _in_even(xs, mods, lat_row_even, g_mix, w_in_even, i, cw, cb, lng, sgw, sgb,
                              SEQ, True)
            za = za.reshape(HY_TILES, BATCH, SEQ, LANES)
            y1 = _longconv(za, 0, za, n_tiles, kf, 0)
            ya = _longconv(y1, 0, za, 2 * n_tiles, kf, n_tiles)
            mix_l = (ya, yb, (w_out_even, i), True)
            if run_ctx:
                kfc = _filter_spectrum(CTX_LEN, *fargs, skip)
                zc, ybc = _in_even(cs, mods, ctx_row, g_mix, w_in_even, i, cw, cb, lng, sgw, sgb,
                                   CTX_LEN, False)
                zc = zc.reshape(HY_TILES, BATCH, CTX_LEN, LANES)
                y1c = _ctx_conv(zc, 0, zc, n_tiles, kfc, 0)
                yac = _ctx_conv(y1c, 0, zc, 2 * n_tiles, kfc, n_tiles)
                mix_c = (yac, ybc, (w_out_even, i), False)
        g_ffn = norm_ffn_g[l].reshape(1, D_MODEL)
        fg = final_norm_g.reshape(1, D_MODEL)
        swap = _col_major(l) != _col_major(l + 1)
        xs = _ffn(xs, mods, lat_row_ffn, g_ffn, w_ffn_in, w_ffn_out, l, fg, l == DEPTH - 1,
                  mix_l, swap)
        if run_ctx:
            cs = _ffn(cs, mods, ctx_row, g_ffn, w_ffn_in, w_ffn_out, l, fg, False, mix_c)
    return xs.reshape(BATCH, SEQ, D_MODEL)
```

```python
import functools
import math

import numpy as np
import jax
import jax.numpy as jnp
from jax import lax
from jax.experimental import pallas as pl
from jax.experimental.pallas import tpu as pltpu

F32 = jnp.float32
BF16 = jnp.bfloat16

D_MODEL = 1024
BATCH = 4
SEQ = 4096
DEPTH = 4
GRID_W = 64
CTX_LEN = 256
N_MOD = 6
NORM_EPS = 1e-6
W_A = D_MODEL // 2
HY_ORDER = 2
HY_SHORT = 3
HY_BANDS = 16
HY_TARGET = 1e-2
HY_FAST_PCT = 0.3
HY_SLOW_PCT = 1.5
W_B = D_MODEL // 2
SGU_GROUPS = 4
SGU_DH = W_B // SGU_GROUPS
CHUNK = 128
D_RNN = ((4 * D_MODEL // 3 + 127) // 128) * 128
RG_HEADS = 16
RG_DH = D_RNN // RG_HEADS
RG_CONV = 4
RG_C = 8.0
D_FF = ((8 * D_MODEL // 3 + 255) // 256) * 256

VMEM_BYTES_V7X = 64 * 1024 * 1024
VMEM_LIMIT = VMEM_BYTES_V7X - 4 * 1024 * 1024
LANES = 128

FFT_N = 2 * SEQ
FFT_N1 = 64
FFT_N2 = 128
FFT_HALF_N1 = FFT_N1 // 2
FFT_PITCH = FFT_N2 + 8
CTX_N = 2 * CTX_LEN


def _cparams(*sem):
    return pltpu.CompilerParams(dimension_semantics=sem, vmem_limit_bytes=VMEM_LIMIT)


def _single_buffered(block_shape, index_map):
    return pl.BlockSpec(block_shape, index_map, pipeline_mode=pl.Buffered(1))


def _layer_spec(stack, layer):
    rest = stack.shape[1:]
    return _single_buffered((None,) + rest, lambda *_: (layer,) + (0,) * len(rest))


@functools.lru_cache(maxsize=None)
def _dft_tables_f32():
    n = FFT_N
    k1 = np.arange(FFT_N1)
    n2 = np.arange(FFT_N2)

    def angle(n1):
        m = (FFT_N2 * n1[None, None, :] + n2[:, None, None]) * k1[None, :, None]
        return 2.0 * np.pi * (m % n) / n

    a = angle(np.arange(FFT_HALF_N1))
    c, s = np.cos(a), np.sin(a)
    g = np.empty((FFT_N2, FFT_N1, 2, 2, FFT_HALF_N1))
    g[:, :, 0, 0], g[:, :, 0, 1] = c, s
    g[:, :, 1, 0], g[:, :, 1, 1] = -s, c
    g = g.reshape(FFT_N2, 2 * FFT_N1, 2 * FFT_HALF_N1)

    a = angle(np.arange(FFT_N1))
    gf = np.stack([np.cos(a), -np.sin(a)], axis=2).reshape(FFT_N2, 2 * FFT_N1, FFT_N1)

    a = np.transpose(angle(np.arange(FFT_HALF_N1)), (0, 2, 1))
    c, s = np.cos(a) / n, np.sin(a) / n
    h = np.empty((FFT_N2, 2, FFT_HALF_N1, FFT_N1, 2))
    h[:, 0, :, :, 0], h[:, 0, :, :, 1] = c, -s
    h[:, 1, :, :, 0], h[:, 1, :, :, 1] = s, c
    h = h.reshape(FFT_N2, 2 * FFT_HALF_N1, 2 * FFT_N1)

    a = 2.0 * np.pi * np.outer(n2, n2) / FFT_N2
    c, s = np.cos(a), np.sin(a)
    f2 = np.block([[c, s], [-s, c]])
    f2i = np.block([[c, -s], [s, c]])

    kk = np.arange(CTX_N)
    a = 2.0 * np.pi * np.outer(kk, np.arange(CTX_LEN)) / CTX_N
    c, s = np.cos(a), np.sin(a)
    fc = np.block([[c, s], [-s, c]])
    a = 2.0 * np.pi * np.outer(kk, np.arange(CTX_N)) / CTX_N
    fcf = np.concatenate([np.cos(a), -np.sin(a)], axis=0)
    a = 2.0 * np.pi * np.outer(np.arange(CTX_LEN), kk) / CTX_N
    c, s = np.cos(a) / CTX_N, np.sin(a) / CTX_N
    fci = np.block([[c, -s], [s, c]])

    tables = dict(g=g, gf=gf, h=h, f2=f2, f2i=f2i, fc=fc, fcf=fcf, fci=fci)
    return {k: np.asarray(v, dtype=np.float32) for k, v in tables.items()}


def _dft_tables():
    return {k: jnp.asarray(v).astype(BF16) for k, v in _dft_tables_f32().items()}


SEQ_TILE = 1024
N1_PER_TILE = SEQ_TILE // FFT_N2


def _scatter_n2(s_ref, n2, val):
    for p in range(s_ref.shape[0]):
        s_ref[p, pl.ds(n2, 2 * FFT_N1, stride=FFT_PITCH), :] = val[:, p * LANES:(p + 1) * LANES]


def _gather_n2(s_ref, n2):
    return jnp.concatenate(
        [s_ref[p, pl.ds(n2, 2 * FFT_N1, stride=FFT_PITCH), :] for p in range(s_ref.shape[0])],
        axis=1)


def _spectrum_rows(s_ref, k1):
    base = pl.multiple_of(k1 * (2 * FFT_PITCH), 8)
    return base, jnp.concatenate(
        [jnp.concatenate([s_ref[p, pl.ds(base + o, FFT_N2), :] for p in range(s_ref.shape[0])],
                         axis=1) for o in (0, FFT_PITCH)], axis=0)


HY_FFN = 64
FILT_ROWS = 1024
FILT_TILES = 2
HIGHEST = lax.Precision.HIGHEST


def _tap_rows(row, seq, n2_major):
    if n2_major:
        n2, n1 = row >> int(math.log2(FFT_N1)), row & (FFT_N1 - 1)
        bwd = n1 >= FFT_HALF_N1
        m = FFT_N2 * (n1 - FFT_HALF_N1) + n2
        fwd_pos = FFT_N2 * n1 + n2
    else:
        bwd = row >= seq
        m = row - seq
        fwd_pos = row
    pos = jnp.where(bwd, seq - m, fwd_pos).astype(F32)
    bwd_f = jnp.where(bwd, 1.0, 0.0)
    keep_f = jnp.where(bwd & (m == 0), 0.0, 1.0)
    return pos, bwd_f, keep_f


def _filter_kernel(ec_ref, f1w_ref, f1b_ref, f2w_ref, f2b_ref, sf_ref, w3f_ref, w3b_ref, b3f_ref,
                   b3b_ref, dl_ref, skip_ref, *rest, seq, n2_major):
    if n2_major:
        gf_ref, f2_ref, o_ref, hid_ref, k_ref, s_ref = rest
    else:
        fcf_ref, o_ref, hid_ref, k_ref = rest
    n_rows = 2 * seq
    rb = min(FILT_ROWS, n_rows)

    def rows_of(i):
        r0 = pl.multiple_of(i * rb, rb)
        return r0, _tap_rows(r0 + lax.broadcasted_iota(jnp.int32, (rb, 1), 0), seq, n2_major)

    @pl.when(pl.program_id(0) == 0)
    def _():
        sf = sf_ref[...]

        def hidden(i, carry):
            r0, (pos, _, _) = rows_of(i)
            t = pos * (1.0 / (seq - 1))
            w = pos * (2.0 * math.pi / seq)
            emb = jnp.sin(w * ec_ref[0:1] + ec_ref[1:2]) + t * ec_ref[2:3]
            h = jnp.sin(sf * (jnp.dot(emb, f1w_ref[...], precision=HIGHEST,
                                      preferred_element_type=F32) + f1b_ref[...]))
            h = jnp.sin(sf * (jnp.dot(h.astype(BF16), f2w_ref[...].astype(BF16),
                                      preferred_element_type=F32) + f2b_ref[...]))
            hid_ref[pl.ds(r0, rb), :] = h
            return carry

        lax.fori_loop(0, n_rows // rb, hidden, 0)

    def taps(i, ssq):
        r0, (pos, bwd_f, keep_f) = rows_of(i)
        h = hid_ref[pl.ds(r0, rb), :].astype(BF16)
        neg_t = -(pos * (1.0 / (seq - 1)))
        ks = []
        for c in range(FILT_TILES):
            kf = jnp.dot(h, w3f_ref[c].astype(BF16), preferred_element_type=F32) + b3f_ref[c]
            kb = jnp.dot(h, w3b_ref[c].astype(BF16), preferred_element_type=F32) + b3b_ref[c]
            ks.append((kf + bwd_f * (kb - kf)) * jnp.exp(neg_t * dl_ref[c]) * keep_f)
        k = jnp.concatenate(ks, axis=1)
        k_ref[pl.ds(r0, rb), :] = k
        return ssq + jnp.sum(k * k, axis=0, keepdims=True)

    ssq = lax.fori_loop(0, n_rows // rb, taps, jnp.zeros((1, FILT_TILES * LANES), F32))
    scale = lax.rsqrt(ssq + NORM_EPS)
    skip = skip_ref[...]

    if n2_major:
        def stage1(n2, carry):
            x = k_ref[pl.ds(pl.multiple_of(n2 * FFT_N1, FFT_N1), FFT_N1), :] * scale
            _scatter_n2(s_ref, n2, jnp.dot(gf_ref[n2], x.astype(BF16),
                                           preferred_element_type=F32))
            return carry

        lax.fori_loop(0, FFT_N2, stage1, 0, unroll=8)

        def stage2(k1, carry):
            _, r = _spectrum_rows(s_ref, k1)
            z = jnp.dot(f2_ref[...], r.astype(BF16), preferred_element_type=F32)
            row = pl.multiple_of(k1 * (2 * FFT_N2), 2 * FFT_N2)
            o_ref[pl.ds(row, FFT_N2), :] = (z[:FFT_N2] + skip).astype(BF16)
            o_ref[pl.ds(row + FFT_N2, FFT_N2), :] = z[FFT_N2:].astype(BF16)
            return carry

        lax.fori_loop(0, FFT_N1, stage2, 0, unroll=8)
    else:
        z = jnp.dot(fcf_ref[...], (k_ref[...] * scale).astype(BF16), preferred_element_type=F32)
        o_ref[:n_rows, :] = (z[:n_rows] + skip).astype(BF16)
        o_ref[n_rows:, :] = z[n_rows:].astype(BF16)


def _filter_spectrum(seq, f1_w, f1_b, f2_w, f2_b, f3_w, f3_b, sin_freq, skip):
    n2_major = seq == SEQ
    t = _dft_tables()
    pad = lambda a, r, c: jnp.pad(a, ((0, r - a.shape[0]), (0, c - a.shape[1])))
    f = jnp.linspace(1e-4, HY_BANDS - 1, HY_BANDS, dtype=F32)
    zeros = lambda n: jnp.zeros((n,), F32)
    ones = lambda n: jnp.ones((n,), F32)
    rest = LANES - 1 - 2 * HY_BANDS
    ec = jnp.stack([
        jnp.concatenate([zeros(1), f, f, zeros(rest)]),
        jnp.concatenate([zeros(1), (math.pi / 2) * ones(HY_BANDS), math.pi * ones(HY_BANDS),
                         zeros(rest)]),
        jnp.concatenate([ones(1), zeros(LANES - 1)]),
    ] + [zeros(LANES)] * 5)
    row = lambda v: pad(v.reshape(1, -1), 1, LANES)
    n_tiles = HY_ORDER * W_A // LANES
    w3 = f3_w.reshape(HY_FFN, HY_ORDER, 2, W_A)
    b3 = f3_b.reshape(HY_ORDER, 2, W_A)
    tiles_w = lambda d: jnp.pad(
        w3[:, :, d, :].reshape(HY_FFN, n_tiles, LANES).transpose(1, 0, 2),
        ((0, 0), (0, LANES - HY_FFN), (0, 0)))
    tiles_b = lambda d: b3[:, d, :].reshape(n_tiles, 1, LANES)
    deltas = jnp.abs(jnp.linspace(math.log(HY_TARGET) / HY_SLOW_PCT,
                                  math.log(HY_TARGET) / HY_FAST_PCT, W_A, dtype=F32))
    dl = jnp.tile(deltas, HY_ORDER).reshape(n_tiles, 1, LANES)
    const = lambda shape: _single_buffered(shape, lambda j: (0,) * len(shape))
    tile = lambda shape: pl.BlockSpec((FILT_TILES,) + shape, lambda j: (j, 0, 0))
    ct = FILT_TILES * LANES
    args = [ec, pad(f1_w, LANES, LANES), row(f1_b), pad(f2_w, LANES, LANES), row(f2_b),
            row(sin_freq), tiles_w(0), tiles_w(1), tiles_b(0), tiles_b(1), dl, skip]
    in_specs = [const((8, LANES)), const((LANES, LANES)), const((1, LANES)),
                const((LANES, LANES)), const((1, LANES)), const((1, LANES)),
                tile((LANES, LANES)), tile((LANES, LANES)), tile((1, LANES)), tile((1, LANES)),
                tile((1, LANES)), pl.BlockSpec((1, ct), lambda j: (0, j))]
    scratch = [pltpu.VMEM((2 * seq, LANES), F32), pltpu.VMEM((2 * seq, ct), F32)]
    if n2_major:
        args += [t["gf"], t["f2"]]
        in_specs += [const(t["gf"].shape), const(t["f2"].shape)]
        scratch += [pltpu.VMEM((FILT_TILES, 2 * FFT_N1 * FFT_PITCH, LANES), F32)]
    else:
        args += [t["fcf"]]
        in_specs += [const(t["fcf"].shape)]
    return pl.pallas_call(
        functools.partial(_filter_kernel, seq=seq, n2_major=n2_major),
        grid=(n_tiles // FILT_TILES,),
        in_specs=in_specs,
        out_specs=pl.BlockSpec((4 * seq, ct), lambda j: (0, j)),
        out_shape=jax.ShapeDtypeStruct((4 * seq, HY_ORDER * W_A), BF16),
        scratch_shapes=scratch,
        compiler_params=_cparams("arbitrary"),
        name="hyena_filter" if n2_major else "hyena_ctx_filter",
    )(*args)


LC_OUT_CHUNKS = 4
LC_N2_PER_CHUNK = FFT_N2 // LC_OUT_CHUNKS


def _longconv_kernel(v_ref, gate_ref, kf_ref, g_ref, h_ref, f2_ref, f2i_ref, o_ref, s_ref, *,
                     flat_in, flat_out):
    ct = v_ref.shape[2]
    t = pl.program_id(1)

    @pl.when(t == 0)
    def _():
        def stage1(n2, carry):
            if flat_in:
                r0 = pl.multiple_of(n2 * FFT_HALF_N1, FFT_HALF_N1)
                blk = [v_ref[b, pl.ds(r0, FFT_HALF_N1), :] for b in range(BATCH)]
            else:
                r0 = pl.multiple_of(n2 * N1_PER_TILE, N1_PER_TILE)
                blk = [jnp.concatenate([v_ref[b, pl.ds(i * SEQ_TILE + r0, N1_PER_TILE), :]
                                        for i in range(SEQ // SEQ_TILE)], axis=0)
                       for b in range(BATCH)]
            x = jnp.concatenate([jnp.concatenate([blk[0], blk[2]], axis=1),
                                 jnp.concatenate([blk[1], blk[3]], axis=1)], axis=0)
            _scatter_n2(s_ref, n2, jnp.dot(g_ref[n2], x.astype(BF16), preferred_element_type=F32))
            return carry

        lax.fori_loop(0, FFT_N2, stage1, 0, unroll=16)

        def stage2(k1, carry):
            base, r = _spectrum_rows(s_ref, k1)
            z = jnp.dot(f2_ref[...], r.astype(BF16), preferred_element_type=F32)
            kf = kf_ref[pl.ds(pl.multiple_of(k1 * (2 * FFT_N2), 2 * FFT_N2), 2 * FFT_N2), :]
            kf = kf.astype(F32)
            kr = jnp.concatenate([kf[:FFT_N2]] * 2, axis=1)
            ki = jnp.concatenate([kf[FFT_N2:]] * 2, axis=1)
            zr, zi = z[:FFT_N2], z[FFT_N2:]
            p = jnp.concatenate([zr * kr - zi * ki, zr * ki + zi * kr], axis=0)
            q = jnp.dot(f2i_ref[...], p.astype(BF16), preferred_element_type=F32)
            for p in range(2):
                s_ref[p, pl.ds(base, FFT_N2), :] = q[:FFT_N2, p * ct:(p + 1) * ct]
                s_ref[p, pl.ds(base + FFT_PITCH, FFT_N2), :] = q[FFT_N2:, p * ct:(p + 1) * ct]
            return carry

        lax.fori_loop(0, FFT_N1, stage2, 0, unroll=8)

    def stage3(j, carry):
        n2 = t * LC_N2_PER_CHUNK + j
        rq = _gather_n2(s_ref, n2)
        y = jnp.dot(h_ref[n2], rq.astype(BF16), preferred_element_type=F32)
        r0 = pl.multiple_of(j * N1_PER_TILE, N1_PER_TILE)
        for b in range(BATCH):
            ri, pair = b % 2, b // 2
            gated = [gate_ref[b, i, pl.ds(r0, N1_PER_TILE), :]
                     * y[ri * FFT_HALF_N1 + i * N1_PER_TILE:ri * FFT_HALF_N1 + (i + 1) * N1_PER_TILE,
                         pair * ct:(pair + 1) * ct]
                     for i in range(SEQ // SEQ_TILE)]
            if flat_out:
                o_ref[b, pl.ds(pl.multiple_of(j * FFT_HALF_N1, FFT_HALF_N1), FFT_HALF_N1), :] = (
                    jnp.concatenate(gated, axis=0).astype(o_ref.dtype))
            else:
                for i in range(SEQ // SEQ_TILE):
                    o_ref[b, i, pl.ds(r0, N1_PER_TILE), :] = gated[i]
        return carry

    lax.fori_loop(0, LC_N2_PER_CHUNK, stage3, 0, unroll=16)


def _longconv(v, v_tile, gate, gate_tile, kf, kf_col, flat_in=False, flat_out=False):
    t = _dft_tables()
    ct = LANES
    n_seq_tiles = SEQ // SEQ_TILE
    rows = SEQ_TILE // LC_OUT_CHUNKS
    gate = gate.reshape(gate.shape[0], BATCH, n_seq_tiles, SEQ_TILE, ct)
    v_index = lambda j, i: (v_tile + j, 0, 0, 0)
    if flat_in:
        v_spec = pl.BlockSpec((None, BATCH, SEQ, ct), v_index)
    else:
        v_spec = _single_buffered((None, BATCH, SEQ, ct), v_index)
    if flat_out:
        out_spec = pl.BlockSpec((None, BATCH, SEQ // LC_OUT_CHUNKS, ct), lambda j, i: (j, 0, i, 0))
        out_shape = jax.ShapeDtypeStruct((W_A // ct, BATCH, SEQ, ct), BF16)
    else:
        out_spec = pl.BlockSpec((None, BATCH, n_seq_tiles, rows, ct),
                                lambda j, i: (j, 0, 0, i, 0))
        out_shape = jax.ShapeDtypeStruct((W_A // ct, BATCH, n_seq_tiles, SEQ_TILE, ct), F32)
    out = pl.pallas_call(
        functools.partial(_longconv_kernel, flat_in=flat_in, flat_out=flat_out),
        grid=(W_A // ct, LC_OUT_CHUNKS),
        in_specs=[
            v_spec,
            pl.BlockSpec((None, BATCH, n_seq_tiles, rows, ct),
                         lambda j, i: (gate_tile + j, 0, 0, i, 0)),
            pl.BlockSpec((2 * FFT_N, ct), lambda j, i: (0, kf_col + j)),
            _single_buffered(t["g"].shape, lambda j, i: (0, 0, 0)),
            _single_buffered(t["h"].shape, lambda j, i: (0, 0, 0)),
            _single_buffered(t["f2"].shape, lambda j, i: (0, 0)),
            _single_buffered(t["f2i"].shape, lambda j, i: (0, 0)),
        ],
        out_specs=out_spec,
        out_shape=out_shape,
        scratch_shapes=[pltpu.VMEM((2, 2 * FFT_N1 * FFT_PITCH, ct), F32)],
        compiler_params=_cparams("arbitrary", "arbitrary"),
        name="hyena_longconv",
    )(v, gate, kf, t["g"], t["h"], t["f2"], t["f2i"])
    return out.reshape(W_A // ct, BATCH, SEQ, ct)


def _ctx_conv_kernel(v_ref, gate_ref, kf_ref, fc_ref, fci_ref, o_ref):
    ct = v_ref.shape[2]
    x = jnp.concatenate([jnp.concatenate([v_ref[0], v_ref[2]], axis=1),
                         jnp.concatenate([v_ref[1], v_ref[3]], axis=1)], axis=0)
    z = jnp.dot(fc_ref[...], x.astype(BF16), preferred_element_type=F32)
    kf = kf_ref[...].astype(F32)
    kr = jnp.concatenate([kf[:CTX_N]] * 2, axis=1)
    ki = jnp.concatenate([kf[CTX_N:]] * 2, axis=1)
    zr, zi = z[:CTX_N], z[CTX_N:]
    p = jnp.concatenate([zr * kr - zi * ki, zr * ki + zi * kr], axis=0)
    y = jnp.dot(fci_ref[...], p.astype(BF16), preferred_element_type=F32)
    for b in range(BATCH):
        ri, pair = b % 2, b // 2
        o_ref[b] = gate_ref[b] * y[ri * CTX_LEN:(ri + 1) * CTX_LEN, pair * ct:(pair + 1) * ct]


def _ctx_conv(v, v_tile, gate, gate_tile, kf, kf_col):
    t = _dft_tables()
    ct = LANES
    return pl.pallas_call(
        _ctx_conv_kernel,
        grid=(W_A // ct,),
        in_specs=[
            pl.BlockSpec((None, BATCH, CTX_LEN, ct), lambda j: (v_tile + j, 0, 0, 0)),
            pl.BlockSpec((None, BATCH, CTX_LEN, ct), lambda j: (gate_tile + j, 0, 0, 0)),
            pl.BlockSpec((2 * CTX_N, ct), lambda j: (0, kf_col + j)),
            pl.BlockSpec(t["fc"].shape, lambda j: (0, 0)),
            pl.BlockSpec(t["fci"].shape, lambda j: (0, 0)),
        ],
        out_specs=pl.BlockSpec((None, BATCH, CTX_LEN, ct), lambda j: (j, 0, 0, 0)),
        out_shape=jax.ShapeDtypeStruct((W_A // ct, BATCH, CTX_LEN, ct), F32),
        compiler_params=_cparams("arbitrary"),
        name="hyena_ctx_conv",
    )(v, gate, kf, t["fc"], t["fci"])


MOD_ROWS = 8
CTX_MOD_ROW = BATCH


def _ada_kernel(c_ref, w_ref, b_ref, o_ref):
    cv = c_ref[...]
    s = cv * jax.nn.sigmoid(cv)
    o_ref[0] = jnp.dot(s.astype(BF16), w_ref[0].astype(BF16),
                       preferred_element_type=F32) + b_ref[0]


def _ada_mods(c, c_ctx, w_ada, b_ada):
    cv = jnp.concatenate(
        [c, c_ctx[None], jnp.zeros((MOD_ROWS - BATCH - 1, D_MODEL), F32)], axis=0)
    n = N_MOD * D_MODEL
    tn = n // 4
    return pl.pallas_call(
        _ada_kernel,
        grid=(DEPTH, n // tn),
        in_specs=[
            pl.BlockSpec((MOD_ROWS, D_MODEL), lambda l, j: (0, 0)),
            pl.BlockSpec((1, D_MODEL, tn), lambda l, j: (l, 0, j)),
            pl.BlockSpec((1, 1, tn), lambda l, j: (l, 0, j)),
        ],
        out_specs=pl.BlockSpec((1, MOD_ROWS, tn), lambda l, j: (l, 0, j)),
        out_shape=jax.ShapeDtypeStruct((DEPTH, MOD_ROWS, n), F32),
        compiler_params=_cparams("arbitrary", "arbitrary"),
        name="ada_mods",
    )(cv, w_ada, b_ada.reshape(DEPTH, 1, n))


def _mod_spec(m, row_fn):
    return pl.BlockSpec((1, 1, D_MODEL), lambda i, *_: (row_fn(i), 0, m))


def _rms_mod(x, g, shift, scale):
    ms = jnp.mean(x * x, axis=-1, keepdims=True)
    return (x * lax.rsqrt(ms + NORM_EPS)) * (g * (1.0 + scale)) + shift


GELU_C = math.sqrt(2.0 / math.pi)


def _gelu(x):
    hx = 0.5 * x
    return hx + hx * jnp.tanh(x * (GELU_C + (GELU_C * 0.044715) * (x * x)))


HALO = 8
PROJ_COLS = 2 * LANES


def _dwconv(z_all, tm, w, b, left, valid):
    n = z_all.shape[0]
    y = b + w[left:left + 1] * z_all[:tm]
    for k in range(w.shape[0]):
        d = k - left
        if d == 0:
            continue
        s = pltpu.roll(z_all, (-d) % n, 0)[:tm]
        if valid is not None:
            s = s * valid(d)
        y = y + s * w[k:k + 1]
    return y


def _halo_specs(tm, n_rows):
    per = tm // HALO
    last = n_rows // HALO - 1
    return [
        pl.BlockSpec((tm, D_MODEL), lambda i: (i, 0)),
        pl.BlockSpec((HALO, D_MODEL), lambda i: (jnp.maximum(i * per - 1, 0), 0)),
        pl.BlockSpec((HALO, D_MODEL), lambda i: (jnp.minimum((i + 1) * per, last), 0)),
    ]


def _normed_tile(x_ref, xp_ref, xn_ref, sh_ref, sc_ref, g_ref, seq_len):
    tm = x_ref.shape[0]
    g, shift, scale = g_ref[...], sh_ref[0], sc_ref[0]
    h0 = _rms_mod(x_ref[:tm // 2, :], g, shift, scale)
    h = _rms_mod(x_ref[tm // 2:, :], g, shift, scale)
    hn = _rms_mod(xn_ref[...], g, shift, scale)
    hp = _rms_mod(xp_ref[...], g, shift, scale)
    if seq_len >= tm:
        r0 = pl.program_id(0) * tm
        hp = hp * jnp.where((r0 & (seq_len - 1)) == 0, 0.0, 1.0)
        hn = hn * jnp.where(((r0 + tm) & (seq_len - 1)) == 0, 0.0, 1.0)
        valid = None
    else:
        hp, hn = jnp.zeros_like(hp), jnp.zeros_like(hn)
        pos = lax.broadcasted_iota(jnp.int32, (tm, 1), 0) & (seq_len - 1)
        valid = lambda d: jnp.where((pos + d >= 0) & (pos + d < seq_len), 1.0, 0.0)
    return [h0.astype(BF16), jnp.concatenate([h, hn, hp], axis=0).astype(BF16)], valid


def _dot_rows(pieces, w):
    return jnp.concatenate([jnp.dot(p, w, preferred_element_type=F32) for p in pieces], axis=0)


EVEN_TM = SEQ_TILE
HY_COLS = 3 * W_A
HY_TILES = HY_COLS // LANES


def _in_even_kernel(x_ref, xp_ref, xn_ref, sh_ref, sc_ref, g_ref, w_ref, cw_ref, cb_ref,
                    lng_ref, sgw_ref, sgb_ref, za_ref, yb_ref, *, seq_len, n2_major):
    tm = x_ref.shape[0]
    pieces, valid = _normed_tile(x_ref, xp_ref, xn_ref, sh_ref, sc_ref, g_ref, seq_len)
    h = jnp.concatenate([pieces[0], pieces[1][:tm // 2]], axis=0)
    cw = PROJ_COLS
    for cc in range(HY_COLS // cw):
        cols = slice(cc * cw, (cc + 1) * cw)
        z_all = _dot_rows(pieces, w_ref[:, cols])
        y = _dwconv(z_all, tm, cw_ref[:, cols], cb_ref[:, cols], 1, valid)
        for c in range(cw // LANES):
            tile = cc * (cw // LANES) + c
            yc = y[:, c * LANES:(c + 1) * LANES]
            if n2_major:
                for j in range(N1_PER_TILE):
                    za_ref[tile, pl.ds(j, FFT_N2, stride=N1_PER_TILE), :] = (
                        yc[j * FFT_N2:(j + 1) * FFT_N2])
            else:
                za_ref[tile] = yc
    u = _gelu(jnp.dot(h, w_ref[:, HY_COLS:HY_COLS + W_B], preferred_element_type=F32))
    vb = _gelu(jnp.dot(h, w_ref[:, HY_COLS + W_B:], preferred_element_type=F32))
    vc = vb - jnp.mean(vb, axis=-1, keepdims=True)
    vn = vc * lax.rsqrt(jnp.mean(vc * vc, axis=-1, keepdims=True) + NORM_EPS) * lng_ref[...]
    vn = vn.astype(BF16)
    n_chunks = tm // CHUNK
    for q in range(SGU_GROUPS):
        cols = slice(q * SGU_DH, (q + 1) * SGU_DH)
        rhs = jnp.concatenate(
            [vn[ch * CHUNK:(ch + 1) * CHUNK, cols] for ch in range(n_chunks)], axis=1)
        s_all = jnp.dot(sgw_ref[q], rhs, preferred_element_type=F32)
        for ch in range(n_chunks):
            rows = slice(ch * CHUNK, (ch + 1) * CHUNK)
            s = s_all[:, ch * SGU_DH:(ch + 1) * SGU_DH] + sgb_ref[:, cols]
            yb_ref[rows, cols] = (u[rows, cols] * s).astype(BF16)


def _in_even(x, mods, mod_row, g, w, layer, cw, cb, lng, sgw, sgb, seq_len, n2_major):
    n_rows = x.shape[0]
    tm = EVEN_TM
    za_shape = (HY_TILES, n_rows, LANES)
    za_spec = pl.BlockSpec((HY_TILES, tm, LANES), lambda i: (0, i, 0))
    const = lambda shape: _single_buffered(shape, lambda i: (0,) * len(shape))
    return pl.pallas_call(
        functools.partial(_in_even_kernel, seq_len=seq_len, n2_major=n2_major),
        grid=(n_rows // tm,),
        in_specs=_halo_specs(tm, n_rows) + [
            _mod_spec(0, mod_row), _mod_spec(1, mod_row),
            const((1, D_MODEL)), _layer_spec(w, layer), const(cw.shape), const(cb.shape),
            const(lng.shape), const(sgw.shape), const(sgb.shape),
        ],
        out_specs=[za_spec, pl.BlockSpec((tm, W_B), lambda i: (i, 0))],
        out_shape=[jax.ShapeDtypeStruct(za_shape, F32),
                   jax.ShapeDtypeStruct((n_rows, W_B), BF16)],
        compiler_params=_cparams("arbitrary"),
        name="in_proj_even",
    )(x, x, x, mods, mods, g, w, cw, cb, lng, sgw, sgb)


FFN_TM = SEQ_TILE
FFN_SUB = 256
GRID_H = SEQ // GRID_W
GT_ROWS = FFN_TM // GRID_W
GT_PITCH = GRID_W + 8


def _even_mix(ya_ref, yb_ref, w_ref, n2_major):
    parts = []
    for c in range(W_A // LANES):
        if n2_major:
            yac = jnp.concatenate(
                [ya_ref[c, pl.ds(j, FFT_N2, stride=N1_PER_TILE), :]
                 for j in range(N1_PER_TILE)], axis=0)
        else:
            yac = ya_ref[c]
        parts.append(yac.astype(BF16))
    y = jnp.concatenate(parts + [yb_ref[...]], axis=1)
    return jnp.dot(y, w_ref[...], preferred_element_type=F32)


def _ffn_kernel(*refs, final_norm, mix, n2_major, swap_grid):
    refs = list(refs)
    x_ref, sh_ref, sc_ref, gt_ref, g_ref, w_in_ref, wo_ref, fg_ref = refs[:8]
    del refs[:8]
    if mix:
        ya_ref, yb_ref, gm_ref, wm_ref = refs[:4]
        del refs[:4]
    o_ref = refs.pop(0)
    x = x_ref[...]
    if mix:
        x = x + gm_ref[0] * _even_mix(ya_ref, yb_ref, wm_ref, n2_major)
    h = _rms_mod(x, g_ref[...], sh_ref[0], sc_ref[0]).astype(BF16)
    acts = []
    for c0 in range(0, D_FF, FFN_SUB):
        c1 = min(c0 + FFN_SUB, D_FF)
        hz = 0.5 * jnp.dot(h, w_in_ref[:, c0:c1], preferred_element_type=F32)
        z2 = jnp.dot(h, w_in_ref[:, D_FF + c0:D_FF + c1], preferred_element_type=F32)
        acts.append(((hz + hz * jnp.tanh(hz)) * z2).astype(BF16))
    y = x + gt_ref[0] * jnp.dot(jnp.concatenate(acts, axis=1), wo_ref[...],
                                preferred_element_type=F32)
    if final_norm:
        y = y * lax.rsqrt(jnp.mean(y * y, axis=-1, keepdims=True) + NORM_EPS) * fg_ref[...]
    if not swap_grid:
        o_ref[...] = y
        return
    s_ref = refs.pop(0)
    nt = D_MODEL // LANES
    for t in range(nt):
        for j in range(GT_ROWS):
            s_ref[t, j * GT_PITCH:j * GT_PITCH + GRID_W, :] = (
                y[j * GRID_W:(j + 1) * GRID_W, t * LANES:(t + 1) * LANES])
    for c in range(GRID_W):
        o_ref[c] = jnp.concatenate(
            [s_ref[t, pl.ds(c, GT_ROWS, stride=GT_PITCH), :] for t in range(nt)], axis=1)


def _ffn(x, mods, mod_row, g, w_in, w_out, layer, final_g, final_norm, mix=None,
         swap_grid=False):
    n_rows = x.shape[0]
    tm = FFN_TM
    const = lambda shape: _single_buffered(shape, lambda i: (0,) * len(shape))
    in_specs = [
        pl.BlockSpec((tm, D_MODEL), lambda i: (i, 0)),
        _mod_spec(3, mod_row), _mod_spec(4, mod_row), _mod_spec(5, mod_row),
        const((1, D_MODEL)), _layer_spec(w_in, layer), _layer_spec(w_out, layer),
        const((1, D_MODEL)),
    ]
    args = [x, mods, mods, mods, g, w_in, w_out, final_g]
    n2_major = False
    if mix is not None:
        ya, yb, (w_mix, mix_layer), n2_major = mix
        nt = W_A // LANES
        in_specs += [pl.BlockSpec((nt, tm, LANES), lambda i: (0, i, 0)),
                     pl.BlockSpec((tm, W_B), lambda i: (i, 0)),
                     _mod_spec(2, mod_row), _layer_spec(w_mix, mix_layer)]
        args += [ya.reshape(nt, n_rows, LANES), yb, mods, w_mix]
    scratch = []
    if swap_grid:
        per_seq = SEQ // tm
        out_spec = pl.BlockSpec((None, GRID_W, GT_ROWS, D_MODEL),
                                lambda i: (i // per_seq, 0, i % per_seq, 0))
        out_shape = jax.ShapeDtypeStruct((n_rows // SEQ, GRID_W, GRID_H, D_MODEL), F32)
        scratch = [pltpu.VMEM((D_MODEL // LANES, GT_ROWS * GT_PITCH, LANES), F32)]
    else:
        out_spec = pl.BlockSpec((tm, D_MODEL), lambda i: (i, 0))
        out_shape = jax.ShapeDtypeStruct((n_rows, D_MODEL), F32)
    out = pl.pallas_call(
        functools.partial(_ffn_kernel, final_norm=final_norm, mix=mix is not None,
                          n2_major=n2_major, swap_grid=swap_grid),
        grid=(n_rows // tm,),
        in_specs=in_specs,
        out_specs=out_spec,
        out_shape=out_shape,
        scratch_shapes=scratch,
        compiler_params=_cparams("arbitrary"),
        name="ffn",
    )(*args)
    return out.reshape(n_rows, D_MODEL)


ODD_TM = 1024
RNN_TILES = D_RNN // LANES


def _in_odd_kernel(x_ref, xp_ref, xn_ref, sh_ref, sc_ref, g_ref, w_ref, cw_ref, cb_ref,
                   gate_ref, xl_ref, *, seq_len, chunk):
    tm = x_ref.shape[0]
    pieces, valid = _normed_tile(x_ref, xp_ref, xn_ref, sh_ref, sc_ref, g_ref, seq_len)
    half = tm // 2
    gate_ref[:half, :] = _gelu(jnp.dot(pieces[0], w_ref[:, :D_RNN],
                                       preferred_element_type=F32)).astype(BF16)
    gate_ref[half:, :] = _gelu(jnp.dot(pieces[1][:half], w_ref[:, :D_RNN],
                                       preferred_element_type=F32)).astype(BF16)
    cw = PROJ_COLS
    for c0 in range(0, D_RNN, cw):
        cols = slice(c0, min(c0 + cw, D_RNN))
        wcols = slice(D_RNN + cols.start, D_RNN + cols.stop)
        z_all = _dot_rows(pieces, w_ref[:, wcols])
        y = _dwconv(z_all, tm, cw_ref[:, cols], cb_ref[:, cols], 2, valid)
        kg = chunk // RG_GROUPS
        for c in range((cols.stop - cols.start) // LANES):
            yc = y[:, c * LANES:(c + 1) * LANES]
            for r0 in range(0, tm, chunk):
                for gq in range(RG_GROUPS):
                    xl_ref[c0 // LANES + c, pl.ds(r0 + gq, kg, stride=RG_GROUPS), :] = (
                        yc[r0 + gq * kg:r0 + (gq + 1) * kg])


def _in_odd(x, mods, mod_row, g, w, layer, cw, cb, seq_len, chunk):
    n_rows = x.shape[0]
    tm = ODD_TM
    const = lambda shape: _single_buffered(shape, lambda i: (0,) * len(shape))
    return pl.pallas_call(
        functools.partial(_in_odd_kernel, seq_len=seq_len, chunk=chunk),
        grid=(n_rows // tm,),
        in_specs=_halo_specs(tm, n_rows) + [
            _mod_spec(0, mod_row), _mod_spec(1, mod_row),
            const((1, D_MODEL)), _layer_spec(w, layer), const(cw.shape), const(cb.shape),
        ],
        out_specs=[pl.BlockSpec((tm, D_RNN), lambda i: (i, 0)),
                   pl.BlockSpec((RNN_TILES, tm, LANES), lambda i: (0, i, 0))],
        out_shape=[jax.ShapeDtypeStruct((n_rows, D_RNN), BF16),
                   jax.ShapeDtypeStruct((RNN_TILES, n_rows, LANES), F32)],
        compiler_params=_cparams("arbitrary"),
        name="in_proj_odd",
    )(x, x, x, mods, mods, g, w, cw, cb)


RG_T = 512
RG_NCH = SEQ // RG_T
RG_GROUPS = 8
RG_PLANES_PER_PASS = 6
SQRT_GUARD = 1e-30
RG_WIN = 3 * LANES


def _rg_window_start(j):
    return min(max(LANES * (j - 1), 0), D_RNN - RG_WIN)


def _rg_chunk(x_ref, w_ref, ba_ref, bx_ref, lam_ref, a_pl, b_pl, carry_ref, reverse, emit):
    t_rows = x_ref.shape[1]
    kg = t_rows // RG_GROUPS
    xb = [x_ref[j].astype(BF16) for j in range(RNN_TILES)]
    for j in range(RNN_TILES):
        tile = slice(j * LANES, (j + 1) * LANES)
        wt = _rg_window_start(j) // LANES
        pre = jnp.dot(jnp.concatenate(xb[wt:wt + RG_WIN // LANES], axis=1), w_ref[j],
                      preferred_element_type=F32)
        lam = lam_ref[:, tile]
        softplus_neg = jnp.maximum(-lam, 0.0) + jnp.log1p(jnp.exp(-jnp.abs(lam)))
        th_r = jnp.tanh(pre[:, :LANES] + ba_ref[:, tile])
        th_i = jnp.tanh(pre[:, LANES:] + bx_ref[:, tile])
        c3 = (-0.5 * RG_C * math.log2(math.e)) * softplus_neg
        av = jnp.exp2(c3 + c3 * th_r)
        hx = 0.5 * x_ref[j]
        y = 1.0 - av * av
        a_pl[j, 0:t_rows, :] = av
        b_pl[j, 0:t_rows, :] = (y * lax.rsqrt(jnp.maximum(y, SQRT_GUARD))) * (hx + hx * th_i)

    def rows_k(pl_ref, j, k):
        return pl_ref[j, k * RG_GROUPS:(k + 1) * RG_GROUPS, :]

    order = list(range(kg))[::-1] if reverse else list(range(kg))
    groups = list(range(RG_GROUPS))[::-1] if reverse else list(range(RG_GROUPS))
    for j0 in range(0, RNN_TILES, RG_PLANES_PER_PASS):
        planes = range(j0, min(j0 + RG_PLANES_PER_PASS, RNN_TILES))
        big_a, big_b = {}, {}
        for n, k in enumerate(order):
            for j in planes:
                ak, bk = rows_k(a_pl, j, k), rows_k(b_pl, j, k)
                if n == 0:
                    big_a[j], big_b[j] = ak, bk
                else:
                    big_b[j] = ak * big_b[j] + bk
                    big_a[j] = ak * big_a[j]
        h = {}
        for j in planes:
            c = carry_ref[j, 0:1, :]
            rows = [None] * RG_GROUPS
            for gq in groups:
                rows[gq] = c
                c = big_a[j][gq:gq + 1] * c + big_b[j][gq:gq + 1]
            carry_ref[j, 0:1, :] = c
            h[j] = jnp.concatenate(rows, axis=0)
        for k in order:
            for j in planes:
                h[j] = rows_k(a_pl, j, k) * h[j] + rows_k(b_pl, j, k)
                b_pl[j, k * RG_GROUPS:(k + 1) * RG_GROUPS, :] = h[j]
        for j in planes:
            emit(j, jnp.concatenate(
                [b_pl[j, pl.ds(gq, kg, stride=RG_GROUPS), :] for gq in range(RG_GROUPS)], axis=0))


def _rg_bwd_kernel(xc_ref, xl_ref, w_ref, ba_ref, bx_ref, lam_ref, oc_ref, ol_ref,
                   a_pl, b_pl, carry_ref):
    def emitter(o_ref):
        def emit(j, hcur):
            o_ref[:, j * LANES:(j + 1) * LANES] = hcur.astype(BF16)
        return emit

    @pl.when(pl.program_id(1) == 0)
    def _():
        carry_ref[...] = jnp.zeros_like(carry_ref)
        _rg_chunk(xc_ref, w_ref, ba_ref, bx_ref, lam_ref, a_pl, b_pl, carry_ref, True,
                  emitter(oc_ref))

    @pl.when(pl.program_id(1) > 0)
    def _():
        _rg_chunk(xl_ref, w_ref, ba_ref, bx_ref, lam_ref, a_pl, b_pl, carry_ref, True,
                  emitter(ol_ref))


def _rg_fwd_kernel(xc_ref, xl_ref, w_ref, ba_ref, bx_ref, lam_ref, gc_ref, gl_ref, hc_ref, hl_ref,
                   rc_ref, rl_ref, gtc_ref, gtl_ref, wo_ref, oc_ref, ol_ref,
                   a_pl, b_pl, carry_ref, y_ref):
    def run(x_ref, g_ref, hb_ref, res_ref, gt_ref, o_ref):
        t_rows = x_ref.shape[1]

        def emit(j, hcur):
            tile = slice(j * LANES, (j + 1) * LANES)
            y_ref[0:t_rows, tile] = (g_ref[:, tile].astype(F32)
                                     * (hcur + hb_ref[:, tile].astype(F32))).astype(BF16)

        _rg_chunk(x_ref, w_ref, ba_ref, bx_ref, lam_ref, a_pl, b_pl, carry_ref, False, emit)
        acc = jnp.dot(y_ref[0:t_rows, :], wo_ref[...], preferred_element_type=F32)
        o_ref[...] = res_ref[...] + gt_ref[0] * acc

    @pl.when(pl.program_id(1) == 0)
    def _():
        carry_ref[...] = jnp.zeros_like(carry_ref)
        run(xc_ref, gc_ref, hc_ref, rc_ref, gtc_ref, oc_ref)

    @pl.when(pl.program_id(1) > 0)
    def _():
        run(xl_ref, gl_ref, hl_ref, rl_ref, gtl_ref, ol_ref)


def _rg_scan(xc, xl, w, ba, bx, lam, reverse, fwd_args=None):
    if reverse:
        chunk = lambda s: RG_NCH - jnp.maximum(s, 1)
    else:
        chunk = lambda s: jnp.maximum(s - 1, 0)
    ctx_spec = lambda d: pl.BlockSpec((None, CTX_LEN, d), lambda b, s: (b, 0, 0))
    lat_spec = lambda d: pl.BlockSpec((None, RG_T, d), lambda b, s: (b, chunk(s), 0))
    const = lambda shape: _single_buffered(shape, lambda b, s: (0,) * len(shape))
    in_specs = [pl.BlockSpec((RNN_TILES, CTX_LEN, LANES), lambda b, s: (0, b, 0)),
                pl.BlockSpec((RNN_TILES, RG_T, LANES), lambda b, s: (0, b * RG_NCH + chunk(s), 0)),
                const(w.shape), const(ba.shape), const(bx.shape), const(lam.shape)]
    args = [xc, xl, w, ba, bx, lam]
    scratch = [pltpu.VMEM((RNN_TILES, RG_T, LANES), F32),
               pltpu.VMEM((RNN_TILES, RG_T, LANES), F32),
               pltpu.VMEM((RNN_TILES, 8, LANES), F32)]
    if reverse:
        body, name, width, dtype = _rg_bwd_kernel, "rglru_bwd", D_RNN, BF16
    else:
        gate_c, gate_l, hb_c, hb_l, res_c, res_l, mods, (w_out, out_layer) = fwd_args
        in_specs += [ctx_spec(D_RNN), lat_spec(D_RNN), ctx_spec(D_RNN), lat_spec(D_RNN),
                     ctx_spec(D_MODEL), lat_spec(D_MODEL),
                     _mod_spec(2, lambda b: CTX_MOD_ROW), _mod_spec(2, lambda b: b),
                     _layer_spec(w_out, out_layer)]
        args += [gate_c, gate_l, hb_c, hb_l, res_c, res_l, mods, mods, w_out]
        scratch += [pltpu.VMEM((RG_T, D_RNN), BF16)]
        body, name, width, dtype = _rg_fwd_kernel, "rglru_fwd", D_MODEL, F32
    return pl.pallas_call(
        body,
        grid=(BATCH, 1 + RG_NCH),
        in_specs=in_specs,
        out_specs=[ctx_spec(width), lat_spec(width)],
        out_shape=[jax.ShapeDtypeStruct((BATCH, CTX_LEN, width), dtype),
                   jax.ShapeDtypeStruct((BATCH, SEQ, width), dtype)],
        scratch_shapes=scratch,
        compiler_params=_cparams("arbitrary", "arbitrary"),
        name=name,
    )(*args)


def _rg_gate_weights(wa, wx):
    def window(w, j):
        ws = _rg_window_start(j)
        win = None
        for h in range(RG_HEADS):
            c0, c1 = max(h * RG_DH, j * LANES), min((h + 1) * RG_DH, (j + 1) * LANES)
            if c0 >= c1:
                continue
            r0 = h * RG_DH - ws
            assert 0 <= r0 and r0 + RG_DH <= RG_WIN
            blk = jnp.pad(w[h, :, c0 - h * RG_DH:c1 - h * RG_DH],
                          ((r0, RG_WIN - RG_DH - r0), (c0 - j * LANES, (j + 1) * LANES - c1)))
            win = blk if win is None else win + blk
        return win

    wins = [jnp.concatenate([window(wa, j), window(wx, j)], axis=1) for j in range(RNN_TILES)]
    return (0.5 * jnp.stack(wins)).astype(BF16)


def _col_major(layer):
    return layer < DEPTH and layer % 2 == 1 and (layer // 2) % 2 == 1


def kernel(x, c, ctx, c_ctx, w_ada, b_ada, norm_mix_g, norm_ffn_g, w_in_even, w_out_even, hy_conv_w, hy_conv_b, hy_f1_w, hy_f1_b, hy_f2_w, hy_f2_b, hy_f3_w, hy_f3_b, hy_sin_freq, hy_skip, sgu_ln_g, sgu_w, sgu_b, w_in_odd, rg_conv_w, rg_conv_b, rg_wa, rg_ba, rg_wx, rg_bx, rg_lam, w_out_odd, w_ffn_in, w_ffn_out, final_norm_g):
    mods_all = _ada_mods(c, c_ctx, w_ada, b_ada)
    xs = x.reshape(BATCH * SEQ, D_MODEL)
    cs = ctx.reshape(BATCH * CTX_LEN, D_MODEL)
    lat_row_even = lambda i: i // (SEQ // EVEN_TM)
    lat_row_odd = lambda i: i // (SEQ // ODD_TM)
    lat_row_ffn = lambda i: i // (SEQ // FFN_TM)
    ctx_row = lambda i: CTX_MOD_ROW
    n_tiles = W_A // LANES
    w_in_even, w_out_even, w_in_odd, w_out_odd, w_ffn_in, w_ffn_out = (
        w.astype(BF16) for w in (w_in_even, w_out_even, w_in_odd, w_out_odd, w_ffn_in, w_ffn_out))
    for l in range(DEPTH):
        run_ctx = l < DEPTH - 1
        is_rec = l % 2 == 1
        i = l // 2
        mods = mods_all[l].reshape(MOD_ROWS, 1, N_MOD * D_MODEL)
        g_mix = norm_mix_g[l].reshape(1, D_MODEL)
        mix_l = mix_c = None
        if is_rec:
            cw, cb = rg_conv_w[i], rg_conv_b[i].reshape(1, D_RNN)
            gate_l, xl = _in_odd(xs, mods, lat_row_odd, g_mix, w_in_odd, i, cw, cb, SEQ, RG_T)
            gate_c, xc = _in_odd(cs, mods, ctx_row, g_mix, w_in_odd, i, cw, cb, CTX_LEN, CTX_LEN)
            gate_l = gate_l.reshape(BATCH, SEQ, D_RNN)
            gate_c = gate_c.reshape(BATCH, CTX_LEN, D_RNN)
            row = lambda v: v.reshape(1, D_RNN)
            half = lambda v: 0.5 * row(v)
            hb_c, hb_l = _rg_scan(xc, xl, _rg_gate_weights(rg_wa[i, 1], rg_wx[i, 1]),
                                  half(rg_ba[i, 1]), half(rg_bx[i, 1]), row(rg_lam[i, 1]), True)
            cs_new, xs = _rg_scan(
                xc, xl, _rg_gate_weights(rg_wa[i, 0], rg_wx[i, 0]),
                half(rg_ba[i, 0]), half(rg_bx[i, 0]), row(rg_lam[i, 0]), False,
                (gate_c, gate_l, hb_c, hb_l, cs.reshape(BATCH, CTX_LEN, D_MODEL),
                 xs.reshape(BATCH, SEQ, D_MODEL), mods, (w_out_odd, i)))
            xs = xs.reshape(BATCH * SEQ, D_MODEL)
            if run_ctx:
                cs = cs_new.reshape(BATCH * CTX_LEN, D_MODEL)
        else:
            cw, cb = hy_conv_w[i], hy_conv_b[i].reshape(1, HY_COLS)
            lng = sgu_ln_g[i].reshape(1, W_B)
            sgw = sgu_w[i].astype(BF16)
            sgb = jnp.repeat(sgu_b[i].T, SGU_DH, axis=1)
            skip = hy_skip[i].reshape(1, HY_ORDER * W_A)
            fargs = (hy_f1_w[i], hy_f1_b[i], hy_f2_w[i], hy_f2_b[i], hy_f3_w[i], hy_f3_b[i],
                     hy_sin_freq[i])
            kf = _filter_spectrum(SEQ, *fargs, skip)
            za, yb = _in_even(xs, mods, lat_row_even, g_mix, w_in_even, i, cw, cb, lng, sgw, sgb,
                              SEQ, True)
            za = za.reshape(HY_TILES, BATCH, SEQ, LANES)
            y1 = _longconv(za, 0, za, n_tiles, kf, 0, flat_out=True)
            ya = _longconv(y1, 0, za, 2 * n_tiles, kf, n_tiles, flat_in=True)
            mix_l = (ya, yb, (w_out_even, i), True)
            if run_ctx:
                kfc = _filter_spectrum(CTX_LEN, *fargs, skip)
                zc, ybc = _in_even(cs, mods, ctx_row, g_mix, w_in_even, i, cw, cb, lng, sgw, sgb,
                                   CTX_LEN, False)
                zc = zc.reshape(HY_TILES, BATCH, CTX_LEN, LANES)
                y1c = _ctx_conv(zc, 0, zc, n_tiles, kfc, 0)
                yac = _ctx_conv(y1c, 0, zc, 2 * n_tiles, kfc, n_tiles)
                mix_c = (yac, ybc, (w_out_even, i), False)
        g_ffn = norm_ffn_g[l].reshape(1, D_MODEL)
        fg = final_norm_g.reshape(1, D_MODEL)
        swap = _col_major(l) != _col_major(l + 1)
        xs = _ffn(xs, mods, lat_row_ffn, g_ffn, w_ffn_in, w_ffn_out, l, fg, l == DEPTH - 1,
                  mix_l, swap)
        if run_ctx:
            cs = _ffn(cs, mods, ctx_row, g_ffn, w_ffn_in, w_ffn_out, l, fg, False, mix_c)
    return xs.reshape(BATCH, SEQ, D_MODEL)
```

```python
import functools
import math

import numpy as np
import jax
import jax.numpy as jnp
from jax import lax
from jax.experimental import pallas as pl
from jax.experimental.pallas import tpu as pltpu

F32 = jnp.float32
BF16 = jnp.bfloat16

D_MODEL = 1024
BATCH = 4
SEQ = 4096
DEPTH = 4
GRID_W = 64
CTX_LEN = 256
N_MOD = 6
NORM_EPS = 1e-6
W_A = D_MODEL // 2
HY_ORDER = 2
HY_SHORT = 3
HY_BANDS = 16
HY_TARGET = 1e-2
HY_FAST_PCT = 0.3
HY_SLOW_PCT = 1.5
W_B = D_MODEL // 2
SGU_GROUPS = 4
SGU_DH = W_B // SGU_GROUPS
CHUNK = 128
D_RNN = ((4 * D_MODEL // 3 + 127) // 128) * 128
RG_HEADS = 16
RG_DH = D_RNN // RG_HEADS
RG_CONV = 4
RG_C = 8.0
D_FF = ((8 * D_MODEL // 3 + 255) // 256) * 256

VMEM_BYTES_V7X = 64 * 1024 * 1024
VMEM_LIMIT = VMEM_BYTES_V7X - 4 * 1024 * 1024
LANES = 128

FFT_N = 2 * SEQ
FFT_N1 = 64
FFT_N2 = 128
FFT_HALF_N1 = FFT_N1 // 2
FFT_PITCH = FFT_N2 + 8
CTX_N = 2 * CTX_LEN


def _cparams(*sem):
    return pltpu.CompilerParams(dimension_semantics=sem, vmem_limit_bytes=VMEM_LIMIT)


def _single_buffered(block_shape, index_map):
    return pl.BlockSpec(block_shape, index_map, pipeline_mode=pl.Buffered(1))


def _layer_spec(stack, layer):
    rest = stack.shape[1:]
    return _single_buffered((None,) + rest, lambda *_: (layer,) + (0,) * len(rest))


@functools.lru_cache(maxsize=None)
def _dft_tables_f32():
    n = FFT_N
    k1 = np.arange(FFT_N1)
    n2 = np.arange(FFT_N2)

    def angle(n1):
        m = (FFT_N2 * n1[None, None, :] + n2[:, None, None]) * k1[None, :, None]
        return 2.0 * np.pi * (m % n) / n

    a = angle(np.arange(FFT_HALF_N1))
    c, s = np.cos(a), np.sin(a)
    g = np.empty((FFT_N2, FFT_N1, 2, 2, FFT_HALF_N1))
    g[:, :, 0, 0], g[:, :, 0, 1] = c, s
    g[:, :, 1, 0], g[:, :, 1, 1] = -s, c
    g = g.reshape(FFT_N2, 2 * FFT_N1, 2 * FFT_HALF_N1)

    a = angle(np.arange(FFT_N1))
    gf = np.stack([np.cos(a), -np.sin(a)], axis=2).reshape(FFT_N2, 2 * FFT_N1, FFT_N1)

    a = np.transpose(angle(np.arange(FFT_HALF_N1)), (0, 2, 1))
    c, s = np.cos(a) / n, np.sin(a) / n
    h = np.empty((FFT_N2, 2, FFT_HALF_N1, FFT_N1, 2))
    h[:, 0, :, :, 0], h[:, 0, :, :, 1] = c, -s
    h[:, 1, :, :, 0], h[:, 1, :, :, 1] = s, c
    h = h.reshape(FFT_N2, 2 * FFT_HALF_N1, 2 * FFT_N1)

    a = 2.0 * np.pi * np.outer(n2, n2) / FFT_N2
    c, s = np.cos(a), np.sin(a)
    f2 = np.block([[c, s], [-s, c]])
    f2i = np.block([[c, -s], [s, c]])

    kk = np.arange(CTX_N)
    a = 2.0 * np.pi * np.outer(kk, np.arange(CTX_LEN)) / CTX_N
    c, s = np.cos(a), np.sin(a)
    fc = np.block([[c, s], [-s, c]])
    a = 2.0 * np.pi * np.outer(kk, np.arange(CTX_N)) / CTX_N
    fcf = np.concatenate([np.cos(a), -np.sin(a)], axis=0)
    a = 2.0 * np.pi * np.outer(np.arange(CTX_LEN), kk) / CTX_N
    c, s = np.cos(a) / CTX_N, np.sin(a) / CTX_N
    fci = np.block([[c, -s], [s, c]])

    tables = dict(g=g, gf=gf, h=h, f2=f2, f2i=f2i, fc=fc, fcf=fcf, fci=fci)
    return {k: np.asarray(v, dtype=np.float32) for k, v in tables.items()}


def _dft_tables():
    return {k: jnp.asarray(v).astype(BF16) for k, v in _dft_tables_f32().items()}


SEQ_TILE = 1024
N1_PER_TILE = SEQ_TILE // FFT_N2


def _scatter_n2(s_ref, n2, val):
    for p in range(s_ref.shape[0]):
        s_ref[p, pl.ds(n2, 2 * FFT_N1, stride=FFT_PITCH), :] = val[:, p * LANES:(p + 1) * LANES]


def _gather_n2(s_ref, n2):
    return jnp.concatenate(
        [s_ref[p, pl.ds(n2, 2 * FFT_N1, stride=FFT_PITCH), :] for p in range(s_ref.shape[0])],
        axis=1)


def _spectrum_rows(s_ref, k1):
    base = pl.multiple_of(k1 * (2 * FFT_PITCH), 8)
    return base, jnp.concatenate(
        [jnp.concatenate([s_ref[p, pl.ds(base + o, FFT_N2), :] for p in range(s_ref.shape[0])],
                         axis=1) for o in (0, FFT_PITCH)], axis=0)


HY_FFN = 64
FILT_ROWS = 1024
FILT_TILES = 2
HIGHEST = lax.Precision.HIGHEST


def _tap_rows(row, seq, n2_major):
    if n2_major:
        n2, n1 = row >> int(math.log2(FFT_N1)), row & (FFT_N1 - 1)
        bwd = n1 >= FFT_HALF_N1
        m = FFT_N2 * (n1 - FFT_HALF_N1) + n2
        fwd_pos = FFT_N2 * n1 + n2
    else:
        bwd = row >= seq
        m = row - seq
        fwd_pos = row
    pos = jnp.where(bwd, seq - m, fwd_pos).astype(F32)
    bwd_f = jnp.where(bwd, 1.0, 0.0)
    keep_f = jnp.where(bwd & (m == 0), 0.0, 1.0)
    return pos, bwd_f, keep_f


def _filter_kernel(ec_ref, f1w_ref, f1b_ref, f2w_ref, f2b_ref, sf_ref, w3f_ref, w3b_ref, b3f_ref,
                   b3b_ref, dl_ref, skip_ref, *rest, seq, n2_major):
    if n2_major:
        gf_ref, f2_ref, o_ref, hid_ref, k_ref, s_ref = rest
    else:
        fcf_ref, o_ref, hid_ref, k_ref = rest
    n_rows = 2 * seq
    rb = min(FILT_ROWS, n_rows)

    def rows_of(i):
        r0 = pl.multiple_of(i * rb, rb)
        return r0, _tap_rows(r0 + lax.broadcasted_iota(jnp.int32, (rb, 1), 0), seq, n2_major)

    @pl.when(pl.program_id(0) == 0)
    def _():
        sf = sf_ref[...]

        def hidden(i, carry):
            r0, (pos, _, _) = rows_of(i)
            t = pos * (1.0 / (seq - 1))
            w = pos * (2.0 * math.pi / seq)
            emb = jnp.sin(w * ec_ref[0:1] + ec_ref[1:2]) + t * ec_ref[2:3]
            h = jnp.sin(sf * (jnp.dot(emb, f1w_ref[...], precision=HIGHEST,
                                      preferred_element_type=F32) + f1b_ref[...]))
            h = jnp.sin(sf * (jnp.dot(h.astype(BF16), f2w_ref[...].astype(BF16),
                                      preferred_element_type=F32) + f2b_ref[...]))
            hid_ref[pl.ds(r0, rb), :] = h
            return carry

        lax.fori_loop(0, n_rows // rb, hidden, 0)

    def taps(i, ssq):
        r0, (pos, bwd_f, keep_f) = rows_of(i)
        h = hid_ref[pl.ds(r0, rb), :].astype(BF16)
        neg_t = -(pos * (1.0 / (seq - 1)))
        ks = []
        for c in range(FILT_TILES):
            kf = jnp.dot(h, w3f_ref[c].astype(BF16), preferred_element_type=F32) + b3f_ref[c]
            kb = jnp.dot(h, w3b_ref[c].astype(BF16), preferred_element_type=F32) + b3b_ref[c]
            ks.append((kf + bwd_f * (kb - kf)) * jnp.exp(neg_t * dl_ref[c]) * keep_f)
        k = jnp.concatenate(ks, axis=1)
        k_ref[pl.ds(r0, rb), :] = k
        return ssq + jnp.sum(k * k, axis=0, keepdims=True)

    ssq = lax.fori_loop(0, n_rows // rb, taps, jnp.zeros((1, FILT_TILES * LANES), F32))
    scale = lax.rsqrt(ssq + NORM_EPS)
    skip = skip_ref[...]

    if n2_major:
        def stage1(n2, carry):
            x = k_ref[pl.ds(pl.multiple_of(n2 * FFT_N1, FFT_N1), FFT_N1), :] * scale
            _scatter_n2(s_ref, n2, jnp.dot(gf_ref[n2], x.astype(BF16),
                                           preferred_element_type=F32))
            return carry

        lax.fori_loop(0, FFT_N2, stage1, 0, unroll=8)

        def stage2(k1, carry):
            _, r = _spectrum_rows(s_ref, k1)
            z = jnp.dot(f2_ref[...], r.astype(BF16), preferred_element_type=F32)
            row = pl.multiple_of(k1 * (2 * FFT_N2), 2 * FFT_N2)
            o_ref[pl.ds(row, FFT_N2), :] = (z[:FFT_N2] + skip).astype(BF16)
            o_ref[pl.ds(row + FFT_N2, FFT_N2), :] = z[FFT_N2:].astype(BF16)
            return carry

        lax.fori_loop(0, FFT_N1, stage2, 0, unroll=8)
    else:
        z = jnp.dot(fcf_ref[...], (k_ref[...] * scale).astype(BF16), preferred_element_type=F32)
        o_ref[:n_rows, :] = (z[:n_rows] + skip).astype(BF16)
        o_ref[n_rows:, :] = z[n_rows:].astype(BF16)


def _filter_spectrum(seq, f1_w, f1_b, f2_w, f2_b, f3_w, f3_b, sin_freq, skip):
    n2_major = seq == SEQ
    t = _dft_tables()
    pad = lambda a, r, c: jnp.pad(a, ((0, r - a.shape[0]), (0, c - a.shape[1])))
    f = jnp.linspace(1e-4, HY_BANDS - 1, HY_BANDS, dtype=F32)
    zeros = lambda n: jnp.zeros((n,), F32)
    ones = lambda n: jnp.ones((n,), F32)
    rest = LANES - 1 - 2 * HY_BANDS
    ec = jnp.stack([
        jnp.concatenate([zeros(1), f, f, zeros(rest)]),
        jnp.concatenate([zeros(1), (math.pi / 2) * ones(HY_BANDS), math.pi * ones(HY_BANDS),
                         zeros(rest)]),
        jnp.concatenate([ones(1), zeros(LANES - 1)]),
    ] + [zeros(LANES)] * 5)
    row = lambda v: pad(v.reshape(1, -1), 1, LANES)
    n_tiles = HY_ORDER * W_A // LANES
    w3 = f3_w.reshape(HY_FFN, HY_ORDER, 2, W_A)
    b3 = f3_b.reshape(HY_ORDER, 2, W_A)
    tiles_w = lambda d: jnp.pad(
        w3[:, :, d, :].reshape(HY_FFN, n_tiles, LANES).transpose(1, 0, 2),
        ((0, 0), (0, LANES - HY_FFN), (0, 0)))
    tiles_b = lambda d: b3[:, d, :].reshape(n_tiles, 1, LANES)
    deltas = jnp.abs(jnp.linspace(math.log(HY_TARGET) / HY_SLOW_PCT,
                                  math.log(HY_TARGET) / HY_FAST_PCT, W_A, dtype=F32))
    dl = jnp.tile(deltas, HY_ORDER).reshape(n_tiles, 1, LANES)
    const = lambda shape: _single_buffered(shape, lambda j: (0,) * len(shape))
    tile = lambda shape: pl.BlockSpec((FILT_TILES,) + shape, lambda j: (j, 0, 0))
    ct = FILT_TILES * LANES
    args = [ec, pad(f1_w, LANES, LANES), row(f1_b), pad(f2_w, LANES, LANES), row(f2_b),
            row(sin_freq), tiles_w(0), tiles_w(1), tiles_b(0), tiles_b(1), dl, skip]
    in_specs = [const((8, LANES)), const((LANES, LANES)), const((1, LANES)),
                const((LANES, LANES)), const((1, LANES)), const((1, LANES)),
                tile((LANES, LANES)), tile((LANES, LANES)), tile((1, LANES)), tile((1, LANES)),
                tile((1, LANES)), pl.BlockSpec((1, ct), lambda j: (0, j))]
    scratch = [pltpu.VMEM((2 * seq, LANES), F32), pltpu.VMEM((2 * seq, ct), F32)]
    if n2_major:
        args += [t["gf"], t["f2"]]
        in_specs += [const(t["gf"].shape), const(t["f2"].shape)]
        scratch += [pltpu.VMEM((FILT_TILES, 2 * FFT_N1 * FFT_PITCH, LANES), F32)]
    else:
        args += [t["fcf"]]
        in_specs += [const(t["fcf"].shape)]
    return pl.pallas_call(
        functools.partial(_filter_kernel, seq=seq, n2_major=n2_major),
        grid=(n_tiles // FILT_TILES,),
        in_specs=in_specs,
        out_specs=pl.BlockSpec((4 * seq, ct), lambda j: (0, j)),
        out_shape=jax.ShapeDtypeStruct((4 * seq, HY_ORDER * W_A), BF16),
        scratch_shapes=scratch,
        compiler_params=_cparams("arbitrary"),
        name="hyena_filter" if n2_major else "hyena_ctx_filter",
    )(*args)


LC_OUT_CHUNKS = 2
LC_N2_PER_CHUNK = FFT_N2 // LC_OUT_CHUNKS


def _longconv_kernel(v_ref, gate_ref, kf_ref, g_ref, h_ref, f2_ref, f2i_ref, o_ref, s_ref, *,
                     flat_in, flat_out):
    ct = v_ref.shape[2]
    t = pl.program_id(1)

    @pl.when(t == 0)
    def _():
        def stage1(n2, carry):
            if flat_in:
                r0 = pl.multiple_of(n2 * FFT_HALF_N1, FFT_HALF_N1)
                blk = [v_ref[b, pl.ds(r0, FFT_HALF_N1), :] for b in range(BATCH)]
            else:
                r0 = pl.multiple_of(n2 * N1_PER_TILE, N1_PER_TILE)
                blk = [jnp.concatenate([v_ref[b, pl.ds(i * SEQ_TILE + r0, N1_PER_TILE), :]
                                        for i in range(SEQ // SEQ_TILE)], axis=0)
                       for b in range(BATCH)]
            x = jnp.concatenate([jnp.concatenate([blk[0], blk[2]], axis=1),
                                 jnp.concatenate([blk[1], blk[3]], axis=1)], axis=0)
            _scatter_n2(s_ref, n2, jnp.dot(g_ref[n2], x.astype(BF16), preferred_element_type=F32))
            return carry

        lax.fori_loop(0, FFT_N2, stage1, 0, unroll=16)

        def stage2(k1, carry):
            base, r = _spectrum_rows(s_ref, k1)
            z = jnp.dot(f2_ref[...], r.astype(BF16), preferred_element_type=F32)
            kf = kf_ref[pl.ds(pl.multiple_of(k1 * (2 * FFT_N2), 2 * FFT_N2), 2 * FFT_N2), :]
            kf = kf.astype(F32)
            kr = jnp.concatenate([kf[:FFT_N2]] * 2, axis=1)
            ki = jnp.concatenate([kf[FFT_N2:]] * 2, axis=1)
            zr, zi = z[:FFT_N2], z[FFT_N2:]
            p = jnp.concatenate([zr * kr - zi * ki, zr * ki + zi * kr], axis=0)
            q = jnp.dot(f2i_ref[...], p.astype(BF16), preferred_element_type=F32)
            for p in range(2):
                s_ref[p, pl.ds(base, FFT_N2), :] = q[:FFT_N2, p * ct:(p + 1) * ct]
                s_ref[p, pl.ds(base + FFT_PITCH, FFT_N2), :] = q[FFT_N2:, p * ct:(p + 1) * ct]
            return carry

        lax.fori_loop(0, FFT_N1, stage2, 0, unroll=8)

    def stage3(j, carry):
        n2 = t * LC_N2_PER_CHUNK + j
        rq = _gather_n2(s_ref, n2)
        y = jnp.dot(h_ref[n2], rq.astype(BF16), preferred_element_type=F32)
        r0 = pl.multiple_of(j * N1_PER_TILE, N1_PER_TILE)
        for b in range(BATCH):
            ri, pair = b % 2, b // 2
            gated = [gate_ref[b, i, pl.ds(r0, N1_PER_TILE), :]
                     * y[ri * FFT_HALF_N1 + i * N1_PER_TILE:ri * FFT_HALF_N1 + (i + 1) * N1_PER_TILE,
                         pair * ct:(pair + 1) * ct]
                     for i in range(SEQ // SEQ_TILE)]
            if flat_out:
                o_ref[b, pl.ds(pl.multiple_of(j * FFT_HALF_N1, FFT_HALF_N1), FFT_HALF_N1), :] = (
                    jnp.concatenate(gated, axis=0).astype(o_ref.dtype))
            else:
                for i in range(SEQ // SEQ_TILE):
                    o_ref[b, i, pl.ds(r0, N1_PER_TILE), :] = gated[i]
        return carry

    lax.fori_loop(0, LC_N2_PER_CHUNK, stage3, 0, unroll=16)


def _longconv(v, v_tile, gate, gate_tile, kf, kf_col, flat_in=False, flat_out=False):
    t = _dft_tables()
    ct = LANES
    n_seq_tiles = SEQ // SEQ_TILE
    rows = SEQ_TILE // LC_OUT_CHUNKS
    gate = gate.reshape(gate.shape[0], BATCH, n_seq_tiles, SEQ_TILE, ct)
    v_index = lambda j, i: (v_tile + j, 0, 0, 0)
    if flat_in:
        v_spec = pl.BlockSpec((None, BATCH, SEQ, ct), v_index)
    else:
        v_spec = _single_buffered((None, BATCH, SEQ, ct), v_index)
    if flat_out:
        out_spec = pl.BlockSpec((None, BATCH, SEQ // LC_OUT_CHUNKS, ct), lambda j, i: (j, 0, i, 0))
        out_shape = jax.ShapeDtypeStruct((W_A // ct, BATCH, SEQ, ct), BF16)
    else:
        out_spec = pl.BlockSpec((None, BATCH, n_seq_tiles, rows, ct),
                                lambda j, i: (j, 0, 0, i, 0))
        out_shape = jax.ShapeDtypeStruct((W_A // ct, BATCH, n_seq_tiles, SEQ_TILE, ct), F32)
    out = pl.pallas_call(
        functools.partial(_longconv_kernel, flat_in=flat_in, flat_out=flat_out),
        grid=(W_A // ct, LC_OUT_CHUNKS),
        in_specs=[
            v_spec,
            pl.BlockSpec((None, BATCH, n_seq_tiles, rows, ct),
                         lambda j, i: (gate_tile + j, 0, 0, i, 0)),
            pl.BlockSpec((2 * FFT_N, ct), lambda j, i: (0, kf_col + j)),
            _single_buffered(t["g"].shape, lambda j, i: (0, 0, 0)),
            _single_buffered(t["h"].shape, lambda j, i: (0, 0, 0)),
            _single_buffered(t["f2"].shape, lambda j, i: (0, 0)),
            _single_buffered(t["f2i"].shape, lambda j, i: (0, 0)),
        ],
        out_specs=out_spec,
        out_shape=out_shape,
        scratch_shapes=[pltpu.VMEM((2, 2 * FFT_N1 * FFT_PITCH, ct), F32)],
        compiler_params=_cparams("arbitrary", "arbitrary"),
        name="hyena_longconv",
    )(v, gate, kf, t["g"], t["h"], t["f2"], t["f2i"])
    return out.reshape(W_A // ct, BATCH, SEQ, ct)


def _ctx_conv_kernel(v_ref, gate_ref, kf_ref, fc_ref, fci_ref, o_ref):
    ct = v_ref.shape[2]
    x = jnp.concatenate([jnp.concatenate([v_ref[0], v_ref[2]], axis=1),
                         jnp.concatenate([v_ref[1], v_ref[3]], axis=1)], axis=0)
    z = jnp.dot(fc_ref[...], x.astype(BF16), preferred_element_type=F32)
    kf = kf_ref[...].astype(F32)
    kr = jnp.concatenate([kf[:CTX_N]] * 2, axis=1)
    ki = jnp.concatenate([kf[CTX_N:]] * 2, axis=1)
    zr, zi = z[:CTX_N], z[CTX_N:]
    p = jnp.concatenate([zr * kr - zi * ki, zr * ki + zi * kr], axis=0)
    y = jnp.dot(fci_ref[...], p.astype(BF16), preferred_element_type=F32)
    for b in range(BATCH):
        ri, pair = b % 2, b // 2
        o_ref[b] = gate_ref[b] * y[ri * CTX_LEN:(ri + 1) * CTX_LEN, pair * ct:(pair + 1) * ct]


def _ctx_conv(v, v_tile, gate, gate_tile, kf, kf_col):
    t = _dft_tables()
    ct = LANES
    return pl.pallas_call(
        _ctx_conv_kernel,
        grid=(W_A // ct,),
        in_specs=[
            pl.BlockSpec((None, BATCH, CTX_LEN, ct), lambda j: (v_tile + j, 0, 0, 0)),
            pl.BlockSpec((None, BATCH, CTX_LEN, ct), lambda j: (gate_tile + j, 0, 0, 0)),
            pl.BlockSpec((2 * CTX_N, ct), lambda j: (0, kf_col + j)),
            pl.BlockSpec(t["fc"].shape, lambda j: (0, 0)),
            pl.BlockSpec(t["fci"].shape, lambda j: (0, 0)),
        ],
        out_specs=pl.BlockSpec((None, BATCH, CTX_LEN, ct), lambda j: (j, 0, 0, 0)),
        out_shape=jax.ShapeDtypeStruct((W_A // ct, BATCH, CTX_LEN, ct), F32),
        compiler_params=_cparams("arbitrary"),
        name="hyena_ctx_conv",
    )(v, gate, kf, t["fc"], t["fci"])


MOD_ROWS = 8
CTX_MOD_ROW = BATCH


def _ada_kernel(c_ref, w_ref, b_ref, o_ref):
    cv = c_ref[...]
    s = cv * jax.nn.sigmoid(cv)
    o_ref[0] = jnp.dot(s.astype(BF16), w_ref[0].astype(BF16),
                       preferred_element_type=F32) + b_ref[0]


def _ada_mods(c, c_ctx, w_ada, b_ada):
    cv = jnp.concatenate(
        [c, c_ctx[None], jnp.zeros((MOD_ROWS - BATCH - 1, D_MODEL), F32)], axis=0)
    n = N_MOD * D_MODEL
    tn = n // 4
    return pl.pallas_call(
        _ada_kernel,
        grid=(DEPTH, n // tn),
        in_specs=[
            pl.BlockSpec((MOD_ROWS, D_MODEL), lambda l, j: (0, 0)),
            pl.BlockSpec((1, D_MODEL, tn), lambda l, j: (l, 0, j)),
            pl.BlockSpec((1, 1, tn), lambda l, j: (l, 0, j)),
        ],
        out_specs=pl.BlockSpec((1, MOD_ROWS, tn), lambda l, j: (l, 0, j)),
        out_shape=jax.ShapeDtypeStruct((DEPTH, MOD_ROWS, n), F32),
        compiler_params=_cparams("arbitrary", "arbitrary"),
        name="ada_mods",
    )(cv, w_ada, b_ada.reshape(DEPTH, 1, n))


def _mod_spec(m, row_fn):
    return pl.BlockSpec((1, 1, D_MODEL), lambda i, *_: (row_fn(i), 0, m))


def _rms_mod(x, g, shift, scale):
    ms = jnp.mean(x * x, axis=-1, keepdims=True)
    return (x * lax.rsqrt(ms + NORM_EPS)) * (g * (1.0 + scale)) + shift


GELU_C = math.sqrt(2.0 / math.pi)


def _gelu(x):
    hx = 0.5 * x
    return hx + hx * jnp.tanh(x * (GELU_C + (GELU_C * 0.044715) * (x * x)))


HALO = 8
PROJ_COLS = 2 * LANES


def _dwconv(z_all, tm, w, b, left, valid):
    n = z_all.shape[0]
    y = b + w[left:left + 1] * z_all[:tm]
    for k in range(w.shape[0]):
        d = k - left
        if d == 0:
            continue
        s = pltpu.roll(z_all, (-d) % n, 0)[:tm]
        if valid is not None:
            s = s * valid(d)
        y = y + s * w[k:k + 1]
    return y


def _halo_specs(tm, n_rows):
    per = tm // HALO
    last = n_rows // HALO - 1
    return [
        pl.BlockSpec((tm, D_MODEL), lambda i: (i, 0)),
        pl.BlockSpec((HALO, D_MODEL), lambda i: (jnp.maximum(i * per - 1, 0), 0)),
        pl.BlockSpec((HALO, D_MODEL), lambda i: (jnp.minimum((i + 1) * per, last), 0)),
    ]


def _normed_tile(x_ref, xp_ref, xn_ref, sh_ref, sc_ref, g_ref, seq_len):
    tm = x_ref.shape[0]
    g, shift, scale = g_ref[...], sh_ref[0], sc_ref[0]
    h0 = _rms_mod(x_ref[:tm // 2, :], g, shift, scale)
    h = _rms_mod(x_ref[tm // 2:, :], g, shift, scale)
    hn = _rms_mod(xn_ref[...], g, shift, scale)
    hp = _rms_mod(xp_ref[...], g, shift, scale)
    if seq_len >= tm:
        r0 = pl.program_id(0) * tm
        hp = hp * jnp.where((r0 & (seq_len - 1)) == 0, 0.0, 1.0)
        hn = hn * jnp.where(((r0 + tm) & (seq_len - 1)) == 0, 0.0, 1.0)
        valid = None
    else:
        hp, hn = jnp.zeros_like(hp), jnp.zeros_like(hn)
        pos = lax.broadcasted_iota(jnp.int32, (tm, 1), 0) & (seq_len - 1)
        valid = lambda d: jnp.where((pos + d >= 0) & (pos + d < seq_len), 1.0, 0.0)
    return [h0.astype(BF16), jnp.concatenate([h, hn, hp], axis=0).astype(BF16)], valid


def _dot_rows(pieces, w):
    return jnp.concatenate([jnp.dot(p, w, preferred_element_type=F32) for p in pieces], axis=0)


EVEN_TM = SEQ_TILE
HY_COLS = 3 * W_A
HY_TILES = HY_COLS // LANES


def _in_even_kernel(x_ref, xp_ref, xn_ref, sh_ref, sc_ref, g_ref, w_ref, cw_ref, cb_ref,
                    lng_ref, sgw_ref, sgb_ref, za_ref, yb_ref, *, seq_len, n2_major):
    tm = x_ref.shape[0]
    pieces, valid = _normed_tile(x_ref, xp_ref, xn_ref, sh_ref, sc_ref, g_ref, seq_len)
    h = jnp.concatenate([pieces[0], pieces[1][:tm // 2]], axis=0)
    cw = PROJ_COLS
    for cc in range(HY_COLS // cw):
        cols = slice(cc * cw, (cc + 1) * cw)
        z_all = _dot_rows(pieces, w_ref[:, cols])
        y = _dwconv(z_all, tm, cw_ref[:, cols], cb_ref[:, cols], 1, valid)
        for c in range(cw // LANES):
            tile = cc * (cw // LANES) + c
            yc = y[:, c * LANES:(c + 1) * LANES]
            if n2_major:
                for j in range(N1_PER_TILE):
                    za_ref[tile, pl.ds(j, FFT_N2, stride=N1_PER_TILE), :] = (
                        yc[j * FFT_N2:(j + 1) * FFT_N2])
            else:
                za_ref[tile] = yc
    u = _gelu(jnp.dot(h, w_ref[:, HY_COLS:HY_COLS + W_B], preferred_element_type=F32))
    vb = _gelu(jnp.dot(h, w_ref[:, HY_COLS + W_B:], preferred_element_type=F32))
    vc = vb - jnp.mean(vb, axis=-1, keepdims=True)
    vn = vc * lax.rsqrt(jnp.mean(vc * vc, axis=-1, keepdims=True) + NORM_EPS) * lng_ref[...]
    vn = vn.astype(BF16)
    n_chunks = tm // CHUNK
    for q in range(SGU_GROUPS):
        cols = slice(q * SGU_DH, (q + 1) * SGU_DH)
        rhs = jnp.concatenate(
            [vn[ch * CHUNK:(ch + 1) * CHUNK, cols] for ch in range(n_chunks)], axis=1)
        s_all = jnp.dot(sgw_ref[q], rhs, preferred_element_type=F32)
        for ch in range(n_chunks):
            rows = slice(ch * CHUNK, (ch + 1) * CHUNK)
            s = s_all[:, ch * SGU_DH:(ch + 1) * SGU_DH] + sgb_ref[:, cols]
            yb_ref[rows, cols] = (u[rows, cols] * s).astype(BF16)


def _in_even(x, mods, mod_row, g, w, layer, cw, cb, lng, sgw, sgb, seq_len, n2_major):
    n_rows = x.shape[0]
    tm = EVEN_TM
    za_shape = (HY_TILES, n_rows, LANES)
    za_spec = pl.BlockSpec((HY_TILES, tm, LANES), lambda i: (0, i, 0))
    const = lambda shape: _single_buffered(shape, lambda i: (0,) * len(shape))
    return pl.pallas_call(
        functools.partial(_in_even_kernel, seq_len=seq_len, n2_major=n2_major),
        grid=(n_rows // tm,),
        in_specs=_halo_specs(tm, n_rows) + [
            _mod_spec(0, mod_row), _mod_spec(1, mod_row),
            const((1, D_MODEL)), _layer_spec(w, layer), const(cw.shape), const(cb.shape),
            const(lng.shape), const(sgw.shape), const(sgb.shape),
        ],
        out_specs=[za_spec, pl.BlockSpec((tm, W_B), lambda i: (i, 0))],
        out_shape=[jax.ShapeDtypeStruct(za_shape, F32),
                   jax.ShapeDtypeStruct((n_rows, W_B), BF16)],
        compiler_params=_cparams("arbitrary"),
        name="in_proj_even",
    )(x, x, x, mods, mods, g, w, cw, cb, lng, sgw, sgb)


FFN_TM = SEQ_TILE
FFN_SUB = 256
GRID_H = SEQ // GRID_W
GT_ROWS = FFN_TM // GRID_W
GT_PITCH = GRID_W + 8


def _even_mix(ya_ref, yb_ref, w_ref, n2_major):
    parts = []
    for c in range(W_A // LANES):
        if n2_major:
            yac = jnp.concatenate(
                [ya_ref[c, pl.ds(j, FFT_N2, stride=N1_PER_TILE), :]
                 for j in range(N1_PER_TILE)], axis=0)
        else:
            yac = ya_ref[c]
        parts.append(yac.astype(BF16))
    y = jnp.concatenate(parts + [yb_ref[...]], axis=1)
    return jnp.dot(y, w_ref[...], preferred_element_type=F32)


def _ffn_kernel(*refs, final_norm, mix, n2_major, swap_grid):
    refs = list(refs)
    x_ref, sh_ref, sc_ref, gt_ref, g_ref, w_in_ref, wo_ref, fg_ref = refs[:8]
    del refs[:8]
    if mix:
        ya_ref, yb_ref, gm_ref, wm_ref = refs[:4]
        del refs[:4]
    o_ref = refs.pop(0)
    x = x_ref[...]
    if mix:
        x = x + gm_ref[0] * _even_mix(ya_ref, yb_ref, wm_ref, n2_major)
    h = _rms_mod(x, g_ref[...], sh_ref[0], sc_ref[0]).astype(BF16)
    acts = []
    for c0 in range(0, D_FF, FFN_SUB):
        c1 = min(c0 + FFN_SUB, D_FF)
        hz = 0.5 * jnp.dot(h, w_in_ref[:, c0:c1], preferred_element_type=F32)
        z2 = jnp.dot(h, w_in_ref[:, D_FF + c0:D_FF + c1], preferred_element_type=F32)
        acts.append(((hz + hz * jnp.tanh(hz)) * z2).astype(BF16))
    y = x + gt_ref[0] * jnp.dot(jnp.concatenate(acts, axis=1), wo_ref[...],
                                preferred_element_type=F32)
    if final_norm:
        y = y * lax.rsqrt(jnp.mean(y * y, axis=-1, keepdims=True) + NORM_EPS) * fg_ref[...]
    if not swap_grid:
        o_ref[...] = y
        return
    s_ref = refs.pop(0)
    nt = D_MODEL // LANES
    for t in range(nt):
        for j in range(GT_ROWS):
            s_ref[t, j * GT_PITCH:j * GT_PITCH + GRID_W, :] = (
                y[j * GRID_W:(j + 1) * GRID_W, t * LANES:(t + 1) * LANES])
    for c in range(GRID_W):
        o_ref[c] = jnp.concatenate(
            [s_ref[t, pl.ds(c, GT_ROWS, stride=GT_PITCH), :] for t in range(nt)], axis=1)


def _ffn(x, mods, mod_row, g, w_in, w_out, layer, final_g, final_norm, mix=None,
         swap_grid=False):
    n_rows = x.shape[0]
    tm = FFN_TM
    const = lambda shape: _single_buffered(shape, lambda i: (0,) * len(shape))
    in_specs = [
        pl.BlockSpec((tm, D_MODEL), lambda i: (i, 0)),
        _mod_spec(3, mod_row), _mod_spec(4, mod_row), _mod_spec(5, mod_row),
        const((1, D_MODEL)), _layer_spec(w_in, layer), _layer_spec(w_out, layer),
        const((1, D_MODEL)),
    ]
    args = [x, mods, mods, mods, g, w_in, w_out, final_g]
    n2_major = False
    if mix is not None:
        ya, yb, (w_mix, mix_layer), n2_major = mix
        nt = W_A // LANES
        in_specs += [pl.BlockSpec((nt, tm, LANES), lambda i: (0, i, 0)),
                     pl.BlockSpec((tm, W_B), lambda i: (i, 0)),
                     _mod_spec(2, mod_row), _layer_spec(w_mix, mix_layer)]
        args += [ya.reshape(nt, n_rows, LANES), yb, mods, w_mix]
    scratch = []
    if swap_grid:
        per_seq = SEQ // tm
        out_spec = pl.BlockSpec((None, GRID_W, GT_ROWS, D_MODEL),
                                lambda i: (i // per_seq, 0, i % per_seq, 0))
        out_shape = jax.ShapeDtypeStruct((n_rows // SEQ, GRID_W, GRID_H, D_MODEL), F32)
        scratch = [pltpu.VMEM((D_MODEL // LANES, GT_ROWS * GT_PITCH, LANES), F32)]
    else:
        out_spec = pl.BlockSpec((tm, D_MODEL), lambda i: (i, 0))
        out_shape = jax.ShapeDtypeStruct((n_rows, D_MODEL), F32)
    out = pl.pallas_call(
        functools.partial(_ffn_kernel, final_norm=final_norm, mix=mix is not None,
                          n2_major=n2_major, swap_grid=swap_grid),
        grid=(n_rows // tm,),
        in_specs=in_specs,
        out_specs=out_spec,
        out_shape=out_shape,
        scratch_shapes=scratch,
        compiler_params=_cparams("arbitrary"),
        name="ffn",
    )(*args)
    return out.reshape(n_rows, D_MODEL)


ODD_TM = 1024
RNN_TILES = D_RNN // LANES


def _in_odd_kernel(x_ref, xp_ref, xn_ref, sh_ref, sc_ref, g_ref, w_ref, cw_ref, cb_ref,
                   gate_ref, xl_ref, *, seq_len, chunk):
    tm = x_ref.shape[0]
    pieces, valid = _normed_tile(x_ref, xp_ref, xn_ref, sh_ref, sc_ref, g_ref, seq_len)
    half = tm // 2
    gate_ref[:half, :] = _gelu(jnp.dot(pieces[0], w_ref[:, :D_RNN],
                                       preferred_element_type=F32)).astype(BF16)
    gate_ref[half:, :] = _gelu(jnp.dot(pieces[1][:half], w_ref[:, :D_RNN],
                                       preferred_element_type=F32)).astype(BF16)
    cw = PROJ_COLS
    for c0 in range(0, D_RNN, cw):
        cols = slice(c0, min(c0 + cw, D_RNN))
        wcols = slice(D_RNN + cols.start, D_RNN + cols.stop)
        z_all = _dot_rows(pieces, w_ref[:, wcols])
        y = _dwconv(z_all, tm, cw_ref[:, cols], cb_ref[:, cols], 2, valid)
        kg = chunk // RG_GROUPS
        for c in range((cols.stop - cols.start) // LANES):
            yc = y[:, c * LANES:(c + 1) * LANES]
            for r0 in range(0, tm, chunk):
                for gq in range(RG_GROUPS):
                    xl_ref[c0 // LANES + c, pl.ds(r0 + gq, kg, stride=RG_GROUPS), :] = (
                        yc[r0 + gq * kg:r0 + (gq + 1) * kg])


def _in_odd(x, mods, mod_row, g, w, layer, cw, cb, seq_len, chunk):
    n_rows = x.shape[0]
    tm = ODD_TM
    const = lambda shape: _single_buffered(shape, lambda i: (0,) * len(shape))
    return pl.pallas_call(
        functools.partial(_in_odd_kernel, seq_len=seq_len, chunk=chunk),
        grid=(n_rows // tm,),
        in_specs=_halo_specs(tm, n_rows) + [
            _mod_spec(0, mod_row), _mod_spec(1, mod_row),
            const((1, D_MODEL)), _layer_spec(w, layer), const(cw.shape), const(cb.shape),
        ],
        out_specs=[pl.BlockSpec((tm, D_RNN), lambda i: (i, 0)),
                   pl.BlockSpec((RNN_TILES, tm, LANES), lambda i: (0, i, 0))],
        out_shape=[jax.ShapeDtypeStruct((n_rows, D_RNN), BF16),
                   jax.ShapeDtypeStruct((RNN_TILES, n_rows, LANES), F32)],
        compiler_params=_cparams("arbitrary"),
        name="in_proj_odd",
    )(x, x, x, mods, mods, g, w, cw, cb)


RG_T = 512
RG_NCH = SEQ // RG_T
RG_GROUPS = 8
RG_PLANES_PER_PASS = 6
SQRT_GUARD = 1e-30
RG_WIN = 3 * LANES


def _rg_window_start(j):
    return min(max(LANES * (j - 1), 0), D_RNN - RG_WIN)


def _rg_chunk(x_ref, w_ref, ba_ref, bx_ref, lam_ref, a_pl, b_pl, carry_ref, reverse, emit):
    t_rows = x_ref.shape[1]
    kg = t_rows // RG_GROUPS
    xb = [x_ref[j].astype(BF16) for j in range(RNN_TILES)]
    for j in range(RNN_TILES):
        tile = slice(j * LANES, (j + 1) * LANES)
        wt = _rg_window_start(j) // LANES
        pre = jnp.dot(jnp.concatenate(xb[wt:wt + RG_WIN // LANES], axis=1), w_ref[j],
                      preferred_element_type=F32)
        lam = lam_ref[:, tile]
        softplus_neg = jnp.maximum(-lam, 0.0) + jnp.log1p(jnp.exp(-jnp.abs(lam)))
        th_r = jnp.tanh(pre[:, :LANES] + ba_ref[:, tile])
        th_i = jnp.tanh(pre[:, LANES:] + bx_ref[:, tile])
        c3 = (-0.5 * RG_C * math.log2(math.e)) * softplus_neg
        av = jnp.exp2(c3 + c3 * th_r)
        hx = 0.5 * x_ref[j]
        y = 1.0 - av * av
        a_pl[j, 0:t_rows, :] = av
        b_pl[j, 0:t_rows, :] = (y * lax.rsqrt(jnp.maximum(y, SQRT_GUARD))) * (hx + hx * th_i)

    def rows_k(pl_ref, j, k):
        return pl_ref[j, k * RG_GROUPS:(k + 1) * RG_GROUPS, :]

    order = list(range(kg))[::-1] if reverse else list(range(kg))
    groups = list(range(RG_GROUPS))[::-1] if reverse else list(range(RG_GROUPS))
    for j0 in range(0, RNN_TILES, RG_PLANES_PER_PASS):
        planes = range(j0, min(j0 + RG_PLANES_PER_PASS, RNN_TILES))
        big_a, big_b = {}, {}
        for n, k in enumerate(order):
            for j in planes:
                ak, bk = rows_k(a_pl, j, k), rows_k(b_pl, j, k)
                if n == 0:
                    big_a[j], big_b[j] = ak, bk
                else:
                    big_b[j] = ak * big_b[j] + bk
                    big_a[j] = ak * big_a[j]
        h = {}
        for j in planes:
            c = carry_ref[j, 0:1, :]
            rows = [None] * RG_GROUPS
            for gq in groups:
                rows[gq] = c
                c = big_a[j][gq:gq + 1] * c + big_b[j][gq:gq + 1]
            carry_ref[j, 0:1, :] = c
            h[j] = jnp.concatenate(rows, axis=0)
        for k in order:
            for j in planes:
                h[j] = rows_k(a_pl, j, k) * h[j] + rows_k(b_pl, j, k)
                b_pl[j, k * RG_GROUPS:(k + 1) * RG_GROUPS, :] = h[j]
        for j in planes:
            emit(j, jnp.concatenate(
                [b_pl[j, pl.ds(gq, kg, stride=RG_GROUPS), :] for gq in range(RG_GROUPS)], axis=0))


def _rg_bwd_kernel(xc_ref, xl_ref, w_ref, ba_ref, bx_ref, lam_ref, oc_ref, ol_ref,
                   a_pl, b_pl, carry_ref):
    def emitter(o_ref):
        def emit(j, hcur):
            o_ref[:, j * LANES:(j + 1) * LANES] = hcur.astype(BF16)
        return emit

    @pl.when(pl.program_id(1) == 0)
    def _():
        carry_ref[...] = jnp.zeros_like(carry_ref)
        _rg_chunk(xc_ref, w_ref, ba_ref, bx_ref, lam_ref, a_pl, b_pl, carry_ref, True,
                  emitter(oc_ref))

    @pl.when(pl.program_id(1) > 0)
    def _():
        _rg_chunk(xl_ref, w_ref, ba_ref, bx_ref, lam_ref, a_pl, b_pl, carry_ref, True,
                  emitter(ol_ref))


def _rg_fwd_kernel(xc_ref, xl_ref, w_ref, ba_ref, bx_ref, lam_ref, gc_ref, gl_ref, hc_ref, hl_ref,
                   rc_ref, rl_ref, gtc_ref, gtl_ref, wo_ref, oc_ref, ol_ref,
                   a_pl, b_pl, carry_ref, y_ref):
    def run(x_ref, g_ref, hb_ref, res_ref, gt_ref, o_ref):
        t_rows = x_ref.shape[1]

        def emit(j, hcur):
            tile = slice(j * LANES, (j + 1) * LANES)
            y_ref[0:t_rows, tile] = (g_ref[:, tile].astype(F32)
                                     * (hcur + hb_ref[:, tile].astype(F32))).astype(BF16)

        _rg_chunk(x_ref, w_ref, ba_ref, bx_ref, lam_ref, a_pl, b_pl, carry_ref, False, emit)
        acc = jnp.dot(y_ref[0:t_rows, :], wo_ref[...], preferred_element_type=F32)
        o_ref[...] = res_ref[...] + gt_ref[0] * acc

    @pl.when(pl.program_id(1) == 0)
    def _():
        carry_ref[...] = jnp.zeros_like(carry_ref)
        run(xc_ref, gc_ref, hc_ref, rc_ref, gtc_ref, oc_ref)

    @pl.when(pl.program_id(1) > 0)
    def _():
        run(xl_ref, gl_ref, hl_ref, rl_ref, gtl_ref, ol_ref)


def _rg_scan(xc, xl, w, ba, bx, lam, reverse, fwd_args=None):
    if reverse:
        chunk = lambda s: RG_NCH - jnp.maximum(s, 1)
    else:
        chunk = lambda s: jnp.maximum(s - 1, 0)
    ctx_spec = lambda d: pl.BlockSpec((None, CTX_LEN, d), lambda b, s: (b, 0, 0))
    lat_spec = lambda d: pl.BlockSpec((None, RG_T, d), lambda b, s: (b, chunk(s), 0))
    const = lambda shape: _single_buffered(shape, lambda b, s: (0,) * len(shape))
    in_specs = [pl.BlockSpec((RNN_TILES, CTX_LEN, LANES), lambda b, s: (0, b, 0)),
                pl.BlockSpec((RNN_TILES, RG_T, LANES), lambda b, s: (0, b * RG_NCH + chunk(s), 0)),
                const(w.shape), const(ba.shape), const(bx.shape), const(lam.shape)]
    args = [xc, xl, w, ba, bx, lam]
    scratch = [pltpu.VMEM((RNN_TILES, RG_T, LANES), F32),
               pltpu.VMEM((RNN_TILES, RG_T, LANES), F32),
               pltpu.VMEM((RNN_TILES, 8, LANES), F32)]
    if reverse:
        body, name, width, dtype = _rg_bwd_kernel, "rglru_bwd", D_RNN, BF16
    else:
        gate_c, gate_l, hb_c, hb_l, res_c, res_l, mods, (w_out, out_layer) = fwd_args
        in_specs += [ctx_spec(D_RNN), lat_spec(D_RNN), ctx_spec(D_RNN), lat_spec(D_RNN),
                     ctx_spec(D_MODEL), lat_spec(D_MODEL),
                     _mod_spec(2, lambda b: CTX_MOD_ROW), _mod_spec(2, lambda b: b),
                     _layer_spec(w_out, out_layer)]
        args += [gate_c, gate_l, hb_c, hb_l, res_c, res_l, mods, mods, w_out]
        scratch += [pltpu.VMEM((RG_T, D_RNN), BF16)]
        body, name, width, dtype = _rg_fwd_kernel, "rglru_fwd", D_MODEL, F32
    return pl.pallas_call(
        body,
        grid=(BATCH, 1 + RG_NCH),
        in_specs=in_specs,
        out_specs=[ctx_spec(width), lat_spec(width)],
        out_shape=[jax.ShapeDtypeStruct((BATCH, CTX_LEN, width), dtype),
                   jax.ShapeDtypeStruct((BATCH, SEQ, width), dtype)],
        scratch_shapes=scratch,
        compiler_params=_cparams("arbitrary", "arbitrary"),
        name=name,
    )(*args)


def _rg_gate_weights(wa, wx):
    def window(w, j):
        ws = _rg_window_start(j)
        win = None
        for h in range(RG_HEADS):
            c0, c1 = max(h * RG_DH, j * LANES), min((h + 1) * RG_DH, (j + 1) * LANES)
            if c0 >= c1:
                continue
            r0 = h * RG_DH - ws
            assert 0 <= r0 and r0 + RG_DH <= RG_WIN
            blk = jnp.pad(w[h, :, c0 - h * RG_DH:c1 - h * RG_DH],
                          ((r0, RG_WIN - RG_DH - r0), (c0 - j * LANES, (j + 1) * LANES - c1)))
            win = blk if win is None else win + blk
        return win

    wins = [jnp.concatenate([window(wa, j), window(wx, j)], axis=1) for j in range(RNN_TILES)]
    return (0.5 * jnp.stack(wins)).astype(BF16)


def _col_major(layer):
    return layer < DEPTH and layer % 2 == 1 and (layer // 2) % 2 == 1


def kernel(x, c, ctx, c_ctx, w_ada, b_ada, norm_mix_g, norm_ffn_g, w_in_even, w_out_even, hy_conv_w, hy_conv_b, hy_f1_w, hy_f1_b, hy_f2_w, hy_f2_b, hy_f3_w, hy_f3_b, hy_sin_freq, hy_skip, sgu_ln_g, sgu_w, sgu_b, w_in_odd, rg_conv_w, rg_conv_b, rg_wa, rg_ba, rg_wx, rg_bx, rg_lam, w_out_odd, w_ffn_in, w_ffn_out, final_norm_g):
    mods_all = _ada_mods(c, c_ctx, w_ada, b_ada)
    xs = x.reshape(BATCH * SEQ, D_MODEL)
    cs = ctx.reshape(BATCH * CTX_LEN, D_MODEL)
    lat_row_even = lambda i: i // (SEQ // EVEN_TM)
    lat_row_odd = lambda i: i // (SEQ // ODD_TM)
    lat_row_ffn = lambda i: i // (SEQ // FFN_TM)
    ctx_row = lambda i: CTX_MOD_ROW
    n_tiles = W_A // LANES
    w_in_even, w_out_even, w_in_odd, w_out_odd, w_ffn_in, w_ffn_out = (
        w.astype(BF16) for w in (w_in_even, w_out_even, w_in_odd, w_out_odd, w_ffn_in, w_ffn_out))
    for l in range(DEPTH):
        run_ctx = l < DEPTH - 1
        is_rec = l % 2 == 1
        i = l // 2
        mods = mods_all[l].reshape(MOD_ROWS, 1, N_MOD * D_MODEL)
        g_mix = norm_mix_g[l].reshape(1, D_MODEL)
        mix_l = mix_c = None
        if is_rec:
            cw, cb = rg_conv_w[i], rg_conv_b[i].reshape(1, D_RNN)
            gate_l, xl = _in_odd(xs, mods, lat_row_odd, g_mix, w_in_odd, i, cw, cb, SEQ, RG_T)
            gate_c, xc = _in_odd(cs, mods, ctx_row, g_mix, w_in_odd, i, cw, cb, CTX_LEN, CTX_LEN)
            gate_l = gate_l.reshape(BATCH, SEQ, D_RNN)
            gate_c = gate_c.reshape(BATCH, CTX_LEN, D_RNN)
            row = lambda v: v.reshape(1, D_RNN)
            half = lambda v: 0.5 * row(v)
            hb_c, hb_l = _rg_scan(xc, xl, _rg_gate_weights(rg_wa[i, 1], rg_wx[i, 1]),
                                  half(rg_ba[i, 1]), half(rg_bx[i, 1]), row(rg_lam[i, 1]), True)
            cs_new, xs = _rg_scan(
                xc, xl, _rg_gate_weights(rg_wa[i, 0], rg_wx[i, 0]),
                half(rg_ba[i, 0]), half(rg_bx[i, 0]), row(rg_lam[i, 0]), False,
                (gate_c, gate_l, hb_c, hb_l, cs.reshape(BATCH, CTX_LEN, D_MODEL),
                 xs.reshape(BATCH, SEQ, D_MODEL), mods, (w_out_odd, i)))
            xs = xs.reshape(BATCH * SEQ, D_MODEL)
            if run_ctx:
                cs = cs_new.reshape(BATCH * CTX_LEN, D_MODEL)
        else:
            cw, cb = hy_conv_w[i], hy_conv_b[i].reshape(1, HY_COLS)
            lng = sgu_ln_g[i].reshape(1, W_B)
            sgw = sgu_w[i].astype(BF16)
            sgb = jnp.repeat(sgu_b[i].T, SGU_DH, axis=1)
            skip = hy_skip[i].reshape(1, HY_ORDER * W_A)
            fargs = (hy_f1_w[i], hy_f1_b[i], hy_f2_w[i], hy_f2_b[i], hy_f3_w[i], hy_f3_b[i],
                     hy_sin_freq[i])
            kf = _filter_spectrum(SEQ, *fargs, skip)
            za, yb = _in_even(xs, mods, lat_row_even, g_mix, w_in_even, i, cw, cb, lng, sgw, sgb,
                              SEQ, True)
            za = za.reshape(HY_TILES, BATCH, SEQ, LANES)
            y1 = _longconv(za, 0, za, n_tiles, kf, 0, flat_out=True)
            ya = _longconv(y1, 0, za, 2 * n_tiles, kf, n_tiles, flat_in=True)
            mix_l = (ya, yb, (w_out_even, i), True)
            if run_ctx:
                kfc = _filter_spectrum(CTX_LEN, *fargs, skip)
                zc, ybc = _in_even(cs, mods, ctx_row, g_mix, w_in_even, i, cw, cb, lng, sgw, sgb,
                                   CTX_LEN, False)
                zc = zc.reshape(HY_TILES, BATCH, CTX_LEN, LANES)
                y1c = _ctx_conv(zc, 0, zc, n_tiles, kfc, 0)
                yac = _ctx_conv(y1c, 0, zc, 2 * n_tiles, kfc, n_tiles)
                mix_c = (yac, ybc, (w_out_even, i), False)
        g_ffn = norm_ffn_g[l].reshape(1, D_MODEL)
        fg = final_norm_g.reshape(1, D_MODEL)
        swap = _col_major(l) != _col_major(l + 1)
        xs = _ffn(xs, mods, lat_row_ffn, g_ffn, w_ffn_in, w_ffn_out, l, fg, l == DEPTH - 1,
                  mix_l, swap)
        if run_ctx:
            cs = _ffn(cs, mods, ctx_row, g_ffn, w_ffn_in, w_ffn_out, l, fg, False, mix_c)
    return xs.reshape(BATCH, SEQ, D_MODEL)
```
